```python
import math
import jax
import jax.numpy as jnp
from jax import lax
import numpy as np

D_MODEL = 1024
BATCH = 4
SEQ = 4096
DEPTH = 2

N_MIXERS = 2
N_SSM_LAYERS = (DEPTH + N_MIXERS - 1) // N_MIXERS
N_ATTN_LAYERS = DEPTH // N_MIXERS

SSM_WIDTH = D_MODEL
SSM_GROUP = 16
SSM_GROUPS = SSM_WIDTH // SSM_GROUP
SSM_STATE = 64
DT_MIN = 1e-3
DT_MAX = 1e-1

ATT_HEADS = 8
HEAD_DIM = D_MODEL // ATT_HEADS
MOBA_BLOCK = 256
MOBA_TOPK = 3
Q_CHUNK = 32

N_GROUPS = 4
EXPERTS_PER_GROUP = 8
N_EXPERTS = N_GROUPS * EXPERTS_PER_GROUP
EXPERT_TOPK = 2
EXPERT_FF = D_MODEL // 2
ROW_BLOCK = 128

RMS_EPS = 1e-6
NEG_INF = -1e30

kernel_name = 'hybrid_s5_moba_hmoe'


def rmsnorm(x, g):
    xf = x.astype(jnp.float32)
    y = xf * lax.rsqrt(jnp.mean(xf * xf, axis=-1, keepdims=True) + RMS_EPS)
    return (y * g.astype(jnp.float32)).astype(x.dtype)


def _ssm_combine(left, right):
    a1r, a1i, b1r, b1i = left
    a2r, a2i, b2r, b2i = right
    return (a2r * a1r - a2i * a1i,
            a2r * a1i + a2i * a1r,
            a2r * b1r - a2i * b1i + b2r,
            a2r * b1i + a2i * b1r + b2i)


def s5_mixer(h, w_in, lam_re, lam_im, log_dt, b_re, b_im, c_re, c_im, d_skip, w_out):
    f32 = jnp.float32
    bsz, s, _ = h.shape
    u = (h @ w_in).astype(f32)
    ug = u.reshape(bsz, s, SSM_GROUPS, SSM_GROUP)
    lr = jnp.minimum(lam_re.astype(f32), -1e-4)
    li = lam_im.astype(f32)
    dt = jnp.exp(log_dt.astype(f32))[:, None]
    mag = jnp.exp(lr * dt)
    abar_re = mag * jnp.cos(li * dt)
    abar_im = mag * jnp.sin(li * dt)
    den = lr * lr + li * li
    nr = abar_re - 1.0
    gam_re = (nr * lr + abar_im * li) / den
    gam_im = (abar_im * lr - nr * li) / den
    br = b_re.astype(f32)
    bi = b_im.astype(f32)
    bb_re = gam_re[..., None] * br - gam_im[..., None] * bi
    bb_im = gam_re[..., None] * bi + gam_im[..., None] * br
    bu_re = jnp.einsum('bsgc,gpc->bsgp', ug, bb_re)
    bu_im = jnp.einsum('bsgc,gpc->bsgp', ug, bb_im)
    a_re = jnp.broadcast_to(abar_re, (1, s, SSM_GROUPS, SSM_STATE))
    a_im = jnp.broadcast_to(abar_im, (1, s, SSM_GROUPS, SSM_STATE))
    _, _, xr, xi = lax.associative_scan(_ssm_combine, (a_re, a_im, bu_re, bu_im), axis=1)
    y = (jnp.einsum('bsgp,gcp->bsgc', xr, c_re.astype(f32))
         - jnp.einsum('bsgp,gcp->bsgc', xi, c_im.astype(f32)))
    y = y.reshape(bsz, s, SSM_WIDTH) + d_skip.astype(f32) * u
    z = jax.nn.gelu(y).astype(h.dtype)
    vg = z @ w_out
    v, g = jnp.split(vg, 2, axis=-1)
    return v * jax.nn.sigmoid(g)


def moba_mixer(h, w_qkv, w_o):
    f32 = jnp.float32
    bsz, s, _ = h.shape
    qkv = (h @ w_qkv).reshape(bsz, s, 3, ATT_HEADS, HEAD_DIM)
    q = qkv[:, :, 0].transpose(0, 2, 1, 3)
    k = qkv[:, :, 1].transpose(0, 2, 1, 3)
    v = qkv[:, :, 2].transpose(0, 2, 1, 3)
    nb = -(-s // MOBA_BLOCK)
    pad = nb * MOBA_BLOCK - s
    kb = jnp.pad(k, ((0, 0), (0, 0), (0, pad), (0, 0))).reshape(bsz, ATT_HEADS, nb, MOBA_BLOCK, HEAD_DIM)
    vb = jnp.pad(v, ((0, 0), (0, 0), (0, pad), (0, 0))).reshape(bsz, ATT_HEADS, nb, MOBA_BLOCK, HEAD_DIM)
    kmean = jnp.mean(kb.astype(f32), axis=3)
    gate = jnp.einsum('bhsd,bhnd->bhsn', q.astype(f32), kmean)
    cur = jnp.arange(s) // MOBA_BLOCK
    past = jnp.arange(nb)[None, :] < cur[:, None]
    gate = jnp.where(past, gate, NEG_INF)
    k_sel = min(MOBA_TOPK, nb)
    _, idx = lax.top_k(gate, k_sel)
    sel_valid = idx < cur[None, None, :, None]
    scale = HEAD_DIM ** -0.5
    b_ix = jnp.arange(bsz)[:, None, None, None]
    h_ix = jnp.arange(ATT_HEADS)[None, :, None, None]

    def chunk(c):
        q0 = c * Q_CHUNK
        qc = lax.dynamic_slice_in_dim(q, q0, Q_CHUNK, axis=2)
        ic = lax.dynamic_slice_in_dim(idx, q0, Q_CHUNK, axis=2)
        vc = lax.dynamic_slice_in_dim(sel_valid, q0, Q_CHUNK, axis=2)
        k_g = kb[b_ix, h_ix, ic]
        v_g = vb[b_ix, h_ix, ic]
        s_sel = jnp.einsum('bhqd,bhqkld->bhqkl', qc, k_g).astype(f32) * scale
        s_sel = jnp.where(vc[..., None], s_sel, NEG_INF).reshape(bsz, ATT_HEADS, Q_CHUNK, k_sel * MOBA_BLOCK)
        own = q0 // MOBA_BLOCK
        k_own = lax.dynamic_index_in_dim(kb, own, axis=2, keepdims=False)
        v_own = lax.dynamic_index_in_dim(vb, own, axis=2, keepdims=False)
        s_own = jnp.einsum('bhqd,bhld->bhql', qc, k_own).astype(f32) * scale
        qpos = q0 + jnp.arange(Q_CHUNK)
        kpos = own * MOBA_BLOCK + jnp.arange(MOBA_BLOCK)
        s_own = jnp.where(kpos[None, :] <= qpos[:, None], s_own, NEG_INF)
        p = jax.nn.softmax(jnp.concatenate([s_sel, s_own], axis=-1), axis=-1)
        p_sel = p[..., :k_sel * MOBA_BLOCK].reshape(bsz, ATT_HEADS, Q_CHUNK, k_sel, MOBA_BLOCK).astype(v.dtype)
        p_own = p[..., k_sel * MOBA_BLOCK:].astype(v.dtype)
        return (jnp.einsum('bhqkl,bhqkld->bhqd', p_sel, v_g)
                + jnp.einsum('bhql,bhld->bhqd', p_own, v_own))

    o = lax.map(chunk, jnp.arange(s // Q_CHUNK))
    o = o.transpose(1, 0, 3, 2, 4).reshape(bsz, s, ATT_HEADS * HEAD_DIM)
    return o @ w_o


def hier_moe(h, w_group, w_expert, w1, w3, w2):
    f32 = jnp.float32
    bsz, s, d = h.shape
    n = bsz * s
    xt = h.reshape(n, d)
    g_prob = jax.nn.softmax((xt @ w_group).astype(f32), axis=-1)
    g_gate, g_idx = lax.top_k(g_prob, 1)
    e_logits = (xt @ w_expert).astype(f32).reshape(n, N_GROUPS, EXPERTS_PER_GROUP)
    e_in = e_logits[jnp.arange(n), g_idx[:, 0]]
    e_gate, e_local = lax.top_k(jax.nn.softmax(e_in, axis=-1), EXPERT_TOPK)
    e_gate = e_gate / jnp.sum(e_gate, axis=-1, keepdims=True)
    gates = (g_gate * e_gate).reshape(-1)
    expert_id = (g_idx * EXPERTS_PER_GROUP + e_local).reshape(-1).astype(jnp.int32)
    nk = n * EXPERT_TOPK
    order = jnp.argsort(expert_id)
    sorted_e = expert_id[order]
    counts = jnp.bincount(expert_id, length=N_EXPERTS)
    starts = jnp.cumsum(counts) - counts
    padded = ((counts + ROW_BLOCK - 1) // ROW_BLOCK) * ROW_BLOCK
    pad_end = jnp.cumsum(padded)
    pad_start = pad_end - padded
    dest = pad_start[sorted_e] + (jnp.arange(nk) - starts[sorted_e])
    cap = nk + N_EXPERTS * ROW_BLOCK
    n_blk = cap // ROW_BLOCK
    row_token = jnp.zeros((cap,), jnp.int32).at[dest].set((order // EXPERT_TOPK).astype(jnp.int32))
    row_gate = jnp.zeros((cap,), f32).at[dest].set(gates[order])
    blk_start = jnp.arange(n_blk) * ROW_BLOCK
    blk_e = jnp.minimum(jnp.sum(pad_end[None, :] <= blk_start[:, None], axis=1), N_EXPERTS - 1).astype(jnp.int32)
    xr = xt[row_token].reshape(n_blk, ROW_BLOCK, d)

    def expert_block(args):
        xb, e = args
        return (jax.nn.silu(xb @ w1[e]) * (xb @ w3[e])) @ w2[e]

    yr = lax.map(expert_block, (xr, blk_e)).reshape(cap, d)
    out = jnp.zeros((n, d), f32).at[row_token].add(yr.astype(f32) * row_gate[:, None])
    return out.astype(h.dtype).reshape(bsz, s, d)


def setup_inputs(seed: int = 0) -> dict:
    key = jax.random.key(seed)
    ks = jax.random.split(key, 24)
    f32 = jnp.float32

    def nrm(k, shape, scale):
        return jax.random.normal(k, shape, f32) * scale

    G, P, C = SSM_GROUPS, SSM_STATE, SSM_GROUP
    return {
        'x': nrm(ks[0], (BATCH, SEQ, D_MODEL), 1.0),
        'ln_mix': 1.0 + nrm(ks[1], (DEPTH, D_MODEL), 0.02),
        'ln_ffn': 1.0 + nrm(ks[2], (DEPTH, D_MODEL), 0.02),
        'ln_final': 1.0 + nrm(ks[3], (D_MODEL,), 0.02),
        'ssm_w_in': nrm(ks[4], (N_SSM_LAYERS, D_MODEL, SSM_WIDTH), D_MODEL ** -0.5),
        'ssm_lam_re': -0.5 + nrm(ks[5], (N_SSM_LAYERS, G, P), 0.01),
        'ssm_lam_im': math.pi * jnp.arange(P, dtype=f32) + nrm(ks[6], (N_SSM_LAYERS, G, P), 0.01),
        'ssm_log_dt': jax.random.uniform(ks[7], (N_SSM_LAYERS, G), f32, math.log(DT_MIN), math.log(DT_MAX)),
        'ssm_b_re': nrm(ks[8], (N_SSM_LAYERS, G, P, C), (2 * C) ** -0.5),
        'ssm_b_im': nrm(ks[9], (N_SSM_LAYERS, G, P, C), (2 * C) ** -0.5),
        'ssm_c_re': nrm(ks[10], (N_SSM_LAYERS, G, C, P), P ** -0.5),
        'ssm_c_im': nrm(ks[11], (N_SSM_LAYERS, G, C, P), P ** -0.5),
        'ssm_d': nrm(ks[12], (N_SSM_LAYERS, SSM_WIDTH), 0.5),
        'ssm_w_out': nrm(ks[13], (N_SSM_LAYERS, SSM_WIDTH, 2 * D_MODEL), SSM_WIDTH ** -0.5),
        'attn_w_qkv': nrm(ks[14], (N_ATTN_LAYERS, D_MODEL, 3 * ATT_HEADS * HEAD_DIM), D_MODEL ** -0.5),
        'attn_w_o': nrm(ks[15], (N_ATTN_LAYERS, ATT_HEADS * HEAD_DIM, D_MODEL), D_MODEL ** -0.5),
        'moe_w_group': nrm(ks[16], (DEPTH, D_MODEL, N_GROUPS), D_MODEL ** -0.5),
        'moe_w_expert': nrm(ks[17], (DEPTH, D_MODEL, N_EXPERTS), D_MODEL ** -0.5),
        'moe_w1': nrm(ks[18], (DEPTH, N_EXPERTS, D_MODEL, EXPERT_FF), D_MODEL ** -0.5),
        'moe_w3': nrm(ks[19], (DEPTH, N_EXPERTS, D_MODEL, EXPERT_FF), D_MODEL ** -0.5),
        'moe_w2': nrm(ks[20], (DEPTH, N_EXPERTS, EXPERT_FF, D_MODEL), EXPERT_FF ** -0.5),
    }


def reference(x, ln_mix, ln_ffn, ln_final, ssm_w_in, ssm_lam_re, ssm_lam_im, ssm_log_dt,
              ssm_b_re, ssm_b_im, ssm_c_re, ssm_c_im, ssm_d, ssm_w_out, attn_w_qkv, attn_w_o,
              moe_w_group, moe_w_expert, moe_w1, moe_w3, moe_w2):
    h = x
    for i in range(DEPTH):
        hn = rmsnorm(h, ln_mix[i])
        j = i // N_MIXERS
        if i % N_MIXERS == 0:
            mix = s5_mixer(hn, ssm_w_in[j], ssm_lam_re[j], ssm_lam_im[j], ssm_log_dt[j],
                           ssm_b_re[j], ssm_b_im[j], ssm_c_re[j], ssm_c_im[j], ssm_d[j], ssm_w_out[j])
        else:
            mix = moba_mixer(hn, attn_w_qkv[j], attn_w_o[j])
        h = h + mix
        h = h + hier_moe(rmsnorm(h, ln_ffn[i]), moe_w_group[i], moe_w_expert[i],
                         moe_w1[i], moe_w3[i], moe_w2[i])
    return rmsnorm(h, ln_final)
```

```python
import functools
import math

import jax
import jax.numpy as jnp
from jax import lax
from jax.experimental import pallas as pl
from jax.experimental.pallas import tpu as pltpu

F32 = jnp.float32
BF16 = jnp.bfloat16

D_MODEL = 1024
RMS_EPS = 1e-6
NEG_INF = -1e30

SSM_GROUP = 16
SSM_GROUPS = D_MODEL // SSM_GROUP
SSM_STATE = 64
SSM_CHUNK = 16
SSM_GB = 4

ATT_HEADS = 8
HEAD_DIM = 128
MOBA_BLOCK = 256
MOBA_TOPK = 3

N_GROUPS = 4
EXPERTS_PER_GROUP = 8
N_EXPERTS = 32
EXPERT_FF = 512
MOE_ROWS = 256

LANES = 128
VMEM_LIMIT = 48 * 1024 * 1024


def _cparams(*sem):
    return pltpu.CompilerParams(dimension_semantics=sem, vmem_limit_bytes=VMEM_LIMIT)


def _rmsnorm(x, g):
    return x * lax.rsqrt(jnp.mean(x * x, axis=-1, keepdims=True) + RMS_EPS) * g


def _norm_proj_kernel(x_ref, g_ref, w_ref, o_ref):
    xn = _rmsnorm(x_ref[...], g_ref[...]).astype(BF16)
    o_ref[...] = jnp.dot(xn, w_ref[...], preferred_element_type=F32).astype(o_ref.dtype)


def _norm_proj(x, g, w_bf16, tm, name):
    n, d = x.shape
    dout = w_bf16.shape[1]
    return pl.pallas_call(
        _norm_proj_kernel,
        out_shape=jax.ShapeDtypeStruct((n, dout), BF16),
        grid=(n // tm,),
        in_specs=[
            pl.BlockSpec((tm, d), lambda i: (i, 0)),
            pl.BlockSpec((1, d), lambda i: (0, 0)),
            pl.BlockSpec((d, dout), lambda i: (0, 0)),
        ],
        out_specs=pl.BlockSpec((tm, dout), lambda i: (i, 0)),
        compiler_params=_cparams("parallel"),
        name=name,
    )(x, g.reshape(1, d), w_bf16)


def _s5_mats(lam_re, lam_im, log_dt, b_re, b_im, c_re, c_im, d_skip):
    g_, p_, c_, t_ = SSM_GROUPS, SSM_STATE, SSM_GROUP, SSM_CHUNK
    lr = jnp.minimum(lam_re, -1e-4)
    li = lam_im
    dt = jnp.exp(log_dt)[:, None]
    mag = jnp.exp(lr * dt)
    abar_re = mag * jnp.cos(li * dt)
    abar_im = mag * jnp.sin(li * dt)
    den = lr * lr + li * li
    nr = abar_re - 1.0
    gam_re = (nr * lr + abar_im * li) / den
    gam_im = (abar_im * lr - nr * li) / den
    bb_re = gam_re[..., None] * b_re - gam_im[..., None] * b_im
    bb_im = gam_re[..., None] * b_im + gam_im[..., None] * b_re
    n = jnp.arange(2 * t_ + 1, dtype=F32)[:, None, None]
    pmag = jnp.exp(n * (lr * dt)[None])
    pr = pmag * jnp.cos(n * (li * dt)[None])
    pi = pmag * jnp.sin(n * (li * dt)[None])
    ca_re = c_re[None] * pr[:, :, None, :] - c_im[None] * pi[:, :, None, :]
    ca_im = c_re[None] * pi[:, :, None, :] + c_im[None] * pr[:, :, None, :]
    kern = (jnp.einsum('ngdp,gpc->ngcd', ca_re[:t_], bb_re, precision='highest')
            - jnp.einsum('ngdp,gpc->ngcd', ca_im[:t_], bb_im, precision='highest'))
    s_ix = jnp.arange(t_)[:, None]
    t_ix = jnp.arange(t_)[None, :]
    lag = t_ix - s_ix
    toe = kern[jnp.clip(lag, 0, t_ - 1)]
    toe = jnp.where((lag >= 0)[:, :, None, None, None], toe, 0.0)
    m = toe.transpose(2, 0, 3, 1, 4).reshape(g_, t_ * c_, t_ * c_)
    prs = pr[t_ - 1 - jnp.arange(t_)]
    pis = pi[t_ - 1 - jnp.arange(t_)]
    w_re = prs[..., None] * bb_re[None] - pis[..., None] * bb_im[None]
    w_im = prs[..., None] * bb_im[None] + pis[..., None] * bb_re[None]
    w_re = w_re.transpose(1, 0, 3, 2).reshape(g_, t_ * c_, p_)
    w_im = w_im.transpose(1, 0, 3, 2).reshape(g_, t_ * c_, p_)
    w = jnp.concatenate([w_re, w_im, w_im, w_re], axis=-1)
    v_re = ca_re[1:t_ + 1].transpose(1, 3, 0, 2).reshape(g_, p_, t_ * c_)
    v_im = -ca_im[1:t_ + 1].transpose(1, 3, 0, 2).reshape(g_, p_, t_ * c_)
    v = jnp.concatenate([v_re, v_im], axis=1)
    z = jnp.zeros_like(pr[0])

    def rows(k):
        ar, ai = pr[k], pi[k]
        return [jnp.concatenate([ar, ar], -1), jnp.concatenate([-ai, ai], -1),
                jnp.concatenate([ai, -ai], -1)]

    coef = jnp.stack(rows(t_) + rows(2 * t_) + [jnp.concatenate([z, z], -1)] * 2, axis=1)
    dsk = jnp.tile(d_skip.reshape(g_, 1, c_), (1, 1, t_))
    return m.astype(BF16), w.astype(BF16), v.astype(BF16), coef.astype(F32), dsk.astype(F32)


def _s5_kernel(u_ref, m_ref, w_ref, v_ref, coef_ref, dsk_ref, o_ref, ea_ref, eb_ref, sp_ref, *, bsz):
    rows = u_ref.shape[1]
    p2 = 2 * SSM_STATE
    row_ix = lax.broadcasted_iota(jnp.int32, (rows, p2), 0)

    def shifted(x, k):
        return jnp.where(row_ix < k * bsz, 0.0, pltpu.roll(x, k * bsz, axis=0))

    for gl in range(SSM_GB):
        sc = jnp.dot(u_ref[gl], w_ref[gl], preferred_element_type=F32)
        sa, sb = sc[:, :p2], sc[:, p2:]
        cf = coef_ref[gl]
        p16, q16a, q16b = cf[0:1], cf[1:2], cf[2:3]
        g1a, g1b = shifted(sa, 1), shifted(sb, 1)
        g2a, g2b = shifted(sa, 2), shifted(sb, 2)
        ea_ref[gl] = g1a + p16 * g2a + q16a * g2b
        eb_ref[gl] = g1b + p16 * g2b + q16b * g2a

    tile_rows = 2 * bsz
    coefs = [coef_ref[gl] for gl in range(SSM_GB)]

    def body(j, carry):
        r0 = pl.multiple_of(j * tile_rows, tile_rows)
        out = []
        for gl in range(SSM_GB):
            s_a, s_b = carry[2 * gl], carry[2 * gl + 1]
            cf = coefs[gl]
            p32, q32a, q32b = cf[3:4], cf[4:5], cf[5:6]
            n_a = p32 * s_a + q32a * s_b + ea_ref[gl, pl.ds(r0, tile_rows), :]
            n_b = p32 * s_b + q32b * s_a + eb_ref[gl, pl.ds(r0, tile_rows), :]
            sp_ref[gl, pl.ds(r0, tile_rows), :] = n_a
            out += [n_a, n_b]
        return tuple(out)

    zero = jnp.zeros((tile_rows, p2), F32)
    lax.fori_loop(0, rows // tile_rows, body, (zero,) * (2 * SSM_GB))

    for gl in range(SSM_GB):
        u = u_ref[gl]
        y = jnp.dot(u, m_ref[gl], preferred_element_type=F32)
        y = y + jnp.dot(sp_ref[gl].astype(BF16), v_ref[gl], preferred_element_type=F32)
        y = y + dsk_ref[gl] * u.astype(F32)
        o_ref[gl] = jax.nn.gelu(y).astype(o_ref.dtype)


def _s5_core(u_flat, mats, bsz):
    m, w, v, coef, dsk = mats
    g_, rows, tc = u_flat.shape
    p2 = 2 * SSM_STATE
    gb = SSM_GB
    spec3 = lambda a, b: pl.BlockSpec((gb, a, b), lambda i: (i, 0, 0))
    return pl.pallas_call(
        functools.partial(_s5_kernel, bsz=bsz),
        out_shape=jax.ShapeDtypeStruct((g_, rows, tc), BF16),
        grid=(g_ // gb,),
        in_specs=[spec3(rows, tc), spec3(tc, tc), spec3(tc, 2 * p2), spec3(p2, tc),
                  spec3(8, p2), spec3(1, tc)],
        out_specs=spec3(rows, tc),
        scratch_shapes=[pltpu.VMEM((gb, rows, p2), F32)] * 3,
        compiler_params=_cparams("parallel"),
        name="s5_core",
    )(u_flat, m, w, v, coef, dsk)


def _glu_out_kernel(z_ref, w_ref, x_ref, o_ref):
    vg = jnp.dot(z_ref[...], w_ref[...], preferred_element_type=F32)
    d = o_ref.shape[1]
    o_ref[...] = x_ref[...] + vg[:, :d] * jax.nn.sigmoid(vg[:, d:])


def _glu_out(z, w_bf16, x, tm):
    n, d = x.shape
    return pl.pallas_call(
        _glu_out_kernel,
        out_shape=jax.ShapeDtypeStruct((n, d), F32),
        grid=(n // tm,),
        in_specs=[
            pl.BlockSpec((tm, z.shape[1]), lambda i: (i, 0)),
            pl.BlockSpec(w_bf16.shape, lambda i: (0, 0)),
            pl.BlockSpec((tm, d), lambda i: (i, 0)),
        ],
        out_specs=pl.BlockSpec((tm, d), lambda i: (i, 0)),
        compiler_params=_cparams("parallel"),
        name="s5_glu_out",
    )(z, w_bf16, x)


def _proj_res_kernel(a_ref, w_ref, x_ref, o_ref):
    o_ref[...] = x_ref[...] + jnp.dot(a_ref[...], w_ref[...], preferred_element_type=F32)


def _proj_res(a, w_bf16, x, tm):
    n, d = x.shape
    return pl.pallas_call(
        _proj_res_kernel,
        out_shape=jax.ShapeDtypeStruct((n, d), F32),
        grid=(n // tm,),
        in_specs=[
            pl.BlockSpec((tm, a.shape[1]), lambda i: (i, 0)),
            pl.BlockSpec(w_bf16.shape, lambda i: (0, 0)),
            pl.BlockSpec((tm, d), lambda i: (i, 0)),
        ],
        out_specs=pl.BlockSpec((tm, d), lambda i: (i, 0)),
        compiler_params=_cparams("parallel"),
        name="attn_out_proj",
    )(a, w_bf16, x)


def _moba_kernel(q_ref, k_ref, v_ref, e_ref, o_ref, *, nb):
    blk = MOBA_BLOCK
    scale = HEAD_DIM ** -0.5
    kf = k_ref[...].astype(F32)
    kmean = jnp.mean(kf.reshape(nb, blk, HEAD_DIM), axis=1)
    lane = lax.broadcasted_iota(jnp.int32, (blk, nb), 1)
    r_ix = lax.broadcasted_iota(jnp.int32, (blk, blk), 0)
    c_ix = lax.broadcasted_iota(jnp.int32, (blk, blk), 1)
    nt = (((1,), (1,)), ((), ()))
    for i in range(nb):
        q = q_ref[i * blk:(i + 1) * blk, :]
        k_own = k_ref[i * blk:(i + 1) * blk, :]
        v_own = v_ref[i * blk:(i + 1) * blk, :]
        s_own = lax.dot_general(q, k_own, nt, preferred_element_type=F32) * scale
        s_own = jnp.where(c_ix <= r_ix, s_own, NEG_INF)
        m = jnp.max(s_own, axis=-1, keepdims=True)
        if i > 0:
            w_ = i * blk
            if i > MOBA_TOPK:
                gate = lax.dot_general(q.astype(F32), kmean, nt, preferred_element_type=F32,
                                       precision=lax.Precision.HIGHEST)
                gate = jnp.where(lane < i, gate, NEG_INF)
                sel = jnp.zeros((blk, nb), jnp.bool_)
                for _ in range(MOBA_TOPK):
                    gm = jnp.max(gate, axis=-1, keepdims=True)
                    first = jnp.min(jnp.where(gate == gm, lane, nb), axis=-1, keepdims=True)
                    pick = lane == first
                    sel = jnp.logical_or(sel, pick)
                    gate = jnp.where(pick, -jnp.inf, gate)
                bias = jnp.where(sel, 0.0, NEG_INF).astype(BF16)
                bias_full = jnp.dot(bias, e_ref[:, :w_], preferred_element_type=F32)
            s_past = lax.dot_general(q, k_ref[:w_, :], nt, preferred_element_type=F32) * scale
            if i > MOBA_TOPK:
                s_past = s_past + bias_full
            m = jnp.maximum(m, jnp.max(s_past, axis=-1, keepdims=True))
            p_past = jnp.exp(s_past - m)
        p_own = jnp.exp(s_own - m)
        l = jnp.sum(p_own, axis=-1, keepdims=True)
        acc = jnp.dot(p_own.astype(BF16), v_own, preferred_element_type=F32)
        if i > 0:
            l = l + jnp.sum(p_past, axis=-1, keepdims=True)
            acc = acc + jnp.dot(p_past.astype(BF16), v_ref[:w_, :], preferred_element_type=F32)
        o_ref[i * blk:(i + 1) * blk, :] = (acc / l).astype(o_ref.dtype)


def _moba(qkv, bsz, seq):
    nb = seq // MOBA_BLOCK
    hh = ATT_HEADS
    blk_of = (jnp.arange(seq) // MOBA_BLOCK)[None, :] == jnp.arange(nb)[:, None]
    e_all = blk_of.astype(BF16)
    spec = lambda off: pl.BlockSpec((seq, HEAD_DIM), lambda b, h: (b, off + h))
    return pl.pallas_call(
        functools.partial(_moba_kernel, nb=nb),
        out_shape=jax.ShapeDtypeStruct((bsz * seq, hh * HEAD_DIM), BF16),
        grid=(bsz, hh),
        in_specs=[spec(0), spec(hh), spec(2 * hh), pl.BlockSpec((nb, seq), lambda b, h: (0, 0))],
        out_specs=pl.BlockSpec((seq, HEAD_DIM), lambda b, h: (b, h)),
        compiler_params=_cparams("parallel", "parallel"),
        name="moba_attn",
    )(qkv, qkv, qkv, e_all)


def _router_kernel(h_ref, g_ref, w_ref, meta_ref, cnt_ref):
    xn = _rmsnorm(h_ref[...], g_ref[...])
    logits = jnp.dot(xn, w_ref[...], preferred_element_type=F32, precision=lax.Precision.HIGHEST)
    tm = logits.shape[0]
    lane = lax.broadcasted_iota(jnp.int32, (tm, LANES), 1)
    ninf = -jnp.inf
    lg = jnp.where(lane < N_GROUPS, logits, ninf)
    gm = jnp.max(lg, axis=-1, keepdims=True)
    g_idx = jnp.min(jnp.where(lg == gm, lane, LANES), axis=-1, keepdims=True)
    g_gate = 1.0 / jnp.sum(jnp.exp(lg - gm), axis=-1, keepdims=True)
    lo = N_GROUPS + EXPERTS_PER_GROUP * g_idx
    le = jnp.where((lane >= lo) & (lane < lo + EXPERTS_PER_GROUP), logits, ninf)
    m1 = jnp.max(le, axis=-1, keepdims=True)
    i1 = jnp.min(jnp.where(le == m1, lane, LANES), axis=-1, keepdims=True)
    le2 = jnp.where(lane == i1, ninf, le)
    m2 = jnp.max(le2, axis=-1, keepdims=True)
    i2 = jnp.min(jnp.where(le2 == m2, lane, LANES), axis=-1, keepdims=True)
    p2 = jnp.exp(m2 - m1)
    gate1 = g_gate / (1.0 + p2)
    gate2 = g_gate * p2 / (1.0 + p2)
    e1 = (i1 - N_GROUPS).astype(F32)
    e2 = (i2 - N_GROUPS).astype(F32)
    meta = jnp.where(lane == 0, e1, jnp.where(lane == 1, e2, jnp.where(lane == 2, gate1,
                     jnp.where(lane == 3, gate2, 0.0))))
    meta_ref[...] = meta
    onehot = ((lane == i1 - N_GROUPS) | (lane == i2 - N_GROUPS)).astype(F32)

    @pl.when(pl.program_id(0) == 0)
    def _():
        cnt_ref[...] = jnp.zeros_like(cnt_ref)

    cnt_ref[...] += jnp.sum(onehot, axis=0, keepdims=True)


def _router(h, g, w_r, tm):
    n, d = h.shape
    return pl.pallas_call(
        _router_kernel,
        out_shape=(jax.ShapeDtypeStruct((n, LANES), F32), jax.ShapeDtypeStruct((1, LANES), F32)),
        grid=(n // tm,),
        in_specs=[
            pl.BlockSpec((tm, d), lambda i: (i, 0)),
            pl.BlockSpec((1, d), lambda i: (0, 0)),
            pl.BlockSpec((d, LANES), lambda i: (0, 0)),
        ],
        out_specs=(pl.BlockSpec((tm, LANES), lambda i: (i, 0)),
                   pl.BlockSpec((1, LANES), lambda i: (0, 0))),
        compiler_params=_cparams("arbitrary"),
        name="moe_router",
    )(h, g.reshape(1, d), w_r)


def _plan_kernel(meta_ref, pstart_ref, tri_ref, dest_ref, base_ref):
    @pl.when(pl.program_id(0) == 0)
    def _():
        base_ref[...] = jnp.zeros_like(base_ref)

    meta = meta_ref[...]
    tm = meta.shape[0]
    lane = lax.broadcasted_iota(jnp.int32, (tm, LANES), 1)
    e1 = meta[:, 0:1].astype(jnp.int32)
    e2 = meta[:, 1:2].astype(jnp.int32)
    oh1 = (lane == e1).astype(F32)
    oh2 = (lane == e2).astype(F32)
    both = oh1 + oh2
    cnt = jnp.dot(tri_ref[...], both.astype(BF16), preferred_element_type=F32) + base_ref[...]
    pos = cnt + pstart_ref[...]
    d1 = jnp.sum(pos * oh1, axis=-1, keepdims=True)
    d2 = jnp.sum(pos * oh2, axis=-1, keepdims=True)
    slab = jnp.where(lane == 0, d1, jnp.where(lane == 1, d2, 0.0))
    dest_ref[...] = jnp.transpose(slab)[0:8, :].astype(jnp.int32)
    base_ref[...] += jnp.sum(both, axis=0, keepdims=True)


def _plan(meta, pstart, tm):
    n = meta.shape[0]
    tri = (jnp.arange(tm)[:, None] > jnp.arange(tm)[None, :]).astype(BF16)
    return pl.pallas_call(
        _plan_kernel,
        out_shape=jax.ShapeDtypeStruct((8, n), jnp.int32),
        grid=(n // tm,),
        in_specs=[
            pl.BlockSpec((tm, LANES), lambda i: (i, 0)),
            pl.BlockSpec((1, LANES), lambda i: (0, 0)),
            pl.BlockSpec((tm, tm), lambda i: (0, 0)),
        ],
        out_specs=pl.BlockSpec((8, tm), lambda i: (0, i)),
        scratch_shapes=[pltpu.VMEM((1, LANES), F32)],
        compiler_params=_cparams("arbitrary"),
        name="moe_plan",
    )(meta, pstart, tri)


def _dispatch_kernel(d1_ref, d2_ref, h_ref, g_ref, xs_ref, buf_ref, sem):
    tm = h_ref.shape[0]
    base = pl.program_id(0) * tm
    buf_ref[...] = _rmsnorm(h_ref[...], g_ref[...])

    def row_copy(r, dst_row):
        return pltpu.make_async_copy(buf_ref.at[pl.ds(r, 1), :], xs_ref.at[pl.ds(dst_row, 1), :], sem)

    def issue(r, c):
        row_copy(r, d1_ref[base + r]).start()
        row_copy(r, d2_ref[base + r]).start()
        return c

    lax.fori_loop(0, tm, issue, 0)

    def drain(r, c):
        row_copy(r, 0).wait()
        row_copy(r, 0).wait()
        return c

    lax.fori_loop(0, tm, drain, 0)


def _dispatch(d1, d2, h, g, cap, tm):
    n, d = h.shape
    return pl.pallas_call(
        _dispatch_kernel,
        out_shape=jax.ShapeDtypeStruct((cap, d), F32),
        grid_spec=pltpu.PrefetchScalarGridSpec(
            num_scalar_prefetch=2,
            grid=(n // tm,),
            in_specs=[
                pl.BlockSpec((tm, d), lambda i, *_: (i, 0)),
                pl.BlockSpec((1, d), lambda i, *_: (0, 0)),
            ],
            out_specs=pl.BlockSpec(memory_space=pl.ANY),
            scratch_shapes=[pltpu.VMEM((tm, d), F32), pltpu.SemaphoreType.DMA],
        ),
        compiler_params=_cparams("arbitrary"),
        name="moe_dispatch",
    )(d1, d2, h, g.reshape(1, d))


def _expert_kernel(be_ref, bv_ref, nu_ref, x_ref, w1_ref, w3_ref, w2_ref, y_ref, w1c, w3c, w2c):
    i = pl.program_id(0)
    used = i < nu_ref[0]
    prev = be_ref[jnp.maximum(i - 1, 0)]
    fresh = jnp.logical_or(i == 0, be_ref[i] != prev)

    @pl.when(jnp.logical_and(used, fresh))
    def _():
        w1c[...] = w1_ref[0, 0].astype(BF16)
        w3c[...] = w3_ref[0, 0].astype(BF16)
        w2c[...] = w2_ref[0, 0].astype(BF16)

    @pl.when(used)
    def _():
        rows = x_ref.shape[0]
        r_ix = lax.broadcasted_iota(jnp.int32, (rows, 1), 0)
        x = jnp.where(r_ix < bv_ref[i], x_ref[...], 0.0).astype(BF16)
        a = jnp.dot(x, w1c[...], preferred_element_type=F32)
        b = jnp.dot(x, w3c[...], preferred_element_type=F32)
        act = (jax.nn.silu(a) * b).astype(BF16)
        y_ref[...] = jnp.dot(act, w2c[...], preferred_element_type=F32)


def _experts(blk_e, blk_valid, n_used, xs, w1, w3, w2, layer):
    cap, d = xs.shape
    rb = MOE_ROWS
    n_blk = cap // rb
    ff = w1.shape[3]
    row_map = lambda i, be, bv, nu: (jnp.minimum(i, nu[0] - 1), 0)
    return pl.pallas_call(
        _expert_kernel,
        out_shape=jax.ShapeDtypeStruct((cap, d), F32),
        grid_spec=pltpu.PrefetchScalarGridSpec(
            num_scalar_prefetch=3,
            grid=(n_blk,),
            in_specs=[
                pl.BlockSpec((rb, d), row_map),
                pl.BlockSpec((1, 1, d, ff), lambda i, be, bv, nu: (layer, be[i], 0, 0)),
                pl.BlockSpec((1, 1, d, ff), lambda i, be, bv, nu: (layer, be[i], 0, 0)),
                pl.BlockSpec((1, 1, ff, d), lambda i, be, bv, nu: (layer, be[i], 0, 0)),
            ],
            out_specs=pl.BlockSpec((rb, d), row_map),
            scratch_shapes=[pltpu.VMEM((d, ff), BF16), pltpu.VMEM((d, ff), BF16),
                            pltpu.VMEM((ff, d), BF16)],
        ),
        compiler_params=_cparams("arbitrary"),
        name="moe_experts",
    )(blk_e, blk_valid, n_used, xs, w1, w3, w2)


def _combine_kernel(d1_ref, d2_ref, h_ref, meta_ref, g_ref, ys_ref, o_ref, buf_ref, sem, *, final_norm):
    tm = h_ref.shape[0]
    base = pl.program_id(0) * tm

    def row_copy(slot, r, src_row):
        return pltpu.make_async_copy(ys_ref.at[pl.ds(src_row, 1), :], buf_ref.at[slot, pl.ds(r, 1), :], sem)

    def issue(r, c):
        row_copy(0, r, d1_ref[base + r]).start()
        row_copy(1, r, d2_ref[base + r]).start()
        return c

    lax.fori_loop(0, tm, issue, 0)

    def drain(r, c):
        row_copy(0, r, 0).wait()
        row_copy(1, r, 0).wait()
        return c

    lax.fori_loop(0, tm, drain, 0)
    meta = meta_ref[...]
    out = h_ref[...] + (meta[:, 2:3] * buf_ref[0] + meta[:, 3:4] * buf_ref[1])
    if final_norm:
        out = _rmsnorm(out, g_ref[...])
    o_ref[...] = out


def _combine(d1, d2, h, meta, ys, g_final, tm, final_norm):
    n, d = h.shape
    return pl.pallas_call(
        functools.partial(_combine_kernel, final_norm=final_norm),
        out_shape=jax.ShapeDtypeStruct((n, d), F32),
        grid_spec=pltpu.PrefetchScalarGridSpec(
            num_scalar_prefetch=2,
            grid=(n // tm,),
            in_specs=[
                pl.BlockSpec((tm, d), lambda i, *_: (i, 0)),
                pl.BlockSpec((tm, LANES), lambda i, *_: (i, 0)),
                pl.BlockSpec((1, d), lambda i, *_: (0, 0)),
                pl.BlockSpec(memory_space=pl.ANY),
            ],
            out_specs=pl.BlockSpec((tm, d), lambda i, *_: (i, 0)),
            scratch_shapes=[pltpu.VMEM((2, tm, d), F32), pltpu.SemaphoreType.DMA],
        ),
        compiler_params=_cparams("arbitrary"),
        name="moe_combine",
    )(d1, d2, h, meta, g_final.reshape(1, d), ys)


def _hier_moe(h, ln_g, w_group, w_expert, w1, w3, w2, layer, g_final, final_norm):
    n, d = h.shape
    rb = MOE_ROWS
    cap = 2 * n + N_EXPERTS * rb
    n_blk = cap // rb
    w_r = jnp.zeros((d, LANES), F32).at[:, :N_GROUPS].set(w_group)
    w_r = w_r.at[:, N_GROUPS:N_GROUPS + N_EXPERTS].set(w_expert)
    meta, counts = _router(h, ln_g, w_r, 512)
    cnt = counts[0, :N_EXPERTS].astype(jnp.int32)
    padded = ((cnt + rb - 1) // rb) * rb
    pad_end = jnp.cumsum(padded)
    pad_start = pad_end - padded
    pstart = jnp.zeros((1, LANES), F32).at[0, :N_EXPERTS].set(pad_start.astype(F32))
    blk_start = jnp.arange(n_blk, dtype=jnp.int32) * rb
    blk_e = jnp.minimum(jnp.sum(pad_end[None, :] <= blk_start[:, None], axis=1), N_EXPERTS - 1).astype(jnp.int32)
    blk_valid = jnp.clip(pad_start[blk_e] + cnt[blk_e] - blk_start, 0, rb).astype(jnp.int32)
    n_used = (pad_end[-1] // rb).astype(jnp.int32).reshape(1)
    dest = _plan(meta, pstart, 256)
    d1, d2 = dest[0], dest[1]
    xs = _dispatch(d1, d2, h, ln_g, cap, 256)
    ys = _experts(blk_e, blk_valid, n_used, xs, w1, w3, w2, layer)
    return _combine(d1, d2, h, meta, ys, g_final, 256, final_norm)


def kernel(x, ln_mix, ln_ffn, ln_final, ssm_w_in, ssm_lam_re, ssm_lam_im, ssm_log_dt, ssm_b_re, ssm_b_im, ssm_c_re, ssm_c_im, ssm_d, ssm_w_out, attn_w_qkv, attn_w_o, moe_w_group, moe_w_expert, moe_w1, moe_w3, moe_w2):
    bsz, seq, d = x.shape
    n = bsz * seq
    h = x.reshape(n, d)
    t_, c_, g_ = SSM_CHUNK, SSM_GROUP, SSM_GROUPS
    nk = seq // t_

    u = _norm_proj(h, ln_mix[0], ssm_w_in[0].astype(BF16), 512, "s5_in_proj")
    u_flat = u.reshape(bsz, nk, t_, g_, c_).transpose(3, 1, 0, 2, 4).reshape(g_, nk * bsz, t_ * c_)
    mats = _s5_mats(ssm_lam_re[0], ssm_lam_im[0], ssm_log_dt[0], ssm_b_re[0], ssm_b_im[0],
                    ssm_c_re[0], ssm_c_im[0], ssm_d[0])
    z_flat = _s5_core(u_flat, mats, bsz)
    z = z_flat.reshape(g_, nk, bsz, t_, c_).transpose(2, 1, 3, 0, 4).reshape(n, d)
    h = _glu_out(z, ssm_w_out[0].astype(BF16), h, 512)
    h = _hier_moe(h, ln_ffn[0], moe_w_group[0], moe_w_expert[0], moe_w1, moe_w3, moe_w2, 0,
                  ln_final, False)

    qkv = _norm_proj(h, ln_mix[1], attn_w_qkv[0].astype(BF16), 512, "attn_qkv_proj")
    o = _moba(qkv, bsz, seq)
    h = _proj_res(o, attn_w_o[0].astype(BF16), h, 512)
    h = _hier_moe(h, ln_ffn[1], moe_w_group[1], moe_w_expert[1], moe_w1, moe_w3, moe_w2, 1,
                  ln_final, True)
    return h.reshape(bsz, seq, d)
```

```python
import functools
import math

import jax
import jax.numpy as jnp
from jax import lax
from jax.experimental import pallas as pl
from jax.experimental.pallas import tpu as pltpu

F32 = jnp.float32
BF16 = jnp.bfloat16

D_MODEL = 1024
RMS_EPS = 1e-6
NEG_INF = -1e30

SSM_GROUP = 16
SSM_GROUPS = D_MODEL // SSM_GROUP
SSM_STATE = 64
SSM_CHUNK = 16
SSM_GB = 8
SSM_WIN = 8

ATT_HEADS = 8
HEAD_DIM = 128
MOBA_BLOCK = 256
MOBA_TOPK = 3

N_GROUPS = 4
EXPERTS_PER_GROUP = 8
N_EXPERTS = 32
EXPERT_FF = 512
MOE_ROWS = 256

LANES = 128
VMEM_LIMIT = 48 * 1024 * 1024

_NT = (((1,), (1,)), ((), ()))
_TN = (((0,), (0,)), ((), ()))


def _cparams(*sem):
    return pltpu.CompilerParams(dimension_semantics=sem, vmem_limit_bytes=VMEM_LIMIT)


def _rmsnorm(x, g):
    return x * lax.rsqrt(jnp.mean(x * x, axis=-1, keepdims=True) + RMS_EPS) * g


def _norm_proj_kernel(x_ref, g_ref, w_ref, o_ref):
    xn = _rmsnorm(x_ref[...], g_ref[...]).astype(BF16)
    o_ref[...] = jnp.dot(xn, w_ref[...], preferred_element_type=F32).astype(o_ref.dtype)


def _norm_proj(x, g, w_bf16, tm, out_dtype, name):
    n, d = x.shape
    dout = w_bf16.shape[1]
    return pl.pallas_call(
        _norm_proj_kernel,
        out_shape=jax.ShapeDtypeStruct((n, dout), out_dtype),
        grid=(n // tm,),
        in_specs=[
            pl.BlockSpec((tm, d), lambda i: (i, 0)),
            pl.BlockSpec((1, d), lambda i: (0, 0)),
            pl.BlockSpec((d, dout), lambda i: (0, 0)),
        ],
        out_specs=pl.BlockSpec((tm, dout), lambda i: (i, 0)),
        compiler_params=_cparams("parallel"),
        name=name,
    )(x, g.reshape(1, d), w_bf16)


def _s5_mats(lam_re, lam_im, log_dt, b_re, b_im, c_re, c_im):
    g_, p_, c_, t_ = SSM_GROUPS, SSM_STATE, SSM_GROUP, SSM_CHUNK
    lr = jnp.minimum(lam_re, -1e-4)
    li = lam_im
    dt = jnp.exp(log_dt)[:, None]
    mag = jnp.exp(lr * dt)
    abar_re = mag * jnp.cos(li * dt)
    abar_im = mag * jnp.sin(li * dt)
    den = lr * lr + li * li
    nr = abar_re - 1.0
    gam_re = (nr * lr + abar_im * li) / den
    gam_im = (abar_im * lr - nr * li) / den
    bb_re = gam_re[..., None] * b_re - gam_im[..., None] * b_im
    bb_im = gam_re[..., None] * b_im + gam_im[..., None] * b_re

    def powers(ns):
        nf = jnp.asarray(ns, F32)[None, :, None]
        pm = jnp.exp(nf * (lr * dt)[:, None, :])
        ang = nf * (li * dt)[:, None, :]
        return pm * jnp.cos(ang), pm * jnp.sin(ang)

    pr, pi = powers(list(range(t_ + 1)))
    ca_re = c_re[:, None] * pr[:, :, None, :] - c_im[:, None] * pi[:, :, None, :]
    ca_im = c_re[:, None] * pi[:, :, None, :] + c_im[:, None] * pr[:, :, None, :]
    ca_n = jnp.concatenate([ca_re[:, :t_], ca_im[:, :t_]], axis=-1).reshape(g_, t_ * c_, 2 * p_)
    bb_s = jnp.concatenate([bb_re, -bb_im], axis=1)
    kflat = jnp.einsum('gxp,gpc->gcx', ca_n, bb_s, precision='highest')
    prs = pr[:, :t_][:, ::-1][:, :, None, :]
    pis = pi[:, :t_][:, ::-1][:, :, None, :]
    bt_re = bb_re.transpose(0, 2, 1)[:, None]
    bt_im = bb_im.transpose(0, 2, 1)[:, None]
    w_re = prs * bt_re - pis * bt_im
    w_im = prs * bt_im + pis * bt_re
    w = jnp.concatenate([w_re, w_im, w_im, w_re], axis=-1).reshape(g_, t_ * c_, 4 * p_)
    prt = pr.transpose(0, 2, 1)[:, :, 1:, None]
    pit = pi.transpose(0, 2, 1)[:, :, 1:, None]
    ct_re = c_re.transpose(0, 2, 1)[:, :, None, :]
    ct_im = c_im.transpose(0, 2, 1)[:, :, None, :]
    v_re = (ct_re * prt - ct_im * pit).reshape(g_, p_, t_ * c_)
    v_im = -(ct_re * pit + ct_im * prt).reshape(g_, p_, t_ * c_)
    v = jnp.concatenate([v_re, v_im], axis=1)
    qr, qi = powers([t_ * (1 << j) for j in range(4)])
    rows = []
    for j in range(4):
        ar, ai = qr[:, j], qi[:, j]
        rows += [jnp.concatenate([ar, ar], -1), jnp.concatenate([-ai, ai], -1),
                 jnp.concatenate([ai, -ai], -1)]
    rows += [jnp.zeros_like(rows[0])] * 4
    coef = jnp.stack(rows, axis=1)
    return kflat.astype(F32), w.astype(BF16), v.astype(BF16), coef.astype(F32)


def _piece_transpose(a, lane):
    for d in (4, 2, 1):
        keep = (lane & (d * SSM_GROUP)) == 0
        nxt = list(a)
        for i in range(8):
            if i & d:
                continue
            lo, hi = a[i], a[i + d]
            nxt[i] = jnp.where(keep, lo, pltpu.roll(hi, d * SSM_GROUP, axis=1))
            nxt[i + d] = jnp.where(keep, pltpu.roll(lo, LANES - d * SSM_GROUP, axis=1), hi)
        a = nxt
    return a


def _s5_kernel(u_ref, kf_ref, w_ref, v_ref, coef_ref, d_ref, o_ref,
               m_ref, vf_ref, zf_ref, ea_ref, eb_ref, sp_ref):
    t_, c_, gb, win = SSM_CHUNK, SSM_GROUP, SSM_GB, SSM_WIN
    nk = u_ref.shape[0] // t_
    p2 = 2 * SSM_STATE
    tc = t_ * c_

    @pl.when(pl.program_id(1) == 0)
    def _():
        lane_tc = lax.broadcasted_iota(jnp.int32, (c_, tc), 1)
        for g in range(gb):
            kf = kf_ref[g]
            for s in range(t_):
                rolled = kf if s == 0 else pltpu.roll(kf, s * c_, axis=1)
                m_ref[g, s * c_:(s + 1) * c_, :] = jnp.where(lane_tc >= s * c_, rolled, 0.0).astype(BF16)

    lane = lax.broadcasted_iota(jnp.int32, (win, LANES), 1)
    halves = tc // LANES

    def token_rows(k8, t):
        return pl.ds(k8 * win * t_ + t, win, stride=t_)

    for k8 in range(nk // win):
        for h in range(halves):
            a = [u_ref[token_rows(k8, 8 * h + i), :] for i in range(8)]
            a = _piece_transpose(a, lane)
            for g in range(gb):
                vf_ref[g, k8 * win:(k8 + 1) * win, h * LANES:(h + 1) * LANES] = a[g]

    row = lax.broadcasted_iota(jnp.int32, (nk, p2), 0)

    def shift(x, d):
        return jnp.where(row < d, 0.0, pltpu.roll(x, d, axis=0))

    for g in range(gb):
        sc = jnp.dot(vf_ref[g].astype(BF16), w_ref[g], preferred_element_type=F32)
        xa, xb = shift(sc[:, :p2], 1), shift(sc[:, p2:], 1)
        cf = coef_ref[g]
        for lvl in range(3):
            pp, qa, qb = cf[3 * lvl:3 * lvl + 1], cf[3 * lvl + 1:3 * lvl + 2], cf[3 * lvl + 2:3 * lvl + 3]
            sa, sb = shift(xa, 1 << lvl), shift(xb, 1 << lvl)
            xa, xb = xa + pp * sa + qa * sb, xb + pp * sb + qb * sa
        ea_ref[g] = xa
        eb_ref[g] = xb

    cfs = [coef_ref[g] for g in range(gb)]
    zero = jnp.zeros((win, p2), F32)
    state = [(zero, zero)] * gb
    for j in range(nk // win):
        rs = slice(j * win, (j + 1) * win)
        for g in range(gb):
            pp, qa, qb = cfs[g][9:10], cfs[g][10:11], cfs[g][11:12]
            s_a, s_b = state[g]
            n_a = pp * s_a + qa * s_b + ea_ref[g, rs, :]
            n_b = pp * s_b + qb * s_a + eb_ref[g, rs, :]
            sp_ref[g, rs, :] = n_a
            state[g] = (n_a, n_b)

    for g in range(gb):
        y = jnp.dot(vf_ref[g].astype(BF16), m_ref[g], preferred_element_type=F32)
        zf_ref[g] = y + jnp.dot(sp_ref[g].astype(BF16), v_ref[g], preferred_element_type=F32)

    dvec = d_ref[...]
    for k8 in range(nk // win):
        for h in range(halves):
            a = [zf_ref[g, k8 * win:(k8 + 1) * win, h * LANES:(h + 1) * LANES] for g in range(gb)]
            a = _piece_transpose(a, lane)
            for i in range(8):
                rows_ = token_rows(k8, 8 * h + i)
                o_ref[rows_, :] = jax.nn.gelu(a[i] + dvec * u_ref[rows_, :])


def _s5_core(u, mats, d_skip, bsz, seq):
    kflat, w, v, coef = mats
    n, d = u.shape
    gb, tc, p2 = SSM_GB, SSM_CHUNK * SSM_GROUP, 2 * SSM_STATE
    nk = seq // SSM_CHUNK
    spec3 = lambda a, b: pl.BlockSpec((gb, a, b), lambda j, bb: (j, 0, 0))
    return pl.pallas_call(
        _s5_kernel,
        out_shape=jax.ShapeDtypeStruct((n, d), F32),
        grid=(SSM_GROUPS // gb, bsz),
        in_specs=[
            pl.BlockSpec((seq, LANES), lambda j, bb: (bb, j)),
            spec3(SSM_GROUP, tc), spec3(tc, 2 * p2), spec3(p2, tc), spec3(16, p2),
            pl.BlockSpec((1, LANES), lambda j, bb: (0, j)),
        ],
        out_specs=pl.BlockSpec((seq, LANES), lambda j, bb: (bb, j)),
        scratch_shapes=[pltpu.VMEM((gb, tc, tc), BF16),
                        pltpu.VMEM((gb, nk, tc), F32), pltpu.VMEM((gb, nk, tc), F32),
                        pltpu.VMEM((gb, nk, p2), F32), pltpu.VMEM((gb, nk, p2), F32),
                        pltpu.VMEM((gb, nk, p2), F32)],
        compiler_params=_cparams("arbitrary", "arbitrary"),
        name="s5_core",
    )(u, kflat, w, v, coef, d_skip.reshape(1, d))


def _glu_out_kernel(z_ref, w_ref, x_ref, o_ref):
    vg = jnp.dot(z_ref[...].astype(BF16), w_ref[...], preferred_element_type=F32)
    d = o_ref.shape[1]
    o_ref[...] = x_ref[...] + vg[:, :d] * jax.nn.sigmoid(vg[:, d:])


def _glu_out(z, w_bf16, x, tm):
    n, d = x.shape
    return pl.pallas_call(
        _glu_out_kernel,
        out_shape=jax.ShapeDtypeStruct((n, d), F32),
        grid=(n // tm,),
        in_specs=[
            pl.BlockSpec((tm, z.shape[1]), lambda i: (i, 0)),
            pl.BlockSpec(w_bf16.shape, lambda i: (0, 0)),
            pl.BlockSpec((tm, d), lambda i: (i, 0)),
        ],
        out_specs=pl.BlockSpec((tm, d), lambda i: (i, 0)),
        compiler_params=_cparams("parallel"),
        name="s5_glu_out",
    )(z, w_bf16, x)


def _proj_res_t_kernel(at_ref, w_ref, x_ref, o_ref):
    o_ref[...] = x_ref[...] + lax.dot_general(at_ref[...], w_ref[...], _TN, preferred_element_type=F32)


def _proj_res_t(a_t, w_bf16, x, tm):
    n, d = x.shape
    return pl.pallas_call(
        _proj_res_t_kernel,
        out_shape=jax.ShapeDtypeStruct((n, d), F32),
        grid=(n // tm,),
        in_specs=[
            pl.BlockSpec((a_t.shape[0], tm), lambda i: (0, i)),
            pl.BlockSpec(w_bf16.shape, lambda i: (0, 0)),
            pl.BlockSpec((tm, d), lambda i: (i, 0)),
        ],
        out_specs=pl.BlockSpec((tm, d), lambda i: (i, 0)),
        compiler_params=_cparams("parallel"),
        name="attn_out_proj",
    )(a_t, w_bf16, x)


def _qkv_kernel(x_ref, g_ref, wq_ref, wk_ref, wv_ref, qt_ref, k_ref, vt_ref):
    xn = _rmsnorm(x_ref[...], g_ref[...]).astype(BF16)
    c = (HEAD_DIM ** -0.5) * math.log2(math.e)
    qt_ref[...] = (lax.dot_general(wq_ref[...], xn, _NT, preferred_element_type=F32) * c).astype(BF16)
    k_ref[...] = jnp.dot(xn, wk_ref[...], preferred_element_type=F32).astype(BF16)
    vt_ref[...] = lax.dot_general(wv_ref[...], xn, _NT, preferred_element_type=F32).astype(BF16)


def _qkv_proj(x, g, w_qkv, tm):
    n, d = x.shape
    da = ATT_HEADS * HEAD_DIM
    wq_t = w_qkv[:, :da].T.astype(BF16)
    wk = w_qkv[:, da:2 * da].astype(BF16)
    wv_t = w_qkv[:, 2 * da:].T.astype(BF16)
    full = lambda shp: pl.BlockSpec(shp, lambda i: (0, 0))
    return pl.pallas_call(
        _qkv_kernel,
        out_shape=(jax.ShapeDtypeStruct((da, n), BF16), jax.ShapeDtypeStruct((n, da), BF16),
                   jax.ShapeDtypeStruct((da, n), BF16)),
        grid=(n // tm,),
        in_specs=[pl.BlockSpec((tm, d), lambda i: (i, 0)), full((1, d)),
                  full((da, d)), full((d, da)), full((da, d))],
        out_specs=(pl.BlockSpec((da, tm), lambda i: (0, i)), pl.BlockSpec((tm, da), lambda i: (i, 0)),
                   pl.BlockSpec((da, tm), lambda i: (0, i))),
        compiler_params=_cparams("parallel"),
        name="attn_qkv_proj",
    )(x, g.reshape(1, d), wq_t, wk, wv_t)


def _moba_kernel(qt_ref, k_ref, vt_ref, et_ref, ot_ref, *, nb):
    blk, dh = MOBA_BLOCK, HEAD_DIM
    kf = k_ref[...].astype(F32)
    kmean = jnp.mean(kf.reshape(nb, blk, dh), axis=1)
    brow = lax.broadcasted_iota(jnp.int32, (nb, blk), 0)
    r_ix = lax.broadcasted_iota(jnp.int32, (blk, blk), 0)
    c_ix = lax.broadcasted_iota(jnp.int32, (blk, blk), 1)
    zpad = jnp.zeros((2 * LANES - dh - nb, blk), BF16)
    for i in range(nb):
        qs = slice(i * blk, (i + 1) * blk)
        qt = qt_ref[:, qs]
        if i > MOBA_TOPK:
            gate = jnp.dot(kmean, qt.astype(F32), preferred_element_type=F32,
                           precision=lax.Precision.HIGHEST)
            gate = jnp.where(brow < i, gate, NEG_INF)
            sel = brow >= i
            for _ in range(MOBA_TOPK):
                gm = jnp.max(gate, axis=0, keepdims=True)
                first = jnp.min(jnp.where(gate == gm, brow, nb), axis=0, keepdims=True)
                pick = brow == first
                sel = jnp.logical_or(sel, pick)
                gate = jnp.where(pick, -jnp.inf, gate)
            bias = jnp.where(sel, 0.0, NEG_INF).astype(BF16)
        else:
            bias = jnp.zeros((nb, blk), BF16)
        q_aug = jnp.concatenate([qt, bias, zpad], axis=0)
        k_own = jnp.concatenate([k_ref[qs, :], et_ref[qs, :]], axis=1)
        s_own = jnp.dot(k_own, q_aug, preferred_element_type=F32)
        s_own = jnp.where(r_ix <= c_ix, s_own, NEG_INF)
        m = jnp.max(s_own, axis=0, keepdims=True)
        if i > 0:
            ps = slice(0, i * blk)
            k_past = jnp.concatenate([k_ref[ps, :], et_ref[ps, :]], axis=1)
            s_past = jnp.dot(k_past, q_aug, preferred_element_type=F32)
            m = jnp.maximum(m, jnp.max(s_past, axis=0, keepdims=True))
            p_past = jnp.exp2(s_past - m)
        p_own = jnp.exp2(s_own - m)
        l = jnp.sum(p_own, axis=0, keepdims=True)
        acc = jnp.dot(vt_ref[:, qs], p_own.astype(BF16), preferred_element_type=F32)
        if i > 0:
            l = l + jnp.sum(p_past, axis=0, keepdims=True)
            acc = acc + jnp.dot(vt_ref[:, ps], p_past.astype(BF16), preferred_element_type=F32)
        ot_ref[:, qs] = (acc / l).astype(ot_ref.dtype)


def _moba(q_t, k, v_t, bsz, seq):
    nb = seq // MOBA_BLOCK
    da = ATT_HEADS * HEAD_DIM
    e_t = ((jnp.arange(seq) // MOBA_BLOCK)[:, None] == jnp.arange(LANES)[None, :]).astype(BF16)
    tspec = pl.BlockSpec((HEAD_DIM, seq), lambda b, h: (h, b))
    return pl.pallas_call(
        functools.partial(_moba_kernel, nb=nb),
        out_shape=jax.ShapeDtypeStruct((da, bsz * seq), BF16),
        grid=(bsz, ATT_HEADS),
        in_specs=[tspec, pl.BlockSpec((seq, HEAD_DIM), lambda b, h: (b, h)), tspec,
                  pl.BlockSpec((seq, LANES), lambda b, h: (0, 0))],
        out_specs=tspec,
        compiler_params=_cparams("parallel", "parallel"),
        name="moba_attn",
    )(q_t, k, v_t, e_t)


def _router_kernel(h_ref, g_ref, w_ref, meta_ref, cnt_ref):
    xn = _rmsnorm(h_ref[...], g_ref[...])
    logits = jnp.dot(xn, w_ref[...], preferred_element_type=F32, precision=lax.Precision.HIGHEST)
    tm = logits.shape[0]
    lane = lax.broadcasted_iota(jnp.int32, (tm, LANES), 1)
    ninf = -jnp.inf
    lg = jnp.where(lane < N_GROUPS, logits, ninf)
    gm = jnp.max(lg, axis=-1, keepdims=True)
    g_idx = jnp.min(jnp.where(lg == gm, lane, LANES), axis=-1, keepdims=True)
    g_gate = 1.0 / jnp.sum(jnp.exp(lg - gm), axis=-1, keepdims=True)
    lo = N_GROUPS + EXPERTS_PER_GROUP * g_idx
    le = jnp.where((lane >= lo) & (lane < lo + EXPERTS_PER_GROUP), logits, ninf)
    m1 = jnp.max(le, axis=-1, keepdims=True)
    i1 = jnp.min(jnp.where(le == m1, lane, LANES), axis=-1, keepdims=True)
    le2 = jnp.where(lane == i1, ninf, le)
    m2 = jnp.max(le2, axis=-1, keepdims=True)
    i2 = jnp.min(jnp.where(le2 == m2, lane, LANES), axis=-1, keepdims=True)
    p2 = jnp.exp(m2 - m1)
    gate1 = g_gate / (1.0 + p2)
    gate2 = g_gate * p2 / (1.0 + p2)
    e1 = (i1 - N_GROUPS).astype(F32)
    e2 = (i2 - N_GROUPS).astype(F32)
    meta = jnp.where(lane == 0, e1, jnp.where(lane == 1, e2, jnp.where(lane == 2, gate1,
                     jnp.where(lane == 3, gate2, 0.0))))
    meta_ref[...] = meta
    onehot = ((lane == i1 - N_GROUPS) | (lane == i2 - N_GROUPS)).astype(F32)

    @pl.when(pl.program_id(0) == 0)
    def _():
        cnt_ref[...] = jnp.zeros_like(cnt_ref)

    cnt_ref[...] += jnp.sum(onehot, axis=0, keepdims=True)


def _router(h, g, w_r, tm):
    n, d = h.shape
    return pl.pallas_call(
        _router_kernel,
        out_shape=(jax.ShapeDtypeStruct((n, LANES), F32), jax.ShapeDtypeStruct((1, LANES), F32)),
        grid=(n // tm,),
        in_specs=[
            pl.BlockSpec((tm, d), lambda i: (i, 0)),
            pl.BlockSpec((1, d), lambda i: (0, 0)),
            pl.BlockSpec((d, LANES), lambda i: (0, 0)),
        ],
        out_specs=(pl.BlockSpec((tm, LANES), lambda i: (i, 0)),
                   pl.BlockSpec((1, LANES), lambda i: (0, 0))),
        compiler_params=_cparams("arbitrary"),
        name="moe_router",
    )(h, g.reshape(1, d), w_r)


def _plan_kernel(meta_ref, pstart_ref, tri_ref, dest_ref, base_ref):
    @pl.when(pl.program_id(0) == 0)
    def _():
        base_ref[...] = jnp.zeros_like(base_ref)

    meta = meta_ref[...]
    tm = meta.shape[0]
    lane = lax.broadcasted_iota(jnp.int32, (tm, LANES), 1)
    e1 = meta[:, 0:1].astype(jnp.int32)
    e2 = meta[:, 1:2].astype(jnp.int32)
    oh1 = (lane == e1).astype(F32)
    oh2 = (lane == e2).astype(F32)
    both = oh1 + oh2
    cnt = jnp.dot(tri_ref[...], both.astype(BF16), preferred_element_type=F32) + base_ref[...]
    pos = cnt + pstart_ref[...]
    d1 = jnp.sum(pos * oh1, axis=-1, keepdims=True)
    d2 = jnp.sum(pos * oh2, axis=-1, keepdims=True)
    slab = jnp.where(lane == 0, d1, jnp.where(lane == 1, d2, 0.0))
    dest_ref[...] = jnp.transpose(slab)[0:8, :].astype(jnp.int32)
    base_ref[...] += jnp.sum(both, axis=0, keepdims=True)


def _plan(meta, pstart, tm):
    n = meta.shape[0]
    tri = (jnp.arange(tm)[:, None] > jnp.arange(tm)[None, :]).astype(BF16)
    return pl.pallas_call(
        _plan_kernel,
        out_shape=jax.ShapeDtypeStruct((8, n), jnp.int32),
        grid=(n // tm,),
        in_specs=[
            pl.BlockSpec((tm, LANES), lambda i: (i, 0)),
            pl.BlockSpec((1, LANES), lambda i: (0, 0)),
            pl.BlockSpec((tm, tm), lambda i: (0, 0)),
        ],
        out_specs=pl.BlockSpec((8, tm), lambda i: (0, i)),
        scratch_shapes=[pltpu.VMEM((1, LANES), F32)],
        compiler_params=_cparams("arbitrary"),
        name="moe_plan",
    )(meta, pstart, tri)


def _dispatch_kernel(d1_ref, d2_ref, h_ref, g_ref, xs_ref, buf_ref, sem):
    tm = h_ref.shape[0]
    base = pl.program_id(0) * tm
    buf_ref[...] = _rmsnorm(h_ref[...], g_ref[...])

    def row_copy(r, dst_row):
        return pltpu.make_async_copy(buf_ref.at[pl.ds(r, 1), :], xs_ref.at[pl.ds(dst_row, 1), :], sem)

    def issue(r, c):
        row_copy(r, d1_ref[base + r]).start()
        row_copy(r, d2_ref[base + r]).start()
        return c

    lax.fori_loop(0, tm, issue, 0)

    def drain(r, c):
        row_copy(r, 0).wait()
        row_copy(r, 0).wait()
        return c

    lax.fori_loop(0, tm, drain, 0)


def _dispatch(d1, d2, h, g, cap, tm):
    n, d = h.shape
    return pl.pallas_call(
        _dispatch_kernel,
        out_shape=jax.ShapeDtypeStruct((cap, d), F32),
        grid_spec=pltpu.PrefetchScalarGridSpec(
            num_scalar_prefetch=2,
            grid=(n // tm,),
            in_specs=[
                pl.BlockSpec((tm, d), lambda i, *_: (i, 0)),
                pl.BlockSpec((1, d), lambda i, *_: (0, 0)),
            ],
            out_specs=pl.BlockSpec(memory_space=pl.ANY),
            scratch_shapes=[pltpu.VMEM((tm, d), F32), pltpu.SemaphoreType.DMA],
        ),
        compiler_params=_cparams("arbitrary"),
        name="moe_dispatch",
    )(d1, d2, h, g.reshape(1, d))


def _expert_kernel(be_ref, bv_ref, nu_ref, x_ref, w1_ref, w3_ref, w2_ref, y_ref, w1c, w3c, w2c):
    i = pl.program_id(0)
    used = i < nu_ref[0]
    prev = be_ref[jnp.maximum(i - 1, 0)]
    fresh = jnp.logical_or(i == 0, be_ref[i] != prev)

    @pl.when(jnp.logical_and(used, fresh))
    def _():
        w1c[...] = w1_ref[0, 0].astype(BF16)
        w3c[...] = w3_ref[0, 0].astype(BF16)
        w2c[...] = w2_ref[0, 0].astype(BF16)

    @pl.when(used)
    def _():
        rows = x_ref.shape[0]
        r_ix = lax.broadcasted_iota(jnp.int32, (rows, 1), 0)
        x = jnp.where(r_ix < bv_ref[i], x_ref[...], 0.0).astype(BF16)
        a = jnp.dot(x, w1c[...], preferred_element_type=F32)
        b = jnp.dot(x, w3c[...], preferred_element_type=F32)
        act = (jax.nn.silu(a) * b).astype(BF16)
        y_ref[...] = jnp.dot(act, w2c[...], preferred_element_type=F32)


def _experts(blk_e, blk_valid, n_used, xs, w1, w3, w2, layer):
    cap, d = xs.shape
    rb = MOE_ROWS
    n_blk = cap // rb
    ff = w1.shape[3]
    row_map = lambda i, be, bv, nu: (jnp.minimum(i, nu[0] - 1), 0)
    return pl.pallas_call(
        _expert_kernel,
        out_shape=jax.ShapeDtypeStruct((cap, d), F32),
        grid_spec=pltpu.PrefetchScalarGridSpec(
            num_scalar_prefetch=3,
            grid=(n_blk,),
            in_specs=[
                pl.BlockSpec((rb, d), row_map),
                pl.BlockSpec((1, 1, d, ff), lambda i, be, bv, nu: (layer, be[i], 0, 0)),
                pl.BlockSpec((1, 1, d, ff), lambda i, be, bv, nu: (layer, be[i], 0, 0)),
                pl.BlockSpec((1, 1, ff, d), lambda i, be, bv, nu: (layer, be[i], 0, 0)),
            ],
            out_specs=pl.BlockSpec((rb, d), row_map),
            scratch_shapes=[pltpu.VMEM((d, ff), BF16), pltpu.VMEM((d, ff), BF16),
                            pltpu.VMEM((ff, d), BF16)],
        ),
        compiler_params=_cparams("arbitrary"),
        name="moe_experts",
    )(blk_e, blk_valid, n_used, xs, w1, w3, w2)


def _combine_kernel(d1_ref, d2_ref, h_ref, meta_ref, g_ref, ys_ref, o_ref, buf_ref, sem, *, final_norm):
    tm = h_ref.shape[0]
    base = pl.program_id(0) * tm

    def row_copy(slot, r, src_row):
        return pltpu.make_async_copy(ys_ref.at[pl.ds(src_row, 1), :], buf_ref.at[slot, pl.ds(r, 1), :], sem)

    def issue(r, c):
        row_copy(0, r, d1_ref[base + r]).start()
        row_copy(1, r, d2_ref[base + r]).start()
        return c

    lax.fori_loop(0, tm, issue, 0)

    def drain(r, c):
        row_copy(0, r, 0).wait()
        row_copy(1, r, 0).wait()
        return c

    lax.fori_loop(0, tm, drain, 0)
    meta = meta_ref[...]
    out = h_ref[...] + (meta[:, 2:3] * buf_ref[0] + meta[:, 3:4] * buf_ref[1])
    if final_norm:
        out = _rmsnorm(out, g_ref[...])
    o_ref[...] = out


def _combine(d1, d2, h, meta, ys, g_final, tm, final_norm):
    n, d = h.shape
    return pl.pallas_call(
        functools.partial(_combine_kernel, final_norm=final_norm),
        out_shape=jax.ShapeDtypeStruct((n, d), F32),
        grid_spec=pltpu.PrefetchScalarGridSpec(
            num_scalar_prefetch=2,
            grid=(n // tm,),
            in_specs=[
                pl.BlockSpec((tm, d), lambda i, *_: (i, 0)),
                pl.BlockSpec((tm, LANES), lambda i, *_: (i, 0)),
                pl.BlockSpec((1, d), lambda i, *_: (0, 0)),
                pl.BlockSpec(memory_space=pl.ANY),
            ],
            out_specs=pl.BlockSpec((tm, d), lambda i, *_: (i, 0)),
            scratch_shapes=[pltpu.VMEM((2, tm, d), F32), pltpu.SemaphoreType.DMA],
        ),
        compiler_params=_cparams("arbitrary"),
        name="moe_combine",
    )(d1, d2, h, meta, g_final.reshape(1, d), ys)


def _hier_moe(h, ln_g, w_group, w_expert, w1, w3, w2, layer, g_final, final_norm):
    n, d = h.shape
    rb = MOE_ROWS
    cap = 2 * n + N_EXPERTS * rb
    n_blk = cap // rb
    w_r = jnp.zeros((d, LANES), F32).at[:, :N_GROUPS].set(w_group)
    w_r = w_r.at[:, N_GROUPS:N_GROUPS + N_EXPERTS].set(w_expert)
    meta, counts = _router(h, ln_g, w_r, 512)
    cnt = counts[0, :N_EXPERTS].astype(jnp.int32)
    padded = ((cnt + rb - 1) // rb) * rb
    pad_end = jnp.cumsum(padded)
    pad_start = pad_end - padded
    pstart = jnp.zeros((1, LANES), F32).at[0, :N_EXPERTS].set(pad_start.astype(F32))
    blk_start = jnp.arange(n_blk, dtype=jnp.int32) * rb
    blk_e = jnp.minimum(jnp.sum(pad_end[None, :] <= blk_start[:, None], axis=1), N_EXPERTS - 1).astype(jnp.int32)
    blk_valid = jnp.clip(pad_start[blk_e] + cnt[blk_e] - blk_start, 0, rb).astype(jnp.int32)
    n_used = (pad_end[-1] // rb).astype(jnp.int32).reshape(1)
    dest = _plan(meta, pstart, 256)
    d1, d2 = dest[0], dest[1]
    xs = _dispatch(d1, d2, h, ln_g, cap, 256)
    ys = _experts(blk_e, blk_valid, n_used, xs, w1, w3, w2, layer)
    return _combine(d1, d2, h, meta, ys, g_final, 256, final_norm)


def kernel(x, ln_mix, ln_ffn, ln_final, ssm_w_in, ssm_lam_re, ssm_lam_im, ssm_log_dt, ssm_b_re, ssm_b_im, ssm_c_re, ssm_c_im, ssm_d, ssm_w_out, attn_w_qkv, attn_w_o, moe_w_group, moe_w_expert, moe_w1, moe_w3, moe_w2):
    bsz, seq, d = x.shape
    n = bsz * seq
    h = x.reshape(n, d)

    u = _norm_proj(h, ln_mix[0], ssm_w_in[0].astype(BF16), 512, F32, "s5_in_proj")
    mats = _s5_mats(ssm_lam_re[0], ssm_lam_im[0], ssm_log_dt[0], ssm_b_re[0], ssm_b_im[0],
                    ssm_c_re[0], ssm_c_im[0])
    z = _s5_core(u, mats, ssm_d[0], bsz, seq)
    h = _glu_out(z, ssm_w_out[0].astype(BF16), h, 512)
    h = _hier_moe(h, ln_ffn[0], moe_w_group[0], moe_w_expert[0], moe_w1, moe_w3, moe_w2, 0,
                  ln_final, False)

    q_t, k, v_t = _qkv_proj(h, ln_mix[1], attn_w_qkv[0], 512)
    o_t = _moba(q_t, k, v_t, bsz, seq)
    h = _proj_res_t(o_t, attn_w_o[0].astype(BF16), h, 512)
    h = _hier_moe(h, ln_ffn[1], moe_w_group[1], moe_w_expert[1], moe_w1, moe_w3, moe_w2, 1,
                  ln_final, True)
    return h.reshape(bsz, seq, d)
```

```python
import functools
import math

import jax
import jax.numpy as jnp
from jax import lax
from jax.experimental import pallas as pl
from jax.experimental.pallas import tpu as pltpu

F32 = jnp.float32
BF16 = jnp.bfloat16

D_MODEL = 1024
RMS_EPS = 1e-6
NEG_INF = -1e30

SSM_GROUP = 16
SSM_GROUPS = D_MODEL // SSM_GROUP
SSM_STATE = 64
SSM_CHUNK = 16
SSM_GB = 8
SSM_WIN = 8

ATT_HEADS = 8
HEAD_DIM = 128
MOBA_BLOCK = 256
MOBA_TOPK = 3

N_GROUPS = 4
EXPERTS_PER_GROUP = 8
N_EXPERTS = 32
EXPERT_FF = 512
MOE_ROWS = 256

LANES = 128
VMEM_LIMIT = 48 * 1024 * 1024

_NT = (((1,), (1,)), ((), ()))
_TN = (((0,), (0,)), ((), ()))


def _cparams(*sem):
    return pltpu.CompilerParams(dimension_semantics=sem, vmem_limit_bytes=VMEM_LIMIT)


def _rmsnorm(x, g):
    return x * lax.rsqrt(jnp.mean(x * x, axis=-1, keepdims=True) + RMS_EPS) * g


def _norm_proj_kernel(x_ref, g_ref, w_ref, o_ref):
    xn = _rmsnorm(x_ref[...], g_ref[...]).astype(BF16)
    o_ref[...] = jnp.dot(xn, w_ref[...], preferred_element_type=F32).astype(o_ref.dtype)


def _norm_proj(x, g, w_bf16, tm, out_dtype, name):
    n, d = x.shape
    dout = w_bf16.shape[1]
    return pl.pallas_call(
        _norm_proj_kernel,
        out_shape=jax.ShapeDtypeStruct((n, dout), out_dtype),
        grid=(n // tm,),
        in_specs=[
            pl.BlockSpec((tm, d), lambda i: (i, 0)),
            pl.BlockSpec((1, d), lambda i: (0, 0)),
            pl.BlockSpec((d, dout), lambda i: (0, 0)),
        ],
        out_specs=pl.BlockSpec((tm, dout), lambda i: (i, 0)),
        compiler_params=_cparams("parallel"),
        name=name,
    )(x, g.reshape(1, d), w_bf16)


def _s5_mats(lam_re, lam_im, log_dt, b_re, b_im, c_re, c_im):
    g_, p_, c_, t_ = SSM_GROUPS, SSM_STATE, SSM_GROUP, SSM_CHUNK
    lr = jnp.minimum(lam_re, -1e-4)
    li = lam_im
    dt = jnp.exp(log_dt)[:, None]
    mag = jnp.exp(lr * dt)
    abar_re = mag * jnp.cos(li * dt)
    abar_im = mag * jnp.sin(li * dt)
    den = lr * lr + li * li
    nr = abar_re - 1.0
    gam_re = (nr * lr + abar_im * li) / den
    gam_im = (abar_im * lr - nr * li) / den
    bb_re = gam_re[..., None] * b_re - gam_im[..., None] * b_im
    bb_im = gam_re[..., None] * b_im + gam_im[..., None] * b_re

    def powers(ns):
        nf = jnp.asarray(ns, F32)[None, :, None]
        pm = jnp.exp(nf * (lr * dt)[:, None, :])
        ang = nf * (li * dt)[:, None, :]
        return pm * jnp.cos(ang), pm * jnp.sin(ang)

    pr, pi = powers(list(range(t_ + 1)))
    ca_re = c_re[:, None] * pr[:, :, None, :] - c_im[:, None] * pi[:, :, None, :]
    ca_im = c_re[:, None] * pi[:, :, None, :] + c_im[:, None] * pr[:, :, None, :]
    ca_n = jnp.concatenate([ca_re[:, :t_], ca_im[:, :t_]], axis=-1).reshape(g_, t_ * c_, 2 * p_)
    bb_s = jnp.concatenate([bb_re, -bb_im], axis=1)
    kflat = jnp.einsum('gxp,gpc->gcx', ca_n, bb_s, precision='highest')
    prs = pr[:, :t_][:, ::-1][:, :, None, :]
    pis = pi[:, :t_][:, ::-1][:, :, None, :]
    bt_re = bb_re.transpose(0, 2, 1)[:, None]
    bt_im = bb_im.transpose(0, 2, 1)[:, None]
    w_re = prs * bt_re - pis * bt_im
    w_im = prs * bt_im + pis * bt_re
    w = jnp.concatenate([w_re, w_im, w_im, w_re], axis=-1).reshape(g_, t_ * c_, 4 * p_)
    prt = pr.transpose(0, 2, 1)[:, :, 1:, None]
    pit = pi.transpose(0, 2, 1)[:, :, 1:, None]
    ct_re = c_re.transpose(0, 2, 1)[:, :, None, :]
    ct_im = c_im.transpose(0, 2, 1)[:, :, None, :]
    v_re = (ct_re * prt - ct_im * pit).reshape(g_, p_, t_ * c_)
    v_im = -(ct_re * pit + ct_im * prt).reshape(g_, p_, t_ * c_)
    v = jnp.concatenate([v_re, v_im], axis=1)
    qr, qi = powers([t_ * (1 << j) for j in range(4)])
    rows = []
    for j in range(4):
        ar, ai = qr[:, j], qi[:, j]
        rows += [jnp.concatenate([ar, ar], -1), jnp.concatenate([-ai, ai], -1),
                 jnp.concatenate([ai, -ai], -1)]
    rows += [jnp.zeros_like(rows[0])] * 4
    coef = jnp.stack(rows, axis=1)
    return kflat.astype(F32), w.astype(BF16), v.astype(BF16), coef.astype(F32)


def _s5_perm():
    r = jnp.arange(8 * LANES)
    col = ((r % LANES) // SSM_GROUP) * LANES + (r // LANES) * SSM_GROUP + r % SSM_GROUP
    p = (col[:, None] == r[None, :]).astype(BF16)
    return p, p.T


def _s5_kernel(u_ref, kf_ref, w_ref, v_ref, coef_ref, d_ref, p_ref, pt_ref, o_ref,
               m_ref, vf_ref, zf_ref, ea_ref, eb_ref, sp_ref):
    t_, c_, gb, win = SSM_CHUNK, SSM_GROUP, SSM_GB, SSM_WIN
    nk = u_ref.shape[0] // t_
    p2 = 2 * SSM_STATE
    tc = t_ * c_

    @pl.when(pl.program_id(1) == 0)
    def _():
        lane_tc = lax.broadcasted_iota(jnp.int32, (c_, tc), 1)
        for g in range(gb):
            kf = kf_ref[g]
            for s in range(t_):
                rolled = kf if s == 0 else pltpu.roll(kf, s * c_, axis=1)
                m_ref[g, s * c_:(s + 1) * c_, :] = jnp.where(lane_tc >= s * c_, rolled, 0.0).astype(BF16)

    halves = tc // LANES

    def timestep(t):
        return pl.ds(t, nk, stride=t_)

    def flat(g):
        return jnp.concatenate([vf_ref[h, :, g * LANES:(g + 1) * LANES] for h in range(halves)], axis=1)

    for h in range(halves):
        x = jnp.concatenate([u_ref[timestep(8 * h + i), :].astype(BF16) for i in range(8)], axis=1)
        vf_ref[h] = jnp.dot(x, p_ref[...], preferred_element_type=F32).astype(BF16)

    row = lax.broadcasted_iota(jnp.int32, (nk, p2), 0)

    def shift(x, d):
        return jnp.where(row < d, 0.0, pltpu.roll(x, d, axis=0))

    for g in range(gb):
        sc = jnp.dot(flat(g), w_ref[g], preferred_element_type=F32)
        xa, xb = shift(sc[:, :p2], 1), shift(sc[:, p2:], 1)
        cf = coef_ref[g]
        for lvl in range(3):
            pp, qa, qb = cf[3 * lvl:3 * lvl + 1], cf[3 * lvl + 1:3 * lvl + 2], cf[3 * lvl + 2:3 * lvl + 3]
            sa, sb = shift(xa, 1 << lvl), shift(xb, 1 << lvl)
            xa, xb = xa + pp * sa + qa * sb, xb + pp * sb + qb * sa
        ea_ref[g] = xa
        eb_ref[g] = xb

    cfs = [coef_ref[g] for g in range(gb)]
    zero = jnp.zeros((win, p2), F32)
    state = [(zero, zero)] * gb
    for j in range(nk // win):
        rs = slice(j * win, (j + 1) * win)
        for g in range(gb):
            pp, qa, qb = cfs[g][9:10], cfs[g][10:11], cfs[g][11:12]
            s_a, s_b = state[g]
            n_a = pp * s_a + qa * s_b + ea_ref[g, rs, :]
            n_b = pp * s_b + qb * s_a + eb_ref[g, rs, :]
            sp_ref[g, rs, :] = n_a
            state[g] = (n_a, n_b)

    for g in range(gb):
        x = flat(g)
        y = jnp.dot(x, m_ref[g], preferred_element_type=F32)
        y = y + jnp.dot(sp_ref[g].astype(BF16), v_ref[g], preferred_element_type=F32)
        z = jax.nn.gelu(y + d_ref[g] * x.astype(F32)).astype(BF16)
        for h in range(halves):
            zf_ref[h, :, g * LANES:(g + 1) * LANES] = z[:, h * LANES:(h + 1) * LANES]

    for h in range(halves):
        zn = jnp.dot(zf_ref[h], pt_ref[...], preferred_element_type=F32)
        for i in range(8):
            o_ref[timestep(8 * h + i), :] = zn[:, i * LANES:(i + 1) * LANES]


def _s5_core(u, mats, d_skip, bsz, seq):
    kflat, w, v, coef = mats
    n, d = u.shape
    gb, tc, p2 = SSM_GB, SSM_CHUNK * SSM_GROUP, 2 * SSM_STATE
    nk = seq // SSM_CHUNK
    halves = tc // LANES
    perm, perm_t = _s5_perm()
    dflat = jnp.tile(d_skip.reshape(SSM_GROUPS, 1, SSM_GROUP), (1, 1, SSM_CHUNK))
    spec3 = lambda a, b: pl.BlockSpec((gb, a, b), lambda j, bb: (j, 0, 0))
    const = pl.BlockSpec(perm.shape, lambda j, bb: (0, 0))
    return pl.pallas_call(
        _s5_kernel,
        out_shape=jax.ShapeDtypeStruct((n, d), F32),
        grid=(SSM_GROUPS // gb, bsz),
        in_specs=[
            pl.BlockSpec((seq, LANES), lambda j, bb: (bb, j)),
            spec3(SSM_GROUP, tc), spec3(tc, 2 * p2), spec3(p2, tc), spec3(16, p2), spec3(1, tc),
            const, const,
        ],
        out_specs=pl.BlockSpec((seq, LANES), lambda j, bb: (bb, j)),
        scratch_shapes=[pltpu.VMEM((gb, tc, tc), BF16),
                        pltpu.VMEM((halves, nk, gb * LANES), BF16),
                        pltpu.VMEM((halves, nk, gb * LANES), BF16),
                        pltpu.VMEM((gb, nk, p2), F32), pltpu.VMEM((gb, nk, p2), F32),
                        pltpu.VMEM((gb, nk, p2), F32)],
        compiler_params=_cparams("arbitrary", "arbitrary"),
        name="s5_core",
    )(u, kflat, w, v, coef, dflat, perm, perm_t)


def _glu_out_kernel(z_ref, w_ref, x_ref, o_ref):
    vg = jnp.dot(z_ref[...].astype(BF16), w_ref[...], preferred_element_type=F32)
    d = o_ref.shape[1]
    o_ref[...] = x_ref[...] + vg[:, :d] * jax.nn.sigmoid(vg[:, d:])


def _glu_out(z, w_bf16, x, tm):
    n, d = x.shape
    return pl.pallas_call(
        _glu_out_kernel,
        out_shape=jax.ShapeDtypeStruct((n, d), F32),
        grid=(n // tm,),
        in_specs=[
            pl.BlockSpec((tm, z.shape[1]), lambda i: (i, 0)),
            pl.BlockSpec(w_bf16.shape, lambda i: (0, 0)),
            pl.BlockSpec((tm, d), lambda i: (i, 0)),
        ],
        out_specs=pl.BlockSpec((tm, d), lambda i: (i, 0)),
        compiler_params=_cparams("parallel"),
        name="s5_glu_out",
    )(z, w_bf16, x)


def _proj_res_t_kernel(at_ref, w_ref, x_ref, o_ref):
    o_ref[...] = x_ref[...] + lax.dot_general(at_ref[...], w_ref[...], _TN, preferred_element_type=F32)


def _proj_res_t(a_t, w_bf16, x, tm):
    n, d = x.shape
    return pl.pallas_call(
        _proj_res_t_kernel,
        out_shape=jax.ShapeDtypeStruct((n, d), F32),
        grid=(n // tm,),
        in_specs=[
            pl.BlockSpec((a_t.shape[0], tm), lambda i: (0, i)),
            pl.BlockSpec(w_bf16.shape, lambda i: (0, 0)),
            pl.BlockSpec((tm, d), lambda i: (i, 0)),
        ],
        out_specs=pl.BlockSpec((tm, d), lambda i: (i, 0)),
        compiler_params=_cparams("parallel"),
        name="attn_out_proj",
    )(a_t, w_bf16, x)


def _qkv_kernel(x_ref, g_ref, wq_ref, wk_ref, wv_ref, qt_ref, k_ref, vt_ref):
    xn = _rmsnorm(x_ref[...], g_ref[...]).astype(BF16)
    c = (HEAD_DIM ** -0.5) * math.log2(math.e)
    qt_ref[...] = (lax.dot_general(wq_ref[...], xn, _NT, preferred_element_type=F32) * c).astype(BF16)
    k_ref[...] = jnp.dot(xn, wk_ref[...], preferred_element_type=F32).astype(BF16)
    vt_ref[...] = lax.dot_general(wv_ref[...], xn, _NT, preferred_element_type=F32).astype(BF16)


def _qkv_proj(x, g, w_qkv, tm):
    n, d = x.shape
    da = ATT_HEADS * HEAD_DIM
    wq_t = w_qkv[:, :da].T.astype(BF16)
    wk = w_qkv[:, da:2 * da].astype(BF16)
    wv_t = w_qkv[:, 2 * da:].T.astype(BF16)
    full = lambda shp: pl.BlockSpec(shp, lambda i: (0, 0))
    return pl.pallas_call(
        _qkv_kernel,
        out_shape=(jax.ShapeDtypeStruct((da, n), BF16), jax.ShapeDtypeStruct((n, da), BF16),
                   jax.ShapeDtypeStruct((da, n), BF16)),
        grid=(n // tm,),
        in_specs=[pl.BlockSpec((tm, d), lambda i: (i, 0)), full((1, d)),
                  full((da, d)), full((d, da)), full((da, d))],
        out_specs=(pl.BlockSpec((da, tm), lambda i: (0, i)), pl.BlockSpec((tm, da), lambda i: (i, 0)),
                   pl.BlockSpec((da, tm), lambda i: (0, i))),
        compiler_params=_cparams("parallel"),
        name="attn_qkv_proj",
    )(x, g.reshape(1, d), wq_t, wk, wv_t)


def _moba_kernel(qt_ref, k_ref, vt_ref, et_ref, ot_ref, *, nb):
    blk, dh = MOBA_BLOCK, HEAD_DIM
    kf = k_ref[...].astype(F32)
    kmean = jnp.mean(kf.reshape(nb, blk, dh), axis=1)
    brow = lax.broadcasted_iota(jnp.int32, (nb, blk), 0)
    r_ix = lax.broadcasted_iota(jnp.int32, (blk, blk), 0)
    c_ix = lax.broadcasted_iota(jnp.int32, (blk, blk), 1)
    zpad = jnp.zeros((2 * LANES - dh - nb, blk), BF16)
    for i in range(nb):
        qs = slice(i * blk, (i + 1) * blk)
        qt = qt_ref[:, qs]
        if i > MOBA_TOPK:
            gate = jnp.dot(kmean, qt.astype(F32), preferred_element_type=F32,
                           precision=lax.Precision.HIGHEST)
            gate = jnp.where(brow < i, gate, NEG_INF)
            sel = brow >= i
            for _ in range(MOBA_TOPK):
                gm = jnp.max(gate, axis=0, keepdims=True)
                first = jnp.min(jnp.where(gate == gm, brow, nb), axis=0, keepdims=True)
                pick = brow == first
                sel = jnp.logical_or(sel, pick)
                gate = jnp.where(pick, -jnp.inf, gate)
            bias = jnp.where(sel, 0.0, NEG_INF).astype(BF16)
        else:
            bias = jnp.zeros((nb, blk), BF16)
        q_aug = jnp.concatenate([qt, bias, zpad], axis=0)
        k_own = jnp.concatenate([k_ref[qs, :], et_ref[qs, :]], axis=1)
        s_own = jnp.dot(k_own, q_aug, preferred_element_type=F32)
        s_own = jnp.where(r_ix <= c_ix, s_own, NEG_INF)
        m = jnp.max(s_own, axis=0, keepdims=True)
        if i > 0:
            ps = slice(0, i * blk)
            k_past = jnp.concatenate([k_ref[ps, :], et_ref[ps, :]], axis=1)
            s_past = jnp.dot(k_past, q_aug, preferred_element_type=F32)
            m = jnp.maximum(m, jnp.max(s_past, axis=0, keepdims=True))
            p_past = jnp.exp2(s_past - m)
        p_own = jnp.exp2(s_own - m)
        l = jnp.sum(p_own, axis=0, keepdims=True)
        acc = jnp.dot(vt_ref[:, qs], p_own.astype(BF16), preferred_element_type=F32)
        if i > 0:
            l = l + jnp.sum(p_past, axis=0, keepdims=True)
            acc = acc + jnp.dot(vt_ref[:, ps], p_past.astype(BF16), preferred_element_type=F32)
        ot_ref[:, qs] = (acc / l).astype(ot_ref.dtype)


def _moba(q_t, k, v_t, bsz, seq):
    nb = seq // MOBA_BLOCK
    da = ATT_HEADS * HEAD_DIM
    e_t = ((jnp.arange(seq) // MOBA_BLOCK)[:, None] == jnp.arange(LANES)[None, :]).astype(BF16)
    tspec = pl.BlockSpec((HEAD_DIM, seq), lambda b, h: (h, b))
    return pl.pallas_call(
        functools.partial(_moba_kernel, nb=nb),
        out_shape=jax.ShapeDtypeStruct((da, bsz * seq), BF16),
        grid=(bsz, ATT_HEADS),
        in_specs=[tspec, pl.BlockSpec((seq, HEAD_DIM), lambda b, h: (b, h)), tspec,
                  pl.BlockSpec((seq, LANES), lambda b, h: (0, 0))],
        out_specs=tspec,
        compiler_params=_cparams("parallel", "parallel"),
        name="moba_attn",
    )(q_t, k, v_t, e_t)


def _pack_halves(x):
    k = x.shape[1] // 2
    lo = pltpu.bitcast(x[:, :k].astype(BF16).astype(F32), jnp.uint32)
    hi = pltpu.bitcast(x[:, k:].astype(BF16).astype(F32), jnp.uint32)
    return (lo >> 16) | (hi & jnp.uint32(0xFFFF0000))


def _unpack_halves(p):
    lo = pltpu.bitcast(p << 16, F32)
    hi = pltpu.bitcast(p & jnp.uint32(0xFFFF0000), F32)
    return lo, hi


def _router_kernel(h_ref, g_ref, w_ref, tri_ref, meta_ref, idx_ref, cnt_ref):
    xn = _rmsnorm(h_ref[...], g_ref[...])
    logits = jnp.dot(xn, w_ref[...], preferred_element_type=F32, precision=lax.Precision.HIGHEST)
    tm = logits.shape[0]
    lane = lax.broadcasted_iota(jnp.int32, (tm, LANES), 1)
    ninf = -jnp.inf
    lg = jnp.where(lane < N_GROUPS, logits, ninf)
    gm = jnp.max(lg, axis=-1, keepdims=True)
    g_idx = jnp.min(jnp.where(lg == gm, lane, LANES), axis=-1, keepdims=True)
    g_gate = 1.0 / jnp.sum(jnp.exp(lg - gm), axis=-1, keepdims=True)
    lo = N_GROUPS + EXPERTS_PER_GROUP * g_idx
    le = jnp.where((lane >= lo) & (lane < lo + EXPERTS_PER_GROUP), logits, ninf)
    m1 = jnp.max(le, axis=-1, keepdims=True)
    i1 = jnp.min(jnp.where(le == m1, lane, LANES), axis=-1, keepdims=True)
    le2 = jnp.where(lane == i1, ninf, le)
    m2 = jnp.max(le2, axis=-1, keepdims=True)
    i2 = jnp.min(jnp.where(le2 == m2, lane, LANES), axis=-1, keepdims=True)
    p2 = jnp.exp(m2 - m1)
    gate1 = g_gate / (1.0 + p2)
    gate2 = g_gate * p2 / (1.0 + p2)
    e1 = (i1 - N_GROUPS).astype(F32)
    e2 = (i2 - N_GROUPS).astype(F32)
    meta = jnp.where(lane == 0, e1, jnp.where(lane == 1, e2, jnp.where(lane == 2, gate1,
                     jnp.where(lane == 3, gate2, 0.0))))
    meta_ref[...] = meta
    oh1 = (lane == i1 - N_GROUPS).astype(F32)
    oh2 = (lane == i2 - N_GROUPS).astype(F32)
    both = oh1 + oh2

    @pl.when(pl.program_id(0) == 0)
    def _():
        cnt_ref[...] = jnp.zeros_like(cnt_ref)

    seen = cnt_ref[...]
    before = jnp.dot(tri_ref[...], both.astype(BF16), preferred_element_type=F32) + seen
    r1 = jnp.sum(before * oh1, axis=-1, keepdims=True)
    r2 = jnp.sum(before * oh2, axis=-1, keepdims=True)
    slab = jnp.where(lane == 0, e1, jnp.where(lane == 1, e2, jnp.where(lane == 2, r1,
                     jnp.where(lane == 3, r2, 0.0))))
    idx_ref[...] = jnp.transpose(slab)[0:8, :].astype(jnp.int32)
    cnt_ref[...] = seen + jnp.sum(both, axis=0, keepdims=True)


def _router(h, g, w_r, tm):
    n, d = h.shape
    tri = (jnp.arange(tm)[:, None] > jnp.arange(tm)[None, :]).astype(BF16)
    return pl.pallas_call(
        _router_kernel,
        out_shape=(jax.ShapeDtypeStruct((n, LANES), F32), jax.ShapeDtypeStruct((8, n), jnp.int32),
                   jax.ShapeDtypeStruct((1, LANES), F32)),
        grid=(n // tm,),
        in_specs=[
            pl.BlockSpec((tm, d), lambda i: (i, 0)),
            pl.BlockSpec((1, d), lambda i: (0, 0)),
            pl.BlockSpec((d, LANES), lambda i: (0, 0)),
            pl.BlockSpec((tm, tm), lambda i: (0, 0)),
        ],
        out_specs=(pl.BlockSpec((tm, LANES), lambda i: (i, 0)),
                   pl.BlockSpec((8, tm), lambda i: (0, i)),
                   pl.BlockSpec((1, LANES), lambda i: (0, 0))),
        compiler_params=_cparams("arbitrary"),
        name="moe_router",
    )(h, g.reshape(1, d), w_r, tri)


MOE_ISSUE_UNROLL = 8


def _dispatch_kernel(d1_ref, d2_ref, plo_ref, pn_ref, nu_ref, h_ref, g_ref, xs_ref,
                     buf_ref, zero_ref, sem, fill_sem):
    i = pl.program_id(0)
    nsteps = pl.num_programs(0)
    tm = h_ref.shape[0]
    rb = zero_ref.shape[0]
    n_blk = xs_ref.shape[0] // rb
    slot = i % 2
    base = i * tm

    def slot_drain(s):
        cp = pltpu.make_async_copy(buf_ref.at[s], xs_ref.at[pl.ds(0, tm), :], sem.at[s])
        cp.wait()
        cp.wait()

    @pl.when(i >= 2)
    def _():
        slot_drain(slot)

    buf_ref[slot] = _pack_halves(_rmsnorm(h_ref[...], g_ref[...]))

    def issue(r8, c):
        for k in range(MOE_ISSUE_UNROLL):
            r = r8 * MOE_ISSUE_UNROLL + k
            src = buf_ref.at[slot, pl.ds(r, 1), :]
            pltpu.make_async_copy(src, xs_ref.at[pl.ds(d1_ref[base + r], 1), :], sem.at[slot]).start()
            pltpu.make_async_copy(src, xs_ref.at[pl.ds(d2_ref[base + r], 1), :], sem.at[slot]).start()
        return c

    lax.fori_loop(0, tm // MOE_ISSUE_UNROLL, issue, 0)

    def pad_copy(row):
        return pltpu.make_async_copy(zero_ref.at[pl.ds(0, 1), :], xs_ref.at[pl.ds(row, 1), :], fill_sem)

    def blk_copy(b):
        return pltpu.make_async_copy(zero_ref, xs_ref.at[pl.ds(b * rb, rb), :], fill_sem)

    @pl.when(i == 0)
    def _():
        zero_ref[...] = jnp.zeros_like(zero_ref)

    @pl.when(i < N_EXPERTS)
    def _():
        lo = plo_ref[i]

        def fill(r, c):
            pad_copy(lo + r).start()
            return c

        lax.fori_loop(0, pn_ref[i], fill, 0)

    @pl.when(i == N_EXPERTS)
    def _():
        def fill(b, c):
            blk_copy(b).start()
            return c

        lax.fori_loop(nu_ref[0], n_blk, fill, 0)

    @pl.when(i == nsteps - 1)
    def _():
        slot_drain(1 - slot)
        slot_drain(slot)

        def per_expert(e, c):
            def one(r, cc):
                pad_copy(0).wait()
                return cc

            return lax.fori_loop(0, pn_ref[e], one, c)

        lax.fori_loop(0, N_EXPERTS, per_expert, 0)

        def one_blk(b, c):
            blk_copy(0).wait()
            return c

        lax.fori_loop(nu_ref[0], n_blk, one_blk, 0)


def _dispatch(d1, d2, pad_lo, pad_n, n_used, h, g, cap, tm):
    n, d = h.shape
    assert n // tm > N_EXPERTS + 1
    return pl.pallas_call(
        _dispatch_kernel,
        out_shape=jax.ShapeDtypeStruct((cap, d // 2), jnp.uint32),
        grid_spec=pltpu.PrefetchScalarGridSpec(
            num_scalar_prefetch=5,
            grid=(n // tm,),
            in_specs=[
                pl.BlockSpec((tm, d), lambda i, *_: (i, 0)),
                pl.BlockSpec((1, d), lambda i, *_: (0, 0)),
            ],
            out_specs=pl.BlockSpec(memory_space=pl.ANY),
            scratch_shapes=[pltpu.VMEM((2, tm, d // 2), jnp.uint32),
                            pltpu.VMEM((MOE_ROWS, d // 2), jnp.uint32),
                            pltpu.SemaphoreType.DMA((2,)), pltpu.SemaphoreType.DMA],
        ),
        compiler_params=_cparams("arbitrary"),
        name="moe_dispatch",
    )(d1, d2, pad_lo, pad_n, n_used, h, g.reshape(1, d))


def _expert_kernel(be_ref, nu_ref, x_ref, w1_ref, w3_ref, w2_ref, y_ref, w1c, w3c, w2c):
    i = pl.program_id(0)
    used = i < nu_ref[0]
    prev = be_ref[jnp.maximum(i - 1, 0)]
    fresh = jnp.logical_or(i == 0, be_ref[i] != prev)

    @pl.when(jnp.logical_and(used, fresh))
    def _():
        w1c[...] = w1_ref[0, 0].astype(BF16)
        w3c[...] = w3_ref[0, 0].astype(BF16)
        w2c[...] = w2_ref[0, 0].astype(BF16)

    @pl.when(used)
    def _():
        lo, hi = _unpack_halves(x_ref[...])
        lo, hi = lo.astype(BF16), hi.astype(BF16)
        hd = lo.shape[1]
        a = (jnp.dot(lo, w1c[:hd, :], preferred_element_type=F32)
             + jnp.dot(hi, w1c[hd:, :], preferred_element_type=F32))
        b = (jnp.dot(lo, w3c[:hd, :], preferred_element_type=F32)
             + jnp.dot(hi, w3c[hd:, :], preferred_element_type=F32))
        act = (jax.nn.silu(a) * b).astype(BF16)
        y_ref[...] = _pack_halves(jnp.dot(act, w2c[...], preferred_element_type=F32))

    @pl.when(jnp.logical_not(used))
    def _():
        y_ref[...] = jnp.zeros_like(y_ref)


def _experts(blk_e, n_used, xs, w1, w3, w2, layer):
    cap, dp = xs.shape
    d = 2 * dp
    rb = MOE_ROWS
    n_blk = cap // rb
    ff = w1.shape[3]
    return pl.pallas_call(
        _expert_kernel,
        out_shape=jax.ShapeDtypeStruct((cap, dp), jnp.uint32),
        grid_spec=pltpu.PrefetchScalarGridSpec(
            num_scalar_prefetch=2,
            grid=(n_blk,),
            in_specs=[
                pl.BlockSpec((rb, dp), lambda i, be, nu: (jnp.minimum(i, jnp.maximum(nu[0] - 1, 0)), 0)),
                pl.BlockSpec((1, 1, d, ff), lambda i, be, nu: (layer, be[i], 0, 0)),
                pl.BlockSpec((1, 1, d, ff), lambda i, be, nu: (layer, be[i], 0, 0)),
                pl.BlockSpec((1, 1, ff, d), lambda i, be, nu: (layer, be[i], 0, 0)),
            ],
            out_specs=pl.BlockSpec((rb, dp), lambda i, be, nu: (i, 0)),
            scratch_shapes=[pltpu.VMEM((d, ff), BF16), pltpu.VMEM((d, ff), BF16),
                            pltpu.VMEM((ff, d), BF16)],
        ),
        compiler_params=_cparams("arbitrary"),
        name="moe_experts",
    )(blk_e, n_used, xs, w1, w3, w2)


def _combine_kernel(d1_ref, d2_ref, h_ref, meta_ref, g_ref, ys_ref, o_ref, buf_ref, sem, *, final_norm):
    i = pl.program_id(0)
    nsteps = pl.num_programs(0)
    tm = h_ref.shape[0]
    slot = i % 2

    def fetch(tile, s):
        base = tile * tm

        def issue(r8, c):
            for k in range(MOE_ISSUE_UNROLL):
                r = r8 * MOE_ISSUE_UNROLL + k
                pltpu.make_async_copy(ys_ref.at[pl.ds(d1_ref[base + r], 1), :],
                                      buf_ref.at[s, 0, pl.ds(r, 1), :], sem.at[s]).start()
                pltpu.make_async_copy(ys_ref.at[pl.ds(d2_ref[base + r], 1), :],
                                      buf_ref.at[s, 1, pl.ds(r, 1), :], sem.at[s]).start()
            return c

        lax.fori_loop(0, tm // MOE_ISSUE_UNROLL, issue, 0)

    @pl.when(i == 0)
    def _():
        fetch(0, 0)

    @pl.when(i + 1 < nsteps)
    def _():
        fetch(i + 1, 1 - slot)

    for j in range(2):
        pltpu.make_async_copy(ys_ref.at[pl.ds(0, tm), :], buf_ref.at[slot, j], sem.at[slot]).wait()

    meta = meta_ref[...]
    g1, g2 = meta[:, 2:3], meta[:, 3:4]
    lo1, hi1 = _unpack_halves(buf_ref[slot, 0])
    lo2, hi2 = _unpack_halves(buf_ref[slot, 1])
    hd = lo1.shape[1]
    out = jnp.concatenate([h_ref[:, :hd] + (g1 * lo1 + g2 * lo2),
                           h_ref[:, hd:] + (g1 * hi1 + g2 * hi2)], axis=1)
    if final_norm:
        out = _rmsnorm(out, g_ref[...])
    o_ref[...] = out


def _combine(d1, d2, h, meta, ys, g_final, tm, final_norm):
    n, d = h.shape
    return pl.pallas_call(
        functools.partial(_combine_kernel, final_norm=final_norm),
        out_shape=jax.ShapeDtypeStruct((n, d), F32),
        grid_spec=pltpu.PrefetchScalarGridSpec(
            num_scalar_prefetch=2,
            grid=(n // tm,),
            in_specs=[
                pl.BlockSpec((tm, d), lambda i, *_: (i, 0)),
                pl.BlockSpec((tm, LANES), lambda i, *_: (i, 0)),
                pl.BlockSpec((1, d), lambda i, *_: (0, 0)),
                pl.BlockSpec(memory_space=pl.ANY),
            ],
            out_specs=pl.BlockSpec((tm, d), lambda i, *_: (i, 0)),
            scratch_shapes=[pltpu.VMEM((2, 2, tm, d // 2), jnp.uint32), pltpu.SemaphoreType.DMA((2,))],
        ),
        compiler_params=_cparams("arbitrary"),
        name="moe_combine",
    )(d1, d2, h, meta, g_final.reshape(1, d), ys)


def _hier_moe(h, ln_g, w_group, w_expert, w1, w3, w2, layer, g_final, final_norm):
    n, d = h.shape
    rb = MOE_ROWS
    cap = 2 * n + N_EXPERTS * rb
    n_blk = cap // rb
    w_r = jnp.zeros((d, LANES), F32).at[:, :N_GROUPS].set(w_group)
    w_r = w_r.at[:, N_GROUPS:N_GROUPS + N_EXPERTS].set(w_expert)
    meta, idx, counts = _router(h, ln_g, w_r, 512)
    cnt = counts[0, :N_EXPERTS].astype(jnp.int32)
    padded = ((cnt + rb - 1) // rb) * rb
    pad_end = jnp.cumsum(padded)
    pad_start = pad_end - padded
    blk_start = jnp.arange(n_blk, dtype=jnp.int32) * rb
    blk_e = jnp.minimum(jnp.sum(pad_end[None, :] <= blk_start[:, None], axis=1), N_EXPERTS - 1).astype(jnp.int32)
    n_used = (pad_end[-1] // rb).astype(jnp.int32).reshape(1)
    experts = jnp.arange(N_EXPERTS, dtype=jnp.int32)[None, :]
    start_of = lambda e: jnp.sum(jnp.where(e[:, None] == experts, pad_start[None, :], 0), axis=1)
    d1 = start_of(idx[0]) + idx[2]
    d2 = start_of(idx[1]) + idx[3]
    xs = _dispatch(d1, d2, pad_start + cnt, padded - cnt, n_used, h, ln_g, cap, 256)
    ys = _experts(blk_e, n_used, xs, w1, w3, w2, layer)
    return _combine(d1, d2, h, meta, ys, g_final, 256, final_norm)


def kernel(x, ln_mix, ln_ffn, ln_final, ssm_w_in, ssm_lam_re, ssm_lam_im, ssm_log_dt, ssm_b_re, ssm_b_im, ssm_c_re, ssm_c_im, ssm_d, ssm_w_out, attn_w_qkv, attn_w_o, moe_w_group, moe_w_expert, moe_w1, moe_w3, moe_w2):
    bsz, seq, d = x.shape
    n = bsz * seq
    h = x.reshape(n, d)

    u = _norm_proj(h, ln_mix[0], ssm_w_in[0].astype(BF16), 512, F32, "s5_in_proj")
    mats = _s5_mats(ssm_lam_re[0], ssm_lam_im[0], ssm_log_dt[0], ssm_b_re[0], ssm_b_im[0],
                    ssm_c_re[0], ssm_c_im[0])
    z = _s5_core(u, mats, ssm_d[0], bsz, seq)
    h = _glu_out(z, ssm_w_out[0].astype(BF16), h, 512)
    h = _hier_moe(h, ln_ffn[0], moe_w_group[0], moe_w_expert[0], moe_w1, moe_w3, moe_w2, 0,
                  ln_final, False)

    q_t, k, v_t = _qkv_proj(h, ln_mix[1], attn_w_qkv[0], 512)
    o_t = _moba(q_t, k, v_t, bsz, seq)
    h = _proj_res_t(o_t, attn_w_o[0].astype(BF16), h, 512)
    h = _hier_moe(h, ln_ffn[1], moe_w_group[1], moe_w_expert[1], moe_w1, moe_w3, moe_w2, 1,
                  ln_final, True)
    return h.reshape(bsz, seq, d)
```

```python
import functools
import math

import jax
import jax.numpy as jnp
from jax import lax
from jax.experimental import pallas as pl
from jax.experimental.pallas import tpu as pltpu

F32 = jnp.float32
BF16 = jnp.bfloat16

D_MODEL = 1024
RMS_EPS = 1e-6
NEG_INF = -1e30

SSM_GROUP = 16
SSM_GROUPS = D_MODEL // SSM_GROUP
SSM_STATE = 64
SSM_CHUNK = 16
SSM_GB = 8
SSM_WIN = 8

ATT_HEADS = 8
HEAD_DIM = 128
MOBA_BLOCK = 256
MOBA_TOPK = 3

N_GROUPS = 4
EXPERTS_PER_GROUP = 8
N_EXPERTS = 32
EXPERT_FF = 512
MOE_ROWS = 256

LANES = 128
VMEM_LIMIT = 48 * 1024 * 1024

_NT = (((1,), (1,)), ((), ()))
_TN = (((0,), (0,)), ((), ()))


def _cparams(*sem):
    return pltpu.CompilerParams(dimension_semantics=sem, vmem_limit_bytes=VMEM_LIMIT)


def _rmsnorm(x, g):
    return x * lax.rsqrt(jnp.mean(x * x, axis=-1, keepdims=True) + RMS_EPS) * g


def _norm_proj_kernel(x_ref, g_ref, w_ref, o_ref):
    xn = _rmsnorm(x_ref[...], g_ref[...]).astype(BF16)
    o_ref[...] = jnp.dot(xn, w_ref[...], preferred_element_type=F32).astype(o_ref.dtype)


def _norm_proj(x, g, w_bf16, tm, out_dtype, name):
    n, d = x.shape
    dout = w_bf16.shape[1]
    return pl.pallas_call(
        _norm_proj_kernel,
        out_shape=jax.ShapeDtypeStruct((n, dout), out_dtype),
        grid=(n // tm,),
        in_specs=[
            pl.BlockSpec((tm, d), lambda i: (i, 0)),
            pl.BlockSpec((1, d), lambda i: (0, 0)),
            pl.BlockSpec((d, dout), lambda i: (0, 0)),
        ],
        out_specs=pl.BlockSpec((tm, dout), lambda i: (i, 0)),
        compiler_params=_cparams("parallel"),
        name=name,
    )(x, g.reshape(1, d), w_bf16)


def _s5_mats(lam_re, lam_im, log_dt, b_re, b_im, c_re, c_im):
    g_, p_, c_, t_ = SSM_GROUPS, SSM_STATE, SSM_GROUP, SSM_CHUNK
    lr = jnp.minimum(lam_re, -1e-4)
    li = lam_im
    dt = jnp.exp(log_dt)[:, None]
    mag = jnp.exp(lr * dt)
    abar_re = mag * jnp.cos(li * dt)
    abar_im = mag * jnp.sin(li * dt)
    den = lr * lr + li * li
    nr = abar_re - 1.0
    gam_re = (nr * lr + abar_im * li) / den
    gam_im = (abar_im * lr - nr * li) / den
    bb_re = gam_re[..., None] * b_re - gam_im[..., None] * b_im
    bb_im = gam_re[..., None] * b_im + gam_im[..., None] * b_re

    def powers(ns):
        nf = jnp.asarray(ns, F32)[None, :, None]
        pm = jnp.exp(nf * (lr * dt)[:, None, :])
        ang = nf * (li * dt)[:, None, :]
        return pm * jnp.cos(ang), pm * jnp.sin(ang)

    pr, pi = powers(list(range(t_ + 1)))
    ca_re = c_re[:, None] * pr[:, :, None, :] - c_im[:, None] * pi[:, :, None, :]
    ca_im = c_re[:, None] * pi[:, :, None, :] + c_im[:, None] * pr[:, :, None, :]
    ca_n = jnp.concatenate([ca_re[:, :t_], ca_im[:, :t_]], axis=-1).reshape(g_, t_ * c_, 2 * p_)
    bb_s = jnp.concatenate([bb_re, -bb_im], axis=1)
    kflat = jnp.einsum('gxp,gpc->gcx', ca_n, bb_s, precision='highest')
    prs = pr[:, :t_][:, ::-1][:, :, None, :]
    pis = pi[:, :t_][:, ::-1][:, :, None, :]
    bt_re = bb_re.transpose(0, 2, 1)[:, None]
    bt_im = bb_im.transpose(0, 2, 1)[:, None]
    w_re = prs * bt_re - pis * bt_im
    w_im = prs * bt_im + pis * bt_re
    w = jnp.concatenate([w_re, w_im, w_im, w_re], axis=-1).reshape(g_, t_ * c_, 4 * p_)
    prt = pr.transpose(0, 2, 1)[:, :, 1:, None]
    pit = pi.transpose(0, 2, 1)[:, :, 1:, None]
    ct_re = c_re.transpose(0, 2, 1)[:, :, None, :]
    ct_im = c_im.transpose(0, 2, 1)[:, :, None, :]
    v_re = (ct_re * prt - ct_im * pit).reshape(g_, p_, t_ * c_)
    v_im = -(ct_re * pit + ct_im * prt).reshape(g_, p_, t_ * c_)
    v = jnp.concatenate([v_re, v_im], axis=1)
    qr, qi = powers([t_ * (1 << j) for j in range(4)])
    rows = []
    for j in range(4):
        ar, ai = qr[:, j], qi[:, j]
        rows += [jnp.concatenate([ar, ar], -1), jnp.concatenate([-ai, ai], -1),
                 jnp.concatenate([ai, -ai], -1)]
    rows += [jnp.zeros_like(rows[0])] * 4
    coef = jnp.stack(rows, axis=1)
    return kflat.astype(F32), w.astype(BF16), v.astype(BF16), coef.astype(F32)


def _s5_perm():
    r = jnp.arange(8 * LANES)
    col = ((r % LANES) // SSM_GROUP) * LANES + (r // LANES) * SSM_GROUP + r % SSM_GROUP
    p = (col[:, None] == r[None, :]).astype(BF16)
    return p, p.T


def _s5_kernel(u_ref, kf_ref, w_ref, v_ref, coef_ref, d_ref, p_ref, pt_ref, o_ref,
               m_ref, vf_ref, zf_ref, ea_ref, eb_ref, sp_ref):
    t_, c_, gb, win = SSM_CHUNK, SSM_GROUP, SSM_GB, SSM_WIN
    nk = u_ref.shape[0] // t_
    p2 = 2 * SSM_STATE
    tc = t_ * c_

    @pl.when(pl.program_id(1) == 0)
    def _():
        lane_tc = lax.broadcasted_iota(jnp.int32, (c_, tc), 1)
        for g in range(gb):
            kf = kf_ref[g]
            for s in range(t_):
                rolled = kf if s == 0 else pltpu.roll(kf, s * c_, axis=1)
                m_ref[g, s * c_:(s + 1) * c_, :] = jnp.where(lane_tc >= s * c_, rolled, 0.0).astype(BF16)

    halves = tc // LANES

    def timestep(t):
        return pl.ds(t, nk, stride=t_)

    def flat(g):
        return jnp.concatenate([vf_ref[h, :, g * LANES:(g + 1) * LANES] for h in range(halves)], axis=1)

    for h in range(halves):
        x = jnp.concatenate([u_ref[timestep(8 * h + i), :].astype(BF16) for i in range(8)], axis=1)
        vf_ref[h] = jnp.dot(x, p_ref[...], preferred_element_type=F32).astype(BF16)

    row = lax.broadcasted_iota(jnp.int32, (nk, p2), 0)

    def shift(x, d):
        return jnp.where(row < d, 0.0, pltpu.roll(x, d, axis=0))

    for g in range(gb):
        sc = jnp.dot(flat(g), w_ref[g], preferred_element_type=F32)
        xa, xb = shift(sc[:, :p2], 1), shift(sc[:, p2:], 1)
        cf = coef_ref[g]
        for lvl in range(3):
            pp, qa, qb = cf[3 * lvl:3 * lvl + 1], cf[3 * lvl + 1:3 * lvl + 2], cf[3 * lvl + 2:3 * lvl + 3]
            sa, sb = shift(xa, 1 << lvl), shift(xb, 1 << lvl)
            xa, xb = xa + pp * sa + qa * sb, xb + pp * sb + qb * sa
        ea_ref[g] = xa
        eb_ref[g] = xb

    cfs = [coef_ref[g] for g in range(gb)]
    zero = jnp.zeros((win, p2), F32)
    state = [(zero, zero)] * gb
    for j in range(nk // win):
        rs = slice(j * win, (j + 1) * win)
        for g in range(gb):
            pp, qa, qb = cfs[g][9:10], cfs[g][10:11], cfs[g][11:12]
            s_a, s_b = state[g]
            n_a = pp * s_a + qa * s_b + ea_ref[g, rs, :]
            n_b = pp * s_b + qb * s_a + eb_ref[g, rs, :]
            sp_ref[g, rs, :] = n_a
            state[g] = (n_a, n_b)

    for g in range(gb):
        x = flat(g)
        y = jnp.dot(x, m_ref[g], preferred_element_type=F32)
        y = y + jnp.dot(sp_ref[g].astype(BF16), v_ref[g], preferred_element_type=F32)
        z = jax.nn.gelu(y + d_ref[g] * x.astype(F32)).astype(BF16)
        for h in range(halves):
            zf_ref[h, :, g * LANES:(g + 1) * LANES] = z[:, h * LANES:(h + 1) * LANES]

    for h in range(halves):
        zn = jnp.dot(zf_ref[h], pt_ref[...], preferred_element_type=F32)
        for i in range(8):
            o_ref[timestep(8 * h + i), :] = zn[:, i * LANES:(i + 1) * LANES]


def _s5_core(u, mats, d_skip, bsz, seq):
    kflat, w, v, coef = mats
    n, d = u.shape
    gb, tc, p2 = SSM_GB, SSM_CHUNK * SSM_GROUP, 2 * SSM_STATE
    nk = seq // SSM_CHUNK
    halves = tc // LANES
    perm, perm_t = _s5_perm()
    dflat = jnp.tile(d_skip.reshape(SSM_GROUPS, 1, SSM_GROUP), (1, 1, SSM_CHUNK))
    spec3 = lambda a, b: pl.BlockSpec((gb, a, b), lambda j, bb: (j, 0, 0))
    const = pl.BlockSpec(perm.shape, lambda j, bb: (0, 0))
    return pl.pallas_call(
        _s5_kernel,
        out_shape=jax.ShapeDtypeStruct((n, d), F32),
        grid=(SSM_GROUPS // gb, bsz),
        in_specs=[
            pl.BlockSpec((seq, LANES), lambda j, bb: (bb, j)),
            spec3(SSM_GROUP, tc), spec3(tc, 2 * p2), spec3(p2, tc), spec3(16, p2), spec3(1, tc),
            const, const,
        ],
        out_specs=pl.BlockSpec((seq, LANES), lambda j, bb: (bb, j)),
        scratch_shapes=[pltpu.VMEM((gb, tc, tc), BF16),
                        pltpu.VMEM((halves, nk, gb * LANES), BF16),
                        pltpu.VMEM((halves, nk, gb * LANES), BF16),
                        pltpu.VMEM((gb, nk, p2), F32), pltpu.VMEM((gb, nk, p2), F32),
                        pltpu.VMEM((gb, nk, p2), F32)],
        compiler_params=_cparams("arbitrary", "arbitrary"),
        name="s5_core",
    )(u, kflat, w, v, coef, dflat, perm, perm_t)


def _glu_out_kernel(z_ref, w_ref, x_ref, o_ref):
    vg = jnp.dot(z_ref[...].astype(BF16), w_ref[...], preferred_element_type=F32)
    d = o_ref.shape[1]
    o_ref[...] = x_ref[...] + vg[:, :d] * jax.nn.sigmoid(vg[:, d:])


def _glu_out(z, w_bf16, x, tm):
    n, d = x.shape
    return pl.pallas_call(
        _glu_out_kernel,
        out_shape=jax.ShapeDtypeStruct((n, d), F32),
        grid=(n // tm,),
        in_specs=[
            pl.BlockSpec((tm, z.shape[1]), lambda i: (i, 0)),
            pl.BlockSpec(w_bf16.shape, lambda i: (0, 0)),
            pl.BlockSpec((tm, d), lambda i: (i, 0)),
        ],
        out_specs=pl.BlockSpec((tm, d), lambda i: (i, 0)),
        compiler_params=_cparams("parallel"),
        name="s5_glu_out",
    )(z, w_bf16, x)


def _proj_res_t_kernel(at_ref, w_ref, x_ref, o_ref):
    o_ref[...] = x_ref[...] + lax.dot_general(at_ref[...], w_ref[...], _TN, preferred_element_type=F32)


def _proj_res_t(a_t, w_bf16, x, tm):
    n, d = x.shape
    return pl.pallas_call(
        _proj_res_t_kernel,
        out_shape=jax.ShapeDtypeStruct((n, d), F32),
        grid=(n // tm,),
        in_specs=[
            pl.BlockSpec((a_t.shape[0], tm), lambda i: (0, i)),
            pl.BlockSpec(w_bf16.shape, lambda i: (0, 0)),
            pl.BlockSpec((tm, d), lambda i: (i, 0)),
        ],
        out_specs=pl.BlockSpec((tm, d), lambda i: (i, 0)),
        compiler_params=_cparams("parallel"),
        name="attn_out_proj",
    )(a_t, w_bf16, x)


def _qkv_kernel(x_ref, g_ref, wq_ref, wk_ref, wv_ref, qt_ref, k_ref, vt_ref):
    xn = _rmsnorm(x_ref[...], g_ref[...]).astype(BF16)
    c = (HEAD_DIM ** -0.5) * math.log2(math.e)
    qt_ref[...] = (lax.dot_general(wq_ref[...], xn, _NT, preferred_element_type=F32) * c).astype(BF16)
    k_ref[...] = jnp.dot(xn, wk_ref[...], preferred_element_type=F32).astype(BF16)
    vt_ref[...] = lax.dot_general(wv_ref[...], xn, _NT, preferred_element_type=F32).astype(BF16)


def _qkv_proj(x, g, w_qkv, tm):
    n, d = x.shape
    da = ATT_HEADS * HEAD_DIM
    wq_t = w_qkv[:, :da].T.astype(BF16)
    wk = w_qkv[:, da:2 * da].astype(BF16)
    wv_t = w_qkv[:, 2 * da:].T.astype(BF16)
    full = lambda shp: pl.BlockSpec(shp, lambda i: (0, 0))
    return pl.pallas_call(
        _qkv_kernel,
        out_shape=(jax.ShapeDtypeStruct((da, n), BF16), jax.ShapeDtypeStruct((n, da), BF16),
                   jax.ShapeDtypeStruct((da, n), BF16)),
        grid=(n // tm,),
        in_specs=[pl.BlockSpec((tm, d), lambda i: (i, 0)), full((1, d)),
                  full((da, d)), full((d, da)), full((da, d))],
        out_specs=(pl.BlockSpec((da, tm), lambda i: (0, i)), pl.BlockSpec((tm, da), lambda i: (i, 0)),
                   pl.BlockSpec((da, tm), lambda i: (0, i))),
        compiler_params=_cparams("parallel"),
        name="attn_qkv_proj",
    )(x, g.reshape(1, d), wq_t, wk, wv_t)


MOBA_HEADS_PER_STEP = 2


def _moba_kernel(qt_ref, k_ref, vt_ref, et_ref, ot_ref, *, nb):
    blk, dh = MOBA_BLOCK, HEAD_DIM
    brow = lax.broadcasted_iota(jnp.int32, (nb, blk), 0)
    r_ix = lax.broadcasted_iota(jnp.int32, (blk, blk), 0)
    c_ix = lax.broadcasted_iota(jnp.int32, (blk, blk), 1)
    zpad = jnp.zeros((2 * LANES - dh - nb, blk), BF16)
    heads = range(MOBA_HEADS_PER_STEP)
    kmeans = [jnp.mean(k_ref[:, hd * dh:(hd + 1) * dh].astype(F32).reshape(nb, blk, dh), axis=1)
              for hd in heads]
    for i, hd in [(i, hd) for i in range(nb) for hd in heads]:
        hs = slice(hd * dh, (hd + 1) * dh)
        qs = slice(i * blk, (i + 1) * blk)
        kmean = kmeans[hd]
        qt = qt_ref[hs, qs]
        if i > MOBA_TOPK:
            gate = jnp.dot(kmean, qt.astype(F32), preferred_element_type=F32,
                           precision=lax.Precision.HIGHEST)
            gate = jnp.where(brow < i, gate, NEG_INF)
            sel = brow >= i
            for _ in range(MOBA_TOPK):
                gm = jnp.max(gate, axis=0, keepdims=True)
                first = jnp.min(jnp.where(gate == gm, brow, nb), axis=0, keepdims=True)
                pick = brow == first
                sel = jnp.logical_or(sel, pick)
                gate = jnp.where(pick, -jnp.inf, gate)
            bias = jnp.where(sel, 0.0, NEG_INF).astype(BF16)
        else:
            bias = jnp.zeros((nb, blk), BF16)
        q_aug = jnp.concatenate([qt, bias, zpad], axis=0)
        k_own = jnp.concatenate([k_ref[qs, hs], et_ref[qs, :]], axis=1)
        s_own = jnp.dot(k_own, q_aug, preferred_element_type=F32)
        s_own = jnp.where(r_ix <= c_ix, s_own, NEG_INF)
        m = jnp.max(s_own, axis=0, keepdims=True)
        if i > 0:
            ps = slice(0, i * blk)
            k_past = jnp.concatenate([k_ref[ps, hs], et_ref[ps, :]], axis=1)
            s_past = jnp.dot(k_past, q_aug, preferred_element_type=F32)
            m = jnp.maximum(m, jnp.max(s_past, axis=0, keepdims=True))
            p_past = jnp.exp2(s_past - m)
        p_own = jnp.exp2(s_own - m)
        l = jnp.sum(p_own, axis=0, keepdims=True)
        acc = jnp.dot(vt_ref[hs, qs], p_own.astype(BF16), preferred_element_type=F32)
        if i > 0:
            l = l + jnp.sum(p_past, axis=0, keepdims=True)
            acc = acc + jnp.dot(vt_ref[hs, ps], p_past.astype(BF16), preferred_element_type=F32)
        ot_ref[hs, qs] = (acc / l).astype(ot_ref.dtype)


def _moba(q_t, k, v_t, bsz, seq):
    nb = seq // MOBA_BLOCK
    da = ATT_HEADS * HEAD_DIM
    hw = MOBA_HEADS_PER_STEP * HEAD_DIM
    e_t = ((jnp.arange(seq) // MOBA_BLOCK)[:, None] == jnp.arange(LANES)[None, :]).astype(BF16)
    tspec = pl.BlockSpec((hw, seq), lambda b, h: (h, b))
    return pl.pallas_call(
        functools.partial(_moba_kernel, nb=nb),
        out_shape=jax.ShapeDtypeStruct((da, bsz * seq), BF16),
        grid=(bsz, ATT_HEADS // MOBA_HEADS_PER_STEP),
        in_specs=[tspec, pl.BlockSpec((seq, hw), lambda b, h: (b, h)), tspec,
                  pl.BlockSpec((seq, LANES), lambda b, h: (0, 0))],
        out_specs=tspec,
        compiler_params=_cparams("parallel", "parallel"),
        name="moba_attn",
    )(q_t, k, v_t, e_t)


def _pack_halves(x):
    k = x.shape[1] // 2
    lo = pltpu.bitcast(x[:, :k].astype(BF16).astype(F32), jnp.uint32)
    hi = pltpu.bitcast(x[:, k:].astype(BF16).astype(F32), jnp.uint32)
    return (lo >> 16) | (hi & jnp.uint32(0xFFFF0000))


def _unpack_halves(p):
    lo = pltpu.bitcast(p << 16, F32)
    hi = pltpu.bitcast(p & jnp.uint32(0xFFFF0000), F32)
    return lo, hi


def _router_kernel(h_ref, g_ref, w_ref, tri_ref, meta_ref, idx_ref, cnt_ref):
    xn = _rmsnorm(h_ref[...], g_ref[...])
    x_hi = xn.astype(BF16)
    x_lo = (xn - x_hi.astype(F32)).astype(BF16)
    t = jnp.dot(x_hi, w_ref[...], preferred_element_type=F32)
    logits = (t[:, :LANES] + t[:, LANES:]) + jnp.dot(x_lo, w_ref[:, :LANES], preferred_element_type=F32)
    tm = logits.shape[0]
    lane = lax.broadcasted_iota(jnp.int32, (tm, LANES), 1)
    ninf = -jnp.inf
    lg = jnp.where(lane < N_GROUPS, logits, ninf)
    gm = jnp.max(lg, axis=-1, keepdims=True)
    g_idx = jnp.min(jnp.where(lg == gm, lane, LANES), axis=-1, keepdims=True)
    g_gate = 1.0 / jnp.sum(jnp.exp(lg - gm), axis=-1, keepdims=True)
    lo = N_GROUPS + EXPERTS_PER_GROUP * g_idx
    le = jnp.where((lane >= lo) & (lane < lo + EXPERTS_PER_GROUP), logits, ninf)
    m1 = jnp.max(le, axis=-1, keepdims=True)
    i1 = jnp.min(jnp.where(le == m1, lane, LANES), axis=-1, keepdims=True)
    le2 = jnp.where(lane == i1, ninf, le)
    m2 = jnp.max(le2, axis=-1, keepdims=True)
    i2 = jnp.min(jnp.where(le2 == m2, lane, LANES), axis=-1, keepdims=True)
    p2 = jnp.exp(m2 - m1)
    gate1 = g_gate / (1.0 + p2)
    gate2 = g_gate * p2 / (1.0 + p2)
    e1 = (i1 - N_GROUPS).astype(F32)
    e2 = (i2 - N_GROUPS).astype(F32)
    meta = jnp.where(lane == 0, e1, jnp.where(lane == 1, e2, jnp.where(lane == 2, gate1,
                     jnp.where(lane == 3, gate2, 0.0))))
    meta_ref[...] = meta
    oh1 = (lane == i1 - N_GROUPS).astype(F32)
    oh2 = (lane == i2 - N_GROUPS).astype(F32)
    both = oh1 + oh2

    @pl.when(pl.program_id(0) == 0)
    def _():
        cnt_ref[...] = jnp.zeros_like(cnt_ref)

    seen = cnt_ref[...]
    before = jnp.dot(tri_ref[...], both.astype(BF16), preferred_element_type=F32) + seen
    r1 = jnp.sum(before * oh1, axis=-1, keepdims=True)
    r2 = jnp.sum(before * oh2, axis=-1, keepdims=True)
    slab = jnp.where(lane == 0, e1, jnp.where(lane == 1, e2, jnp.where(lane == 2, r1,
                     jnp.where(lane == 3, r2, 0.0))))
    idx_ref[...] = jnp.transpose(slab)[0:8, :].astype(jnp.int32)
    cnt_ref[...] = seen + jnp.sum(both, axis=0, keepdims=True)


def _router(h, g, w_r, tm):
    n, d = h.shape
    tri = (jnp.arange(tm)[:, None] > jnp.arange(tm)[None, :]).astype(BF16)
    return pl.pallas_call(
        _router_kernel,
        out_shape=(jax.ShapeDtypeStruct((n, LANES), F32), jax.ShapeDtypeStruct((8, n), jnp.int32),
                   jax.ShapeDtypeStruct((1, LANES), F32)),
        grid=(n // tm,),
        in_specs=[
            pl.BlockSpec((tm, d), lambda i: (i, 0)),
            pl.BlockSpec((1, d), lambda i: (0, 0)),
            pl.BlockSpec((d, 2 * LANES), lambda i: (0, 0)),
            pl.BlockSpec((tm, tm), lambda i: (0, 0)),
        ],
        out_specs=(pl.BlockSpec((tm, LANES), lambda i: (i, 0)),
                   pl.BlockSpec((8, tm), lambda i: (0, i)),
                   pl.BlockSpec((1, LANES), lambda i: (0, 0))),
        compiler_params=_cparams("arbitrary"),
        name="moe_router",
    )(h, g.reshape(1, d), w_r, tri)


MOE_ISSUE_UNROLL = 8


def _dispatch_kernel(d1_ref, d2_ref, plo_ref, pn_ref, nu_ref, h_ref, g_ref, xs_ref,
                     buf_ref, zero_ref, sem, fill_sem):
    i = pl.program_id(0)
    nsteps = pl.num_programs(0)
    tm = h_ref.shape[0]
    rb = zero_ref.shape[0]
    n_blk = xs_ref.shape[0] // rb
    slot = i % 2
    base = i * tm

    def slot_drain(s):
        cp = pltpu.make_async_copy(buf_ref.at[s], xs_ref.at[pl.ds(0, tm), :], sem.at[s])
        cp.wait()
        cp.wait()

    @pl.when(i >= 2)
    def _():
        slot_drain(slot)

    buf_ref[slot] = _pack_halves(_rmsnorm(h_ref[...], g_ref[...]))

    def issue(r8, c):
        for k in range(MOE_ISSUE_UNROLL):
            r = r8 * MOE_ISSUE_UNROLL + k
            src = buf_ref.at[slot, pl.ds(r, 1), :]
            pltpu.make_async_copy(src, xs_ref.at[pl.ds(d1_ref[base + r], 1), :], sem.at[slot]).start(priority=0)
            pltpu.make_async_copy(src, xs_ref.at[pl.ds(d2_ref[base + r], 1), :], sem.at[slot]).start(priority=1)
        return c

    lax.fori_loop(0, tm // MOE_ISSUE_UNROLL, issue, 0)

    def pad_copy(row):
        return pltpu.make_async_copy(zero_ref.at[pl.ds(0, 1), :], xs_ref.at[pl.ds(row, 1), :], fill_sem)

    def blk_copy(b):
        return pltpu.make_async_copy(zero_ref, xs_ref.at[pl.ds(b * rb, rb), :], fill_sem)

    @pl.when(i == 0)
    def _():
        zero_ref[...] = jnp.zeros_like(zero_ref)

    @pl.when(i < N_EXPERTS)
    def _():
        lo = plo_ref[i]

        def fill(r, c):
            pad_copy(lo + r).start()
            return c

        lax.fori_loop(0, pn_ref[i], fill, 0)

    @pl.when(i == N_EXPERTS)
    def _():
        def fill(b, c):
            blk_copy(b).start()
            return c

        lax.fori_loop(nu_ref[0], n_blk, fill, 0)

    @pl.when(i == nsteps - 1)
    def _():
        slot_drain(1 - slot)
        slot_drain(slot)

        def per_expert(e, c):
            def one(r, cc):
                pad_copy(0).wait()
                return cc

            return lax.fori_loop(0, pn_ref[e], one, c)

        lax.fori_loop(0, N_EXPERTS, per_expert, 0)

        def one_blk(b, c):
            blk_copy(0).wait()
            return c

        lax.fori_loop(nu_ref[0], n_blk, one_blk, 0)


def _dispatch(d1, d2, pad_lo, pad_n, n_used, h, g, cap, tm):
    n, d = h.shape
    assert n // tm > N_EXPERTS + 1
    return pl.pallas_call(
        _dispatch_kernel,
        out_shape=jax.ShapeDtypeStruct((cap, d // 2), jnp.uint32),
        grid_spec=pltpu.PrefetchScalarGridSpec(
            num_scalar_prefetch=5,
            grid=(n // tm,),
            in_specs=[
                pl.BlockSpec((tm, d), lambda i, *_: (i, 0)),
                pl.BlockSpec((1, d), lambda i, *_: (0, 0)),
            ],
            out_specs=pl.BlockSpec(memory_space=pl.ANY),
            scratch_shapes=[pltpu.VMEM((2, tm, d // 2), jnp.uint32),
                            pltpu.VMEM((MOE_ROWS, d // 2), jnp.uint32),
                            pltpu.SemaphoreType.DMA((2,)), pltpu.SemaphoreType.DMA],
        ),
        compiler_params=_cparams("arbitrary"),
        name="moe_dispatch",
    )(d1, d2, pad_lo, pad_n, n_used, h, g.reshape(1, d))


def _expert_kernel(be_ref, nu_ref, x_ref, w1_ref, w3_ref, w2_ref, y_ref, w1c, w3c, w2c):
    i = pl.program_id(0)
    used = i < nu_ref[0]
    prev = be_ref[jnp.maximum(i - 1, 0)]
    fresh = jnp.logical_or(i == 0, be_ref[i] != prev)

    @pl.when(jnp.logical_and(used, fresh))
    def _():
        w1c[...] = w1_ref[0, 0].astype(BF16)
        w3c[...] = w3_ref[0, 0].astype(BF16)
        w2c[...] = w2_ref[0, 0].astype(BF16)

    @pl.when(used)
    def _():
        lo, hi = _unpack_halves(x_ref[...])
        lo, hi = lo.astype(BF16), hi.astype(BF16)
        hd = lo.shape[1]
        hf = w1c.shape[1] // 2
        y = None
        for c in range(2):
            cs = slice(c * hf, (c + 1) * hf)
            a = (jnp.dot(lo, w1c[:hd, cs], preferred_element_type=F32)
                 + jnp.dot(hi, w1c[hd:, cs], preferred_element_type=F32))
            b = (jnp.dot(lo, w3c[:hd, cs], preferred_element_type=F32)
                 + jnp.dot(hi, w3c[hd:, cs], preferred_element_type=F32))
            act = (jax.nn.silu(a) * b).astype(BF16)
            part = jnp.dot(act, w2c[cs, :], preferred_element_type=F32)
            y = part if y is None else y + part
        y_ref[...] = _pack_halves(y)

    @pl.when(jnp.logical_not(used))
    def _():
        y_ref[...] = jnp.zeros_like(y_ref)


def _experts(blk_e, n_used, xs, w1, w3, w2, layer):
    cap, dp = xs.shape
    d = 2 * dp
    rb = MOE_ROWS
    n_blk = cap // rb
    ff = w1.shape[3]
    return pl.pallas_call(
        _expert_kernel,
        out_shape=jax.ShapeDtypeStruct((cap, dp), jnp.uint32),
        grid_spec=pltpu.PrefetchScalarGridSpec(
            num_scalar_prefetch=2,
            grid=(n_blk,),
            in_specs=[
                pl.BlockSpec((rb, dp), lambda i, be, nu: (jnp.minimum(i, jnp.maximum(nu[0] - 1, 0)), 0)),
                pl.BlockSpec((1, 1, d, ff), lambda i, be, nu: (layer, be[i], 0, 0)),
                pl.BlockSpec((1, 1, d, ff), lambda i, be, nu: (layer, be[i], 0, 0)),
                pl.BlockSpec((1, 1, ff, d), lambda i, be, nu: (layer, be[i], 0, 0)),
            ],
            out_specs=pl.BlockSpec((rb, dp), lambda i, be, nu: (i, 0)),
            scratch_shapes=[pltpu.VMEM((d, ff), BF16), pltpu.VMEM((d, ff), BF16),
                            pltpu.VMEM((ff, d), BF16)],
        ),
        compiler_params=_cparams("arbitrary"),
        name="moe_experts",
    )(blk_e, n_used, xs, w1, w3, w2)


def _combine_kernel(d1_ref, d2_ref, h_ref, meta_ref, g_ref, ys_ref, o_ref, buf_ref, sem, *, final_norm):
    i = pl.program_id(0)
    nsteps = pl.num_programs(0)
    tm = h_ref.shape[0]
    slot = i % 2

    def fetch(tile, s):
        base = tile * tm

        def issue(r8, c):
            for k in range(MOE_ISSUE_UNROLL):
                r = r8 * MOE_ISSUE_UNROLL + k
                pltpu.make_async_copy(ys_ref.at[pl.ds(d1_ref[base + r], 1), :],
                                      buf_ref.at[s, 0, pl.ds(r, 1), :], sem.at[s]).start(priority=0)
                pltpu.make_async_copy(ys_ref.at[pl.ds(d2_ref[base + r], 1), :],
                                      buf_ref.at[s, 1, pl.ds(r, 1), :], sem.at[s]).start(priority=1)
            return c

        lax.fori_loop(0, tm // MOE_ISSUE_UNROLL, issue, 0)

    @pl.when(i == 0)
    def _():
        fetch(0, 0)

    @pl.when(i + 1 < nsteps)
    def _():
        fetch(i + 1, 1 - slot)

    for j in range(2):
        pltpu.make_async_copy(ys_ref.at[pl.ds(0, tm), :], buf_ref.at[slot, j], sem.at[slot]).wait()

    meta = meta_ref[...]
    g1, g2 = meta[:, 2:3], meta[:, 3:4]
    lo1, hi1 = _unpack_halves(buf_ref[slot, 0])
    lo2, hi2 = _unpack_halves(buf_ref[slot, 1])
    hd = lo1.shape[1]
    out = jnp.concatenate([h_ref[:, :hd] + (g1 * lo1 + g2 * lo2),
                           h_ref[:, hd:] + (g1 * hi1 + g2 * hi2)], axis=1)
    if final_norm:
        out = _rmsnorm(out, g_ref[...])
    o_ref[...] = out


def _combine(d1, d2, h, meta, ys, g_final, tm, final_norm):
    n, d = h.shape
    return pl.pallas_call(
        functools.partial(_combine_kernel, final_norm=final_norm),
        out_shape=jax.ShapeDtypeStruct((n, d), F32),
        grid_spec=pltpu.PrefetchScalarGridSpec(
            num_scalar_prefetch=2,
            grid=(n // tm,),
            in_specs=[
                pl.BlockSpec((tm, d), lambda i, *_: (i, 0)),
                pl.BlockSpec((tm, LANES), lambda i, *_: (i, 0)),
                pl.BlockSpec((1, d), lambda i, *_: (0, 0)),
                pl.BlockSpec(memory_space=pl.ANY),
            ],
            out_specs=pl.BlockSpec((tm, d), lambda i, *_: (i, 0)),
            scratch_shapes=[pltpu.VMEM((2, 2, tm, d // 2), jnp.uint32), pltpu.SemaphoreType.DMA((2,))],
        ),
        compiler_params=_cparams("arbitrary"),
        name="moe_combine",
    )(d1, d2, h, meta, g_final.reshape(1, d), ys)


def _hier_moe(h, ln_g, w_group, w_expert, w1, w3, w2, layer, g_final, final_norm):
    n, d = h.shape
    rb = MOE_ROWS
    cap = 2 * n + N_EXPERTS * rb
    n_blk = cap // rb
    w_r = jnp.zeros((d, LANES), F32).at[:, :N_GROUPS].set(w_group)
    w_r = w_r.at[:, N_GROUPS:N_GROUPS + N_EXPERTS].set(w_expert)
    w_hi = w_r.astype(BF16)
    w_lo = (w_r - w_hi.astype(F32)).astype(BF16)
    meta, idx, counts = _router(h, ln_g, jnp.concatenate([w_hi, w_lo], axis=1), 512)
    cnt = counts[0, :N_EXPERTS].astype(jnp.int32)
    padded = ((cnt + rb - 1) // rb) * rb
    pad_end = jnp.cumsum(padded)
    pad_start = pad_end - padded
    blk_start = jnp.arange(n_blk, dtype=jnp.int32) * rb
    blk_e = jnp.minimum(jnp.sum(pad_end[None, :] <= blk_start[:, None], axis=1), N_EXPERTS - 1).astype(jnp.int32)
    n_used = (pad_end[-1] // rb).astype(jnp.int32).reshape(1)
    experts = jnp.arange(N_EXPERTS, dtype=jnp.int32)[None, :]
    start_of = lambda e: jnp.sum(jnp.where(e[:, None] == experts, pad_start[None, :], 0), axis=1)
    d1 = start_of(idx[0]) + idx[2]
    d2 = start_of(idx[1]) + idx[3]
    xs = _dispatch(d1, d2, pad_start + cnt, padded - cnt, n_used, h, ln_g, cap, 256)
    ys = _experts(blk_e, n_used, xs, w1, w3, w2, layer)
    return _combine(d1, d2, h, meta, ys, g_final, 256, final_norm)


def kernel(x, ln_mix, ln_ffn, ln_final, ssm_w_in, ssm_lam_re, ssm_lam_im, ssm_log_dt, ssm_b_re, ssm_b_im, ssm_c_re, ssm_c_im, ssm_d, ssm_w_out, attn_w_qkv, attn_w_o, moe_w_group, moe_w_expert, moe_w1, moe_w3, moe_w2):
    bsz, seq, d = x.shape
    n = bsz * seq
    h = x.reshape(n, d)

    u = _norm_proj(h, ln_mix[0], ssm_w_in[0].astype(BF16), 512, F32, "s5_in_proj")
    mats = _s5_mats(ssm_lam_re[0], ssm_lam_im[0], ssm_log_dt[0], ssm_b_re[0], ssm_b_im[0],
                    ssm_c_re[0], ssm_c_im[0])
    z = _s5_core(u, mats, ssm_d[0], bsz, seq)
    h = _glu_out(z, ssm_w_out[0].astype(BF16), h, 512)
    h = _hier_moe(h, ln_ffn[0], moe_w_group[0], moe_w_expert[0], moe_w1, moe_w3, moe_w2, 0,
                  ln_final, False)

    q_t, k, v_t = _qkv_proj(h, ln_mix[1], attn_w_qkv[0], 512)
    o_t = _moba(q_t, k, v_t, bsz, seq)
    h = _proj_res_t(o_t, attn_w_o[0].astype(BF16), h, 512)
    h = _hier_moe(h, ln_ffn[1], moe_w_group[1], moe_w_expert[1], moe_w1, moe_w3, moe_w2, 1,
                  ln_final, True)
    return h.reshape(bsz, seq, d)
```

```python
import functools
import math

import jax
import jax.numpy as jnp
from jax import lax
from jax.experimental import pallas as pl
from jax.experimental.pallas import tpu as pltpu

F32 = jnp.float32
BF16 = jnp.bfloat16

D_MODEL = 1024
RMS_EPS = 1e-6
NEG_INF = -1e30

SSM_GROUP = 16
SSM_GROUPS = D_MODEL // SSM_GROUP
SSM_STATE = 64
SSM_CHUNK = 16
SSM_GB = 8
SSM_WIN = 8

ATT_HEADS = 8
HEAD_DIM = 128
MOBA_BLOCK = 256
MOBA_TOPK = 3

N_GROUPS = 4
EXPERTS_PER_GROUP = 8
N_EXPERTS = 32
EXPERT_FF = 512
MOE_ROWS = 256

LANES = 128
VMEM_LIMIT = 48 * 1024 * 1024

_NT = (((1,), (1,)), ((), ()))
_TN = (((0,), (0,)), ((), ()))


def _cparams(*sem):
    return pltpu.CompilerParams(dimension_semantics=sem, vmem_limit_bytes=VMEM_LIMIT)


def _rmsnorm(x, g):
    return x * lax.rsqrt(jnp.mean(x * x, axis=-1, keepdims=True) + RMS_EPS) * g


def _norm_proj_kernel(x_ref, g_ref, w_ref, o_ref):
    xn = _rmsnorm(x_ref[...], g_ref[...]).astype(BF16)
    o_ref[...] = jnp.dot(xn, w_ref[...], preferred_element_type=F32).astype(o_ref.dtype)


def _norm_proj(x, g, w_bf16, tm, out_dtype, name):
    n, d = x.shape
    dout = w_bf16.shape[1]
    return pl.pallas_call(
        _norm_proj_kernel,
        out_shape=jax.ShapeDtypeStruct((n, dout), out_dtype),
        grid=(n // tm,),
        in_specs=[
            pl.BlockSpec((tm, d), lambda i: (i, 0)),
            pl.BlockSpec((1, d), lambda i: (0, 0)),
            pl.BlockSpec((d, dout), lambda i: (0, 0)),
        ],
        out_specs=pl.BlockSpec((tm, dout), lambda i: (i, 0)),
        compiler_params=_cparams("parallel"),
        name=name,
    )(x, g.reshape(1, d), w_bf16)


def _s5_mats(lam_re, lam_im, log_dt, b_re, b_im, c_re, c_im):
    g_, p_, c_, t_ = SSM_GROUPS, SSM_STATE, SSM_GROUP, SSM_CHUNK
    lr = jnp.minimum(lam_re, -1e-4)
    li = lam_im
    dt = jnp.exp(log_dt)[:, None]
    mag = jnp.exp(lr * dt)
    abar_re = mag * jnp.cos(li * dt)
    abar_im = mag * jnp.sin(li * dt)
    den = lr * lr + li * li
    nr = abar_re - 1.0
    gam_re = (nr * lr + abar_im * li) / den
    gam_im = (abar_im * lr - nr * li) / den
    bb_re = gam_re[..., None] * b_re - gam_im[..., None] * b_im
    bb_im = gam_re[..., None] * b_im + gam_im[..., None] * b_re

    def powers(ns):
        nf = jnp.asarray(ns, F32)[None, :, None]
        pm = jnp.exp(nf * (lr * dt)[:, None, :])
        ang = nf * (li * dt)[:, None, :]
        return pm * jnp.cos(ang), pm * jnp.sin(ang)

    pr, pi = powers(list(range(t_ + 1)))
    ca_re = c_re[:, None] * pr[:, :, None, :] - c_im[:, None] * pi[:, :, None, :]
    ca_im = c_re[:, None] * pi[:, :, None, :] + c_im[:, None] * pr[:, :, None, :]
    ca_n = jnp.concatenate([ca_re[:, :t_], ca_im[:, :t_]], axis=-1).reshape(g_, t_ * c_, 2 * p_)
    bb_s = jnp.concatenate([bb_re, -bb_im], axis=1)
    kflat = jnp.einsum('gxp,gpc->gcx', ca_n, bb_s, precision='highest')
    prs = pr[:, :t_][:, ::-1][:, :, None, :]
    pis = pi[:, :t_][:, ::-1][:, :, None, :]
    bt_re = bb_re.transpose(0, 2, 1)[:, None]
    bt_im = bb_im.transpose(0, 2, 1)[:, None]
    w_re = prs * bt_re - pis * bt_im
    w_im = prs * bt_im + pis * bt_re
    w = jnp.concatenate([w_re, w_im, w_im, w_re], axis=-1).reshape(g_, t_ * c_, 4 * p_)
    prt = pr.transpose(0, 2, 1)[:, :, 1:, None]
    pit = pi.transpose(0, 2, 1)[:, :, 1:, None]
    ct_re = c_re.transpose(0, 2, 1)[:, :, None, :]
    ct_im = c_im.transpose(0, 2, 1)[:, :, None, :]
    v_re = (ct_re * prt - ct_im * pit).reshape(g_, p_, t_ * c_)
    v_im = -(ct_re * pit + ct_im * prt).reshape(g_, p_, t_ * c_)
    v = jnp.concatenate([v_re, v_im], axis=1)
    qr, qi = powers([t_ * (1 << j) for j in range(4)])
    rows = []
    for j in range(4):
        ar, ai = qr[:, j], qi[:, j]
        rows += [jnp.concatenate([ar, ar], -1), jnp.concatenate([-ai, ai], -1),
                 jnp.concatenate([ai, -ai], -1)]
    rows += [jnp.zeros_like(rows[0])] * 4
    coef = jnp.stack(rows, axis=1)
    return kflat.astype(F32), w.astype(BF16), v.astype(BF16), coef.astype(F32)


def _s5_perm():
    r = jnp.arange(8 * LANES)
    col = ((r % LANES) // SSM_GROUP) * LANES + (r // LANES) * SSM_GROUP + r % SSM_GROUP
    p = (col[:, None] == r[None, :]).astype(BF16)
    return p, p.T


def _s5_kernel(u_ref, kf_ref, w_ref, v_ref, coef_ref, d_ref, p_ref, pt_ref, o_ref,
               m_ref, vf_ref, zf_ref, ea_ref, eb_ref, sp_ref):
    t_, c_, gb, win = SSM_CHUNK, SSM_GROUP, SSM_GB, SSM_WIN
    nk = u_ref.shape[0] // t_
    p2 = 2 * SSM_STATE
    tc = t_ * c_

    @pl.when(pl.program_id(1) == 0)
    def _():
        lane_tc = lax.broadcasted_iota(jnp.int32, (c_, tc), 1)
        for g in range(gb):
            kf = kf_ref[g]
            for s in range(t_):
                rolled = kf if s == 0 else pltpu.roll(kf, s * c_, axis=1)
                m_ref[g, s * c_:(s + 1) * c_, :] = jnp.where(lane_tc >= s * c_, rolled, 0.0).astype(BF16)

    halves = tc // LANES

    def timestep(t):
        return pl.ds(t, nk, stride=t_)

    def flat(g):
        return jnp.concatenate([vf_ref[h, :, g * LANES:(g + 1) * LANES] for h in range(halves)], axis=1)

    for h in range(halves):
        x = jnp.concatenate([u_ref[timestep(8 * h + i), :].astype(BF16) for i in range(8)], axis=1)
        vf_ref[h] = jnp.dot(x, p_ref[...], preferred_element_type=F32).astype(BF16)

    row = lax.broadcasted_iota(jnp.int32, (nk, p2), 0)

    def shift(x, d):
        return jnp.where(row < d, 0.0, pltpu.roll(x, d, axis=0))

    for g in range(gb):
        sc = jnp.dot(flat(g), w_ref[g], preferred_element_type=F32)
        xa, xb = shift(sc[:, :p2], 1), shift(sc[:, p2:], 1)
        cf = coef_ref[g]
        for lvl in range(3):
            pp, qa, qb = cf[3 * lvl:3 * lvl + 1], cf[3 * lvl + 1:3 * lvl + 2], cf[3 * lvl + 2:3 * lvl + 3]
            sa, sb = shift(xa, 1 << lvl), shift(xb, 1 << lvl)
            xa, xb = xa + pp * sa + qa * sb, xb + pp * sb + qb * sa
        ea_ref[g] = xa
        eb_ref[g] = xb

    cfs = [coef_ref[g] for g in range(gb)]
    zero = jnp.zeros((win, p2), F32)
    state = [(zero, zero)] * gb
    for j in range(nk // win):
        rs = slice(j * win, (j + 1) * win)
        for g in range(gb):
            pp, qa, qb = cfs[g][9:10], cfs[g][10:11], cfs[g][11:12]
            s_a, s_b = state[g]
            n_a = pp * s_a + qa * s_b + ea_ref[g, rs, :]
            n_b = pp * s_b + qb * s_a + eb_ref[g, rs, :]
            sp_ref[g, rs, :] = n_a
            state[g] = (n_a, n_b)

    for g in range(gb):
        x = flat(g)
        y = jnp.dot(x, m_ref[g], preferred_element_type=F32)
        y = y + jnp.dot(sp_ref[g].astype(BF16), v_ref[g], preferred_element_type=F32)
        z = jax.nn.gelu(y + d_ref[g] * x.astype(F32)).astype(BF16)
        for h in range(halves):
            zf_ref[h, :, g * LANES:(g + 1) * LANES] = z[:, h * LANES:(h + 1) * LANES]

    for h in range(halves):
        zn = jnp.dot(zf_ref[h], pt_ref[...], preferred_element_type=F32)
        for i in range(8):
            o_ref[timestep(8 * h + i), :] = zn[:, i * LANES:(i + 1) * LANES]


def _s5_core(u, mats, d_skip, bsz, seq):
    kflat, w, v, coef = mats
    n, d = u.shape
    gb, tc, p2 = SSM_GB, SSM_CHUNK * SSM_GROUP, 2 * SSM_STATE
    nk = seq // SSM_CHUNK
    halves = tc // LANES
    perm, perm_t = _s5_perm()
    dflat = jnp.tile(d_skip.reshape(SSM_GROUPS, 1, SSM_GROUP), (1, 1, SSM_CHUNK))
    spec3 = lambda a, b: pl.BlockSpec((gb, a, b), lambda j, bb: (j, 0, 0))
    const = pl.BlockSpec(perm.shape, lambda j, bb: (0, 0))
    return pl.pallas_call(
        _s5_kernel,
        out_shape=jax.ShapeDtypeStruct((n, d), F32),
        grid=(SSM_GROUPS // gb, bsz),
        in_specs=[
            pl.BlockSpec((seq, LANES), lambda j, bb: (bb, j)),
            spec3(SSM_GROUP, tc), spec3(tc, 2 * p2), spec3(p2, tc), spec3(16, p2), spec3(1, tc),
            const, const,
        ],
        out_specs=pl.BlockSpec((seq, LANES), lambda j, bb: (bb, j)),
        scratch_shapes=[pltpu.VMEM((gb, tc, tc), BF16),
                        pltpu.VMEM((halves, nk, gb * LANES), BF16),
                        pltpu.VMEM((halves, nk, gb * LANES), BF16),
                        pltpu.VMEM((gb, nk, p2), F32), pltpu.VMEM((gb, nk, p2), F32),
                        pltpu.VMEM((gb, nk, p2), F32)],
        compiler_params=_cparams("arbitrary", "arbitrary"),
        name="s5_core",
    )(u, kflat, w, v, coef, dflat, perm, perm_t)


def _glu_out_kernel(z_ref, w_ref, x_ref, o_ref):
    vg = jnp.dot(z_ref[...].astype(BF16), w_ref[...], preferred_element_type=F32)
    d = o_ref.shape[1]
    o_ref[...] = x_ref[...] + vg[:, :d] * jax.nn.sigmoid(vg[:, d:])


def _glu_out(z, w_bf16, x, tm):
    n, d = x.shape
    return pl.pallas_call(
        _glu_out_kernel,
        out_shape=jax.ShapeDtypeStruct((n, d), F32),
        grid=(n // tm,),
        in_specs=[
            pl.BlockSpec((tm, z.shape[1]), lambda i: (i, 0)),
            pl.BlockSpec(w_bf16.shape, lambda i: (0, 0)),
            pl.BlockSpec((tm, d), lambda i: (i, 0)),
        ],
        out_specs=pl.BlockSpec((tm, d), lambda i: (i, 0)),
        compiler_params=_cparams("parallel"),
        name="s5_glu_out",
    )(z, w_bf16, x)


def _proj_res_t_kernel(at_ref, w_ref, x_ref, o_ref):
    o_ref[...] = x_ref[...] + lax.dot_general(at_ref[...], w_ref[...], _TN, preferred_element_type=F32)


def _proj_res_t(a_t, w_bf16, x, tm):
    n, d = x.shape
    return pl.pallas_call(
        _proj_res_t_kernel,
        out_shape=jax.ShapeDtypeStruct((n, d), F32),
        grid=(n // tm,),
        in_specs=[
            pl.BlockSpec((a_t.shape[0], tm), lambda i: (0, i)),
            pl.BlockSpec(w_bf16.shape, lambda i: (0, 0)),
            pl.BlockSpec((tm, d), lambda i: (i, 0)),
        ],
        out_specs=pl.BlockSpec((tm, d), lambda i: (i, 0)),
        compiler_params=_cparams("parallel"),
        name="attn_out_proj",
    )(a_t, w_bf16, x)


def _qkv_kernel(x_ref, g_ref, wq_ref, wk_ref, wv_ref, qt_ref, k_ref, vt_ref):
    xn = _rmsnorm(x_ref[...], g_ref[...]).astype(BF16)
    c = (HEAD_DIM ** -0.5) * math.log2(math.e)
    qt_ref[...] = (lax.dot_general(wq_ref[...], xn, _NT, preferred_element_type=F32) * c).astype(BF16)
    k_ref[...] = jnp.dot(xn, wk_ref[...], preferred_element_type=F32).astype(BF16)
    vt = lax.dot_general(wv_ref[...], xn, _NT, preferred_element_type=F32).astype(BF16)
    for c in range(vt_ref.shape[0]):
        vt_ref[c] = vt[:, c * MOBA_BLOCK:(c + 1) * MOBA_BLOCK]


def _qkv_proj(x, g, w_qkv, tm):
    n, d = x.shape
    da = ATT_HEADS * HEAD_DIM
    wq_t = w_qkv[:, :da].T.astype(BF16)
    wk = w_qkv[:, da:2 * da].astype(BF16)
    wv_t = w_qkv[:, 2 * da:].T.astype(BF16)
    full = lambda shp: pl.BlockSpec(shp, lambda i: (0, 0))
    tb = tm // MOBA_BLOCK
    return pl.pallas_call(
        _qkv_kernel,
        out_shape=(jax.ShapeDtypeStruct((da, n), BF16), jax.ShapeDtypeStruct((n, da), BF16),
                   jax.ShapeDtypeStruct((n // MOBA_BLOCK, da, MOBA_BLOCK), BF16)),
        grid=(n // tm,),
        in_specs=[pl.BlockSpec((tm, d), lambda i: (i, 0)), full((1, d)),
                  full((da, d)), full((d, da)), full((da, d))],
        out_specs=(pl.BlockSpec((da, tm), lambda i: (0, i)), pl.BlockSpec((tm, da), lambda i: (i, 0)),
                   pl.BlockSpec((tb, da, MOBA_BLOCK), lambda i: (i, 0, 0))),
        compiler_params=_cparams("parallel"),
        name="attn_qkv_proj",
    )(x, g.reshape(1, d), wq_t, wk, wv_t)


def _moba_kernel(qt_ref, k_ref, vt_ref, et_ref, ot_ref, s_scr, acc_scr, m_scr, l_scr, *, nb):
    blk, dh = MOBA_BLOCK, HEAD_DIM
    brow = lax.broadcasted_iota(jnp.int32, (nb, blk), 0)
    r_ix = lax.broadcasted_iota(jnp.int32, (blk, blk), 0)
    c_ix = lax.broadcasted_iota(jnp.int32, (blk, blk), 1)
    zpad = jnp.zeros((2 * LANES - dh - nb, blk), BF16)
    kmean = jnp.mean(k_ref[...].astype(F32).reshape(nb, blk, dh), axis=1)

    def k_aug(t):
        rows = pl.ds(pl.multiple_of(t * blk, blk), blk)
        return jnp.concatenate([k_ref[rows, :], et_ref[rows, :]], axis=1)

    def absorb(s, vt):
        m_old = m_scr[...]
        m_new = jnp.maximum(m_old, jnp.max(s, axis=0, keepdims=True))
        alpha = jnp.exp2(m_old - m_new)
        p = jnp.exp2(s - m_new)
        l_scr[...] = alpha * l_scr[...] + jnp.sum(p, axis=0, keepdims=True)
        acc_scr[...] = alpha * acc_scr[...] + jnp.dot(vt, p.astype(BF16), preferred_element_type=F32)
        m_scr[...] = m_new

    for i in range(nb):
        qs = slice(i * blk, (i + 1) * blk)
        qt = qt_ref[:, qs]
        if i > MOBA_TOPK:
            gate = jnp.dot(kmean, qt.astype(F32), preferred_element_type=F32,
                           precision=lax.Precision.HIGHEST)
            gate = jnp.where(brow < i, gate, NEG_INF)
            sel = brow >= i
            for _ in range(MOBA_TOPK):
                gm = jnp.max(gate, axis=0, keepdims=True)
                first = jnp.min(jnp.where(gate == gm, brow, nb), axis=0, keepdims=True)
                pick = brow == first
                sel = jnp.logical_or(sel, pick)
                gate = jnp.where(pick, -jnp.inf, gate)
            bias = jnp.where(sel, 0.0, NEG_INF).astype(BF16)
        else:
            bias = jnp.zeros((nb, blk), BF16)
        q_aug = jnp.concatenate([qt, bias, zpad], axis=0)
        m_scr[...] = jnp.full(m_scr.shape, NEG_INF, F32)
        l_scr[...] = jnp.zeros(l_scr.shape, F32)
        acc_scr[...] = jnp.zeros(acc_scr.shape, F32)
        s_scr[0] = jnp.dot(k_aug(0), q_aug, preferred_element_type=F32)
        if i > 0:
            def body(j, c):
                s_cur = s_scr[j % 2]
                s_scr[(j + 1) % 2] = jnp.dot(k_aug(j + 1), q_aug, preferred_element_type=F32)
                absorb(s_cur, vt_ref[j])
                return c

            lax.fori_loop(0, i, body, 0)
        absorb(jnp.where(r_ix <= c_ix, s_scr[i % 2], NEG_INF), vt_ref[i])
        ot_ref[:, qs] = (acc_scr[...] / l_scr[...]).astype(ot_ref.dtype)


def _moba(q_t, k, v_t, bsz, seq):
    nb = seq // MOBA_BLOCK
    da = ATT_HEADS * HEAD_DIM
    e_t = ((jnp.arange(seq) // MOBA_BLOCK)[:, None] == jnp.arange(LANES)[None, :]).astype(BF16)
    tspec = pl.BlockSpec((HEAD_DIM, seq), lambda b, h: (h, b))
    return pl.pallas_call(
        functools.partial(_moba_kernel, nb=nb),
        out_shape=jax.ShapeDtypeStruct((da, bsz * seq), BF16),
        grid=(bsz, ATT_HEADS),
        in_specs=[tspec, pl.BlockSpec((seq, HEAD_DIM), lambda b, h: (b, h)),
                  pl.BlockSpec((nb, HEAD_DIM, MOBA_BLOCK), lambda b, h: (b, h, 0)),
                  pl.BlockSpec((seq, LANES), lambda b, h: (0, 0))],
        out_specs=tspec,
        scratch_shapes=[pltpu.VMEM((2, MOBA_BLOCK, MOBA_BLOCK), F32),
                        pltpu.VMEM((HEAD_DIM, MOBA_BLOCK), F32),
                        pltpu.VMEM((1, MOBA_BLOCK), F32), pltpu.VMEM((1, MOBA_BLOCK), F32)],
        compiler_params=_cparams("parallel", "parallel"),
        name="moba_attn",
    )(q_t, k, v_t, e_t)


def _pack_halves(x):
    k = x.shape[1] // 2
    lo = pltpu.bitcast(x[:, :k].astype(BF16).astype(F32), jnp.uint32)
    hi = pltpu.bitcast(x[:, k:].astype(BF16).astype(F32), jnp.uint32)
    return (lo >> 16) | (hi & jnp.uint32(0xFFFF0000))


def _unpack_halves(p):
    lo = pltpu.bitcast(p << 16, F32)
    hi = pltpu.bitcast(p & jnp.uint32(0xFFFF0000), F32)
    return lo, hi


def _router_kernel(h_ref, g_ref, w_ref, tri_ref, meta_ref, idx_ref, cnt_ref):
    xn = _rmsnorm(h_ref[...], g_ref[...])
    x_hi = xn.astype(BF16)
    x_lo = (xn - x_hi.astype(F32)).astype(BF16)
    t = jnp.dot(x_hi, w_ref[...], preferred_element_type=F32)
    logits = (t[:, :LANES] + t[:, LANES:]) + jnp.dot(x_lo, w_ref[:, :LANES], preferred_element_type=F32)
    tm = logits.shape[0]
    lane = lax.broadcasted_iota(jnp.int32, (tm, LANES), 1)
    ninf = -jnp.inf
    lg = jnp.where(lane < N_GROUPS, logits, ninf)
    gm = jnp.max(lg, axis=-1, keepdims=True)
    g_idx = jnp.min(jnp.where(lg == gm, lane, LANES), axis=-1, keepdims=True)
    g_gate = 1.0 / jnp.sum(jnp.exp(lg - gm), axis=-1, keepdims=True)
    lo = N_GROUPS + EXPERTS_PER_GROUP * g_idx
    le = jnp.where((lane >= lo) & (lane < lo + EXPERTS_PER_GROUP), logits, ninf)
    m1 = jnp.max(le, axis=-1, keepdims=True)
    i1 = jnp.min(jnp.where(le == m1, lane, LANES), axis=-1, keepdims=True)
    le2 = jnp.where(lane == i1, ninf, le)
    m2 = jnp.max(le2, axis=-1, keepdims=True)
    i2 = jnp.min(jnp.where(le2 == m2, lane, LANES), axis=-1, keepdims=True)
    p2 = jnp.exp(m2 - m1)
    gate1 = g_gate / (1.0 + p2)
    gate2 = g_gate * p2 / (1.0 + p2)
    e1 = (i1 - N_GROUPS).astype(F32)
    e2 = (i2 - N_GROUPS).astype(F32)
    meta = jnp.where(lane == 0, e1, jnp.where(lane == 1, e2, jnp.where(lane == 2, gate1,
                     jnp.where(lane == 3, gate2, 0.0))))
    meta_ref[...] = meta
    oh1 = (lane == i1 - N_GROUPS).astype(F32)
    oh2 = (lane == i2 - N_GROUPS).astype(F32)
    both = oh1 + oh2

    @pl.when(pl.program_id(0) == 0)
    def _():
        cnt_ref[...] = jnp.zeros_like(cnt_ref)

    seen = cnt_ref[...]
    before = jnp.dot(tri_ref[...], both.astype(BF16), preferred_element_type=F32) + seen
    r1 = jnp.sum(before * oh1, axis=-1, keepdims=True)
    r2 = jnp.sum(before * oh2, axis=-1, keepdims=True)
    slab = jnp.where(lane == 0, e1, jnp.where(lane == 1, e2, jnp.where(lane == 2, r1,
                     jnp.where(lane == 3, r2, 0.0))))
    idx_ref[...] = jnp.transpose(slab)[0:8, :].astype(jnp.int32)
    cnt_ref[...] = seen + jnp.sum(both, axis=0, keepdims=True)


def _router(h, g, w_r, tm):
    n, d = h.shape
    tri = (jnp.arange(tm)[:, None] > jnp.arange(tm)[None, :]).astype(BF16)
    return pl.pallas_call(
        _router_kernel,
        out_shape=(jax.ShapeDtypeStruct((n, LANES), F32), jax.ShapeDtypeStruct((8, n), jnp.int32),
                   jax.ShapeDtypeStruct((1, LANES), F32)),
        grid=(n // tm,),
        in_specs=[
            pl.BlockSpec((tm, d), lambda i: (i, 0)),
            pl.BlockSpec((1, d), lambda i: (0, 0)),
            pl.BlockSpec((d, 2 * LANES), lambda i: (0, 0)),
            pl.BlockSpec((tm, tm), lambda i: (0, 0)),
        ],
        out_specs=(pl.BlockSpec((tm, LANES), lambda i: (i, 0)),
                   pl.BlockSpec((8, tm), lambda i: (0, i)),
                   pl.BlockSpec((1, LANES), lambda i: (0, 0))),
        compiler_params=_cparams("arbitrary"),
        name="moe_router",
    )(h, g.reshape(1, d), w_r, tri)


MOE_ISSUE_UNROLL = 8


def _dispatch_kernel(d1_ref, d2_ref, plo_ref, pn_ref, nu_ref, h_ref, g_ref, xs_ref,
                     buf_ref, zero_ref, sem, fill_sem):
    i = pl.program_id(0)
    nsteps = pl.num_programs(0)
    tm = h_ref.shape[0]
    rb = zero_ref.shape[0]
    n_blk = xs_ref.shape[0] // rb
    slot = i % 2
    base = i * tm

    def slot_drain(s):
        cp = pltpu.make_async_copy(buf_ref.at[s], xs_ref.at[pl.ds(0, tm), :], sem.at[s])
        cp.wait()
        cp.wait()

    @pl.when(i >= 2)
    def _():
        slot_drain(slot)

    buf_ref[slot] = _pack_halves(_rmsnorm(h_ref[...], g_ref[...]))

    def issue(r8, c):
        for k in range(MOE_ISSUE_UNROLL):
            r = r8 * MOE_ISSUE_UNROLL + k
            src = buf_ref.at[slot, pl.ds(r, 1), :]
            pltpu.make_async_copy(src, xs_ref.at[pl.ds(d1_ref[base + r], 1), :], sem.at[slot]).start(priority=0)
            pltpu.make_async_copy(src, xs_ref.at[pl.ds(d2_ref[base + r], 1), :], sem.at[slot]).start(priority=1)
        return c

    lax.fori_loop(0, tm // MOE_ISSUE_UNROLL, issue, 0)

    def pad_copy(row):
        return pltpu.make_async_copy(zero_ref.at[pl.ds(0, 1), :], xs_ref.at[pl.ds(row, 1), :], fill_sem)

    def blk_copy(b):
        return pltpu.make_async_copy(zero_ref, xs_ref.at[pl.ds(b * rb, rb), :], fill_sem)

    @pl.when(i == 0)
    def _():
        zero_ref[...] = jnp.zeros_like(zero_ref)

    @pl.when(i < N_EXPERTS)
    def _():
        lo = plo_ref[i]

        def fill(r, c):
            pad_copy(lo + r).start()
            return c

        lax.fori_loop(0, pn_ref[i], fill, 0)

    @pl.when(i == N_EXPERTS)
    def _():
        def fill(b, c):
            blk_copy(b).start()
            return c

        lax.fori_loop(nu_ref[0], n_blk, fill, 0)

    @pl.when(i == nsteps - 1)
    def _():
        slot_drain(1 - slot)
        slot_drain(slot)

        def per_expert(e, c):
            def one(r, cc):
                pad_copy(0).wait()
                return cc

            return lax.fori_loop(0, pn_ref[e], one, c)

        lax.fori_loop(0, N_EXPERTS, per_expert, 0)

        def one_blk(b, c):
            blk_copy(0).wait()
            return c

        lax.fori_loop(nu_ref[0], n_blk, one_blk, 0)


def _dispatch(d1, d2, pad_lo, pad_n, n_used, h, g, cap, tm):
    n, d = h.shape
    assert n // tm > N_EXPERTS + 1
    return pl.pallas_call(
        _dispatch_kernel,
        out_shape=jax.ShapeDtypeStruct((cap, d // 2), jnp.uint32),
        grid_spec=pltpu.PrefetchScalarGridSpec(
            num_scalar_prefetch=5,
            grid=(n // tm,),
            in_specs=[
                pl.BlockSpec((tm, d), lambda i, *_: (i, 0)),
                pl.BlockSpec((1, d), lambda i, *_: (0, 0)),
            ],
            out_specs=pl.BlockSpec(memory_space=pl.ANY),
            scratch_shapes=[pltpu.VMEM((2, tm, d // 2), jnp.uint32),
                            pltpu.VMEM((MOE_ROWS, d // 2), jnp.uint32),
                            pltpu.SemaphoreType.DMA((2,)), pltpu.SemaphoreType.DMA],
        ),
        compiler_params=_cparams("arbitrary"),
        name="moe_dispatch",
    )(d1, d2, pad_lo, pad_n, n_used, h, g.reshape(1, d))


def _expert_kernel(be_ref, nu_ref, x_ref, w1_ref, w3_ref, w2_ref, y_ref, w1c, w3c, w2c):
    i = pl.program_id(0)
    used = i < nu_ref[0]
    prev = be_ref[jnp.maximum(i - 1, 0)]
    fresh = jnp.logical_or(i == 0, be_ref[i] != prev)

    @pl.when(jnp.logical_and(used, fresh))
    def _():
        w1c[...] = w1_ref[0, 0].astype(BF16)
        w3c[...] = w3_ref[0, 0].astype(BF16)
        w2c[...] = w2_ref[0, 0].astype(BF16)

    @pl.when(used)
    def _():
        lo, hi = _unpack_halves(x_ref[...])
        lo, hi = lo.astype(BF16), hi.astype(BF16)
        hd = lo.shape[1]
        hf = w1c.shape[1] // 2
        y = None
        for c in range(2):
            cs = slice(c * hf, (c + 1) * hf)
            a = (jnp.dot(lo, w1c[:hd, cs], preferred_element_type=F32)
                 + jnp.dot(hi, w1c[hd:, cs], preferred_element_type=F32))
            b = (jnp.dot(lo, w3c[:hd, cs], preferred_element_type=F32)
                 + jnp.dot(hi, w3c[hd:, cs], preferred_element_type=F32))
            act = (jax.nn.silu(a) * b).astype(BF16)
            part = jnp.dot(act, w2c[cs, :], preferred_element_type=F32)
            y = part if y is None else y + part
        y_ref[...] = _pack_halves(y)

    @pl.when(jnp.logical_not(used))
    def _():
        y_ref[...] = jnp.zeros_like(y_ref)


def _experts(blk_e, n_used, xs, w1, w3, w2, layer):
    cap, dp = xs.shape
    d = 2 * dp
    rb = MOE_ROWS
    n_blk = cap // rb
    ff = w1.shape[3]
    return pl.pallas_call(
        _expert_kernel,
        out_shape=jax.ShapeDtypeStruct((cap, dp), jnp.uint32),
        grid_spec=pltpu.PrefetchScalarGridSpec(
            num_scalar_prefetch=2,
            grid=(n_blk,),
            in_specs=[
                pl.BlockSpec((rb, dp), lambda i, be, nu: (jnp.minimum(i, jnp.maximum(nu[0] - 1, 0)), 0)),
                pl.BlockSpec((1, 1, d, ff), lambda i, be, nu: (layer, be[i], 0, 0)),
                pl.BlockSpec((1, 1, d, ff), lambda i, be, nu: (layer, be[i], 0, 0)),
                pl.BlockSpec((1, 1, ff, d), lambda i, be, nu: (layer, be[i], 0, 0)),
            ],
            out_specs=pl.BlockSpec((rb, dp), lambda i, be, nu: (i, 0)),
            scratch_shapes=[pltpu.VMEM((d, ff), BF16), pltpu.VMEM((d, ff), BF16),
                            pltpu.VMEM((ff, d), BF16)],
        ),
        compiler_params=_cparams("arbitrary"),
        name="moe_experts",
    )(blk_e, n_used, xs, w1, w3, w2)


def _combine_kernel(d1_ref, d2_ref, h_ref, meta_ref, g_ref, ys_ref, o_ref, buf_ref, sem, *, final_norm):
    i = pl.program_id(0)
    nsteps = pl.num_programs(0)
    tm = h_ref.shape[0]
    slot = i % 2

    def fetch(tile, s):
        base = tile * tm

        def issue(r8, c):
            for k in range(MOE_ISSUE_UNROLL):
                r = r8 * MOE_ISSUE_UNROLL + k
                pltpu.make_async_copy(ys_ref.at[pl.ds(d1_ref[base + r], 1), :],
                                      buf_ref.at[s, 0, pl.ds(r, 1), :], sem.at[s]).start(priority=0)
                pltpu.make_async_copy(ys_ref.at[pl.ds(d2_ref[base + r], 1), :],
                                      buf_ref.at[s, 1, pl.ds(r, 1), :], sem.at[s]).start(priority=1)
            return c

        lax.fori_loop(0, tm // MOE_ISSUE_UNROLL, issue, 0)

    @pl.when(i == 0)
    def _():
        fetch(0, 0)

    @pl.when(i + 1 < nsteps)
    def _():
        fetch(i + 1, 1 - slot)

    for j in range(2):
        pltpu.make_async_copy(ys_ref.at[pl.ds(0, tm), :], buf_ref.at[slot, j], sem.at[slot]).wait()

    meta = meta_ref[...]
    g1, g2 = meta[:, 2:3], meta[:, 3:4]
    lo1, hi1 = _unpack_halves(buf_ref[slot, 0])
    lo2, hi2 = _unpack_halves(buf_ref[slot, 1])
    hd = lo1.shape[1]
    out = jnp.concatenate([h_ref[:, :hd] + (g1 * lo1 + g2 * lo2),
                           h_ref[:, hd:] + (g1 * hi1 + g2 * hi2)], axis=1)
    if final_norm:
        out = _rmsnorm(out, g_ref[...])
    o_ref[...] = out


def _combine(d1, d2, h, meta, ys, g_final, tm, final_norm):
    n, d = h.shape
    return pl.pallas_call(
        functools.partial(_combine_kernel, final_norm=final_norm),
        out_shape=jax.ShapeDtypeStruct((n, d), F32),
        grid_spec=pltpu.PrefetchScalarGridSpec(
            num_scalar_prefetch=2,
            grid=(n // tm,),
            in_specs=[
                pl.BlockSpec((tm, d), lambda i, *_: (i, 0)),
                pl.BlockSpec((tm, LANES), lambda i, *_: (i, 0)),
                pl.BlockSpec((1, d), lambda i, *_: (0, 0)),
                pl.BlockSpec(memory_space=pl.ANY),
            ],
            out_specs=pl.BlockSpec((tm, d), lambda i, *_: (i, 0)),
            scratch_shapes=[pltpu.VMEM((2, 2, tm, d // 2), jnp.uint32), pltpu.SemaphoreType.DMA((2,))],
        ),
        compiler_params=_cparams("arbitrary"),
        name="moe_combine",
    )(d1, d2, h, meta, g_final.reshape(1, d), ys)


def _hier_moe(h, ln_g, w_group, w_expert, w1, w3, w2, layer, g_final, final_norm):
    n, d = h.shape
    rb = MOE_ROWS
    cap = 2 * n + N_EXPERTS * rb
    n_blk = cap // rb
    w_r = jnp.zeros((d, LANES), F32).at[:, :N_GROUPS].set(w_group)
    w_r = w_r.at[:, N_GROUPS:N_GROUPS + N_EXPERTS].set(w_expert)
    w_hi = w_r.astype(BF16)
    w_lo = (w_r - w_hi.astype(F32)).astype(BF16)
    meta, idx, counts = _router(h, ln_g, jnp.concatenate([w_hi, w_lo], axis=1), 512)
    cnt = counts[0, :N_EXPERTS].astype(jnp.int32)
    padded = ((cnt + rb - 1) // rb) * rb
    pad_end = jnp.cumsum(padded)
    pad_start = pad_end - padded
    blk_start = jnp.arange(n_blk, dtype=jnp.int32) * rb
    blk_e = jnp.minimum(jnp.sum(pad_end[None, :] <= blk_start[:, None], axis=1), N_EXPERTS - 1).astype(jnp.int32)
    n_used = (pad_end[-1] // rb).astype(jnp.int32).reshape(1)
    experts = jnp.arange(N_EXPERTS, dtype=jnp.int32)[None, :]
    start_of = lambda e: jnp.sum(jnp.where(e[:, None] == experts, pad_start[None, :], 0), axis=1)
    d1 = start_of(idx[0]) + idx[2]
    d2 = start_of(idx[1]) + idx[3]
    xs = _dispatch(d1, d2, pad_start + cnt, padded - cnt, n_used, h, ln_g, cap, 256)
    ys = _experts(blk_e, n_used, xs, w1, w3, w2, layer)
    return _combine(d1, d2, h, meta, ys, g_final, 256, final_norm)


def kernel(x, ln_mix, ln_ffn, ln_final, ssm_w_in, ssm_lam_re, ssm_lam_im, ssm_log_dt, ssm_b_re, ssm_b_im, ssm_c_re, ssm_c_im, ssm_d, ssm_w_out, attn_w_qkv, attn_w_o, moe_w_group, moe_w_expert, moe_w1, moe_w3, moe_w2):
    bsz, seq, d = x.shape
    n = bsz * seq
    h = x.reshape(n, d)

    u = _norm_proj(h, ln_mix[0], ssm_w_in[0].astype(BF16), 512, F32, "s5_in_proj")
    mats = _s5_mats(ssm_lam_re[0], ssm_lam_im[0], ssm_log_dt[0], ssm_b_re[0], ssm_b_im[0],
                    ssm_c_re[0], ssm_c_im[0])
    z = _s5_core(u, mats, ssm_d[0], bsz, seq)
    h = _glu_out(z, ssm_w_out[0].astype(BF16), h, 512)
    h = _hier_moe(h, ln_ffn[0], moe_w_group[0], moe_w_expert[0], moe_w1, moe_w3, moe_w2, 0,
                  ln_final, False)

    q_t, k, v_t = _qkv_proj(h, ln_mix[1], attn_w_qkv[0], 512)
    o_t = _moba(q_t, k, v_t, bsz, seq)
    h = _proj_res_t(o_t, attn_w_o[0].astype(BF16), h, 512)
    h = _hier_moe(h, ln_ffn[1], moe_w_group[1], moe_w_expert[1], moe_w1, moe_w3, moe_w2, 1,
                  ln_final, True)
    return h.reshape(bsz, seq, d)
```

```python
import functools
import math

import jax
import jax.numpy as jnp
from jax import lax
from jax.experimental import pallas as pl
from jax.experimental.pallas import tpu as pltpu

F32 = jnp.float32
BF16 = jnp.bfloat16

D_MODEL = 1024
RMS_EPS = 1e-6
NEG_INF = -1e30

SSM_GROUP = 16
SSM_GROUPS = D_MODEL // SSM_GROUP
SSM_STATE = 64
SSM_CHUNK = 16
SSM_GB = 8
SSM_WIN = 8

ATT_HEADS = 8
HEAD_DIM = 128
MOBA_BLOCK = 256
MOBA_TOPK = 3

N_GROUPS = 4
EXPERTS_PER_GROUP = 8
N_EXPERTS = 32
EXPERT_FF = 512
MOE_ROWS = 256

LANES = 128
VMEM_LIMIT = 48 * 1024 * 1024

_NT = (((1,), (1,)), ((), ()))
_TN = (((0,), (0,)), ((), ()))


def _cparams(*sem):
    return pltpu.CompilerParams(dimension_semantics=sem, vmem_limit_bytes=VMEM_LIMIT)


def _rmsnorm(x, g):
    return x * lax.rsqrt(jnp.mean(x * x, axis=-1, keepdims=True) + RMS_EPS) * g


def _norm_proj_kernel(x_ref, g_ref, w_ref, o_ref):
    xn = _rmsnorm(x_ref[...], g_ref[...]).astype(BF16)
    o_ref[...] = jnp.dot(xn, w_ref[...], preferred_element_type=F32).astype(o_ref.dtype)


def _norm_proj(x, g, w_bf16, tm, out_dtype, name):
    n, d = x.shape
    dout = w_bf16.shape[1]
    return pl.pallas_call(
        _norm_proj_kernel,
        out_shape=jax.ShapeDtypeStruct((n, dout), out_dtype),
        grid=(n // tm,),
        in_specs=[
            pl.BlockSpec((tm, d), lambda i: (i, 0)),
            pl.BlockSpec((1, d), lambda i: (0, 0)),
            pl.BlockSpec((d, dout), lambda i: (0, 0)),
        ],
        out_specs=pl.BlockSpec((tm, dout), lambda i: (i, 0)),
        compiler_params=_cparams("parallel"),
        name=name,
    )(x, g.reshape(1, d), w_bf16)


def _s5_mats(lam_re, lam_im, log_dt, b_re, b_im, c_re, c_im):
    g_, p_, c_, t_ = SSM_GROUPS, SSM_STATE, SSM_GROUP, SSM_CHUNK
    lr = jnp.minimum(lam_re, -1e-4)
    li = lam_im
    dt = jnp.exp(log_dt)[:, None]
    mag = jnp.exp(lr * dt)
    abar_re = mag * jnp.cos(li * dt)
    abar_im = mag * jnp.sin(li * dt)
    den = lr * lr + li * li
    nr = abar_re - 1.0
    gam_re = (nr * lr + abar_im * li) / den
    gam_im = (abar_im * lr - nr * li) / den
    bb_re = gam_re[..., None] * b_re - gam_im[..., None] * b_im
    bb_im = gam_re[..., None] * b_im + gam_im[..., None] * b_re

    def powers(ns):
        nf = jnp.asarray(ns, F32)[None, :, None]
        pm = jnp.exp(nf * (lr * dt)[:, None, :])
        ang = nf * (li * dt)[:, None, :]
        return pm * jnp.cos(ang), pm * jnp.sin(ang)

    pr, pi = powers(list(range(t_ + 1)))
    ca_re = c_re[:, None] * pr[:, :, None, :] - c_im[:, None] * pi[:, :, None, :]
    ca_im = c_re[:, None] * pi[:, :, None, :] + c_im[:, None] * pr[:, :, None, :]
    ca_n = jnp.concatenate([ca_re[:, :t_], ca_im[:, :t_]], axis=-1).reshape(g_, t_ * c_, 2 * p_)
    bb_s = jnp.concatenate([bb_re, -bb_im], axis=1)
    kflat = jnp.einsum('gxp,gpc->gcx', ca_n, bb_s, precision='highest')
    prs = pr[:, :t_][:, ::-1][:, :, None, :]
    pis = pi[:, :t_][:, ::-1][:, :, None, :]
    bt_re = bb_re.transpose(0, 2, 1)[:, None]
    bt_im = bb_im.transpose(0, 2, 1)[:, None]
    w_re = prs * bt_re - pis * bt_im
    w_im = prs * bt_im + pis * bt_re
    w = jnp.concatenate([w_re, w_im, w_im, w_re], axis=-1).reshape(g_, t_ * c_, 4 * p_)
    prt = pr.transpose(0, 2, 1)[:, :, 1:, None]
    pit = pi.transpose(0, 2, 1)[:, :, 1:, None]
    ct_re = c_re.transpose(0, 2, 1)[:, :, None, :]
    ct_im = c_im.transpose(0, 2, 1)[:, :, None, :]
    v_re = (ct_re * prt - ct_im * pit).reshape(g_, p_, t_ * c_)
    v_im = -(ct_re * pit + ct_im * prt).reshape(g_, p_, t_ * c_)
    v = jnp.concatenate([v_re, v_im], axis=1)
    qr, qi = powers([t_ * (1 << j) for j in range(4)])
    rows = []
    for j in range(4):
        ar, ai = qr[:, j], qi[:, j]
        rows += [jnp.concatenate([ar, ar], -1), jnp.concatenate([-ai, ai], -1),
                 jnp.concatenate([ai, -ai], -1)]
    rows += [jnp.zeros_like(rows[0])] * 4
    coef = jnp.stack(rows, axis=1)
    return kflat.astype(F32), w.astype(BF16), v.astype(BF16), coef.astype(F32)


def _s5_perm():
    r = jnp.arange(8 * LANES)
    col = ((r % LANES) // SSM_GROUP) * LANES + (r // LANES) * SSM_GROUP + r % SSM_GROUP
    p = (col[:, None] == r[None, :]).astype(BF16)
    return p, p.T


def _s5_kernel(u_ref, kf_ref, w_ref, v_ref, coef_ref, d_ref, p_ref, pt_ref, o_ref,
               m_ref, vf_ref, zf_ref, ea_ref, eb_ref, sp_ref):
    t_, c_, gb, win = SSM_CHUNK, SSM_GROUP, SSM_GB, SSM_WIN
    nk = u_ref.shape[0] // t_
    p2 = 2 * SSM_STATE
    tc = t_ * c_

    @pl.when(pl.program_id(1) == 0)
    def _():
        lane_tc = lax.broadcasted_iota(jnp.int32, (c_, tc), 1)
        for g in range(gb):
            kf = kf_ref[g]
            for s in range(t_):
                rolled = kf if s == 0 else pltpu.roll(kf, s * c_, axis=1)
                m_ref[g, s * c_:(s + 1) * c_, :] = jnp.where(lane_tc >= s * c_, rolled, 0.0).astype(BF16)

    halves = tc // LANES

    def timestep(t):
        return pl.ds(t, nk, stride=t_)

    def flat(g):
        return jnp.concatenate([vf_ref[h, :, g * LANES:(g + 1) * LANES] for h in range(halves)], axis=1)

    for h in range(halves):
        x = jnp.concatenate([u_ref[timestep(8 * h + i), :].astype(BF16) for i in range(8)], axis=1)
        vf_ref[h] = jnp.dot(x, p_ref[...], preferred_element_type=F32).astype(BF16)

    row = lax.broadcasted_iota(jnp.int32, (nk, p2), 0)

    def shift(x, d):
        return jnp.where(row < d, 0.0, pltpu.roll(x, d, axis=0))

    for g in range(gb):
        sc = jnp.dot(flat(g), w_ref[g], preferred_element_type=F32)
        xa, xb = shift(sc[:, :p2], 1), shift(sc[:, p2:], 1)
        cf = coef_ref[g]
        for lvl in range(3):
            pp, qa, qb = cf[3 * lvl:3 * lvl + 1], cf[3 * lvl + 1:3 * lvl + 2], cf[3 * lvl + 2:3 * lvl + 3]
            sa, sb = shift(xa, 1 << lvl), shift(xb, 1 << lvl)
            xa, xb = xa + pp * sa + qa * sb, xb + pp * sb + qb * sa
        ea_ref[g] = xa
        eb_ref[g] = xb

    cfs = [coef_ref[g] for g in range(gb)]
    zero = jnp.zeros((win, p2), F32)
    state = [(zero, zero)] * gb
    for j in range(nk // win):
        rs = slice(j * win, (j + 1) * win)
        for g in range(gb):
            pp, qa, qb = cfs[g][9:10], cfs[g][10:11], cfs[g][11:12]
            s_a, s_b = state[g]
            n_a = pp * s_a + qa * s_b + ea_ref[g, rs, :]
            n_b = pp * s_b + qb * s_a + eb_ref[g, rs, :]
            sp_ref[g, rs, :] = n_a
            state[g] = (n_a, n_b)

    for g in range(gb):
        x = flat(g)
        y = jnp.dot(x, m_ref[g], preferred_element_type=F32)
        y = y + jnp.dot(sp_ref[g].astype(BF16), v_ref[g], preferred_element_type=F32)
        z = jax.nn.gelu(y + d_ref[g] * x.astype(F32)).astype(BF16)
        for h in range(halves):
            zf_ref[h, :, g * LANES:(g + 1) * LANES] = z[:, h * LANES:(h + 1) * LANES]

    for h in range(halves):
        zn = jnp.dot(zf_ref[h], pt_ref[...], preferred_element_type=F32)
        for i in range(8):
            o_ref[timestep(8 * h + i), :] = zn[:, i * LANES:(i + 1) * LANES]


def _s5_core(u, mats, d_skip, bsz, seq):
    kflat, w, v, coef = mats
    n, d = u.shape
    gb, tc, p2 = SSM_GB, SSM_CHUNK * SSM_GROUP, 2 * SSM_STATE
    nk = seq // SSM_CHUNK
    halves = tc // LANES
    perm, perm_t = _s5_perm()
    dflat = jnp.tile(d_skip.reshape(SSM_GROUPS, 1, SSM_GROUP), (1, 1, SSM_CHUNK))
    spec3 = lambda a, b: pl.BlockSpec((gb, a, b), lambda j, bb: (j, 0, 0))
    const = pl.BlockSpec(perm.shape, lambda j, bb: (0, 0))
    return pl.pallas_call(
        _s5_kernel,
        out_shape=jax.ShapeDtypeStruct((n, d), F32),
        grid=(SSM_GROUPS // gb, bsz),
        in_specs=[
            pl.BlockSpec((seq, LANES), lambda j, bb: (bb, j)),
            spec3(SSM_GROUP, tc), spec3(tc, 2 * p2), spec3(p2, tc), spec3(16, p2), spec3(1, tc),
            const, const,
        ],
        out_specs=pl.BlockSpec((seq, LANES), lambda j, bb: (bb, j)),
        scratch_shapes=[pltpu.VMEM((gb, tc, tc), BF16),
                        pltpu.VMEM((halves, nk, gb * LANES), BF16),
                        pltpu.VMEM((halves, nk, gb * LANES), BF16),
                        pltpu.VMEM((gb, nk, p2), F32), pltpu.VMEM((gb, nk, p2), F32),
                        pltpu.VMEM((gb, nk, p2), F32)],
        compiler_params=_cparams("arbitrary", "arbitrary"),
        name="s5_core",
    )(u, kflat, w, v, coef, dflat, perm, perm_t)


def _glu_out_kernel(z_ref, w_ref, x_ref, o_ref):
    vg = jnp.dot(z_ref[...].astype(BF16), w_ref[...], preferred_element_type=F32)
    d = o_ref.shape[1]
    o_ref[...] = x_ref[...] + vg[:, :d] * jax.nn.sigmoid(vg[:, d:])


def _glu_out(z, w_bf16, x, tm):
    n, d = x.shape
    return pl.pallas_call(
        _glu_out_kernel,
        out_shape=jax.ShapeDtypeStruct((n, d), F32),
        grid=(n // tm,),
        in_specs=[
            pl.BlockSpec((tm, z.shape[1]), lambda i: (i, 0)),
            pl.BlockSpec(w_bf16.shape, lambda i: (0, 0)),
            pl.BlockSpec((tm, d), lambda i: (i, 0)),
        ],
        out_specs=pl.BlockSpec((tm, d), lambda i: (i, 0)),
        compiler_params=_cparams("parallel"),
        name="s5_glu_out",
    )(z, w_bf16, x)


def _proj_res_t_kernel(at_ref, w_ref, x_ref, o_ref):
    o_ref[...] = x_ref[...] + lax.dot_general(at_ref[...], w_ref[...], _TN, preferred_element_type=F32)


def _proj_res_t(a_t, w_bf16, x, tm):
    n, d = x.shape
    return pl.pallas_call(
        _proj_res_t_kernel,
        out_shape=jax.ShapeDtypeStruct((n, d), F32),
        grid=(n // tm,),
        in_specs=[
            pl.BlockSpec((a_t.shape[0], tm), lambda i: (0, i)),
            pl.BlockSpec(w_bf16.shape, lambda i: (0, 0)),
            pl.BlockSpec((tm, d), lambda i: (i, 0)),
        ],
        out_specs=pl.BlockSpec((tm, d), lambda i: (i, 0)),
        compiler_params=_cparams("parallel"),
        name="attn_out_proj",
    )(a_t, w_bf16, x)


def _qkv_kernel(x_ref, g_ref, wq_ref, wk_ref, wv_ref, qt_ref, k_ref, vt_ref):
    xn = _rmsnorm(x_ref[...], g_ref[...]).astype(BF16)
    c = (HEAD_DIM ** -0.5) * math.log2(math.e)
    qt_ref[...] = (lax.dot_general(wq_ref[...], xn, _NT, preferred_element_type=F32) * c).astype(BF16)
    k_ref[...] = jnp.dot(xn, wk_ref[...], preferred_element_type=F32).astype(BF16)
    vt = lax.dot_general(wv_ref[...], xn, _NT, preferred_element_type=F32).astype(BF16)
    for c in range(vt_ref.shape[0]):
        vt_ref[c] = vt[:, c * MOBA_BLOCK:(c + 1) * MOBA_BLOCK]


def _qkv_proj(x, g, w_qkv, tm):
    n, d = x.shape
    da = ATT_HEADS * HEAD_DIM
    wq_t = w_qkv[:, :da].T.astype(BF16)
    wk = w_qkv[:, da:2 * da].astype(BF16)
    wv_t = w_qkv[:, 2 * da:].T.astype(BF16)
    full = lambda shp: pl.BlockSpec(shp, lambda i: (0, 0))
    tb = tm // MOBA_BLOCK
    return pl.pallas_call(
        _qkv_kernel,
        out_shape=(jax.ShapeDtypeStruct((da, n), BF16), jax.ShapeDtypeStruct((n, da), BF16),
                   jax.ShapeDtypeStruct((n // MOBA_BLOCK, da, MOBA_BLOCK), BF16)),
        grid=(n // tm,),
        in_specs=[pl.BlockSpec((tm, d), lambda i: (i, 0)), full((1, d)),
                  full((da, d)), full((d, da)), full((da, d))],
        out_specs=(pl.BlockSpec((da, tm), lambda i: (0, i)), pl.BlockSpec((tm, da), lambda i: (i, 0)),
                   pl.BlockSpec((tb, da, MOBA_BLOCK), lambda i: (i, 0, 0))),
        compiler_params=_cparams("parallel"),
        name="attn_qkv_proj",
    )(x, g.reshape(1, d), wq_t, wk, wv_t)


MOBA_CHAINS = 4
MOBA_SUM_ROWS = 16


def _moba_kernel(qt_ref, k_ref, vt_ref, et_ref, ot_ref, s_scr, acc_scr, m_scr, *, nb):
    blk, dh = MOBA_BLOCK, HEAD_DIM
    brow = lax.broadcasted_iota(jnp.int32, (nb, blk), 0)
    r_ix = lax.broadcasted_iota(jnp.int32, (blk, blk), 0)
    c_ix = lax.broadcasted_iota(jnp.int32, (blk, blk), 1)
    zpad = jnp.zeros((2 * LANES - dh - nb, blk), BF16)
    kmean = jnp.mean(k_ref[...].astype(F32).reshape(nb, blk, dh), axis=1)

    def k_aug(t):
        rows = pl.ds(pl.multiple_of(t * blk, blk), blk)
        return jnp.concatenate([k_ref[rows, :], et_ref[rows, :]], axis=1)

    ones = jnp.ones((MOBA_SUM_ROWS, blk), BF16)

    def values(t):
        return jnp.concatenate([vt_ref[t], ones], axis=0)

    def absorb(c, s, vt):
        m_old = m_scr[c]
        m_new = jnp.maximum(m_old, jnp.max(s, axis=0, keepdims=True))
        p = jnp.exp2(s - m_new).astype(BF16)
        acc_scr[c] = jnp.exp2(m_old - m_new) * acc_scr[c] + jnp.dot(vt, p, preferred_element_type=F32)
        m_scr[c] = m_new

    def query_block(i):
        qs = slice(i * blk, (i + 1) * blk)
        qt = qt_ref[:, qs]
        if i > MOBA_TOPK:
            gate = jnp.dot(kmean, qt.astype(F32), preferred_element_type=F32,
                           precision=lax.Precision.HIGHEST)
            gate = jnp.where(brow < i, gate, NEG_INF)
            sel = brow >= i
            for _ in range(MOBA_TOPK):
                gm = jnp.max(gate, axis=0, keepdims=True)
                first = jnp.min(jnp.where(gate == gm, brow, nb), axis=0, keepdims=True)
                pick = brow == first
                sel = jnp.logical_or(sel, pick)
                gate = jnp.where(pick, -jnp.inf, gate)
            bias = jnp.where(sel, 0.0, NEG_INF).astype(BF16)
        else:
            bias = jnp.zeros((nb, blk), BF16)
        return jnp.concatenate([qt, bias, zpad], axis=0)

    nq = MOBA_CHAINS
    for i0 in range(0, nb, nq):
        chains = range(min(nq, nb - i0))
        q_aug = [query_block(i0 + c) for c in chains]
        m_scr[...] = jnp.full(m_scr.shape, NEG_INF, F32)
        acc_scr[...] = jnp.zeros(acc_scr.shape, F32)

        def scores(t, slot, cs):
            ka = k_aug(t)
            for c in cs:
                s_scr[c, slot] = jnp.dot(ka, q_aug[c], preferred_element_type=F32)

        scores(0, 0, chains)
        if i0 > 0:
            def body(j, carry):
                cur = [s_scr[c, j % 2] for c in chains]
                scores(j + 1, (j + 1) % 2, chains)
                vt = values(j)
                for c in chains:
                    absorb(c, cur[c], vt)
                return carry

            lax.fori_loop(0, i0, body, 0)
        for t in chains:
            cur = {c: s_scr[c, (i0 + t) % 2] for c in chains if c >= t}
            later = [c for c in chains if c > t]
            if later:
                scores(i0 + t + 1, (i0 + t + 1) % 2, later)
            vt = values(i0 + t)
            absorb(t, jnp.where(r_ix <= c_ix, cur[t], NEG_INF), vt)
            for c in later:
                absorb(c, cur[c], vt)
        for c in chains:
            qs = slice((i0 + c) * blk, (i0 + c + 1) * blk)
            ot_ref[:, qs] = (acc_scr[c, :dh, :] / acc_scr[c, dh:dh + 1, :]).astype(ot_ref.dtype)


def _moba(q_t, k, v_t, bsz, seq):
    nb = seq // MOBA_BLOCK
    da = ATT_HEADS * HEAD_DIM
    e_t = ((jnp.arange(seq) // MOBA_BLOCK)[:, None] == jnp.arange(LANES)[None, :]).astype(BF16)
    tspec = pl.BlockSpec((HEAD_DIM, seq), lambda b, h: (h, b))
    return pl.pallas_call(
        functools.partial(_moba_kernel, nb=nb),
        out_shape=jax.ShapeDtypeStruct((da, bsz * seq), BF16),
        grid=(bsz, ATT_HEADS),
        in_specs=[tspec, pl.BlockSpec((seq, HEAD_DIM), lambda b, h: (b, h)),
                  pl.BlockSpec((nb, HEAD_DIM, MOBA_BLOCK), lambda b, h: (b, h, 0)),
                  pl.BlockSpec((seq, LANES), lambda b, h: (0, 0))],
        out_specs=tspec,
        scratch_shapes=[pltpu.VMEM((MOBA_CHAINS, 2, MOBA_BLOCK, MOBA_BLOCK), F32),
                        pltpu.VMEM((MOBA_CHAINS, HEAD_DIM + MOBA_SUM_ROWS, MOBA_BLOCK), F32),
                        pltpu.VMEM((MOBA_CHAINS, 1, MOBA_BLOCK), F32)],
        compiler_params=_cparams("parallel", "parallel"),
        name="moba_attn",
    )(q_t, k, v_t, e_t)


def _pack_halves(x):
    k = x.shape[1] // 2
    lo = pltpu.bitcast(x[:, :k].astype(BF16).astype(F32), jnp.uint32)
    hi = pltpu.bitcast(x[:, k:].astype(BF16).astype(F32), jnp.uint32)
    return (lo >> 16) | (hi & jnp.uint32(0xFFFF0000))


def _unpack_halves(p):
    lo = pltpu.bitcast(p << 16, F32)
    hi = pltpu.bitcast(p & jnp.uint32(0xFFFF0000), F32)
    return lo, hi


def _router_kernel(h_ref, g_ref, w_ref, tri_ref, meta_ref, idx_ref, cnt_ref):
    xn = _rmsnorm(h_ref[...], g_ref[...])
    x_hi = xn.astype(BF16)
    x_lo = (xn - x_hi.astype(F32)).astype(BF16)
    t = jnp.dot(x_hi, w_ref[...], preferred_element_type=F32)
    logits = (t[:, :LANES] + t[:, LANES:]) + jnp.dot(x_lo, w_ref[:, :LANES], preferred_element_type=F32)
    tm = logits.shape[0]
    lane = lax.broadcasted_iota(jnp.int32, (tm, LANES), 1)
    ninf = -jnp.inf
    lg = jnp.where(lane < N_GROUPS, logits, ninf)
    gm = jnp.max(lg, axis=-1, keepdims=True)
    g_idx = jnp.min(jnp.where(lg == gm, lane, LANES), axis=-1, keepdims=True)
    g_gate = 1.0 / jnp.sum(jnp.exp(lg - gm), axis=-1, keepdims=True)
    lo = N_GROUPS + EXPERTS_PER_GROUP * g_idx
    le = jnp.where((lane >= lo) & (lane < lo + EXPERTS_PER_GROUP), logits, ninf)
    m1 = jnp.max(le, axis=-1, keepdims=True)
    i1 = jnp.min(jnp.where(le == m1, lane, LANES), axis=-1, keepdims=True)
    le2 = jnp.where(lane == i1, ninf, le)
    m2 = jnp.max(le2, axis=-1, keepdims=True)
    i2 = jnp.min(jnp.where(le2 == m2, lane, LANES), axis=-1, keepdims=True)
    p2 = jnp.exp(m2 - m1)
    gate1 = g_gate / (1.0 + p2)
    gate2 = g_gate * p2 / (1.0 + p2)
    e1 = (i1 - N_GROUPS).astype(F32)
    e2 = (i2 - N_GROUPS).astype(F32)
    meta = jnp.where(lane == 0, e1, jnp.where(lane == 1, e2, jnp.where(lane == 2, gate1,
                     jnp.where(lane == 3, gate2, 0.0))))
    meta_ref[...] = meta
    oh1 = (lane == i1 - N_GROUPS).astype(F32)
    oh2 = (lane == i2 - N_GROUPS).astype(F32)
    both = oh1 + oh2

    @pl.when(pl.program_id(0) == 0)
    def _():
        cnt_ref[...] = jnp.zeros_like(cnt_ref)

    seen = cnt_ref[...]
    before = jnp.dot(tri_ref[...], both.astype(BF16), preferred_element_type=F32) + seen
    r1 = jnp.sum(before * oh1, axis=-1, keepdims=True)
    r2 = jnp.sum(before * oh2, axis=-1, keepdims=True)
    slab = jnp.where(lane == 0, e1, jnp.where(lane == 1, e2, jnp.where(lane == 2, r1,
                     jnp.where(lane == 3, r2, 0.0))))
    idx_ref[...] = jnp.transpose(slab)[0:8, :].astype(jnp.int32)
    cnt_ref[...] = seen + jnp.sum(both, axis=0, keepdims=True)


def _router(h, g, w_r, tm):
    n, d = h.shape
    tri = (jnp.arange(tm)[:, None] > jnp.arange(tm)[None, :]).astype(BF16)
    return pl.pallas_call(
        _router_kernel,
        out_shape=(jax.ShapeDtypeStruct((n, LANES), F32), jax.ShapeDtypeStruct((8, n), jnp.int32),
                   jax.ShapeDtypeStruct((1, LANES), F32)),
        grid=(n // tm,),
        in_specs=[
            pl.BlockSpec((tm, d), lambda i: (i, 0)),
            pl.BlockSpec((1, d), lambda i: (0, 0)),
            pl.BlockSpec((d, 2 * LANES), lambda i: (0, 0)),
            pl.BlockSpec((tm, tm), lambda i: (0, 0)),
        ],
        out_specs=(pl.BlockSpec((tm, LANES), lambda i: (i, 0)),
                   pl.BlockSpec((8, tm), lambda i: (0, i)),
                   pl.BlockSpec((1, LANES), lambda i: (0, 0))),
        compiler_params=_cparams("arbitrary"),
        name="moe_router",
    )(h, g.reshape(1, d), w_r, tri)


MOE_ISSUE_UNROLL = 8


def _dispatch_kernel(d1_ref, d2_ref, plo_ref, pn_ref, nu_ref, h_ref, g_ref, xs_ref,
                     buf_ref, zero_ref, sem, fill_sem):
    i = pl.program_id(0)
    nsteps = pl.num_programs(0)
    tm = h_ref.shape[0]
    rb = zero_ref.shape[0]
    n_blk = xs_ref.shape[0] // rb
    slot = i % 2
    base = i * tm

    def slot_drain(s):
        cp = pltpu.make_async_copy(buf_ref.at[s], xs_ref.at[pl.ds(0, tm), :], sem.at[s])
        cp.wait()
        cp.wait()

    @pl.when(i >= 2)
    def _():
        slot_drain(slot)

    buf_ref[slot] = _pack_halves(_rmsnorm(h_ref[...], g_ref[...]))

    def issue(r8, c):
        for k in range(MOE_ISSUE_UNROLL):
            r = r8 * MOE_ISSUE_UNROLL + k
            src = buf_ref.at[slot, pl.ds(r, 1), :]
            pltpu.make_async_copy(src, xs_ref.at[pl.ds(d1_ref[base + r], 1), :], sem.at[slot]).start(priority=0)
            pltpu.make_async_copy(src, xs_ref.at[pl.ds(d2_ref[base + r], 1), :], sem.at[slot]).start(priority=1)
        return c

    lax.fori_loop(0, tm // MOE_ISSUE_UNROLL, issue, 0)

    def pad_copy(row):
        return pltpu.make_async_copy(zero_ref.at[pl.ds(0, 1), :], xs_ref.at[pl.ds(row, 1), :], fill_sem)

    def blk_copy(b):
        return pltpu.make_async_copy(zero_ref, xs_ref.at[pl.ds(b * rb, rb), :], fill_sem)

    @pl.when(i == 0)
    def _():
        zero_ref[...] = jnp.zeros_like(zero_ref)

    @pl.when(i < N_EXPERTS)
    def _():
        lo = plo_ref[i]

        def fill(r, c):
            pad_copy(lo + r).start()
            return c

        lax.fori_loop(0, pn_ref[i], fill, 0)

    @pl.when(i == N_EXPERTS)
    def _():
        def fill(b, c):
            blk_copy(b).start()
            return c

        lax.fori_loop(nu_ref[0], n_blk, fill, 0)

    @pl.when(i == nsteps - 1)
    def _():
        slot_drain(1 - slot)
        slot_drain(slot)

        def per_expert(e, c):
            def one(r, cc):
                pad_copy(0).wait()
                return cc

            return lax.fori_loop(0, pn_ref[e], one, c)

        lax.fori_loop(0, N_EXPERTS, per_expert, 0)

        def one_blk(b, c):
            blk_copy(0).wait()
            return c

        lax.fori_loop(nu_ref[0], n_blk, one_blk, 0)


def _dispatch(d1, d2, pad_lo, pad_n, n_used, h, g, cap, tm):
    n, d = h.shape
    assert n // tm > N_EXPERTS + 1
    return pl.pallas_call(
        _dispatch_kernel,
        out_shape=jax.ShapeDtypeStruct((cap, d // 2), jnp.uint32),
        grid_spec=pltpu.PrefetchScalarGridSpec(
            num_scalar_prefetch=5,
            grid=(n // tm,),
            in_specs=[
                pl.BlockSpec((tm, d), lambda i, *_: (i, 0)),
                pl.BlockSpec((1, d), lambda i, *_: (0, 0)),
            ],
            out_specs=pl.BlockSpec(memory_space=pl.ANY),
            scratch_shapes=[pltpu.VMEM((2, tm, d // 2), jnp.uint32),
                            pltpu.VMEM((MOE_ROWS, d // 2), jnp.uint32),
                            pltpu.SemaphoreType.DMA((2,)), pltpu.SemaphoreType.DMA],
        ),
        compiler_params=_cparams("arbitrary"),
        name="moe_dispatch",
    )(d1, d2, pad_lo, pad_n, n_used, h, g.reshape(1, d))


def _expert_kernel(be_ref, nu_ref, x_ref, w1_ref, w3_ref, w2_ref, y_ref, w1c, w3c, w2c):
    i = pl.program_id(0)
    used = i < nu_ref[0]
    prev = be_ref[jnp.maximum(i - 1, 0)]
    fresh = jnp.logical_or(i == 0, be_ref[i] != prev)

    @pl.when(jnp.logical_and(used, fresh))
    def _():
        w1c[...] = w1_ref[0, 0].astype(BF16)
        w3c[...] = w3_ref[0, 0].astype(BF16)
        w2c[...] = w2_ref[0, 0].astype(BF16)

    @pl.when(used)
    def _():
        lo, hi = _unpack_halves(x_ref[...])
        lo, hi = lo.astype(BF16), hi.astype(BF16)
        hd = lo.shape[1]
        hf = w1c.shape[1] // 2
        y = None
        for c in range(2):
            cs = slice(c * hf, (c + 1) * hf)
            a = (jnp.dot(lo, w1c[:hd, cs], preferred_element_type=F32)
                 + jnp.dot(hi, w1c[hd:, cs], preferred_element_type=F32))
            b = (jnp.dot(lo, w3c[:hd, cs], preferred_element_type=F32)
                 + jnp.dot(hi, w3c[hd:, cs], preferred_element_type=F32))
            act = (jax.nn.silu(a) * b).astype(BF16)
            part = jnp.dot(act, w2c[cs, :], preferred_element_type=F32)
            y = part if y is None else y + part
        y_ref[...] = _pack_halves(y)

    @pl.when(jnp.logical_not(used))
    def _():
        y_ref[...] = jnp.zeros_like(y_ref)


def _experts(blk_e, n_used, xs, w1, w3, w2, layer):
    cap, dp = xs.shape
    d = 2 * dp
    rb = MOE_ROWS
    n_blk = cap // rb
    ff = w1.shape[3]
    return pl.pallas_call(
        _expert_kernel,
        out_shape=jax.ShapeDtypeStruct((cap, dp), jnp.uint32),
        grid_spec=pltpu.PrefetchScalarGridSpec(
            num_scalar_prefetch=2,
            grid=(n_blk,),
            in_specs=[
                pl.BlockSpec((rb, dp), lambda i, be, nu: (jnp.minimum(i, jnp.maximum(nu[0] - 1, 0)), 0)),
                pl.BlockSpec((1, 1, d, ff), lambda i, be, nu: (layer, be[i], 0, 0)),
                pl.BlockSpec((1, 1, d, ff), lambda i, be, nu: (layer, be[i], 0, 0)),
                pl.BlockSpec((1, 1, ff, d), lambda i, be, nu: (layer, be[i], 0, 0)),
            ],
            out_specs=pl.BlockSpec((rb, dp), lambda i, be, nu: (i, 0)),
            scratch_shapes=[pltpu.VMEM((d, ff), BF16), pltpu.VMEM((d, ff), BF16),
                            pltpu.VMEM((ff, d), BF16)],
        ),
        compiler_params=_cparams("arbitrary"),
        name="moe_experts",
    )(blk_e, n_used, xs, w1, w3, w2)


def _combine_kernel(d1_ref, d2_ref, h_ref, meta_ref, g_ref, ys_ref, o_ref, buf_ref, sem, *, final_norm):
    i = pl.program_id(0)
    nsteps = pl.num_programs(0)
    tm = h_ref.shape[0]
    slot = i % 2

    def fetch(tile, s):
        base = tile * tm

        def issue(r8, c):
            for k in range(MOE_ISSUE_UNROLL):
                r = r8 * MOE_ISSUE_UNROLL + k
                pltpu.make_async_copy(ys_ref.at[pl.ds(d1_ref[base + r], 1), :],
                                      buf_ref.at[s, 0, pl.ds(r, 1), :], sem.at[s]).start(priority=0)
                pltpu.make_async_copy(ys_ref.at[pl.ds(d2_ref[base + r], 1), :],
                                      buf_ref.at[s, 1, pl.ds(r, 1), :], sem.at[s]).start(priority=1)
            return c

        lax.fori_loop(0, tm // MOE_ISSUE_UNROLL, issue, 0)

    @pl.when(i == 0)
    def _():
        fetch(0, 0)

    @pl.when(i + 1 < nsteps)
    def _():
        fetch(i + 1, 1 - slot)

    for j in range(2):
        pltpu.make_async_copy(ys_ref.at[pl.ds(0, tm), :], buf_ref.at[slot, j], sem.at[slot]).wait()

    meta = meta_ref[...]
    g1, g2 = meta[:, 2:3], meta[:, 3:4]
    lo1, hi1 = _unpack_halves(buf_ref[slot, 0])
    lo2, hi2 = _unpack_halves(buf_ref[slot, 1])
    hd = lo1.shape[1]
    out = jnp.concatenate([h_ref[:, :hd] + (g1 * lo1 + g2 * lo2),
                           h_ref[:, hd:] + (g1 * hi1 + g2 * hi2)], axis=1)
    if final_norm:
        out = _rmsnorm(out, g_ref[...])
    o_ref[...] = out


def _combine(d1, d2, h, meta, ys, g_final, tm, final_norm):
    n, d = h.shape
    return pl.pallas_call(
        functools.partial(_combine_kernel, final_norm=final_norm),
        out_shape=jax.ShapeDtypeStruct((n, d), F32),
        grid_spec=pltpu.PrefetchScalarGridSpec(
            num_scalar_prefetch=2,
            grid=(n // tm,),
            in_specs=[
                pl.BlockSpec((tm, d), lambda i, *_: (i, 0)),
                pl.BlockSpec((tm, LANES), lambda i, *_: (i, 0)),
                pl.BlockSpec((1, d), lambda i, *_: (0, 0)),
                pl.BlockSpec(memory_space=pl.ANY),
            ],
            out_specs=pl.BlockSpec((tm, d), lambda i, *_: (i, 0)),
            scratch_shapes=[pltpu.VMEM((2, 2, tm, d // 2), jnp.uint32), pltpu.SemaphoreType.DMA((2,))],
        ),
        compiler_params=_cparams("arbitrary"),
        name="moe_combine",
    )(d1, d2, h, meta, g_final.reshape(1, d), ys)


def _hier_moe(h, ln_g, w_group, w_expert, w1, w3, w2, layer, g_final, final_norm):
    n, d = h.shape
    rb = MOE_ROWS
    cap = 2 * n + N_EXPERTS * rb
    n_blk = cap // rb
    w_r = jnp.zeros((d, LANES), F32).at[:, :N_GROUPS].set(w_group)
    w_r = w_r.at[:, N_GROUPS:N_GROUPS + N_EXPERTS].set(w_expert)
    w_hi = w_r.astype(BF16)
    w_lo = (w_r - w_hi.astype(F32)).astype(BF16)
    meta, idx, counts = _router(h, ln_g, jnp.concatenate([w_hi, w_lo], axis=1), 512)
    cnt = counts[0, :N_EXPERTS].astype(jnp.int32)
    padded = ((cnt + rb - 1) // rb) * rb
    pad_end = jnp.cumsum(padded)
    pad_start = pad_end - padded
    blk_start = jnp.arange(n_blk, dtype=jnp.int32) * rb
    blk_e = jnp.minimum(jnp.sum(pad_end[None, :] <= blk_start[:, None], axis=1), N_EXPERTS - 1).astype(jnp.int32)
    n_used = (pad_end[-1] // rb).astype(jnp.int32).reshape(1)
    experts = jnp.arange(N_EXPERTS, dtype=jnp.int32)[None, :]
    start_of = lambda e: jnp.sum(jnp.where(e[:, None] == experts, pad_start[None, :], 0), axis=1)
    d1 = start_of(idx[0]) + idx[2]
    d2 = start_of(idx[1]) + idx[3]
    xs = _dispatch(d1, d2, pad_start + cnt, padded - cnt, n_used, h, ln_g, cap, 256)
    ys = _experts(blk_e, n_used, xs, w1, w3, w2, layer)
    return _combine(d1, d2, h, meta, ys, g_final, 256, final_norm)


def kernel(x, ln_mix, ln_ffn, ln_final, ssm_w_in, ssm_lam_re, ssm_lam_im, ssm_log_dt, ssm_b_re, ssm_b_im, ssm_c_re, ssm_c_im, ssm_d, ssm_w_out, attn_w_qkv, attn_w_o, moe_w_group, moe_w_expert, moe_w1, moe_w3, moe_w2):
    bsz, seq, d = x.shape
    n = bsz * seq
    h = x.reshape(n, d)

    u = _norm_proj(h, ln_mix[0], ssm_w_in[0].astype(BF16), 512, F32, "s5_in_proj")
    mats = _s5_mats(ssm_lam_re[0], ssm_lam_im[0], ssm_log_dt[0], ssm_b_re[0], ssm_b_im[0],
                    ssm_c_re[0], ssm_c_im[0])
    z = _s5_core(u, mats, ssm_d[0], bsz, seq)
    h = _glu_out(z, ssm_w_out[0].astype(BF16), h, 512)
    h = _hier_moe(h, ln_ffn[0], moe_w_group[0], moe_w_expert[0], moe_w1, moe_w3, moe_w2, 0,
                  ln_final, False)

    q_t, k, v_t = _qkv_proj(h, ln_mix[1], attn_w_qkv[0], 512)
    o_t = _moba(q_t, k, v_t, bsz, seq)
    h = _proj_res_t(o_t, attn_w_o[0].astype(BF16), h, 512)
    h = _hier_moe(h, ln_ffn[1], moe_w_group[1], moe_w_expert[1], moe_w1, moe_w3, moe_w2, 1,
                  ln_final, True)
    return h.reshape(bsz, seq, d)
```

```python
import functools
import math

import jax
import jax.numpy as jnp
from jax import lax
from jax.experimental import pallas as pl
from jax.experimental.pallas import tpu as pltpu

F32 = jnp.float32
BF16 = jnp.bfloat16

D_MODEL = 1024
RMS_EPS = 1e-6
NEG_INF = -1e30

SSM_GROUP = 16
SSM_GROUPS = D_MODEL // SSM_GROUP
SSM_STATE = 64
SSM_CHUNK = 16
SSM_GB = 8
SSM_WIN = 8

ATT_HEADS = 8
HEAD_DIM = 128
MOBA_BLOCK = 256
MOBA_TOPK = 3

N_GROUPS = 4
EXPERTS_PER_GROUP = 8
N_EXPERTS = 32
EXPERT_FF = 512
MOE_ROWS = 256

LANES = 128
VMEM_LIMIT = 48 * 1024 * 1024

_NT = (((1,), (1,)), ((), ()))
_TN = (((0,), (0,)), ((), ()))


def _cparams(*sem):
    return pltpu.CompilerParams(dimension_semantics=sem, vmem_limit_bytes=VMEM_LIMIT)


def _rmsnorm(x, g):
    return x * lax.rsqrt(jnp.mean(x * x, axis=-1, keepdims=True) + RMS_EPS) * g


def _norm_proj_kernel(x_ref, g_ref, w_ref, o_ref):
    xn = _rmsnorm(x_ref[...], g_ref[...]).astype(BF16)
    o_ref[...] = jnp.dot(xn, w_ref[...], preferred_element_type=F32).astype(o_ref.dtype)


def _norm_proj(x, g, w_bf16, tm, out_dtype, name):
    n, d = x.shape
    dout = w_bf16.shape[1]
    return pl.pallas_call(
        _norm_proj_kernel,
        out_shape=jax.ShapeDtypeStruct((n, dout), out_dtype),
        grid=(n // tm,),
        in_specs=[
            pl.BlockSpec((tm, d), lambda i: (i, 0)),
            pl.BlockSpec((1, d), lambda i: (0, 0)),
            pl.BlockSpec((d, dout), lambda i: (0, 0)),
        ],
        out_specs=pl.BlockSpec((tm, dout), lambda i: (i, 0)),
        compiler_params=_cparams("parallel"),
        name=name,
    )(x, g.reshape(1, d), w_bf16)


def _s5_mats(lam_re, lam_im, log_dt, b_re, b_im, c_re, c_im):
    g_, p_, c_, t_ = SSM_GROUPS, SSM_STATE, SSM_GROUP, SSM_CHUNK
    lr = jnp.minimum(lam_re, -1e-4)
    li = lam_im
    dt = jnp.exp(log_dt)[:, None]
    mag = jnp.exp(lr * dt)
    abar_re = mag * jnp.cos(li * dt)
    abar_im = mag * jnp.sin(li * dt)
    den = lr * lr + li * li
    nr = abar_re - 1.0
    gam_re = (nr * lr + abar_im * li) / den
    gam_im = (abar_im * lr - nr * li) / den
    bb_re = gam_re[..., None] * b_re - gam_im[..., None] * b_im
    bb_im = gam_re[..., None] * b_im + gam_im[..., None] * b_re

    def powers(ns):
        nf = jnp.asarray(ns, F32)[None, :, None]
        pm = jnp.exp(nf * (lr * dt)[:, None, :])
        ang = nf * (li * dt)[:, None, :]
        return pm * jnp.cos(ang), pm * jnp.sin(ang)

    pr, pi = powers(list(range(t_ + 1)))
    ca_re = c_re[:, None] * pr[:, :, None, :] - c_im[:, None] * pi[:, :, None, :]
    ca_im = c_re[:, None] * pi[:, :, None, :] + c_im[:, None] * pr[:, :, None, :]
    ca_n = jnp.concatenate([ca_re[:, :t_], ca_im[:, :t_]], axis=-1).reshape(g_, t_ * c_, 2 * p_)
    bb_s = jnp.concatenate([bb_re, -bb_im], axis=1)
    kflat = jnp.einsum('gxp,gpc->gcx', ca_n, bb_s, precision='highest')
    prs = pr[:, :t_][:, ::-1][:, :, None, :]
    pis = pi[:, :t_][:, ::-1][:, :, None, :]
    bt_re = bb_re.transpose(0, 2, 1)[:, None]
    bt_im = bb_im.transpose(0, 2, 1)[:, None]
    w_re = prs * bt_re - pis * bt_im
    w_im = prs * bt_im + pis * bt_re
    w = jnp.concatenate([w_re, w_im, w_im, w_re], axis=-1).reshape(g_, t_ * c_, 4 * p_)
    prt = pr.transpose(0, 2, 1)[:, :, 1:, None]
    pit = pi.transpose(0, 2, 1)[:, :, 1:, None]
    ct_re = c_re.transpose(0, 2, 1)[:, :, None, :]
    ct_im = c_im.transpose(0, 2, 1)[:, :, None, :]
    v_re = (ct_re * prt - ct_im * pit).reshape(g_, p_, t_ * c_)
    v_im = -(ct_re * pit + ct_im * prt).reshape(g_, p_, t_ * c_)
    v = jnp.concatenate([v_re, v_im], axis=1)
    qr, qi = powers([t_ * (1 << j) for j in range(4)])
    rows = []
    for j in range(4):
        ar, ai = qr[:, j], qi[:, j]
        rows += [jnp.concatenate([ar, ar], -1), jnp.concatenate([-ai, ai], -1),
                 jnp.concatenate([ai, -ai], -1)]
    rows += [jnp.zeros_like(rows[0])] * 4
    coef = jnp.stack(rows, axis=1)
    return kflat.astype(F32), w.astype(BF16), v.astype(BF16), coef.astype(F32)


def _s5_perm():
    r = jnp.arange(8 * LANES)
    col = ((r % LANES) // SSM_GROUP) * LANES + (r // LANES) * SSM_GROUP + r % SSM_GROUP
    p = (col[:, None] == r[None, :]).astype(BF16)
    return p, p.T


def _s5_kernel(u_ref, kf_ref, w_ref, v_ref, coef_ref, d_ref, p_ref, pt_ref, o_ref,
               m_ref, vf_ref, zf_ref, ea_ref, eb_ref, sp_ref):
    t_, c_, gb, win = SSM_CHUNK, SSM_GROUP, SSM_GB, SSM_WIN
    nk = u_ref.shape[0] // t_
    p2 = 2 * SSM_STATE
    tc = t_ * c_

    @pl.when(pl.program_id(1) == 0)
    def _():
        lane_tc = lax.broadcasted_iota(jnp.int32, (c_, tc), 1)
        for g in range(gb):
            kf = kf_ref[g]
            for s in range(t_):
                rolled = kf if s == 0 else pltpu.roll(kf, s * c_, axis=1)
                m_ref[g, s * c_:(s + 1) * c_, :] = jnp.where(lane_tc >= s * c_, rolled, 0.0).astype(BF16)

    halves = tc // LANES

    def timestep(t):
        return pl.ds(t, nk, stride=t_)

    def flat(g):
        return jnp.concatenate([vf_ref[h, :, g * LANES:(g + 1) * LANES] for h in range(halves)], axis=1)

    for h in range(halves):
        x = jnp.concatenate([u_ref[timestep(8 * h + i), :].astype(BF16) for i in range(8)], axis=1)
        vf_ref[h] = jnp.dot(x, p_ref[...], preferred_element_type=F32).astype(BF16)

    row = lax.broadcasted_iota(jnp.int32, (nk, p2), 0)

    def shift(x, d):
        return jnp.where(row < d, 0.0, pltpu.roll(x, d, axis=0))

    for g in range(gb):
        sc = jnp.dot(flat(g), w_ref[g], preferred_element_type=F32)
        xa, xb = shift(sc[:, :p2], 1), shift(sc[:, p2:], 1)
        cf = coef_ref[g]
        for lvl in range(3):
            pp, qa, qb = cf[3 * lvl:3 * lvl + 1], cf[3 * lvl + 1:3 * lvl + 2], cf[3 * lvl + 2:3 * lvl + 3]
            sa, sb = shift(xa, 1 << lvl), shift(xb, 1 << lvl)
            xa, xb = xa + pp * sa + qa * sb, xb + pp * sb + qb * sa
        ea_ref[g] = xa
        eb_ref[g] = xb

    cfs = [coef_ref[g] for g in range(gb)]
    zero = jnp.zeros((win, p2), F32)
    state = [(zero, zero)] * gb
    for j in range(nk // win):
        rs = slice(j * win, (j + 1) * win)
        for g in range(gb):
            pp, qa, qb = cfs[g][9:10], cfs[g][10:11], cfs[g][11:12]
            s_a, s_b = state[g]
            n_a = pp * s_a + qa * s_b + ea_ref[g, rs, :]
            n_b = pp * s_b + qb * s_a + eb_ref[g, rs, :]
            sp_ref[g, rs, :] = n_a
            state[g] = (n_a, n_b)

    for g in range(gb):
        x = flat(g)
        y = jnp.dot(x, m_ref[g], preferred_element_type=F32)
        y = y + jnp.dot(sp_ref[g].astype(BF16), v_ref[g], preferred_element_type=F32)
        z = jax.nn.gelu(y + d_ref[g] * x.astype(F32)).astype(BF16)
        for h in range(halves):
            zf_ref[h, :, g * LANES:(g + 1) * LANES] = z[:, h * LANES:(h + 1) * LANES]

    for h in range(halves):
        zn = jnp.dot(zf_ref[h], pt_ref[...], preferred_element_type=F32)
        for i in range(8):
            o_ref[timestep(8 * h + i), :] = zn[:, i * LANES:(i + 1) * LANES]


def _s5_core(u, mats, d_skip, bsz, seq):
    kflat, w, v, coef = mats
    n, d = u.shape
    gb, tc, p2 = SSM_GB, SSM_CHUNK * SSM_GROUP, 2 * SSM_STATE
    nk = seq // SSM_CHUNK
    halves = tc // LANES
    perm, perm_t = _s5_perm()
    dflat = jnp.tile(d_skip.reshape(SSM_GROUPS, 1, SSM_GROUP), (1, 1, SSM_CHUNK))
    spec3 = lambda a, b: pl.BlockSpec((gb, a, b), lambda j, bb: (j, 0, 0))
    const = pl.BlockSpec(perm.shape, lambda j, bb: (0, 0))
    return pl.pallas_call(
        _s5_kernel,
        out_shape=jax.ShapeDtypeStruct((n, d), F32),
        grid=(SSM_GROUPS // gb, bsz),
        in_specs=[
            pl.BlockSpec((seq, LANES), lambda j, bb: (bb, j)),
            spec3(SSM_GROUP, tc), spec3(tc, 2 * p2), spec3(p2, tc), spec3(16, p2), spec3(1, tc),
            const, const,
        ],
        out_specs=pl.BlockSpec((seq, LANES), lambda j, bb: (bb, j)),
        scratch_shapes=[pltpu.VMEM((gb, tc, tc), BF16),
                        pltpu.VMEM((halves, nk, gb * LANES), BF16),
                        pltpu.VMEM((halves, nk, gb * LANES), BF16),
                        pltpu.VMEM((gb, nk, p2), F32), pltpu.VMEM((gb, nk, p2), F32),
                        pltpu.VMEM((gb, nk, p2), F32)],
        compiler_params=_cparams("arbitrary", "arbitrary"),
        name="s5_core",
    )(u, kflat, w, v, coef, dflat, perm, perm_t)


def _glu_out_kernel(z_ref, w_ref, x_ref, o_ref):
    vg = jnp.dot(z_ref[...].astype(BF16), w_ref[...], preferred_element_type=F32)
    d = o_ref.shape[1]
    o_ref[...] = x_ref[...] + vg[:, :d] * jax.nn.sigmoid(vg[:, d:])


def _glu_out(z, w_bf16, x, tm):
    n, d = x.shape
    return pl.pallas_call(
        _glu_out_kernel,
        out_shape=jax.ShapeDtypeStruct((n, d), F32),
        grid=(n // tm,),
        in_specs=[
            pl.BlockSpec((tm, z.shape[1]), lambda i: (i, 0)),
            pl.BlockSpec(w_bf16.shape, lambda i: (0, 0)),
            pl.BlockSpec((tm, d), lambda i: (i, 0)),
        ],
        out_specs=pl.BlockSpec((tm, d), lambda i: (i, 0)),
        compiler_params=_cparams("parallel"),
        name="s5_glu_out",
    )(z, w_bf16, x)


def _proj_res_t_kernel(at_ref, w_ref, x_ref, o_ref):
    o_ref[...] = x_ref[...] + lax.dot_general(at_ref[...], w_ref[...], _TN, preferred_element_type=F32)


def _proj_res_t(a_t, w_bf16, x, tm):
    n, d = x.shape
    return pl.pallas_call(
        _proj_res_t_kernel,
        out_shape=jax.ShapeDtypeStruct((n, d), F32),
        grid=(n // tm,),
        in_specs=[
            pl.BlockSpec((a_t.shape[0], tm), lambda i: (0, i)),
            pl.BlockSpec(w_bf16.shape, lambda i: (0, 0)),
            pl.BlockSpec((tm, d), lambda i: (i, 0)),
        ],
        out_specs=pl.BlockSpec((tm, d), lambda i: (i, 0)),
        compiler_params=_cparams("parallel"),
        name="attn_out_proj",
    )(a_t, w_bf16, x)


def _qkv_kernel(x_ref, g_ref, wq_ref, wk_ref, wv_ref, qt_ref, k_ref, vt_ref):
    xn = _rmsnorm(x_ref[...], g_ref[...]).astype(BF16)
    c = (HEAD_DIM ** -0.5) * math.log2(math.e)
    qt_ref[...] = (lax.dot_general(wq_ref[...], xn, _NT, preferred_element_type=F32) * c).astype(BF16)
    k_ref[...] = jnp.dot(xn, wk_ref[...], preferred_element_type=F32).astype(BF16)
    vt = lax.dot_general(wv_ref[...], xn, _NT, preferred_element_type=F32).astype(BF16)
    for c in range(vt_ref.shape[0]):
        vt_ref[c] = vt[:, c * MOBA_BLOCK:(c + 1) * MOBA_BLOCK]


def _qkv_proj(x, g, w_qkv, tm):
    n, d = x.shape
    da = ATT_HEADS * HEAD_DIM
    wq_t = w_qkv[:, :da].T.astype(BF16)
    wk = w_qkv[:, da:2 * da].astype(BF16)
    wv_t = w_qkv[:, 2 * da:].T.astype(BF16)
    full = lambda shp: pl.BlockSpec(shp, lambda i: (0, 0))
    tb = tm // MOBA_BLOCK
    return pl.pallas_call(
        _qkv_kernel,
        out_shape=(jax.ShapeDtypeStruct((da, n), BF16), jax.ShapeDtypeStruct((n, da), BF16),
                   jax.ShapeDtypeStruct((n // MOBA_BLOCK, da, MOBA_BLOCK), BF16)),
        grid=(n // tm,),
        in_specs=[pl.BlockSpec((tm, d), lambda i: (i, 0)), full((1, d)),
                  full((da, d)), full((d, da)), full((da, d))],
        out_specs=(pl.BlockSpec((da, tm), lambda i: (0, i)), pl.BlockSpec((tm, da), lambda i: (i, 0)),
                   pl.BlockSpec((tb, da, MOBA_BLOCK), lambda i: (i, 0, 0))),
        compiler_params=_cparams("parallel"),
        name="attn_qkv_proj",
    )(x, g.reshape(1, d), wq_t, wk, wv_t)


MOBA_CHAINS = 4
MOBA_SUM_ROWS = 16


def _moba_kernel(qt_ref, k_ref, vt_ref, et_ref, ot_ref, s_scr, acc_scr, m_scr, *, nb):
    blk, dh = MOBA_BLOCK, HEAD_DIM
    brow = lax.broadcasted_iota(jnp.int32, (nb, blk), 0)
    r_ix = lax.broadcasted_iota(jnp.int32, (blk, blk), 0)
    c_ix = lax.broadcasted_iota(jnp.int32, (blk, blk), 1)
    zpad = jnp.zeros((2 * LANES - dh - nb, blk), BF16)
    kmean = jnp.mean(k_ref[...].astype(F32).reshape(nb, blk, dh), axis=1)

    def k_aug(t):
        rows = pl.ds(pl.multiple_of(t * blk, blk), blk)
        return jnp.concatenate([k_ref[rows, :], et_ref[rows, :]], axis=1)

    ones = jnp.ones((MOBA_SUM_ROWS, blk), BF16)

    def values(t):
        return jnp.concatenate([vt_ref[t], ones], axis=0)

    def absorb(c, s, vt):
        m_old = m_scr[c]
        m_new = jnp.maximum(m_old, jnp.max(s, axis=0, keepdims=True))
        p = jnp.exp2(s - m_new).astype(BF16)
        acc_scr[c] = jnp.exp2(m_old - m_new) * acc_scr[c] + jnp.dot(vt, p, preferred_element_type=F32)
        m_scr[c] = m_new

    def query_block(i):
        qs = slice(i * blk, (i + 1) * blk)
        qt = qt_ref[:, qs]
        if i > MOBA_TOPK:
            gate = jnp.dot(kmean, qt.astype(F32), preferred_element_type=F32,
                           precision=lax.Precision.HIGHEST)
            gate = jnp.where(brow < i, gate, NEG_INF)
            sel = brow >= i
            for _ in range(MOBA_TOPK):
                gm = jnp.max(gate, axis=0, keepdims=True)
                first = jnp.min(jnp.where(gate == gm, brow, nb), axis=0, keepdims=True)
                pick = brow == first
                sel = jnp.logical_or(sel, pick)
                gate = jnp.where(pick, -jnp.inf, gate)
            bias = jnp.where(sel, 0.0, NEG_INF).astype(BF16)
        else:
            bias = jnp.zeros((nb, blk), BF16)
        return jnp.concatenate([qt, bias, zpad], axis=0)

    nq = MOBA_CHAINS
    for i0 in range(0, nb, nq):
        chains = range(min(nq, nb - i0))
        q_aug = [query_block(i0 + c) for c in chains]
        m_scr[...] = jnp.full(m_scr.shape, NEG_INF, F32)
        acc_scr[...] = jnp.zeros(acc_scr.shape, F32)

        def scores(t, slot, cs):
            ka = k_aug(t)
            for c in cs:
                s_scr[c, slot] = jnp.dot(ka, q_aug[c], preferred_element_type=F32)

        scores(0, 0, chains)
        if i0 > 0:
            def body(j, carry):
                cur = [s_scr[c, j % 2] for c in chains]
                scores(j + 1, (j + 1) % 2, chains)
                vt = values(j)
                for c in chains:
                    absorb(c, cur[c], vt)
                return carry

            lax.fori_loop(0, i0, body, 0)
        for t in chains:
            cur = {c: s_scr[c, (i0 + t) % 2] for c in chains if c >= t}
            later = [c for c in chains if c > t]
            if later:
                scores(i0 + t + 1, (i0 + t + 1) % 2, later)
            vt = values(i0 + t)
            absorb(t, jnp.where(r_ix <= c_ix, cur[t], NEG_INF), vt)
            for c in later:
                absorb(c, cur[c], vt)
        for c in chains:
            qs = slice((i0 + c) * blk, (i0 + c + 1) * blk)
            ot_ref[:, qs] = (acc_scr[c, :dh, :] / acc_scr[c, dh:dh + 1, :]).astype(ot_ref.dtype)


def _moba(q_t, k, v_t, bsz, seq):
    nb = seq // MOBA_BLOCK
    da = ATT_HEADS * HEAD_DIM
    e_t = ((jnp.arange(seq) // MOBA_BLOCK)[:, None] == jnp.arange(LANES)[None, :]).astype(BF16)
    tspec = pl.BlockSpec((HEAD_DIM, seq), lambda b, h: (h, b))
    return pl.pallas_call(
        functools.partial(_moba_kernel, nb=nb),
        out_shape=jax.ShapeDtypeStruct((da, bsz * seq), BF16),
        grid=(bsz, ATT_HEADS),
        in_specs=[tspec, pl.BlockSpec((seq, HEAD_DIM), lambda b, h: (b, h)),
                  pl.BlockSpec((nb, HEAD_DIM, MOBA_BLOCK), lambda b, h: (b, h, 0)),
                  pl.BlockSpec((seq, LANES), lambda b, h: (0, 0))],
        out_specs=tspec,
        scratch_shapes=[pltpu.VMEM((MOBA_CHAINS, 2, MOBA_BLOCK, MOBA_BLOCK), F32),
                        pltpu.VMEM((MOBA_CHAINS, HEAD_DIM + MOBA_SUM_ROWS, MOBA_BLOCK), F32),
                        pltpu.VMEM((MOBA_CHAINS, 1, MOBA_BLOCK), F32)],
        compiler_params=_cparams("parallel", "parallel"),
        name="moba_attn",
    )(q_t, k, v_t, e_t)


def _router_kernel(h_ref, g_ref, w_ref, tri_ref, meta_ref, idx_ref, cnt_ref):
    xn = _rmsnorm(h_ref[...], g_ref[...])
    x_hi = xn.astype(BF16)
    x_lo = (xn - x_hi.astype(F32)).astype(BF16)
    t = jnp.dot(x_hi, w_ref[...], preferred_element_type=F32)
    logits = (t[:, :LANES] + t[:, LANES:]) + jnp.dot(x_lo, w_ref[:, :LANES], preferred_element_type=F32)
    tm = logits.shape[0]
    lane = lax.broadcasted_iota(jnp.int32, (tm, LANES), 1)
    ninf = -jnp.inf
    lg = jnp.where(lane < N_GROUPS, logits, ninf)
    gm = jnp.max(lg, axis=-1, keepdims=True)
    g_idx = jnp.min(jnp.where(lg == gm, lane, LANES), axis=-1, keepdims=True)
    g_gate = 1.0 / jnp.sum(jnp.exp(lg - gm), axis=-1, keepdims=True)
    lo = N_GROUPS + EXPERTS_PER_GROUP * g_idx
    le = jnp.where((lane >= lo) & (lane < lo + EXPERTS_PER_GROUP), logits, ninf)
    m1 = jnp.max(le, axis=-1, keepdims=True)
    i1 = jnp.min(jnp.where(le == m1, lane, LANES), axis=-1, keepdims=True)
    le2 = jnp.where(lane == i1, ninf, le)
    m2 = jnp.max(le2, axis=-1, keepdims=True)
    i2 = jnp.min(jnp.where(le2 == m2, lane, LANES), axis=-1, keepdims=True)
    p2 = jnp.exp(m2 - m1)
    gate1 = g_gate / (1.0 + p2)
    gate2 = g_gate * p2 / (1.0 + p2)
    e1 = (i1 - N_GROUPS).astype(F32)
    e2 = (i2 - N_GROUPS).astype(F32)
    meta = jnp.where(lane == 0, e1, jnp.where(lane == 1, e2, jnp.where(lane == 2, gate1,
                     jnp.where(lane == 3, gate2, 0.0))))
    meta_ref[...] = meta
    oh1 = (lane == i1 - N_GROUPS).astype(F32)
    oh2 = (lane == i2 - N_GROUPS).astype(F32)
    both = oh1 + oh2

    @pl.when(pl.program_id(0) == 0)
    def _():
        cnt_ref[...] = jnp.zeros_like(cnt_ref)

    seen = cnt_ref[...]
    before = jnp.dot(tri_ref[...], both.astype(BF16), preferred_element_type=F32) + seen
    r1 = jnp.sum(before * oh1, axis=-1, keepdims=True)
    r2 = jnp.sum(before * oh2, axis=-1, keepdims=True)
    slab = jnp.where(lane == 0, e1, jnp.where(lane == 1, e2, jnp.where(lane == 2, r1,
                     jnp.where(lane == 3, r2, 0.0))))
    idx_ref[...] = jnp.transpose(slab)[0:8, :].astype(jnp.int32)
    cnt_ref[...] = seen + jnp.sum(both, axis=0, keepdims=True)


def _router(h, g, w_r, tm):
    n, d = h.shape
    tri = (jnp.arange(tm)[:, None] > jnp.arange(tm)[None, :]).astype(BF16)
    return pl.pallas_call(
        _router_kernel,
        out_shape=(jax.ShapeDtypeStruct((n, LANES), F32), jax.ShapeDtypeStruct((8, n), jnp.int32),
                   jax.ShapeDtypeStruct((1, LANES), F32)),
        grid=(n // tm,),
        in_specs=[
            pl.BlockSpec((tm, d), lambda i: (i, 0)),
            pl.BlockSpec((1, d), lambda i: (0, 0)),
            pl.BlockSpec((d, 2 * LANES), lambda i: (0, 0)),
            pl.BlockSpec((tm, tm), lambda i: (0, 0)),
        ],
        out_specs=(pl.BlockSpec((tm, LANES), lambda i: (i, 0)),
                   pl.BlockSpec((8, tm), lambda i: (0, i)),
                   pl.BlockSpec((1, LANES), lambda i: (0, 0))),
        compiler_params=_cparams("arbitrary"),
        name="moe_router",
    )(h, g.reshape(1, d), w_r, tri)


MOE_ISSUE_UNROLL = 8


def _dispatch_kernel(d1_ref, d2_ref, plo_ref, pn_ref, nu_ref, h_ref, g_ref, xs_ref,
                     buf_ref, zero_ref, sem, fill_sem):
    i = pl.program_id(0)
    nsteps = pl.num_programs(0)
    tm = h_ref.shape[0]
    rb = zero_ref.shape[0]
    n_blk = xs_ref.shape[0] // rb
    slot = i % 2
    base = i * tm

    def slot_drain(s):
        cp = pltpu.make_async_copy(buf_ref.at[s], xs_ref.at[pl.ds(0, tm), :], sem.at[s])
        cp.wait()
        cp.wait()

    @pl.when(i >= 2)
    def _():
        slot_drain(slot)

    buf_ref[slot] = _rmsnorm(h_ref[...], g_ref[...])

    def issue(r8, c):
        for k in range(MOE_ISSUE_UNROLL):
            r = r8 * MOE_ISSUE_UNROLL + k
            src = buf_ref.at[slot, pl.ds(r, 1), :]
            pltpu.make_async_copy(src, xs_ref.at[pl.ds(d1_ref[base + r], 1), :], sem.at[slot]).start()
            pltpu.make_async_copy(src, xs_ref.at[pl.ds(d2_ref[base + r], 1), :], sem.at[slot]).start()
        return c

    lax.fori_loop(0, tm // MOE_ISSUE_UNROLL, issue, 0)

    def pad_copy(row):
        return pltpu.make_async_copy(zero_ref.at[pl.ds(0, 1), :], xs_ref.at[pl.ds(row, 1), :], fill_sem)

    def blk_copy(b):
        return pltpu.make_async_copy(zero_ref, xs_ref.at[pl.ds(b * rb, rb), :], fill_sem)

    @pl.when(i == 0)
    def _():
        zero_ref[...] = jnp.zeros_like(zero_ref)

    @pl.when(i < N_EXPERTS)
    def _():
        lo = plo_ref[i]

        def fill(r, c):
            pad_copy(lo + r).start()
            return c

        lax.fori_loop(0, pn_ref[i], fill, 0)

    @pl.when(i == N_EXPERTS)
    def _():
        def fill(b, c):
            blk_copy(b).start()
            return c

        lax.fori_loop(nu_ref[0], n_blk, fill, 0)

    @pl.when(i == nsteps - 1)
    def _():
        slot_drain(1 - slot)
        slot_drain(slot)

        def per_expert(e, c):
            def one(r, cc):
                pad_copy(0).wait()
                return cc

            return lax.fori_loop(0, pn_ref[e], one, c)

        lax.fori_loop(0, N_EXPERTS, per_expert, 0)

        def one_blk(b, c):
            blk_copy(0).wait()
            return c

        lax.fori_loop(nu_ref[0], n_blk, one_blk, 0)


def _dispatch(d1, d2, pad_lo, pad_n, n_used, h, g, cap, tm):
    n, d = h.shape
    assert n // tm > N_EXPERTS + 1
    return pl.pallas_call(
        _dispatch_kernel,
        out_shape=jax.ShapeDtypeStruct((cap, d), F32),
        grid_spec=pltpu.PrefetchScalarGridSpec(
            num_scalar_prefetch=5,
            grid=(n // tm,),
            in_specs=[
                pl.BlockSpec((tm, d), lambda i, *_: (i, 0)),
                pl.BlockSpec((1, d), lambda i, *_: (0, 0)),
            ],
            out_specs=pl.BlockSpec(memory_space=pl.ANY),
            scratch_shapes=[pltpu.VMEM((2, tm, d), F32),
                            pltpu.VMEM((MOE_ROWS, d), F32),
                            pltpu.SemaphoreType.DMA((2,)), pltpu.SemaphoreType.DMA],
        ),
        compiler_params=_cparams("arbitrary"),
        name="moe_dispatch",
    )(d1, d2, pad_lo, pad_n, n_used, h, g.reshape(1, d))


def _expert_kernel(be_ref, nu_ref, x_ref, w1_ref, w3_ref, w2_ref, y_ref, w1c, w3c, w2c):
    i = pl.program_id(0)
    used = i < nu_ref[0]
    prev = be_ref[jnp.maximum(i - 1, 0)]
    fresh = jnp.logical_or(i == 0, be_ref[i] != prev)

    @pl.when(jnp.logical_and(used, fresh))
    def _():
        w1c[...] = w1_ref[0, 0].astype(BF16)
        w3c[...] = w3_ref[0, 0].astype(BF16)
        w2c[...] = w2_ref[0, 0].astype(BF16)

    @pl.when(used)
    def _():
        x = x_ref[...].astype(BF16)
        hf = w1c.shape[1] // 2
        y = None
        for c in range(2):
            cs = slice(c * hf, (c + 1) * hf)
            a = jnp.dot(x, w1c[:, cs], preferred_element_type=F32)
            b = jnp.dot(x, w3c[:, cs], preferred_element_type=F32)
            act = (jax.nn.silu(a) * b).astype(BF16)
            part = jnp.dot(act, w2c[cs, :], preferred_element_type=F32)
            y = part if y is None else y + part
        y_ref[...] = y

    @pl.when(jnp.logical_not(used))
    def _():
        y_ref[...] = jnp.zeros_like(y_ref)


def _experts(blk_e, n_used, xs, w1, w3, w2, layer):
    cap, d = xs.shape
    rb = MOE_ROWS
    n_blk = cap // rb
    ff = w1.shape[3]
    return pl.pallas_call(
        _expert_kernel,
        out_shape=jax.ShapeDtypeStruct((cap, d), F32),
        grid_spec=pltpu.PrefetchScalarGridSpec(
            num_scalar_prefetch=2,
            grid=(n_blk,),
            in_specs=[
                pl.BlockSpec((rb, d), lambda i, be, nu: (jnp.minimum(i, jnp.maximum(nu[0] - 1, 0)), 0)),
                pl.BlockSpec((1, 1, d, ff), lambda i, be, nu: (layer, be[i], 0, 0)),
                pl.BlockSpec((1, 1, d, ff), lambda i, be, nu: (layer, be[i], 0, 0)),
                pl.BlockSpec((1, 1, ff, d), lambda i, be, nu: (layer, be[i], 0, 0)),
            ],
            out_specs=pl.BlockSpec((rb, d), lambda i, be, nu: (i, 0)),
            scratch_shapes=[pltpu.VMEM((d, ff), BF16), pltpu.VMEM((d, ff), BF16),
                            pltpu.VMEM((ff, d), BF16)],
        ),
        compiler_params=_cparams("arbitrary"),
        name="moe_experts",
    )(blk_e, n_used, xs, w1, w3, w2)


def _combine_kernel(d1_ref, d2_ref, h_ref, meta_ref, g_ref, ys_ref, o_ref, buf_ref, sem, *, final_norm):
    i = pl.program_id(0)
    nsteps = pl.num_programs(0)
    tm = h_ref.shape[0]
    slot = i % 2

    def fetch(tile, s):
        base = tile * tm

        def issue(r8, c):
            for k in range(MOE_ISSUE_UNROLL):
                r = r8 * MOE_ISSUE_UNROLL + k
                pltpu.make_async_copy(ys_ref.at[pl.ds(d1_ref[base + r], 1), :],
                                      buf_ref.at[s, 0, pl.ds(r, 1), :], sem.at[s]).start()
                pltpu.make_async_copy(ys_ref.at[pl.ds(d2_ref[base + r], 1), :],
                                      buf_ref.at[s, 1, pl.ds(r, 1), :], sem.at[s]).start()
            return c

        lax.fori_loop(0, tm // MOE_ISSUE_UNROLL, issue, 0)

    @pl.when(i == 0)
    def _():
        fetch(0, 0)

    @pl.when(i + 1 < nsteps)
    def _():
        fetch(i + 1, 1 - slot)

    for j in range(2):
        pltpu.make_async_copy(ys_ref.at[pl.ds(0, tm), :], buf_ref.at[slot, j], sem.at[slot]).wait()

    meta = meta_ref[...]
    out = h_ref[...] + (meta[:, 2:3] * buf_ref[slot, 0] + meta[:, 3:4] * buf_ref[slot, 1])
    if final_norm:
        out = _rmsnorm(out, g_ref[...])
    o_ref[...] = out


def _combine(d1, d2, h, meta, ys, g_final, tm, final_norm):
    n, d = h.shape
    return pl.pallas_call(
        functools.partial(_combine_kernel, final_norm=final_norm),
        out_shape=jax.ShapeDtypeStruct((n, d), F32),
        grid_spec=pltpu.PrefetchScalarGridSpec(
            num_scalar_prefetch=2,
            grid=(n // tm,),
            in_specs=[
                pl.BlockSpec((tm, d), lambda i, *_: (i, 0)),
                pl.BlockSpec((tm, LANES), lambda i, *_: (i, 0)),
                pl.BlockSpec((1, d), lambda i, *_: (0, 0)),
                pl.BlockSpec(memory_space=pl.ANY),
            ],
            out_specs=pl.BlockSpec((tm, d), lambda i, *_: (i, 0)),
            scratch_shapes=[pltpu.VMEM((2, 2, tm, d), F32), pltpu.SemaphoreType.DMA((2,))],
        ),
        compiler_params=_cparams("arbitrary"),
        name="moe_combine",
    )(d1, d2, h, meta, g_final.reshape(1, d), ys)


def _hier_moe(h, ln_g, w_group, w_expert, w1, w3, w2, layer, g_final, final_norm):
    n, d = h.shape
    rb = MOE_ROWS
    cap = 2 * n + N_EXPERTS * rb
    n_blk = cap // rb
    w_r = jnp.zeros((d, LANES), F32).at[:, :N_GROUPS].set(w_group)
    w_r = w_r.at[:, N_GROUPS:N_GROUPS + N_EXPERTS].set(w_expert)
    w_hi = w_r.astype(BF16)
    w_lo = (w_r - w_hi.astype(F32)).astype(BF16)
    meta, idx, counts = _router(h, ln_g, jnp.concatenate([w_hi, w_lo], axis=1), 512)
    cnt = counts[0, :N_EXPERTS].astype(jnp.int32)
    padded = ((cnt + rb - 1) // rb) * rb
    pad_end = jnp.cumsum(padded)
    pad_start = pad_end - padded
    blk_start = jnp.arange(n_blk, dtype=jnp.int32) * rb
    blk_e = jnp.minimum(jnp.sum(pad_end[None, :] <= blk_start[:, None], axis=1), N_EXPERTS - 1).astype(jnp.int32)
    n_used = (pad_end[-1] // rb).astype(jnp.int32).reshape(1)
    experts = jnp.arange(N_EXPERTS, dtype=jnp.int32)[None, :]
    start_of = lambda e: jnp.sum(jnp.where(e[:, None] == experts, pad_start[None, :], 0), axis=1)
    d1 = start_of(idx[0]) + idx[2]
    d2 = start_of(idx[1]) + idx[3]
    xs = _dispatch(d1, d2, pad_start + cnt, padded - cnt, n_used, h, ln_g, cap, 256)
    ys = _experts(blk_e, n_used, xs, w1, w3, w2, layer)
    return _combine(d1, d2, h, meta, ys, g_final, 256, final_norm)


def kernel(x, ln_mix, ln_ffn, ln_final, ssm_w_in, ssm_lam_re, ssm_lam_im, ssm_log_dt, ssm_b_re, ssm_b_im, ssm_c_re, ssm_c_im, ssm_d, ssm_w_out, attn_w_qkv, attn_w_o, moe_w_group, moe_w_expert, moe_w1, moe_w3, moe_w2):
    bsz, seq, d = x.shape
    n = bsz * seq
    h = x.reshape(n, d)

    u = _norm_proj(h, ln_mix[0], ssm_w_in[0].astype(BF16), 512, F32, "s5_in_proj")
    mats = _s5_mats(ssm_lam_re[0], ssm_lam_im[0], ssm_log_dt[0], ssm_b_re[0], ssm_b_im[0],
                    ssm_c_re[0], ssm_c_im[0])
    z = _s5_core(u, mats, ssm_d[0], bsz, seq)
    h = _glu_out(z, ssm_w_out[0].astype(BF16), h, 512)
    h = _hier_moe(h, ln_ffn[0], moe_w_group[0], moe_w_expert[0], moe_w1, moe_w3, moe_w2, 0,
                  ln_final, False)

    q_t, k, v_t = _qkv_proj(h, ln_mix[1], attn_w_qkv[0], 512)
    o_t = _moba(q_t, k, v_t, bsz, seq)
    h = _proj_res_t(o_t, attn_w_o[0].astype(BF16), h, 512)
    h = _hier_moe(h, ln_ffn[1], moe_w_group[1], moe_w_expert[1], moe_w1, moe_w3, moe_w2, 1,
                  ln_final, True)
    return h.reshape(bsz, seq, d)
```

```python
import functools
import math

import jax
import jax.numpy as jnp
from jax import lax
from jax.experimental import pallas as pl
from jax.experimental.pallas import tpu as pltpu

F32 = jnp.float32
BF16 = jnp.bfloat16

D_MODEL = 1024
RMS_EPS = 1e-6
NEG_INF = -1e30

SSM_GROUP = 16
SSM_GROUPS = D_MODEL // SSM_GROUP
SSM_STATE = 64
SSM_CHUNK = 16
SSM_GB = 8
SSM_WIN = 8

ATT_HEADS = 8
HEAD_DIM = 128
MOBA_BLOCK = 256
MOBA_TOPK = 3

N_GROUPS = 4
EXPERTS_PER_GROUP = 8
N_EXPERTS = 32
EXPERT_FF = 512
MOE_ROWS = 256

LANES = 128
VMEM_LIMIT = 48 * 1024 * 1024

_NT = (((1,), (1,)), ((), ()))
_TN = (((0,), (0,)), ((), ()))


def _cparams(*sem):
    return pltpu.CompilerParams(dimension_semantics=sem, vmem_limit_bytes=VMEM_LIMIT)


def _rmsnorm(x, g):
    return x * lax.rsqrt(jnp.mean(x * x, axis=-1, keepdims=True) + RMS_EPS) * g


def _norm_proj_kernel(x_ref, g_ref, w_ref, o_ref):
    xn = _rmsnorm(x_ref[...], g_ref[...]).astype(BF16)
    o_ref[...] = jnp.dot(xn, w_ref[...], preferred_element_type=F32).astype(o_ref.dtype)


def _norm_proj(x, g, w_bf16, tm, out_dtype, name):
    n, d = x.shape
    dout = w_bf16.shape[1]
    return pl.pallas_call(
        _norm_proj_kernel,
        out_shape=jax.ShapeDtypeStruct((n, dout), out_dtype),
        grid=(n // tm,),
        in_specs=[
            pl.BlockSpec((tm, d), lambda i: (i, 0)),
            pl.BlockSpec((1, d), lambda i: (0, 0)),
            pl.BlockSpec((d, dout), lambda i: (0, 0)),
        ],
        out_specs=pl.BlockSpec((tm, dout), lambda i: (i, 0)),
        compiler_params=_cparams("parallel"),
        name=name,
    )(x, g.reshape(1, d), w_bf16)


def _s5_mats(lam_re, lam_im, log_dt, b_re, b_im, c_re, c_im):
    g_, p_, c_, t_ = SSM_GROUPS, SSM_STATE, SSM_GROUP, SSM_CHUNK
    lr = jnp.minimum(lam_re, -1e-4)
    li = lam_im
    dt = jnp.exp(log_dt)[:, None]
    mag = jnp.exp(lr * dt)
    abar_re = mag * jnp.cos(li * dt)
    abar_im = mag * jnp.sin(li * dt)
    den = lr * lr + li * li
    nr = abar_re - 1.0
    gam_re = (nr * lr + abar_im * li) / den
    gam_im = (abar_im * lr - nr * li) / den
    bb_re = gam_re[..., None] * b_re - gam_im[..., None] * b_im
    bb_im = gam_re[..., None] * b_im + gam_im[..., None] * b_re

    def powers(ns):
        nf = jnp.asarray(ns, F32)[None, :, None]
        pm = jnp.exp(nf * (lr * dt)[:, None, :])
        ang = nf * (li * dt)[:, None, :]
        return pm * jnp.cos(ang), pm * jnp.sin(ang)

    pr, pi = powers(list(range(t_ + 1)))
    ca_re = c_re[:, None] * pr[:, :, None, :] - c_im[:, None] * pi[:, :, None, :]
    ca_im = c_re[:, None] * pi[:, :, None, :] + c_im[:, None] * pr[:, :, None, :]
    ca_n = jnp.concatenate([ca_re[:, :t_], ca_im[:, :t_]], axis=-1).reshape(g_, t_ * c_, 2 * p_)
    bb_s = jnp.concatenate([bb_re, -bb_im], axis=1)
    kflat = jnp.einsum('gxp,gpc->gcx', ca_n, bb_s, precision='highest')
    prs = pr[:, :t_][:, ::-1][:, :, None, :]
    pis = pi[:, :t_][:, ::-1][:, :, None, :]
    bt_re = bb_re.transpose(0, 2, 1)[:, None]
    bt_im = bb_im.transpose(0, 2, 1)[:, None]
    w_re = prs * bt_re - pis * bt_im
    w_im = prs * bt_im + pis * bt_re
    w = jnp.concatenate([w_re, w_im, w_im, w_re], axis=-1).reshape(g_, t_ * c_, 4 * p_)
    prt = pr.transpose(0, 2, 1)[:, :, 1:, None]
    pit = pi.transpose(0, 2, 1)[:, :, 1:, None]
    ct_re = c_re.transpose(0, 2, 1)[:, :, None, :]
    ct_im = c_im.transpose(0, 2, 1)[:, :, None, :]
    v_re = (ct_re * prt - ct_im * pit).reshape(g_, p_, t_ * c_)
    v_im = -(ct_re * pit + ct_im * prt).reshape(g_, p_, t_ * c_)
    v = jnp.concatenate([v_re, v_im], axis=1)
    qr, qi = powers([t_ * (1 << j) for j in range(4)])
    rows = []
    for j in range(4):
        ar, ai = qr[:, j], qi[:, j]
        rows += [jnp.concatenate([ar, ar], -1), jnp.concatenate([-ai, ai], -1),
                 jnp.concatenate([ai, -ai], -1)]
    rows += [jnp.zeros_like(rows[0])] * 4
    coef = jnp.stack(rows, axis=1)
    return kflat.astype(F32), w.astype(BF16), v.astype(BF16), coef.astype(F32)


def _s5_perm():
    r = jnp.arange(8 * LANES)
    col = ((r % LANES) // SSM_GROUP) * LANES + (r // LANES) * SSM_GROUP + r % SSM_GROUP
    p = (col[:, None] == r[None, :]).astype(BF16)
    return p, p.T


def _s5_kernel(u_ref, kf_ref, w_ref, v_ref, coef_ref, d_ref, p_ref, pt_ref, o_ref,
               m_ref, vf_ref, zf_ref, ea_ref, eb_ref, sp_ref):
    t_, c_, gb, win = SSM_CHUNK, SSM_GROUP, SSM_GB, SSM_WIN
    nk = u_ref.shape[0] // t_
    p2 = 2 * SSM_STATE
    tc = t_ * c_

    @pl.when(pl.program_id(1) == 0)
    def _():
        lane_tc = lax.broadcasted_iota(jnp.int32, (c_, tc), 1)
        for g in range(gb):
            kf = kf_ref[g]
            for s in range(t_):
                rolled = kf if s == 0 else pltpu.roll(kf, s * c_, axis=1)
                m_ref[g, s * c_:(s + 1) * c_, :] = jnp.where(lane_tc >= s * c_, rolled, 0.0).astype(BF16)

    halves = tc // LANES

    def timestep(t):
        return pl.ds(t, nk, stride=t_)

    def flat(g):
        return jnp.concatenate([vf_ref[h, :, g * LANES:(g + 1) * LANES] for h in range(halves)], axis=1)

    for h in range(halves):
        x = jnp.concatenate([u_ref[timestep(8 * h + i), :].astype(BF16) for i in range(8)], axis=1)
        vf_ref[h] = jnp.dot(x, p_ref[...], preferred_element_type=F32).astype(BF16)

    row = lax.broadcasted_iota(jnp.int32, (nk, p2), 0)

    def shift(x, d):
        return jnp.where(row < d, 0.0, pltpu.roll(x, d, axis=0))

    for g in range(gb):
        sc = jnp.dot(flat(g), w_ref[g], preferred_element_type=F32)
        xa, xb = shift(sc[:, :p2], 1), shift(sc[:, p2:], 1)
        cf = coef_ref[g]
        for lvl in range(3):
            pp, qa, qb = cf[3 * lvl:3 * lvl + 1], cf[3 * lvl + 1:3 * lvl + 2], cf[3 * lvl + 2:3 * lvl + 3]
            sa, sb = shift(xa, 1 << lvl), shift(xb, 1 << lvl)
            xa, xb = xa + pp * sa + qa * sb, xb + pp * sb + qb * sa
        ea_ref[g] = xa
        eb_ref[g] = xb

    cfs = [coef_ref[g] for g in range(gb)]
    zero = jnp.zeros((win, p2), F32)
    state = [(zero, zero)] * gb
    for j in range(nk // win):
        rs = slice(j * win, (j + 1) * win)
        for g in range(gb):
            pp, qa, qb = cfs[g][9:10], cfs[g][10:11], cfs[g][11:12]
            s_a, s_b = state[g]
            n_a = pp * s_a + qa * s_b + ea_ref[g, rs, :]
            n_b = pp * s_b + qb * s_a + eb_ref[g, rs, :]
            sp_ref[g, rs, :] = n_a
            state[g] = (n_a, n_b)

    for g in range(gb):
        x = flat(g)
        y = jnp.dot(x, m_ref[g], preferred_element_type=F32)
        y = y + jnp.dot(sp_ref[g].astype(BF16), v_ref[g], preferred_element_type=F32)
        z = jax.nn.gelu(y + d_ref[g] * x.astype(F32)).astype(BF16)
        for h in range(halves):
            zf_ref[h, :, g * LANES:(g + 1) * LANES] = z[:, h * LANES:(h + 1) * LANES]

    for h in range(halves):
        zn = jnp.dot(zf_ref[h], pt_ref[...], preferred_element_type=F32)
        for i in range(8):
            o_ref[timestep(8 * h + i), :] = zn[:, i * LANES:(i + 1) * LANES]


def _s5_core(u, mats, d_skip, bsz, seq):
    kflat, w, v, coef = mats
    n, d = u.shape
    gb, tc, p2 = SSM_GB, SSM_CHUNK * SSM_GROUP, 2 * SSM_STATE
    nk = seq // SSM_CHUNK
    halves = tc // LANES
    perm, perm_t = _s5_perm()
    dflat = jnp.tile(d_skip.reshape(SSM_GROUPS, 1, SSM_GROUP), (1, 1, SSM_CHUNK))
    spec3 = lambda a, b: pl.BlockSpec((gb, a, b), lambda j, bb: (j, 0, 0))
    const = pl.BlockSpec(perm.shape, lambda j, bb: (0, 0))
    return pl.pallas_call(
        _s5_kernel,
        out_shape=jax.ShapeDtypeStruct((n, d), F32),
        grid=(SSM_GROUPS // gb, bsz),
        in_specs=[
            pl.BlockSpec((seq, LANES), lambda j, bb: (bb, j)),
            spec3(SSM_GROUP, tc), spec3(tc, 2 * p2), spec3(p2, tc), spec3(16, p2), spec3(1, tc),
            const, const,
        ],
        out_specs=pl.BlockSpec((seq, LANES), lambda j, bb: (bb, j)),
        scratch_shapes=[pltpu.VMEM((gb, tc, tc), BF16),
                        pltpu.VMEM((halves, nk, gb * LANES), BF16),
                        pltpu.VMEM((halves, nk, gb * LANES), BF16),
                        pltpu.VMEM((gb, nk, p2), F32), pltpu.VMEM((gb, nk, p2), F32),
                        pltpu.VMEM((gb, nk, p2), F32)],
        compiler_params=_cparams("arbitrary", "arbitrary"),
        name="s5_core",
    )(u, kflat, w, v, coef, dflat, perm, perm_t)


def _glu_out_kernel(z_ref, w_ref, x_ref, o_ref):
    vg = jnp.dot(z_ref[...].astype(BF16), w_ref[...], preferred_element_type=F32)
    d = o_ref.shape[1]
    o_ref[...] = x_ref[...] + vg[:, :d] * jax.nn.sigmoid(vg[:, d:])


def _glu_out(z, w_bf16, x, tm):
    n, d = x.shape
    return pl.pallas_call(
        _glu_out_kernel,
        out_shape=jax.ShapeDtypeStruct((n, d), F32),
        grid=(n // tm,),
        in_specs=[
            pl.BlockSpec((tm, z.shape[1]), lambda i: (i, 0)),
            pl.BlockSpec(w_bf16.shape, lambda i: (0, 0)),
            pl.BlockSpec((tm, d), lambda i: (i, 0)),
        ],
        out_specs=pl.BlockSpec((tm, d), lambda i: (i, 0)),
        compiler_params=_cparams("parallel"),
        name="s5_glu_out",
    )(z, w_bf16, x)


def _proj_res_t_kernel(at_ref, w_ref, x_ref, o_ref):
    o_ref[...] = x_ref[...] + lax.dot_general(at_ref[...], w_ref[...], _TN, preferred_element_type=F32)


def _proj_res_t(a_t, w_bf16, x, tm):
    n, d = x.shape
    return pl.pallas_call(
        _proj_res_t_kernel,
        out_shape=jax.ShapeDtypeStruct((n, d), F32),
        grid=(n // tm,),
        in_specs=[
            pl.BlockSpec((a_t.shape[0], tm), lambda i: (0, i)),
            pl.BlockSpec(w_bf16.shape, lambda i: (0, 0)),
            pl.BlockSpec((tm, d), lambda i: (i, 0)),
        ],
        out_specs=pl.BlockSpec((tm, d), lambda i: (i, 0)),
        compiler_params=_cparams("parallel"),
        name="attn_out_proj",
    )(a_t, w_bf16, x)


def _qkv_kernel(x_ref, g_ref, wq_ref, wk_ref, wv_ref, qt_ref, k_ref, vt_ref):
    xn = _rmsnorm(x_ref[...], g_ref[...]).astype(BF16)
    c = (HEAD_DIM ** -0.5) * math.log2(math.e)
    qt_ref[...] = (lax.dot_general(wq_ref[...], xn, _NT, preferred_element_type=F32) * c).astype(BF16)
    k_ref[...] = jnp.dot(xn, wk_ref[...], preferred_element_type=F32).astype(BF16)
    vt = lax.dot_general(wv_ref[...], xn, _NT, preferred_element_type=F32).astype(BF16)
    for c in range(vt_ref.shape[0]):
        vt_ref[c] = vt[:, c * MOBA_BLOCK:(c + 1) * MOBA_BLOCK]


def _qkv_proj(x, g, w_qkv, tm):
    n, d = x.shape
    da = ATT_HEADS * HEAD_DIM
    wq_t = w_qkv[:, :da].T.astype(BF16)
    wk = w_qkv[:, da:2 * da].astype(BF16)
    wv_t = w_qkv[:, 2 * da:].T.astype(BF16)
    full = lambda shp: pl.BlockSpec(shp, lambda i: (0, 0))
    tb = tm // MOBA_BLOCK
    return pl.pallas_call(
        _qkv_kernel,
        out_shape=(jax.ShapeDtypeStruct((da, n), BF16), jax.ShapeDtypeStruct((n, da), BF16),
                   jax.ShapeDtypeStruct((n // MOBA_BLOCK, da, MOBA_BLOCK), BF16)),
        grid=(n // tm,),
        in_specs=[pl.BlockSpec((tm, d), lambda i: (i, 0)), full((1, d)),
                  full((da, d)), full((d, da)), full((da, d))],
        out_specs=(pl.BlockSpec((da, tm), lambda i: (0, i)), pl.BlockSpec((tm, da), lambda i: (i, 0)),
                   pl.BlockSpec((tb, da, MOBA_BLOCK), lambda i: (i, 0, 0))),
        compiler_params=_cparams("parallel"),
        name="attn_qkv_proj",
    )(x, g.reshape(1, d), wq_t, wk, wv_t)


MOBA_CHAINS = 4
MOBA_SUM_ROWS = 16


def _moba_kernel(qt_ref, k_ref, vt_ref, et_ref, ot_ref, s_scr, acc_scr, m_scr, *, nb):
    blk, dh = MOBA_BLOCK, HEAD_DIM
    brow = lax.broadcasted_iota(jnp.int32, (nb, blk), 0)
    r_ix = lax.broadcasted_iota(jnp.int32, (blk, blk), 0)
    c_ix = lax.broadcasted_iota(jnp.int32, (blk, blk), 1)
    zpad = jnp.zeros((2 * LANES - dh - nb, blk), BF16)
    kmean = jnp.mean(k_ref[...].astype(F32).reshape(nb, blk, dh), axis=1)

    def k_aug(t):
        rows = pl.ds(pl.multiple_of(t * blk, blk), blk)
        return jnp.concatenate([k_ref[rows, :], et_ref[rows, :]], axis=1)

    ones = jnp.ones((MOBA_SUM_ROWS, blk), BF16)

    def values(t):
        return jnp.concatenate([vt_ref[t], ones], axis=0)

    def absorb(c, s, vt):
        m_old = m_scr[c]
        m_new = jnp.maximum(m_old, jnp.max(s, axis=0, keepdims=True))
        p = jnp.exp2(s - m_new).astype(BF16)
        acc_scr[c] = jnp.exp2(m_old - m_new) * acc_scr[c] + jnp.dot(vt, p, preferred_element_type=F32)
        m_scr[c] = m_new

    def query_block(i):
        qs = slice(i * blk, (i + 1) * blk)
        qt = qt_ref[:, qs]
        if i > MOBA_TOPK:
            gate = jnp.dot(kmean, qt.astype(F32), preferred_element_type=F32,
                           precision=lax.Precision.HIGHEST)
            gate = jnp.where(brow < i, gate, NEG_INF)
            sel = brow >= i
            for _ in range(MOBA_TOPK):
                gm = jnp.max(gate, axis=0, keepdims=True)
                first = jnp.min(jnp.where(gate == gm, brow, nb), axis=0, keepdims=True)
                pick = brow == first
                sel = jnp.logical_or(sel, pick)
                gate = jnp.where(pick, -jnp.inf, gate)
            bias = jnp.where(sel, 0.0, NEG_INF).astype(BF16)
        else:
            bias = jnp.zeros((nb, blk), BF16)
        return jnp.concatenate([qt, bias, zpad], axis=0)

    nq = MOBA_CHAINS
    for i0 in range(0, nb, nq):
        chains = range(min(nq, nb - i0))
        q_aug = [query_block(i0 + c) for c in chains]
        m_scr[...] = jnp.full(m_scr.shape, NEG_INF, F32)
        acc_scr[...] = jnp.zeros(acc_scr.shape, F32)

        def scores(t, slot, cs):
            ka = k_aug(t)
            for c in cs:
                s_scr[c, slot] = jnp.dot(ka, q_aug[c], preferred_element_type=F32)

        scores(0, 0, chains)
        if i0 > 0:
            def body(j, carry):
                cur = [s_scr[c, j % 2] for c in chains]
                scores(j + 1, (j + 1) % 2, chains)
                vt = values(j)
                for c in chains:
                    absorb(c, cur[c], vt)
                return carry

            lax.fori_loop(0, i0, body, 0)
        for t in chains:
            cur = {c: s_scr[c, (i0 + t) % 2] for c in chains if c >= t}
            later = [c for c in chains if c > t]
            if later:
                scores(i0 + t + 1, (i0 + t + 1) % 2, later)
            vt = values(i0 + t)
            absorb(t, jnp.where(r_ix <= c_ix, cur[t], NEG_INF), vt)
            for c in later:
                absorb(c, cur[c], vt)
        for c in chains:
            qs = slice((i0 + c) * blk, (i0 + c + 1) * blk)
            ot_ref[:, qs] = (acc_scr[c, :dh, :] / acc_scr[c, dh:dh + 1, :]).astype(ot_ref.dtype)


def _moba(q_t, k, v_t, bsz, seq):
    nb = seq // MOBA_BLOCK
    da = ATT_HEADS * HEAD_DIM
    e_t = ((jnp.arange(seq) // MOBA_BLOCK)[:, None] == jnp.arange(LANES)[None, :]).astype(BF16)
    tspec = pl.BlockSpec((HEAD_DIM, seq), lambda b, h: (h, b))
    return pl.pallas_call(
        functools.partial(_moba_kernel, nb=nb),
        out_shape=jax.ShapeDtypeStruct((da, bsz * seq), BF16),
        grid=(bsz, ATT_HEADS),
        in_specs=[tspec, pl.BlockSpec((seq, HEAD_DIM), lambda b, h: (b, h)),
                  pl.BlockSpec((nb, HEAD_DIM, MOBA_BLOCK), lambda b, h: (b, h, 0)),
                  pl.BlockSpec((seq, LANES), lambda b, h: (0, 0))],
        out_specs=tspec,
        scratch_shapes=[pltpu.VMEM((MOBA_CHAINS, 2, MOBA_BLOCK, MOBA_BLOCK), F32),
                        pltpu.VMEM((MOBA_CHAINS, HEAD_DIM + MOBA_SUM_ROWS, MOBA_BLOCK), F32),
                        pltpu.VMEM((MOBA_CHAINS, 1, MOBA_BLOCK), F32)],
        compiler_params=_cparams("parallel", "parallel"),
        name="moba_attn",
    )(q_t, k, v_t, e_t)


def _router_kernel(h_ref, g_ref, w_ref, tri_ref, meta_ref, idx_ref, cnt_ref):
    xn = _rmsnorm(h_ref[...], g_ref[...])
    x_hi = xn.astype(BF16)
    x_lo = (xn - x_hi.astype(F32)).astype(BF16)
    t = jnp.dot(x_hi, w_ref[...], preferred_element_type=F32)
    logits = (t[:, :LANES] + t[:, LANES:]) + jnp.dot(x_lo, w_ref[:, :LANES], preferred_element_type=F32)
    tm = logits.shape[0]
    lane = lax.broadcasted_iota(jnp.int32, (tm, LANES), 1)
    ninf = -jnp.inf
    lg = jnp.where(lane < N_GROUPS, logits, ninf)
    gm = jnp.max(lg, axis=-1, keepdims=True)
    g_idx = jnp.min(jnp.where(lg == gm, lane, LANES), axis=-1, keepdims=True)
    g_gate = 1.0 / jnp.sum(jnp.exp(lg - gm), axis=-1, keepdims=True)
    lo = N_GROUPS + EXPERTS_PER_GROUP * g_idx
    le = jnp.where((lane >= lo) & (lane < lo + EXPERTS_PER_GROUP), logits, ninf)
    m1 = jnp.max(le, axis=-1, keepdims=True)
    i1 = jnp.min(jnp.where(le == m1, lane, LANES), axis=-1, keepdims=True)
    le2 = jnp.where(lane == i1, ninf, le)
    m2 = jnp.max(le2, axis=-1, keepdims=True)
    i2 = jnp.min(jnp.where(le2 == m2, lane, LANES), axis=-1, keepdims=True)
    p2 = jnp.exp(m2 - m1)
    gate1 = g_gate / (1.0 + p2)
    gate2 = g_gate * p2 / (1.0 + p2)
    e1 = (i1 - N_GROUPS).astype(F32)
    e2 = (i2 - N_GROUPS).astype(F32)
    meta = jnp.where(lane == 0, e1, jnp.where(lane == 1, e2, jnp.where(lane == 2, gate1,
                     jnp.where(lane == 3, gate2, 0.0))))
    meta_ref[...] = meta
    oh1 = (lane == i1 - N_GROUPS).astype(F32)
    oh2 = (lane == i2 - N_GROUPS).astype(F32)
    both = oh1 + oh2

    @pl.when(pl.program_id(0) == 0)
    def _():
        cnt_ref[...] = jnp.zeros_like(cnt_ref)

    seen = cnt_ref[...]
    before = jnp.dot(tri_ref[...], both.astype(BF16), preferred_element_type=F32) + seen
    r1 = jnp.sum(before * oh1, axis=-1, keepdims=True)
    r2 = jnp.sum(before * oh2, axis=-1, keepdims=True)
    slab = jnp.where(lane == 0, e1, jnp.where(lane == 1, e2, jnp.where(lane == 2, r1,
                     jnp.where(lane == 3, r2, 0.0))))
    idx_ref[...] = jnp.transpose(slab)[0:8, :].astype(jnp.int32)
    cnt_ref[...] = seen + jnp.sum(both, axis=0, keepdims=True)


def _router(h, g, w_r, tm):
    n, d = h.shape
    tri = (jnp.arange(tm)[:, None] > jnp.arange(tm)[None, :]).astype(BF16)
    return pl.pallas_call(
        _router_kernel,
        out_shape=(jax.ShapeDtypeStruct((n, LANES), F32), jax.ShapeDtypeStruct((8, n), jnp.int32),
                   jax.ShapeDtypeStruct((1, LANES), F32)),
        grid=(n // tm,),
        in_specs=[
            pl.BlockSpec((tm, d), lambda i: (i, 0)),
            pl.BlockSpec((1, d), lambda i: (0, 0)),
            pl.BlockSpec((d, 2 * LANES), lambda i: (0, 0)),
            pl.BlockSpec((tm, tm), lambda i: (0, 0)),
        ],
        out_specs=(pl.BlockSpec((tm, LANES), lambda i: (i, 0)),
                   pl.BlockSpec((8, tm), lambda i: (0, i)),
                   pl.BlockSpec((1, LANES), lambda i: (0, 0))),
        compiler_params=_cparams("arbitrary"),
        name="moe_router",
    )(h, g.reshape(1, d), w_r, tri)


MOE_ISSUE_UNROLL = 8
ROW_TILE = 8


def _row_tile(off):
    return pl.ds(pl.multiple_of(off, ROW_TILE), ROW_TILE)


def _store_row_tiles(ref, x):
    rows = x.shape[0]
    for c in range(ROW_TILE):
        ref[pl.ds(c, rows, stride=ROW_TILE), :] = x[:, c * LANES:(c + 1) * LANES]


def _load_row_tiles(ref, rows):
    return jnp.concatenate([ref[pl.ds(c, rows, stride=ROW_TILE), :] for c in range(ROW_TILE)], axis=1)


def _dispatch_kernel(d1_ref, d2_ref, plo_ref, pn_ref, nu_ref, h_ref, g_ref, xs_ref,
                     buf_ref, zero_ref, sem, fill_sem):
    i = pl.program_id(0)
    nsteps = pl.num_programs(0)
    tm = h_ref.shape[0]
    rt = ROW_TILE
    rb = zero_ref.shape[0] // rt
    n_blk = xs_ref.shape[0] // (rb * rt)
    slot = i % 2
    base = i * tm

    def slot_drain(s):
        cp = pltpu.make_async_copy(buf_ref.at[s], xs_ref.at[pl.ds(0, tm * rt), :], sem.at[s])
        cp.wait()
        cp.wait()

    @pl.when(i >= 2)
    def _():
        slot_drain(slot)

    _store_row_tiles(buf_ref.at[slot], _rmsnorm(h_ref[...], g_ref[...]))

    def issue(r8, c):
        for k in range(MOE_ISSUE_UNROLL):
            r = r8 * MOE_ISSUE_UNROLL + k
            src = buf_ref.at[slot, _row_tile(r * rt), :]
            pltpu.make_async_copy(src, xs_ref.at[_row_tile(d1_ref[base + r]), :], sem.at[slot]).start(priority=0)
            pltpu.make_async_copy(src, xs_ref.at[_row_tile(d2_ref[base + r]), :], sem.at[slot]).start(priority=1)
        return c

    lax.fori_loop(0, tm // MOE_ISSUE_UNROLL, issue, 0)

    def pad_copy(off):
        return pltpu.make_async_copy(zero_ref.at[pl.ds(0, rt), :], xs_ref.at[_row_tile(off), :], fill_sem)

    def blk_copy(b):
        return pltpu.make_async_copy(zero_ref, xs_ref.at[pl.ds(pl.multiple_of(b * (rb * rt), rt), rb * rt), :],
                                     fill_sem)

    @pl.when(i == 0)
    def _():
        zero_ref[...] = jnp.zeros_like(zero_ref)

    @pl.when(i < N_EXPERTS)
    def _():
        lo = plo_ref[i]

        def fill(r, c):
            pad_copy(lo + r * rt).start()
            return c

        lax.fori_loop(0, pn_ref[i], fill, 0)

    @pl.when(i == N_EXPERTS)
    def _():
        def fill(b, c):
            blk_copy(b).start()
            return c

        lax.fori_loop(nu_ref[0], n_blk, fill, 0)

    @pl.when(i == nsteps - 1)
    def _():
        slot_drain(1 - slot)
        slot_drain(slot)

        def per_expert(e, c):
            def one(r, cc):
                pad_copy(0).wait()
                return cc

            return lax.fori_loop(0, pn_ref[e], one, c)

        lax.fori_loop(0, N_EXPERTS, per_expert, 0)

        def one_blk(b, c):
            blk_copy(0).wait()
            return c

        lax.fori_loop(nu_ref[0], n_blk, one_blk, 0)


def _dispatch(d1, d2, pad_lo, pad_n, n_used, h, g, cap, tm):
    n, d = h.shape
    assert n // tm > N_EXPERTS + 1 and d == ROW_TILE * LANES
    return pl.pallas_call(
        _dispatch_kernel,
        out_shape=jax.ShapeDtypeStruct((cap * ROW_TILE, LANES), F32),
        grid_spec=pltpu.PrefetchScalarGridSpec(
            num_scalar_prefetch=5,
            grid=(n // tm,),
            in_specs=[
                pl.BlockSpec((tm, d), lambda i, *_: (i, 0)),
                pl.BlockSpec((1, d), lambda i, *_: (0, 0)),
            ],
            out_specs=pl.BlockSpec(memory_space=pl.ANY),
            scratch_shapes=[pltpu.VMEM((2, tm * ROW_TILE, LANES), F32),
                            pltpu.VMEM((MOE_ROWS * ROW_TILE, LANES), F32),
                            pltpu.SemaphoreType.DMA((2,)), pltpu.SemaphoreType.DMA],
        ),
        compiler_params=_cparams("arbitrary"),
        name="moe_dispatch",
    )(d1, d2, pad_lo, pad_n, n_used, h, g.reshape(1, d))


def _expert_kernel(be_ref, nu_ref, x_ref, w1_ref, w3_ref, w2_ref, y_ref, w1c, w3c, w2c):
    i = pl.program_id(0)
    used = i < nu_ref[0]
    prev = be_ref[jnp.maximum(i - 1, 0)]
    fresh = jnp.logical_or(i == 0, be_ref[i] != prev)

    @pl.when(jnp.logical_and(used, fresh))
    def _():
        w1c[...] = w1_ref[0, 0].astype(BF16)
        w3c[...] = w3_ref[0, 0].astype(BF16)
        w2c[...] = w2_ref[0, 0].astype(BF16)

    @pl.when(used)
    def _():
        rb = x_ref.shape[0] // ROW_TILE
        x = _load_row_tiles(x_ref, rb).astype(BF16)
        hf = w1c.shape[1] // 2
        y = None
        for c in range(2):
            cs = slice(c * hf, (c + 1) * hf)
            a = jnp.dot(x, w1c[:, cs], preferred_element_type=F32)
            b = jnp.dot(x, w3c[:, cs], preferred_element_type=F32)
            act = (jax.nn.silu(a) * b).astype(BF16)
            part = jnp.dot(act, w2c[cs, :], preferred_element_type=F32)
            y = part if y is None else y + part
        _store_row_tiles(y_ref, y)

    @pl.when(jnp.logical_not(used))
    def _():
        y_ref[...] = jnp.zeros_like(y_ref)


def _experts(blk_e, n_used, xs, w1, w3, w2, layer):
    d = ROW_TILE * LANES
    rb = MOE_ROWS
    n_blk = xs.shape[0] // (rb * ROW_TILE)
    ff = w1.shape[3]
    blk = (rb * ROW_TILE, LANES)
    return pl.pallas_call(
        _expert_kernel,
        out_shape=jax.ShapeDtypeStruct(xs.shape, F32),
        grid_spec=pltpu.PrefetchScalarGridSpec(
            num_scalar_prefetch=2,
            grid=(n_blk,),
            in_specs=[
                pl.BlockSpec(blk, lambda i, be, nu: (jnp.minimum(i, jnp.maximum(nu[0] - 1, 0)), 0)),
                pl.BlockSpec((1, 1, d, ff), lambda i, be, nu: (layer, be[i], 0, 0)),
                pl.BlockSpec((1, 1, d, ff), lambda i, be, nu: (layer, be[i], 0, 0)),
                pl.BlockSpec((1, 1, ff, d), lambda i, be, nu: (layer, be[i], 0, 0)),
            ],
            out_specs=pl.BlockSpec(blk, lambda i, be, nu: (i, 0)),
            scratch_shapes=[pltpu.VMEM((d, ff), BF16), pltpu.VMEM((d, ff), BF16),
                            pltpu.VMEM((ff, d), BF16)],
        ),
        compiler_params=_cparams("arbitrary"),
        name="moe_experts",
    )(blk_e, n_used, xs, w1, w3, w2)


def _combine_kernel(d1_ref, d2_ref, h_ref, meta_ref, g_ref, ys_ref, o_ref, buf_ref, sem, *, final_norm):
    i = pl.program_id(0)
    nsteps = pl.num_programs(0)
    tm = h_ref.shape[0]
    slot = i % 2

    def fetch(tile, s):
        base = tile * tm

        def issue(r8, c):
            for k in range(MOE_ISSUE_UNROLL):
                r = r8 * MOE_ISSUE_UNROLL + k
                pltpu.make_async_copy(ys_ref.at[_row_tile(d1_ref[base + r]), :],
                                      buf_ref.at[s, 0, _row_tile(r * ROW_TILE), :], sem.at[s]).start(priority=0)
                pltpu.make_async_copy(ys_ref.at[_row_tile(d2_ref[base + r]), :],
                                      buf_ref.at[s, 1, _row_tile(r * ROW_TILE), :], sem.at[s]).start(priority=1)
            return c

        lax.fori_loop(0, tm // MOE_ISSUE_UNROLL, issue, 0)

    @pl.when(i == 0)
    def _():
        fetch(0, 0)

    @pl.when(i + 1 < nsteps)
    def _():
        fetch(i + 1, 1 - slot)

    for j in range(2):
        pltpu.make_async_copy(ys_ref.at[pl.ds(0, tm * ROW_TILE), :], buf_ref.at[slot, j], sem.at[slot]).wait()

    meta = meta_ref[...]
    y1 = _load_row_tiles(buf_ref.at[slot, 0], tm)
    y2 = _load_row_tiles(buf_ref.at[slot, 1], tm)
    out = h_ref[...] + (meta[:, 2:3] * y1 + meta[:, 3:4] * y2)
    if final_norm:
        out = _rmsnorm(out, g_ref[...])
    o_ref[...] = out


def _combine(d1, d2, h, meta, ys, g_final, tm, final_norm):
    n, d = h.shape
    return pl.pallas_call(
        functools.partial(_combine_kernel, final_norm=final_norm),
        out_shape=jax.ShapeDtypeStruct((n, d), F32),
        grid_spec=pltpu.PrefetchScalarGridSpec(
            num_scalar_prefetch=2,
            grid=(n // tm,),
            in_specs=[
                pl.BlockSpec((tm, d), lambda i, *_: (i, 0)),
                pl.BlockSpec((tm, LANES), lambda i, *_: (i, 0)),
                pl.BlockSpec((1, d), lambda i, *_: (0, 0)),
                pl.BlockSpec(memory_space=pl.ANY),
            ],
            out_specs=pl.BlockSpec((tm, d), lambda i, *_: (i, 0)),
            scratch_shapes=[pltpu.VMEM((2, 2, tm * ROW_TILE, LANES), F32), pltpu.SemaphoreType.DMA((2,))],
        ),
        compiler_params=_cparams("arbitrary"),
        name="moe_combine",
    )(d1, d2, h, meta, g_final.reshape(1, d), ys)


def _hier_moe(h, ln_g, w_group, w_expert, w1, w3, w2, layer, g_final, final_norm):
    n, d = h.shape
    rb = MOE_ROWS
    cap = 2 * n + N_EXPERTS * rb
    n_blk = cap // rb
    w_r = jnp.zeros((d, LANES), F32).at[:, :N_GROUPS].set(w_group)
    w_r = w_r.at[:, N_GROUPS:N_GROUPS + N_EXPERTS].set(w_expert)
    w_hi = w_r.astype(BF16)
    w_lo = (w_r - w_hi.astype(F32)).astype(BF16)
    meta, idx, counts = _router(h, ln_g, jnp.concatenate([w_hi, w_lo], axis=1), 512)
    cnt = counts[0, :N_EXPERTS].astype(jnp.int32)
    padded = ((cnt + rb - 1) // rb) * rb
    pad_end = jnp.cumsum(padded)
    pad_start = pad_end - padded
    blk_start = jnp.arange(n_blk, dtype=jnp.int32) * rb
    blk_e = jnp.minimum(jnp.sum(pad_end[None, :] <= blk_start[:, None], axis=1), N_EXPERTS - 1).astype(jnp.int32)
    n_used = (pad_end[-1] // rb).astype(jnp.int32).reshape(1)
    experts = jnp.arange(N_EXPERTS, dtype=jnp.int32)[None, :]
    start_of = lambda e: jnp.sum(jnp.where(e[:, None] == experts, pad_start[None, :], 0), axis=1)
    d1 = (start_of(idx[0]) + idx[2]) * ROW_TILE
    d2 = (start_of(idx[1]) + idx[3]) * ROW_TILE
    xs = _dispatch(d1, d2, (pad_start + cnt) * ROW_TILE, padded - cnt, n_used, h, ln_g, cap, 256)
    ys = _experts(blk_e, n_used, xs, w1, w3, w2, layer)
    return _combine(d1, d2, h, meta, ys, g_final, 256, final_norm)


def kernel(x, ln_mix, ln_ffn, ln_final, ssm_w_in, ssm_lam_re, ssm_lam_im, ssm_log_dt, ssm_b_re, ssm_b_im, ssm_c_re, ssm_c_im, ssm_d, ssm_w_out, attn_w_qkv, attn_w_o, moe_w_group, moe_w_expert, moe_w1, moe_w3, moe_w2):
    bsz, seq, d = x.shape
    n = bsz * seq
    h = x.reshape(n, d)

    u = _norm_proj(h, ln_mix[0], ssm_w_in[0].astype(BF16), 512, F32, "s5_in_proj")
    mats = _s5_mats(ssm_lam_re[0], ssm_lam_im[0], ssm_log_dt[0], ssm_b_re[0], ssm_b_im[0],
                    ssm_c_re[0], ssm_c_im[0])
    z = _s5_core(u, mats, ssm_d[0], bsz, seq)
    h = _glu_out(z, ssm_w_out[0].astype(BF16), h, 512)
    h = _hier_moe(h, ln_ffn[0], moe_w_group[0], moe_w_expert[0], moe_w1, moe_w3, moe_w2, 0,
                  ln_final, False)

    q_t, k, v_t = _qkv_proj(h, ln_mix[1], attn_w_qkv[0], 512)
    o_t = _moba(q_t, k, v_t, bsz, seq)
    h = _proj_res_t(o_t, attn_w_o[0].astype(BF16), h, 512)
    h = _hier_moe(h, ln_ffn[1], moe_w_group[1], moe_w_expert[1], moe_w1, moe_w3, moe_w2, 1,
                  ln_final, True)
    return h.reshape(bsz, seq, d)
```

```python
import functools
import math

import jax
import jax.numpy as jnp
from jax import lax
from jax.experimental import pallas as pl
from jax.experimental.pallas import tpu as pltpu

F32 = jnp.float32
BF16 = jnp.bfloat16

D_MODEL = 1024
RMS_EPS = 1e-6
NEG_INF = -1e30

SSM_GROUP = 16
SSM_GROUPS = D_MODEL // SSM_GROUP
SSM_STATE = 64
SSM_CHUNK = 16
SSM_GB = 8
SSM_WIN = 8

ATT_HEADS = 8
HEAD_DIM = 128
MOBA_BLOCK = 256
MOBA_TOPK = 3

N_GROUPS = 4
EXPERTS_PER_GROUP = 8
N_EXPERTS = 32
EXPERT_FF = 512
MOE_ROWS = 256
MOE_DISPATCH_ROWS = 256
MOE_COMBINE_ROWS = 512
DENSE_ROWS = 1024

LANES = 128
VMEM_LIMIT = 56 * 1024 * 1024

_NT = (((1,), (1,)), ((), ()))
_TN = (((0,), (0,)), ((), ()))


def _cparams(*sem):
    return pltpu.CompilerParams(dimension_semantics=sem, vmem_limit_bytes=VMEM_LIMIT)


def _rmsnorm(x, g):
    return x * lax.rsqrt(jnp.mean(x * x, axis=-1, keepdims=True) + RMS_EPS) * g


def _norm_proj_kernel(x_ref, g_ref, w_ref, o_ref):
    xn = _rmsnorm(x_ref[...], g_ref[...]).astype(BF16)
    o_ref[...] = jnp.dot(xn, w_ref[...], preferred_element_type=F32).astype(o_ref.dtype)


def _norm_proj(x, g, w_bf16, tm, out_dtype, name):
    n, d = x.shape
    dout = w_bf16.shape[1]
    return pl.pallas_call(
        _norm_proj_kernel,
        out_shape=jax.ShapeDtypeStruct((n, dout), out_dtype),
        grid=(n // tm,),
        in_specs=[
            pl.BlockSpec((tm, d), lambda i: (i, 0)),
            pl.BlockSpec((1, d), lambda i: (0, 0)),
            pl.BlockSpec((d, dout), lambda i: (0, 0)),
        ],
        out_specs=pl.BlockSpec((tm, dout), lambda i: (i, 0)),
        compiler_params=_cparams("parallel"),
        name=name,
    )(x, g.reshape(1, d), w_bf16)


def _s5_mats(lam_re, lam_im, log_dt, b_re, b_im, c_re, c_im):
    g_, p_, c_, t_ = SSM_GROUPS, SSM_STATE, SSM_GROUP, SSM_CHUNK
    lr = jnp.minimum(lam_re, -1e-4)
    li = lam_im
    dt = jnp.exp(log_dt)[:, None]
    mag = jnp.exp(lr * dt)
    abar_re = mag * jnp.cos(li * dt)
    abar_im = mag * jnp.sin(li * dt)
    den = lr * lr + li * li
    nr = abar_re - 1.0
    gam_re = (nr * lr + abar_im * li) / den
    gam_im = (abar_im * lr - nr * li) / den
    bb_re = gam_re[..., None] * b_re - gam_im[..., None] * b_im
    bb_im = gam_re[..., None] * b_im + gam_im[..., None] * b_re

    def powers(ns):
        nf = jnp.asarray(ns, F32)[None, :, None]
        pm = jnp.exp(nf * (lr * dt)[:, None, :])
        ang = nf * (li * dt)[:, None, :]
        return pm * jnp.cos(ang), pm * jnp.sin(ang)

    pr, pi = powers(list(range(t_ + 1)))
    ca_re = c_re[:, None] * pr[:, :, None, :] - c_im[:, None] * pi[:, :, None, :]
    ca_im = c_re[:, None] * pi[:, :, None, :] + c_im[:, None] * pr[:, :, None, :]
    ca_n = jnp.concatenate([ca_re[:, :t_], ca_im[:, :t_]], axis=-1).reshape(g_, t_ * c_, 2 * p_)
    bb_s = jnp.concatenate([bb_re, -bb_im], axis=1)
    kflat = jnp.einsum('gxp,gpc->gcx', ca_n, bb_s, precision='highest')
    prs = pr[:, :t_][:, ::-1][:, :, None, :]
    pis = pi[:, :t_][:, ::-1][:, :, None, :]
    bt_re = bb_re.transpose(0, 2, 1)[:, None]
    bt_im = bb_im.transpose(0, 2, 1)[:, None]
    w_re = prs * bt_re - pis * bt_im
    w_im = prs * bt_im + pis * bt_re
    w = jnp.concatenate([w_re, w_im, w_im, w_re], axis=-1).reshape(g_, t_ * c_, 4 * p_)
    prt = pr.transpose(0, 2, 1)[:, :, 1:, None]
    pit = pi.transpose(0, 2, 1)[:, :, 1:, None]
    ct_re = c_re.transpose(0, 2, 1)[:, :, None, :]
    ct_im = c_im.transpose(0, 2, 1)[:, :, None, :]
    v_re = (ct_re * prt - ct_im * pit).reshape(g_, p_, t_ * c_)
    v_im = -(ct_re * pit + ct_im * prt).reshape(g_, p_, t_ * c_)
    v = jnp.concatenate([v_re, v_im], axis=1)
    qr, qi = powers([t_ * (1 << j) for j in range(4)])
    rows = []
    for j in range(4):
        ar, ai = qr[:, j], qi[:, j]
        rows += [jnp.concatenate([ar, ar], -1), jnp.concatenate([-ai, ai], -1),
                 jnp.concatenate([ai, -ai], -1)]
    rows += [jnp.zeros_like(rows[0])] * 4
    coef = jnp.stack(rows, axis=1)
    return kflat.astype(F32), w.astype(BF16), v.astype(BF16), coef.astype(F32)


def _s5_perm():
    r = jnp.arange(8 * LANES)
    col = ((r % LANES) // SSM_GROUP) * LANES + (r // LANES) * SSM_GROUP + r % SSM_GROUP
    p = (col[:, None] == r[None, :]).astype(BF16)
    return p, p.T


def _s5_kernel(u_ref, kf_ref, w_ref, v_ref, coef_ref, d_ref, p_ref, pt_ref, o_ref,
               m_ref, vf_ref, zf_ref, ea_ref, eb_ref, sp_ref):
    t_, c_, gb, win = SSM_CHUNK, SSM_GROUP, SSM_GB, SSM_WIN
    nk = u_ref.shape[0] // t_
    p2 = 2 * SSM_STATE
    tc = t_ * c_

    @pl.when(pl.program_id(1) == 0)
    def _():
        lane_tc = lax.broadcasted_iota(jnp.int32, (c_, tc), 1)
        for g in range(gb):
            kf = kf_ref[g]
            for s in range(t_):
                rolled = kf if s == 0 else pltpu.roll(kf, s * c_, axis=1)
                m_ref[g, s * c_:(s + 1) * c_, :] = jnp.where(lane_tc >= s * c_, rolled, 0.0).astype(BF16)

    halves = tc // LANES

    def timestep(t):
        return pl.ds(t, nk, stride=t_)

    def flat(g):
        return jnp.concatenate([vf_ref[h, :, g * LANES:(g + 1) * LANES] for h in range(halves)], axis=1)

    for h in range(halves):
        x = jnp.concatenate([u_ref[timestep(8 * h + i), :].astype(BF16) for i in range(8)], axis=1)
        vf_ref[h] = jnp.dot(x, p_ref[...], preferred_element_type=F32).astype(BF16)

    row = lax.broadcasted_iota(jnp.int32, (nk, p2), 0)

    def shift(x, d):
        return jnp.where(row < d, 0.0, pltpu.roll(x, d, axis=0))

    for g in range(gb):
        sc = jnp.dot(flat(g), w_ref[g], preferred_element_type=F32)
        xa, xb = shift(sc[:, :p2], 1), shift(sc[:, p2:], 1)
        cf = coef_ref[g]
        for lvl in range(3):
            pp, qa, qb = cf[3 * lvl:3 * lvl + 1], cf[3 * lvl + 1:3 * lvl + 2], cf[3 * lvl + 2:3 * lvl + 3]
            sa, sb = shift(xa, 1 << lvl), shift(xb, 1 << lvl)
            xa, xb = xa + pp * sa + qa * sb, xb + pp * sb + qb * sa
        ea_ref[g] = xa
        eb_ref[g] = xb

    cfs = [coef_ref[g] for g in range(gb)]
    zero = jnp.zeros((win, p2), F32)
    state = [(zero, zero)] * gb
    for j in range(nk // win):
        rs = slice(j * win, (j + 1) * win)
        for g in range(gb):
            pp, qa, qb = cfs[g][9:10], cfs[g][10:11], cfs[g][11:12]
            s_a, s_b = state[g]
            n_a = pp * s_a + qa * s_b + ea_ref[g, rs, :]
            n_b = pp * s_b + qb * s_a + eb_ref[g, rs, :]
            sp_ref[g, rs, :] = n_a
            state[g] = (n_a, n_b)

    for g in range(gb):
        x = flat(g)
        y = jnp.dot(x, m_ref[g], preferred_element_type=F32)
        y = y + jnp.dot(sp_ref[g].astype(BF16), v_ref[g], preferred_element_type=F32)
        z = jax.nn.gelu(y + d_ref[g] * x.astype(F32)).astype(BF16)
        for h in range(halves):
            zf_ref[h, :, g * LANES:(g + 1) * LANES] = z[:, h * LANES:(h + 1) * LANES]

    for h in range(halves):
        zn = jnp.dot(zf_ref[h], pt_ref[...], preferred_element_type=F32)
        for i in range(8):
            o_ref[timestep(8 * h + i), :] = zn[:, i * LANES:(i + 1) * LANES]


def _s5_core(u, mats, d_skip, bsz, seq):
    kflat, w, v, coef = mats
    n, d = u.shape
    gb, tc, p2 = SSM_GB, SSM_CHUNK * SSM_GROUP, 2 * SSM_STATE
    nk = seq // SSM_CHUNK
    halves = tc // LANES
    perm, perm_t = _s5_perm()
    dflat = jnp.tile(d_skip.reshape(SSM_GROUPS, 1, SSM_GROUP), (1, 1, SSM_CHUNK))
    spec3 = lambda a, b: pl.BlockSpec((gb, a, b), lambda j, bb: (j, 0, 0))
    const = pl.BlockSpec(perm.shape, lambda j, bb: (0, 0))
    return pl.pallas_call(
        _s5_kernel,
        out_shape=jax.ShapeDtypeStruct((n, d), F32),
        grid=(SSM_GROUPS // gb, bsz),
        in_specs=[
            pl.BlockSpec((seq, LANES), lambda j, bb: (bb, j)),
            spec3(SSM_GROUP, tc), spec3(tc, 2 * p2), spec3(p2, tc), spec3(16, p2), spec3(1, tc),
            const, const,
        ],
        out_specs=pl.BlockSpec((seq, LANES), lambda j, bb: (bb, j)),
        scratch_shapes=[pltpu.VMEM((gb, tc, tc), BF16),
                        pltpu.VMEM((halves, nk, gb * LANES), BF16),
                        pltpu.VMEM((halves, nk, gb * LANES), BF16),
                        pltpu.VMEM((gb, nk, p2), F32), pltpu.VMEM((gb, nk, p2), F32),
                        pltpu.VMEM((gb, nk, p2), F32)],
        compiler_params=_cparams("arbitrary", "arbitrary"),
        name="s5_core",
    )(u, kflat, w, v, coef, dflat, perm, perm_t)


def _glu_out_kernel(z_ref, w_ref, x_ref, o_ref):
    vg = jnp.dot(z_ref[...].astype(BF16), w_ref[...], preferred_element_type=F32)
    d = o_ref.shape[1]
    o_ref[...] = x_ref[...] + vg[:, :d] * jax.nn.sigmoid(vg[:, d:])


def _glu_out(z, w_bf16, x, tm):
    n, d = x.shape
    return pl.pallas_call(
        _glu_out_kernel,
        out_shape=jax.ShapeDtypeStruct((n, d), F32),
        grid=(n // tm,),
        in_specs=[
            pl.BlockSpec((tm, z.shape[1]), lambda i: (i, 0)),
            pl.BlockSpec(w_bf16.shape, lambda i: (0, 0)),
            pl.BlockSpec((tm, d), lambda i: (i, 0)),
        ],
        out_specs=pl.BlockSpec((tm, d), lambda i: (i, 0)),
        compiler_params=_cparams("parallel"),
        name="s5_glu_out",
    )(z, w_bf16, x)


def _proj_res_t_kernel(at_ref, w_ref, x_ref, o_ref):
    o_ref[...] = x_ref[...] + lax.dot_general(at_ref[...], w_ref[...], _TN, preferred_element_type=F32)


def _proj_res_t(a_t, w_bf16, x, tm):
    n, d = x.shape
    return pl.pallas_call(
        _proj_res_t_kernel,
        out_shape=jax.ShapeDtypeStruct((n, d), F32),
        grid=(n // tm,),
        in_specs=[
            pl.BlockSpec((a_t.shape[0], tm), lambda i: (0, i)),
            pl.BlockSpec(w_bf16.shape, lambda i: (0, 0)),
            pl.BlockSpec((tm, d), lambda i: (i, 0)),
        ],
        out_specs=pl.BlockSpec((tm, d), lambda i: (i, 0)),
        compiler_params=_cparams("parallel"),
        name="attn_out_proj",
    )(a_t, w_bf16, x)


def _qkv_kernel(x_ref, g_ref, wq_ref, wk_ref, wv_ref, qt_ref, k_ref, vt_ref):
    xn = _rmsnorm(x_ref[...], g_ref[...]).astype(BF16)
    c = (HEAD_DIM ** -0.5) * math.log2(math.e)
    qt_ref[...] = (lax.dot_general(wq_ref[...], xn, _NT, preferred_element_type=F32) * c).astype(BF16)
    k_ref[...] = jnp.dot(xn, wk_ref[...], preferred_element_type=F32).astype(BF16)
    vt = lax.dot_general(wv_ref[...], xn, _NT, preferred_element_type=F32).astype(BF16)
    for c in range(vt_ref.shape[0]):
        vt_ref[c] = vt[:, c * MOBA_BLOCK:(c + 1) * MOBA_BLOCK]


def _qkv_proj(x, g, w_qkv, tm):
    n, d = x.shape
    da = ATT_HEADS * HEAD_DIM
    wq_t = w_qkv[:, :da].T.astype(BF16)
    wk = w_qkv[:, da:2 * da].astype(BF16)
    wv_t = w_qkv[:, 2 * da:].T.astype(BF16)
    full = lambda shp: pl.BlockSpec(shp, lambda i: (0, 0))
    tb = tm // MOBA_BLOCK
    return pl.pallas_call(
        _qkv_kernel,
        out_shape=(jax.ShapeDtypeStruct((da, n), BF16), jax.ShapeDtypeStruct((n, da), BF16),
                   jax.ShapeDtypeStruct((n // MOBA_BLOCK, da, MOBA_BLOCK), BF16)),
        grid=(n // tm,),
        in_specs=[pl.BlockSpec((tm, d), lambda i: (i, 0)), full((1, d)),
                  full((da, d)), full((d, da)), full((da, d))],
        out_specs=(pl.BlockSpec((da, tm), lambda i: (0, i)), pl.BlockSpec((tm, da), lambda i: (i, 0)),
                   pl.BlockSpec((tb, da, MOBA_BLOCK), lambda i: (i, 0, 0))),
        compiler_params=_cparams("parallel"),
        name="attn_qkv_proj",
    )(x, g.reshape(1, d), wq_t, wk, wv_t)


MOBA_CHAINS = 4
MOBA_SUM_ROWS = 16


def _moba_kernel(qt_ref, k_ref, vt_ref, et_ref, ot_ref, s_scr, acc_scr, m_scr, *, nb):
    blk, dh = MOBA_BLOCK, HEAD_DIM
    brow = lax.broadcasted_iota(jnp.int32, (nb, blk), 0)
    r_ix = lax.broadcasted_iota(jnp.int32, (blk, blk), 0)
    c_ix = lax.broadcasted_iota(jnp.int32, (blk, blk), 1)
    zpad = jnp.zeros((2 * LANES - dh - nb, blk), BF16)
    kmean = jnp.mean(k_ref[...].astype(F32).reshape(nb, blk, dh), axis=1)

    def k_aug(t):
        rows = pl.ds(pl.multiple_of(t * blk, blk), blk)
        return jnp.concatenate([k_ref[rows, :], et_ref[rows, :]], axis=1)

    ones = jnp.ones((MOBA_SUM_ROWS, blk), BF16)

    def values(t):
        return jnp.concatenate([vt_ref[t], ones], axis=0)

    def absorb(c, s, vt):
        m_old = m_scr[c]
        m_new = jnp.maximum(m_old, jnp.max(s, axis=0, keepdims=True))
        p = jnp.exp2(s - m_new).astype(BF16)
        acc_scr[c] = jnp.exp2(m_old - m_new) * acc_scr[c] + jnp.dot(vt, p, preferred_element_type=F32)
        m_scr[c] = m_new

    def query_block(i):
        qs = slice(i * blk, (i + 1) * blk)
        qt = qt_ref[:, qs]
        if i > MOBA_TOPK:
            gate = jnp.dot(kmean, qt.astype(F32), preferred_element_type=F32,
                           precision=lax.Precision.HIGHEST)
            gate = jnp.where(brow < i, gate, NEG_INF)
            sel = brow >= i
            for _ in range(MOBA_TOPK):
                gm = jnp.max(gate, axis=0, keepdims=True)
                first = jnp.min(jnp.where(gate == gm, brow, nb), axis=0, keepdims=True)
                pick = brow == first
                sel = jnp.logical_or(sel, pick)
                gate = jnp.where(pick, -jnp.inf, gate)
            bias = jnp.where(sel, 0.0, NEG_INF).astype(BF16)
        else:
            bias = jnp.zeros((nb, blk), BF16)
        return jnp.concatenate([qt, bias, zpad], axis=0)

    nq = MOBA_CHAINS
    for i0 in range(0, nb, nq):
        chains = range(min(nq, nb - i0))
        q_aug = [query_block(i0 + c) for c in chains]
        m_scr[...] = jnp.full(m_scr.shape, NEG_INF, F32)
        acc_scr[...] = jnp.zeros(acc_scr.shape, F32)

        def scores(t, slot, cs):
            ka = k_aug(t)
            for c in cs:
                s_scr[c, slot] = jnp.dot(ka, q_aug[c], preferred_element_type=F32)

        scores(0, 0, chains)
        if i0 > 0:
            def body(j, carry):
                cur = [s_scr[c, j % 2] for c in chains]
                scores(j + 1, (j + 1) % 2, chains)
                vt = values(j)
                for c in chains:
                    absorb(c, cur[c], vt)
                return carry

            lax.fori_loop(0, i0, body, 0)
        for t in chains:
            cur = {c: s_scr[c, (i0 + t) % 2] for c in chains if c >= t}
            later = [c for c in chains if c > t]
            if later:
                scores(i0 + t + 1, (i0 + t + 1) % 2, later)
            vt = values(i0 + t)
            absorb(t, jnp.where(r_ix <= c_ix, cur[t], NEG_INF), vt)
            for c in later:
                absorb(c, cur[c], vt)
        for c in chains:
            qs = slice((i0 + c) * blk, (i0 + c + 1) * blk)
            ot_ref[:, qs] = (acc_scr[c, :dh, :] / acc_scr[c, dh:dh + 1, :]).astype(ot_ref.dtype)


def _moba(q_t, k, v_t, bsz, seq):
    nb = seq // MOBA_BLOCK
    da = ATT_HEADS * HEAD_DIM
    e_t = ((jnp.arange(seq) // MOBA_BLOCK)[:, None] == jnp.arange(LANES)[None, :]).astype(BF16)
    tspec = pl.BlockSpec((HEAD_DIM, seq), lambda b, h: (h, b))
    return pl.pallas_call(
        functools.partial(_moba_kernel, nb=nb),
        out_shape=jax.ShapeDtypeStruct((da, bsz * seq), BF16),
        grid=(bsz, ATT_HEADS),
        in_specs=[tspec, pl.BlockSpec((seq, HEAD_DIM), lambda b, h: (b, h)),
                  pl.BlockSpec((nb, HEAD_DIM, MOBA_BLOCK), lambda b, h: (b, h, 0)),
                  pl.BlockSpec((seq, LANES), lambda b, h: (0, 0))],
        out_specs=tspec,
        scratch_shapes=[pltpu.VMEM((MOBA_CHAINS, 2, MOBA_BLOCK, MOBA_BLOCK), F32),
                        pltpu.VMEM((MOBA_CHAINS, HEAD_DIM + MOBA_SUM_ROWS, MOBA_BLOCK), F32),
                        pltpu.VMEM((MOBA_CHAINS, 1, MOBA_BLOCK), F32)],
        compiler_params=_cparams("parallel", "parallel"),
        name="moba_attn",
    )(q_t, k, v_t, e_t)


def _router_kernel(h_ref, g_ref, w_ref, tri_ref, meta_ref, idx_ref, cnt_ref):
    xn = _rmsnorm(h_ref[...], g_ref[...])
    x_hi = xn.astype(BF16)
    x_lo = (xn - x_hi.astype(F32)).astype(BF16)
    t = jnp.dot(x_hi, w_ref[...], preferred_element_type=F32)
    logits = (t[:, :LANES] + t[:, LANES:]) + jnp.dot(x_lo, w_ref[:, :LANES], preferred_element_type=F32)
    tm = logits.shape[0]
    lane = lax.broadcasted_iota(jnp.int32, (tm, LANES), 1)
    ninf = -jnp.inf
    lg = jnp.where(lane < N_GROUPS, logits, ninf)
    gm = jnp.max(lg, axis=-1, keepdims=True)
    g_idx = jnp.min(jnp.where(lg == gm, lane, LANES), axis=-1, keepdims=True)
    g_gate = 1.0 / jnp.sum(jnp.exp(lg - gm), axis=-1, keepdims=True)
    lo = N_GROUPS + EXPERTS_PER_GROUP * g_idx
    le = jnp.where((lane >= lo) & (lane < lo + EXPERTS_PER_GROUP), logits, ninf)
    m1 = jnp.max(le, axis=-1, keepdims=True)
    i1 = jnp.min(jnp.where(le == m1, lane, LANES), axis=-1, keepdims=True)
    le2 = jnp.where(lane == i1, ninf, le)
    m2 = jnp.max(le2, axis=-1, keepdims=True)
    i2 = jnp.min(jnp.where(le2 == m2, lane, LANES), axis=-1, keepdims=True)
    p2 = jnp.exp(m2 - m1)
    gate1 = g_gate / (1.0 + p2)
    gate2 = g_gate * p2 / (1.0 + p2)
    e1 = (i1 - N_GROUPS).astype(F32)
    e2 = (i2 - N_GROUPS).astype(F32)
    meta = jnp.where(lane == 0, e1, jnp.where(lane == 1, e2, jnp.where(lane == 2, gate1,
                     jnp.where(lane == 3, gate2, 0.0))))
    meta_ref[...] = meta
    oh1 = (lane == i1 - N_GROUPS).astype(F32)
    oh2 = (lane == i2 - N_GROUPS).astype(F32)
    both = oh1 + oh2

    @pl.when(pl.program_id(0) == 0)
    def _():
        cnt_ref[...] = jnp.zeros_like(cnt_ref)

    seen = cnt_ref[...]
    before = jnp.dot(tri_ref[...], both.astype(BF16), preferred_element_type=F32) + seen
    r1 = jnp.sum(before * oh1, axis=-1, keepdims=True)
    r2 = jnp.sum(before * oh2, axis=-1, keepdims=True)
    slab = jnp.where(lane == 0, e1, jnp.where(lane == 1, e2, jnp.where(lane == 2, r1,
                     jnp.where(lane == 3, r2, 0.0))))
    idx_ref[...] = jnp.transpose(slab)[0:8, :].astype(jnp.int32)
    cnt_ref[...] = seen + jnp.sum(both, axis=0, keepdims=True)


def _router(h, g, w_r, tm):
    n, d = h.shape
    tri = (jnp.arange(tm)[:, None] > jnp.arange(tm)[None, :]).astype(BF16)
    return pl.pallas_call(
        _router_kernel,
        out_shape=(jax.ShapeDtypeStruct((n, LANES), F32), jax.ShapeDtypeStruct((8, n), jnp.int32),
                   jax.ShapeDtypeStruct((1, LANES), F32)),
        grid=(n // tm,),
        in_specs=[
            pl.BlockSpec((tm, d), lambda i: (i, 0)),
            pl.BlockSpec((1, d), lambda i: (0, 0)),
            pl.BlockSpec((d, 2 * LANES), lambda i: (0, 0)),
            pl.BlockSpec((tm, tm), lambda i: (0, 0)),
        ],
        out_specs=(pl.BlockSpec((tm, LANES), lambda i: (i, 0)),
                   pl.BlockSpec((8, tm), lambda i: (0, i)),
                   pl.BlockSpec((1, LANES), lambda i: (0, 0))),
        compiler_params=_cparams("arbitrary"),
        name="moe_router",
    )(h, g.reshape(1, d), w_r, tri)


MOE_ISSUE_UNROLL = 8
ROW_TILE = 8


def _row_tile(off):
    return pl.ds(pl.multiple_of(off, ROW_TILE), ROW_TILE)


def _store_row_tiles(ref, x):
    rows = x.shape[0]
    for c in range(ROW_TILE):
        ref[pl.ds(c, rows, stride=ROW_TILE), :] = x[:, c * LANES:(c + 1) * LANES]


def _load_row_tiles(ref, rows):
    return jnp.concatenate([ref[pl.ds(c, rows, stride=ROW_TILE), :] for c in range(ROW_TILE)], axis=1)


def _dispatch_kernel(d1_ref, d2_ref, plo_ref, pn_ref, nu_ref, h_ref, g_ref, xs_ref,
                     buf_ref, zero_ref, sem, fill_sem):
    i = pl.program_id(0)
    nsteps = pl.num_programs(0)
    tm = h_ref.shape[0]
    rt = ROW_TILE
    rb = zero_ref.shape[0] // rt
    n_blk = xs_ref.shape[0] // (rb * rt)
    slot = i % 2
    base = i * tm

    def slot_drain(s):
        cp = pltpu.make_async_copy(buf_ref.at[s], xs_ref.at[pl.ds(0, tm * rt), :], sem.at[s])
        cp.wait()
        cp.wait()

    @pl.when(i >= 2)
    def _():
        slot_drain(slot)

    _store_row_tiles(buf_ref.at[slot], _rmsnorm(h_ref[...], g_ref[...]))

    def issue(r8, c):
        for k in range(MOE_ISSUE_UNROLL):
            r = r8 * MOE_ISSUE_UNROLL + k
            src = buf_ref.at[slot, _row_tile(r * rt), :]
            pltpu.make_async_copy(src, xs_ref.at[_row_tile(d1_ref[base + r]), :], sem.at[slot]).start(priority=0)
            pltpu.make_async_copy(src, xs_ref.at[_row_tile(d2_ref[base + r]), :], sem.at[slot]).start(priority=1)
        return c

    lax.fori_loop(0, tm // MOE_ISSUE_UNROLL, issue, 0)

    def pad_copy(off):
        return pltpu.make_async_copy(zero_ref.at[pl.ds(0, rt), :], xs_ref.at[_row_tile(off), :], fill_sem)

    def blk_copy(b):
        return pltpu.make_async_copy(zero_ref, xs_ref.at[pl.ds(pl.multiple_of(b * (rb * rt), rt), rb * rt), :],
                                     fill_sem)

    @pl.when(i == 0)
    def _():
        zero_ref[...] = jnp.zeros_like(zero_ref)

    @pl.when(i < N_EXPERTS)
    def _():
        lo = plo_ref[i]

        def fill(r, c):
            pad_copy(lo + r * rt).start()
            return c

        lax.fori_loop(0, pn_ref[i], fill, 0)

    @pl.when(i == N_EXPERTS)
    def _():
        def fill(b, c):
            blk_copy(b).start()
            return c

        lax.fori_loop(nu_ref[0], n_blk, fill, 0)

    @pl.when(i == nsteps - 1)
    def _():
        slot_drain(1 - slot)
        slot_drain(slot)

        def per_expert(e, c):
            def one(r, cc):
                pad_copy(0).wait()
                return cc

            return lax.fori_loop(0, pn_ref[e], one, c)

        lax.fori_loop(0, N_EXPERTS, per_expert, 0)

        def one_blk(b, c):
            blk_copy(0).wait()
            return c

        lax.fori_loop(nu_ref[0], n_blk, one_blk, 0)


def _dispatch(d1, d2, pad_lo, pad_n, n_used, h, g, cap, tm):
    n, d = h.shape
    assert n // tm > N_EXPERTS + 1 and d == ROW_TILE * LANES
    return pl.pallas_call(
        _dispatch_kernel,
        out_shape=jax.ShapeDtypeStruct((cap * ROW_TILE, LANES), F32),
        grid_spec=pltpu.PrefetchScalarGridSpec(
            num_scalar_prefetch=5,
            grid=(n // tm,),
            in_specs=[
                pl.BlockSpec((tm, d), lambda i, *_: (i, 0)),
                pl.BlockSpec((1, d), lambda i, *_: (0, 0)),
            ],
            out_specs=pl.BlockSpec(memory_space=pl.ANY),
            scratch_shapes=[pltpu.VMEM((2, tm * ROW_TILE, LANES), F32),
                            pltpu.VMEM((MOE_ROWS * ROW_TILE, LANES), F32),
                            pltpu.SemaphoreType.DMA((2,)), pltpu.SemaphoreType.DMA],
        ),
        compiler_params=_cparams("arbitrary"),
        name="moe_dispatch",
    )(d1, d2, pad_lo, pad_n, n_used, h, g.reshape(1, d))


def _expert_kernel(be_ref, nu_ref, x_ref, w1_ref, w3_ref, w2_ref, y_ref, w1c, w3c, w2c):
    i = pl.program_id(0)
    used = i < nu_ref[0]
    prev = be_ref[jnp.maximum(i - 1, 0)]
    fresh = jnp.logical_or(i == 0, be_ref[i] != prev)

    @pl.when(jnp.logical_and(used, fresh))
    def _():
        w1c[...] = w1_ref[0, 0].astype(BF16)
        w3c[...] = w3_ref[0, 0].astype(BF16)
        w2c[...] = w2_ref[0, 0].astype(BF16)

    @pl.when(used)
    def _():
        rb = x_ref.shape[0] // ROW_TILE
        x = _load_row_tiles(x_ref, rb).astype(BF16)
        hf = w1c.shape[1] // 2
        y = None
        for c in range(2):
            cs = slice(c * hf, (c + 1) * hf)
            a = jnp.dot(x, w1c[:, cs], preferred_element_type=F32)
            b = jnp.dot(x, w3c[:, cs], preferred_element_type=F32)
            act = (jax.nn.silu(a) * b).astype(BF16)
            part = jnp.dot(act, w2c[cs, :], preferred_element_type=F32)
            y = part if y is None else y + part
        _store_row_tiles(y_ref, y)

    @pl.when(jnp.logical_not(used))
    def _():
        y_ref[...] = jnp.zeros_like(y_ref)


def _experts(blk_e, n_used, xs, w1, w3, w2, layer):
    d = ROW_TILE * LANES
    rb = MOE_ROWS
    n_blk = xs.shape[0] // (rb * ROW_TILE)
    ff = w1.shape[3]
    blk = (rb * ROW_TILE, LANES)
    return pl.pallas_call(
        _expert_kernel,
        out_shape=jax.ShapeDtypeStruct(xs.shape, F32),
        grid_spec=pltpu.PrefetchScalarGridSpec(
            num_scalar_prefetch=2,
            grid=(n_blk,),
            in_specs=[
                pl.BlockSpec(blk, lambda i, be, nu: (jnp.minimum(i, jnp.maximum(nu[0] - 1, 0)), 0)),
                pl.BlockSpec((1, 1, d, ff), lambda i, be, nu: (layer, be[i], 0, 0)),
                pl.BlockSpec((1, 1, d, ff), lambda i, be, nu: (layer, be[i], 0, 0)),
                pl.BlockSpec((1, 1, ff, d), lambda i, be, nu: (layer, be[i], 0, 0)),
            ],
            out_specs=pl.BlockSpec(blk, lambda i, be, nu: (i, 0)),
            scratch_shapes=[pltpu.VMEM((d, ff), BF16), pltpu.VMEM((d, ff), BF16),
                            pltpu.VMEM((ff, d), BF16)],
        ),
        compiler_params=_cparams("arbitrary"),
        name="moe_experts",
    )(blk_e, n_used, xs, w1, w3, w2)


def _combine_kernel(d1_ref, d2_ref, h_ref, meta_ref, g_ref, ys_ref, o_ref, buf_ref, sem, *, final_norm):
    i = pl.program_id(0)
    nsteps = pl.num_programs(0)
    tm = h_ref.shape[0]
    slot = i % 2

    def fetch(tile, s):
        base = tile * tm

        def issue(r8, c):
            for k in range(MOE_ISSUE_UNROLL):
                r = r8 * MOE_ISSUE_UNROLL + k
                pltpu.make_async_copy(ys_ref.at[_row_tile(d1_ref[base + r]), :],
                                      buf_ref.at[s, 0, _row_tile(r * ROW_TILE), :], sem.at[s]).start(priority=0)
                pltpu.make_async_copy(ys_ref.at[_row_tile(d2_ref[base + r]), :],
                                      buf_ref.at[s, 1, _row_tile(r * ROW_TILE), :], sem.at[s]).start(priority=1)
            return c

        lax.fori_loop(0, tm // MOE_ISSUE_UNROLL, issue, 0)

    @pl.when(i == 0)
    def _():
        fetch(0, 0)

    @pl.when(i + 1 < nsteps)
    def _():
        fetch(i + 1, 1 - slot)

    for j in range(2):
        pltpu.make_async_copy(ys_ref.at[pl.ds(0, tm * ROW_TILE), :], buf_ref.at[slot, j], sem.at[slot]).wait()

    meta = meta_ref[...]
    y1 = _load_row_tiles(buf_ref.at[slot, 0], tm)
    y2 = _load_row_tiles(buf_ref.at[slot, 1], tm)
    out = h_ref[...] + (meta[:, 2:3] * y1 + meta[:, 3:4] * y2)
    if final_norm:
        out = _rmsnorm(out, g_ref[...])
    o_ref[...] = out


def _combine(d1, d2, h, meta, ys, g_final, tm, final_norm):
    n, d = h.shape
    return pl.pallas_call(
        functools.partial(_combine_kernel, final_norm=final_norm),
        out_shape=jax.ShapeDtypeStruct((n, d), F32),
        grid_spec=pltpu.PrefetchScalarGridSpec(
            num_scalar_prefetch=2,
            grid=(n // tm,),
            in_specs=[
                pl.BlockSpec((tm, d), lambda i, *_: (i, 0)),
                pl.BlockSpec((tm, LANES), lambda i, *_: (i, 0)),
                pl.BlockSpec((1, d), lambda i, *_: (0, 0)),
                pl.BlockSpec(memory_space=pl.ANY),
            ],
            out_specs=pl.BlockSpec((tm, d), lambda i, *_: (i, 0)),
            scratch_shapes=[pltpu.VMEM((2, 2, tm * ROW_TILE, LANES), F32), pltpu.SemaphoreType.DMA((2,))],
        ),
        compiler_params=_cparams("arbitrary"),
        name="moe_combine",
    )(d1, d2, h, meta, g_final.reshape(1, d), ys)


def _hier_moe(h, ln_g, w_group, w_expert, w1, w3, w2, layer, g_final, final_norm):
    n, d = h.shape
    rb = MOE_ROWS
    cap = 2 * n + N_EXPERTS * rb
    n_blk = cap // rb
    w_r = jnp.zeros((d, LANES), F32).at[:, :N_GROUPS].set(w_group)
    w_r = w_r.at[:, N_GROUPS:N_GROUPS + N_EXPERTS].set(w_expert)
    w_hi = w_r.astype(BF16)
    w_lo = (w_r - w_hi.astype(F32)).astype(BF16)
    meta, idx, counts = _router(h, ln_g, jnp.concatenate([w_hi, w_lo], axis=1), DENSE_ROWS)
    cnt = counts[0, :N_EXPERTS].astype(jnp.int32)
    padded = ((cnt + rb - 1) // rb) * rb
    pad_end = jnp.cumsum(padded)
    pad_start = pad_end - padded
    blk_start = jnp.arange(n_blk, dtype=jnp.int32) * rb
    blk_e = jnp.minimum(jnp.sum(pad_end[None, :] <= blk_start[:, None], axis=1), N_EXPERTS - 1).astype(jnp.int32)
    n_used = (pad_end[-1] // rb).astype(jnp.int32).reshape(1)
    experts = jnp.arange(N_EXPERTS, dtype=jnp.int32)[None, :]
    start_of = lambda e: jnp.sum(jnp.where(e[:, None] == experts, pad_start[None, :], 0), axis=1)
    d1 = (start_of(idx[0]) + idx[2]) * ROW_TILE
    d2 = (start_of(idx[1]) + idx[3]) * ROW_TILE
    xs = _dispatch(d1, d2, (pad_start + cnt) * ROW_TILE, padded - cnt, n_used, h, ln_g, cap, MOE_DISPATCH_ROWS)
    ys = _experts(blk_e, n_used, xs, w1, w3, w2, layer)
    return _combine(d1, d2, h, meta, ys, g_final, MOE_COMBINE_ROWS, final_norm)


def kernel(x, ln_mix, ln_ffn, ln_final, ssm_w_in, ssm_lam_re, ssm_lam_im, ssm_log_dt, ssm_b_re, ssm_b_im, ssm_c_re, ssm_c_im, ssm_d, ssm_w_out, attn_w_qkv, attn_w_o, moe_w_group, moe_w_expert, moe_w1, moe_w3, moe_w2):
    bsz, seq, d = x.shape
    n = bsz * seq
    h = x.reshape(n, d)

    u = _norm_proj(h, ln_mix[0], ssm_w_in[0].astype(BF16), DENSE_ROWS, F32, "s5_in_proj")
    mats = _s5_mats(ssm_lam_re[0], ssm_lam_im[0], ssm_log_dt[0], ssm_b_re[0], ssm_b_im[0],
                    ssm_c_re[0], ssm_c_im[0])
    z = _s5_core(u, mats, ssm_d[0], bsz, seq)
    h = _glu_out(z, ssm_w_out[0].astype(BF16), h, DENSE_ROWS)
    h = _hier_moe(h, ln_ffn[0], moe_w_group[0], moe_w_expert[0], moe_w1, moe_w3, moe_w2, 0,
                  ln_final, False)

    q_t, k, v_t = _qkv_proj(h, ln_mix[1], attn_w_qkv[0], DENSE_ROWS)
    o_t = _moba(q_t, k, v_t, bsz, seq)
    h = _proj_res_t(o_t, attn_w_o[0].astype(BF16), h, DENSE_ROWS)
    h = _hier_moe(h, ln_ffn[1], moe_w_group[1], moe_w_expert[1], moe_w1, moe_w3, moe_w2, 1,
                  ln_final, True)
    return h.reshape(bsz, seq, d)
```

```python
import functools
import math

import jax
import jax.numpy as jnp
from jax import lax
from jax.experimental import pallas as pl
from jax.experimental.pallas import tpu as pltpu

F32 = jnp.float32
BF16 = jnp.bfloat16

D_MODEL = 1024
RMS_EPS = 1e-6
NEG_INF = -1e30

SSM_GROUP = 16
SSM_GROUPS = D_MODEL // SSM_GROUP
SSM_STATE = 64
SSM_CHUNK = 16
SSM_GB = 8
SSM_WIN = 8

ATT_HEADS = 8
HEAD_DIM = 128
MOBA_BLOCK = 256
MOBA_TOPK = 3

N_GROUPS = 4
EXPERTS_PER_GROUP = 8
N_EXPERTS = 32
EXPERT_FF = 512
MOE_ROWS = 256
MOE_DISPATCH_ROWS = 256
MOE_COMBINE_ROWS = 256
DENSE_ROWS = 1024
MIXER_OUT_ROWS = 512

LANES = 128
VMEM_LIMIT = 56 * 1024 * 1024

_NT = (((1,), (1,)), ((), ()))
_TN = (((0,), (0,)), ((), ()))


def _cparams(*sem):
    return pltpu.CompilerParams(dimension_semantics=sem, vmem_limit_bytes=VMEM_LIMIT)


def _rmsnorm(x, g):
    return x * lax.rsqrt(jnp.mean(x * x, axis=-1, keepdims=True) + RMS_EPS) * g


def _norm_proj_kernel(x_ref, g_ref, w_ref, o_ref):
    xn = _rmsnorm(x_ref[...], g_ref[...]).astype(BF16)
    o_ref[...] = jnp.dot(xn, w_ref[...], preferred_element_type=F32).astype(o_ref.dtype)


def _norm_proj(x, g, w_bf16, tm, out_dtype, name):
    n, d = x.shape
    dout = w_bf16.shape[1]
    return pl.pallas_call(
        _norm_proj_kernel,
        out_shape=jax.ShapeDtypeStruct((n, dout), out_dtype),
        grid=(n // tm,),
        in_specs=[
            pl.BlockSpec((tm, d), lambda i: (i, 0)),
            pl.BlockSpec((1, d), lambda i: (0, 0)),
            pl.BlockSpec((d, dout), lambda i: (0, 0)),
        ],
        out_specs=pl.BlockSpec((tm, dout), lambda i: (i, 0)),
        compiler_params=_cparams("parallel"),
        name=name,
    )(x, g.reshape(1, d), w_bf16)


def _s5_mats(lam_re, lam_im, log_dt, b_re, b_im, c_re, c_im):
    g_, p_, c_, t_ = SSM_GROUPS, SSM_STATE, SSM_GROUP, SSM_CHUNK
    lr = jnp.minimum(lam_re, -1e-4)
    li = lam_im
    dt = jnp.exp(log_dt)[:, None]
    mag = jnp.exp(lr * dt)
    abar_re = mag * jnp.cos(li * dt)
    abar_im = mag * jnp.sin(li * dt)
    den = lr * lr + li * li
    nr = abar_re - 1.0
    gam_re = (nr * lr + abar_im * li) / den
    gam_im = (abar_im * lr - nr * li) / den
    bb_re = gam_re[..., None] * b_re - gam_im[..., None] * b_im
    bb_im = gam_re[..., None] * b_im + gam_im[..., None] * b_re

    def powers(ns):
        nf = jnp.asarray(ns, F32)[None, :, None]
        pm = jnp.exp(nf * (lr * dt)[:, None, :])
        ang = nf * (li * dt)[:, None, :]
        return pm * jnp.cos(ang), pm * jnp.sin(ang)

    pr, pi = powers(list(range(t_ + 1)))
    ca_re = c_re[:, None] * pr[:, :, None, :] - c_im[:, None] * pi[:, :, None, :]
    ca_im = c_re[:, None] * pi[:, :, None, :] + c_im[:, None] * pr[:, :, None, :]
    ca_n = jnp.concatenate([ca_re[:, :t_], ca_im[:, :t_]], axis=-1).reshape(g_, t_ * c_, 2 * p_)
    bb_s = jnp.concatenate([bb_re, -bb_im], axis=1)
    kflat = jnp.einsum('gxp,gpc->gcx', ca_n, bb_s, precision='highest')
    prs = pr[:, :t_][:, ::-1][:, :, None, :]
    pis = pi[:, :t_][:, ::-1][:, :, None, :]
    bt_re = bb_re.transpose(0, 2, 1)[:, None]
    bt_im = bb_im.transpose(0, 2, 1)[:, None]
    w_re = prs * bt_re - pis * bt_im
    w_im = prs * bt_im + pis * bt_re
    w = jnp.concatenate([w_re, w_im, w_im, w_re], axis=-1).reshape(g_, t_ * c_, 4 * p_)
    prt = pr.transpose(0, 2, 1)[:, :, 1:, None]
    pit = pi.transpose(0, 2, 1)[:, :, 1:, None]
    ct_re = c_re.transpose(0, 2, 1)[:, :, None, :]
    ct_im = c_im.transpose(0, 2, 1)[:, :, None, :]
    v_re = (ct_re * prt - ct_im * pit).reshape(g_, p_, t_ * c_)
    v_im = -(ct_re * pit + ct_im * prt).reshape(g_, p_, t_ * c_)
    v = jnp.concatenate([v_re, v_im], axis=1)
    qr, qi = powers([t_ * (1 << j) for j in range(4)])
    rows = []
    for j in range(4):
        ar, ai = qr[:, j], qi[:, j]
        rows += [jnp.concatenate([ar, ar], -1), jnp.concatenate([-ai, ai], -1),
                 jnp.concatenate([ai, -ai], -1)]
    rows += [jnp.zeros_like(rows[0])] * 4
    coef = jnp.stack(rows, axis=1)
    return kflat.astype(F32), w.astype(BF16), v.astype(BF16), coef.astype(F32)


def _s5_perm():
    r = jnp.arange(8 * LANES)
    col = ((r % LANES) // SSM_GROUP) * LANES + (r // LANES) * SSM_GROUP + r % SSM_GROUP
    p = (col[:, None] == r[None, :]).astype(BF16)
    return p, p.T


def _s5_kernel(u_ref, kf_ref, w_ref, v_ref, coef_ref, d_ref, p_ref, pt_ref, o_ref,
               m_ref, vf_ref, zf_ref, ea_ref, eb_ref, sp_ref):
    t_, c_, gb, win = SSM_CHUNK, SSM_GROUP, SSM_GB, SSM_WIN
    nk = u_ref.shape[0] // t_
    p2 = 2 * SSM_STATE
    tc = t_ * c_

    @pl.when(pl.program_id(1) == 0)
    def _():
        lane_tc = lax.broadcasted_iota(jnp.int32, (c_, tc), 1)
        for g in range(gb):
            kf = kf_ref[g]
            for s in range(t_):
                rolled = kf if s == 0 else pltpu.roll(kf, s * c_, axis=1)
                m_ref[g, s * c_:(s + 1) * c_, :] = jnp.where(lane_tc >= s * c_, rolled, 0.0).astype(BF16)

    halves = tc // LANES

    def timestep(t):
        return pl.ds(t, nk, stride=t_)

    def flat(g):
        return jnp.concatenate([vf_ref[h, :, g * LANES:(g + 1) * LANES] for h in range(halves)], axis=1)

    for h in range(halves):
        x = jnp.concatenate([u_ref[timestep(8 * h + i), :].astype(BF16) for i in range(8)], axis=1)
        vf_ref[h] = jnp.dot(x, p_ref[...], preferred_element_type=F32).astype(BF16)

    row = lax.broadcasted_iota(jnp.int32, (nk, p2), 0)

    def shift(x, d):
        return jnp.where(row < d, 0.0, pltpu.roll(x, d, axis=0))

    for g in range(gb):
        sc = jnp.dot(flat(g), w_ref[g], preferred_element_type=F32)
        xa, xb = shift(sc[:, :p2], 1), shift(sc[:, p2:], 1)
        cf = coef_ref[g]
        for lvl in range(3):
            pp, qa, qb = cf[3 * lvl:3 * lvl + 1], cf[3 * lvl + 1:3 * lvl + 2], cf[3 * lvl + 2:3 * lvl + 3]
            sa, sb = shift(xa, 1 << lvl), shift(xb, 1 << lvl)
            xa, xb = xa + pp * sa + qa * sb, xb + pp * sb + qb * sa
        ea_ref[g] = xa
        eb_ref[g] = xb

    cfs = [coef_ref[g] for g in range(gb)]
    zero = jnp.zeros((win, p2), F32)
    state = [(zero, zero)] * gb
    for j in range(nk // win):
        rs = slice(j * win, (j + 1) * win)
        for g in range(gb):
            pp, qa, qb = cfs[g][9:10], cfs[g][10:11], cfs[g][11:12]
            s_a, s_b = state[g]
            n_a = pp * s_a + qa * s_b + ea_ref[g, rs, :]
            n_b = pp * s_b + qb * s_a + eb_ref[g, rs, :]
            sp_ref[g, rs, :] = n_a
            state[g] = (n_a, n_b)

    for g in range(gb):
        x = flat(g)
        y = jnp.dot(x, m_ref[g], preferred_element_type=F32)
        y = y + jnp.dot(sp_ref[g].astype(BF16), v_ref[g], preferred_element_type=F32)
        z = jax.nn.gelu(y + d_ref[g] * x.astype(F32)).astype(BF16)
        for h in range(halves):
            zf_ref[h, :, g * LANES:(g + 1) * LANES] = z[:, h * LANES:(h + 1) * LANES]

    for h in range(halves):
        zn = jnp.dot(zf_ref[h], pt_ref[...], preferred_element_type=F32)
        for i in range(8):
            o_ref[timestep(8 * h + i), :] = zn[:, i * LANES:(i + 1) * LANES]


def _s5_core(u, mats, d_skip, bsz, seq):
    kflat, w, v, coef = mats
    n, d = u.shape
    gb, tc, p2 = SSM_GB, SSM_CHUNK * SSM_GROUP, 2 * SSM_STATE
    nk = seq // SSM_CHUNK
    halves = tc // LANES
    perm, perm_t = _s5_perm()
    dflat = jnp.tile(d_skip.reshape(SSM_GROUPS, 1, SSM_GROUP), (1, 1, SSM_CHUNK))
    spec3 = lambda a, b: pl.BlockSpec((gb, a, b), lambda j, bb: (j, 0, 0))
    const = pl.BlockSpec(perm.shape, lambda j, bb: (0, 0))
    return pl.pallas_call(
        _s5_kernel,
        out_shape=jax.ShapeDtypeStruct((n, d), F32),
        grid=(SSM_GROUPS // gb, bsz),
        in_specs=[
            pl.BlockSpec((seq, LANES), lambda j, bb: (bb, j)),
            spec3(SSM_GROUP, tc), spec3(tc, 2 * p2), spec3(p2, tc), spec3(16, p2), spec3(1, tc),
            const, const,
        ],
        out_specs=pl.BlockSpec((seq, LANES), lambda j, bb: (bb, j)),
        scratch_shapes=[pltpu.VMEM((gb, tc, tc), BF16),
                        pltpu.VMEM((halves, nk, gb * LANES), BF16),
                        pltpu.VMEM((halves, nk, gb * LANES), BF16),
                        pltpu.VMEM((gb, nk, p2), F32), pltpu.VMEM((gb, nk, p2), F32),
                        pltpu.VMEM((gb, nk, p2), F32)],
        compiler_params=_cparams("arbitrary", "arbitrary"),
        name="s5_core",
    )(u, kflat, w, v, coef, dflat, perm, perm_t)


def _glu_out_kernel(z_ref, w_ref, x_ref, g_ref, wr_ref, tri_ref, o_ref, meta_ref, idx_ref, cnt_ref):
    vg = jnp.dot(z_ref[...].astype(BF16), w_ref[...], preferred_element_type=F32)
    d = o_ref.shape[1]
    h = x_ref[...] + vg[:, :d] * jax.nn.sigmoid(vg[:, d:])
    o_ref[...] = h
    _route(h, g_ref, wr_ref, tri_ref, meta_ref, idx_ref, cnt_ref)


def _proj_res_t_kernel(at_ref, w_ref, x_ref, g_ref, wr_ref, tri_ref, o_ref, meta_ref, idx_ref, cnt_ref):
    h = x_ref[...] + lax.dot_general(at_ref[...], w_ref[...], _TN, preferred_element_type=F32)
    o_ref[...] = h
    _route(h, g_ref, wr_ref, tri_ref, meta_ref, idx_ref, cnt_ref)


def _mixer_out(kernel_fn, a, a_spec, w_bf16, x, ln_g, w_router, tm, name):
    n, d = x.shape
    tri = (jnp.arange(tm)[:, None] > jnp.arange(tm)[None, :]).astype(BF16)
    full = lambda arr: pl.BlockSpec(arr.shape, lambda i: (0, 0))
    h, meta, idx, counts = pl.pallas_call(
        kernel_fn,
        out_shape=(jax.ShapeDtypeStruct((n, d), F32), jax.ShapeDtypeStruct((n, LANES), F32),
                   jax.ShapeDtypeStruct((8, n), jnp.int32), jax.ShapeDtypeStruct((1, LANES), F32)),
        grid=(n // tm,),
        in_specs=[a_spec, full(w_bf16), pl.BlockSpec((tm, d), lambda i: (i, 0)),
                  pl.BlockSpec((1, d), lambda i: (0, 0)), full(w_router), full(tri)],
        out_specs=(pl.BlockSpec((tm, d), lambda i: (i, 0)), pl.BlockSpec((tm, LANES), lambda i: (i, 0)),
                   pl.BlockSpec((8, tm), lambda i: (0, i)), pl.BlockSpec((1, LANES), lambda i: (0, 0))),
        compiler_params=_cparams("arbitrary"),
        name=name,
    )(a, w_bf16, x, ln_g.reshape(1, d), w_router, tri)
    return h, (meta, idx, counts)


def _glu_out(z, w_bf16, x, ln_g, w_router, tm):
    spec = pl.BlockSpec((tm, z.shape[1]), lambda i: (i, 0))
    return _mixer_out(_glu_out_kernel, z, spec, w_bf16, x, ln_g, w_router, tm, "s5_glu_out")


def _proj_res_t(a_t, w_bf16, x, ln_g, w_router, tm):
    spec = pl.BlockSpec((a_t.shape[0], tm), lambda i: (0, i))
    return _mixer_out(_proj_res_t_kernel, a_t, spec, w_bf16, x, ln_g, w_router, tm, "attn_out_proj")


def _qkv_kernel(x_ref, g_ref, wq_ref, wk_ref, wv_ref, qt_ref, k_ref, vt_ref):
    xn = _rmsnorm(x_ref[...], g_ref[...]).astype(BF16)
    c = (HEAD_DIM ** -0.5) * math.log2(math.e)
    qt_ref[...] = (lax.dot_general(wq_ref[...], xn, _NT, preferred_element_type=F32) * c).astype(BF16)
    k_ref[...] = jnp.dot(xn, wk_ref[...], preferred_element_type=F32).astype(BF16)
    vt = lax.dot_general(wv_ref[...], xn, _NT, preferred_element_type=F32).astype(BF16)
    for c in range(vt_ref.shape[0]):
        vt_ref[c] = vt[:, c * MOBA_BLOCK:(c + 1) * MOBA_BLOCK]


def _qkv_proj(x, g, w_qkv, tm):
    n, d = x.shape
    da = ATT_HEADS * HEAD_DIM
    wq_t = w_qkv[:, :da].T.astype(BF16)
    wk = w_qkv[:, da:2 * da].astype(BF16)
    wv_t = w_qkv[:, 2 * da:].T.astype(BF16)
    full = lambda shp: pl.BlockSpec(shp, lambda i: (0, 0))
    tb = tm // MOBA_BLOCK
    return pl.pallas_call(
        _qkv_kernel,
        out_shape=(jax.ShapeDtypeStruct((da, n), BF16), jax.ShapeDtypeStruct((n, da), BF16),
                   jax.ShapeDtypeStruct((n // MOBA_BLOCK, da, MOBA_BLOCK), BF16)),
        grid=(n // tm,),
        in_specs=[pl.BlockSpec((tm, d), lambda i: (i, 0)), full((1, d)),
                  full((da, d)), full((d, da)), full((da, d))],
        out_specs=(pl.BlockSpec((da, tm), lambda i: (0, i)), pl.BlockSpec((tm, da), lambda i: (i, 0)),
                   pl.BlockSpec((tb, da, MOBA_BLOCK), lambda i: (i, 0, 0))),
        compiler_params=_cparams("parallel"),
        name="attn_qkv_proj",
    )(x, g.reshape(1, d), wq_t, wk, wv_t)


MOBA_CHAINS = 4
MOBA_SUM_ROWS = 16


def _moba_kernel(qt_ref, k_ref, vt_ref, et_ref, ot_ref, s_scr, acc_scr, m_scr, *, nb):
    blk, dh = MOBA_BLOCK, HEAD_DIM
    brow = lax.broadcasted_iota(jnp.int32, (nb, blk), 0)
    r_ix = lax.broadcasted_iota(jnp.int32, (blk, blk), 0)
    c_ix = lax.broadcasted_iota(jnp.int32, (blk, blk), 1)
    zpad = jnp.zeros((2 * LANES - dh - nb, blk), BF16)
    kmean = jnp.mean(k_ref[...].astype(F32).reshape(nb, blk, dh), axis=1)

    def k_aug(t):
        rows = pl.ds(pl.multiple_of(t * blk, blk), blk)
        return jnp.concatenate([k_ref[rows, :], et_ref[rows, :]], axis=1)

    ones = jnp.ones((MOBA_SUM_ROWS, blk), BF16)

    def values(t):
        return jnp.concatenate([vt_ref[t], ones], axis=0)

    def absorb(c, s, vt):
        m_old = m_scr[c]
        m_new = jnp.maximum(m_old, jnp.max(s, axis=0, keepdims=True))
        p = jnp.exp2(s - m_new).astype(BF16)
        acc_scr[c] = jnp.exp2(m_old - m_new) * acc_scr[c] + jnp.dot(vt, p, preferred_element_type=F32)
        m_scr[c] = m_new

    def query_block(i):
        qs = slice(i * blk, (i + 1) * blk)
        qt = qt_ref[:, qs]
        if i > MOBA_TOPK:
            gate = jnp.dot(kmean, qt.astype(F32), preferred_element_type=F32,
                           precision=lax.Precision.HIGHEST)
            gate = jnp.where(brow < i, gate, NEG_INF)
            sel = brow >= i
            for _ in range(MOBA_TOPK):
                gm = jnp.max(gate, axis=0, keepdims=True)
                first = jnp.min(jnp.where(gate == gm, brow, nb), axis=0, keepdims=True)
                pick = brow == first
                sel = jnp.logical_or(sel, pick)
                gate = jnp.where(pick, -jnp.inf, gate)
            bias = jnp.where(sel, 0.0, NEG_INF).astype(BF16)
        else:
            bias = jnp.zeros((nb, blk), BF16)
        return jnp.concatenate([qt, bias, zpad], axis=0)

    nq = MOBA_CHAINS
    for i0 in range(0, nb, nq):
        chains = range(min(nq, nb - i0))
        q_aug = [query_block(i0 + c) for c in chains]
        m_scr[...] = jnp.full(m_scr.shape, NEG_INF, F32)
        acc_scr[...] = jnp.zeros(acc_scr.shape, F32)

        def scores(t, slot, cs):
            ka = k_aug(t)
            for c in cs:
                s_scr[c, slot] = jnp.dot(ka, q_aug[c], preferred_element_type=F32)

        scores(0, 0, chains)
        if i0 > 0:
            def body(j, carry):
                cur = [s_scr[c, j % 2] for c in chains]
                scores(j + 1, (j + 1) % 2, chains)
                vt = values(j)
                for c in chains:
                    absorb(c, cur[c], vt)
                return carry

            lax.fori_loop(0, i0, body, 0)
        for t in chains:
            cur = {c: s_scr[c, (i0 + t) % 2] for c in chains if c >= t}
            later = [c for c in chains if c > t]
            if later:
                scores(i0 + t + 1, (i0 + t + 1) % 2, later)
            vt = values(i0 + t)
            absorb(t, jnp.where(r_ix <= c_ix, cur[t], NEG_INF), vt)
            for c in later:
                absorb(c, cur[c], vt)
        for c in chains:
            qs = slice((i0 + c) * blk, (i0 + c + 1) * blk)
            ot_ref[:, qs] = (acc_scr[c, :dh, :] / acc_scr[c, dh:dh + 1, :]).astype(ot_ref.dtype)


def _moba(q_t, k, v_t, bsz, seq):
    nb = seq // MOBA_BLOCK
    da = ATT_HEADS * HEAD_DIM
    e_t = ((jnp.arange(seq) // MOBA_BLOCK)[:, None] == jnp.arange(LANES)[None, :]).astype(BF16)
    tspec = pl.BlockSpec((HEAD_DIM, seq), lambda b, h: (h, b))
    return pl.pallas_call(
        functools.partial(_moba_kernel, nb=nb),
        out_shape=jax.ShapeDtypeStruct((da, bsz * seq), BF16),
        grid=(bsz, ATT_HEADS),
        in_specs=[tspec, pl.BlockSpec((seq, HEAD_DIM), lambda b, h: (b, h)),
                  pl.BlockSpec((nb, HEAD_DIM, MOBA_BLOCK), lambda b, h: (b, h, 0)),
                  pl.BlockSpec((seq, LANES), lambda b, h: (0, 0))],
        out_specs=tspec,
        scratch_shapes=[pltpu.VMEM((MOBA_CHAINS, 2, MOBA_BLOCK, MOBA_BLOCK), F32),
                        pltpu.VMEM((MOBA_CHAINS, HEAD_DIM + MOBA_SUM_ROWS, MOBA_BLOCK), F32),
                        pltpu.VMEM((MOBA_CHAINS, 1, MOBA_BLOCK), F32)],
        compiler_params=_cparams("parallel", "parallel"),
        name="moba_attn",
    )(q_t, k, v_t, e_t)


def _router_weights(w_group, w_expert):
    d = w_group.shape[0]
    w_r = jnp.zeros((d, LANES), F32).at[:, :N_GROUPS].set(w_group)
    w_r = w_r.at[:, N_GROUPS:N_GROUPS + N_EXPERTS].set(w_expert)
    w_hi = w_r.astype(BF16)
    w_lo = (w_r - w_hi.astype(F32)).astype(BF16)
    return jnp.concatenate([w_hi, w_lo], axis=1)


def _route(h, g_ref, w_ref, tri_ref, meta_ref, idx_ref, cnt_ref):
    xn = _rmsnorm(h, g_ref[...])
    x_hi = xn.astype(BF16)
    x_lo = (xn - x_hi.astype(F32)).astype(BF16)
    t = jnp.dot(x_hi, w_ref[...], preferred_element_type=F32)
    logits = (t[:, :LANES] + t[:, LANES:]) + jnp.dot(x_lo, w_ref[:, :LANES], preferred_element_type=F32)
    tm = logits.shape[0]
    lane = lax.broadcasted_iota(jnp.int32, (tm, LANES), 1)
    ninf = -jnp.inf
    lg = jnp.where(lane < N_GROUPS, logits, ninf)
    gm = jnp.max(lg, axis=-1, keepdims=True)
    g_idx = jnp.min(jnp.where(lg == gm, lane, LANES), axis=-1, keepdims=True)
    g_gate = 1.0 / jnp.sum(jnp.exp(lg - gm), axis=-1, keepdims=True)
    lo = N_GROUPS + EXPERTS_PER_GROUP * g_idx
    le = jnp.where((lane >= lo) & (lane < lo + EXPERTS_PER_GROUP), logits, ninf)
    m1 = jnp.max(le, axis=-1, keepdims=True)
    i1 = jnp.min(jnp.where(le == m1, lane, LANES), axis=-1, keepdims=True)
    le2 = jnp.where(lane == i1, ninf, le)
    m2 = jnp.max(le2, axis=-1, keepdims=True)
    i2 = jnp.min(jnp.where(le2 == m2, lane, LANES), axis=-1, keepdims=True)
    p2 = jnp.exp(m2 - m1)
    gate1 = g_gate / (1.0 + p2)
    gate2 = g_gate * p2 / (1.0 + p2)
    e1 = (i1 - N_GROUPS).astype(F32)
    e2 = (i2 - N_GROUPS).astype(F32)
    meta = jnp.where(lane == 0, e1, jnp.where(lane == 1, e2, jnp.where(lane == 2, gate1,
                     jnp.where(lane == 3, gate2, 0.0))))
    meta_ref[...] = meta
    oh1 = (lane == i1 - N_GROUPS).astype(F32)
    oh2 = (lane == i2 - N_GROUPS).astype(F32)
    both = oh1 + oh2

    @pl.when(pl.program_id(0) == 0)
    def _():
        cnt_ref[...] = jnp.zeros_like(cnt_ref)

    seen = cnt_ref[...]
    before = jnp.dot(tri_ref[...], both.astype(BF16), preferred_element_type=F32) + seen
    r1 = jnp.sum(before * oh1, axis=-1, keepdims=True)
    r2 = jnp.sum(before * oh2, axis=-1, keepdims=True)
    slab = jnp.where(lane == 0, e1, jnp.where(lane == 1, e2, jnp.where(lane == 2, r1,
                     jnp.where(lane == 3, r2, 0.0))))
    idx_ref[...] = jnp.transpose(slab)[0:8, :].astype(jnp.int32)
    cnt_ref[...] = seen + jnp.sum(both, axis=0, keepdims=True)


MOE_ISSUE_UNROLL = 8
ROW_TILE = 8


def _row_tile(off):
    return pl.ds(pl.multiple_of(off, ROW_TILE), ROW_TILE)


def _store_row_tiles(ref, x):
    rows = x.shape[0]
    for c in range(ROW_TILE):
        ref[pl.ds(c, rows, stride=ROW_TILE), :] = x[:, c * LANES:(c + 1) * LANES]


def _load_row_tiles(ref, rows):
    return jnp.concatenate([ref[pl.ds(c, rows, stride=ROW_TILE), :] for c in range(ROW_TILE)], axis=1)


def _dispatch_kernel(d1_ref, d2_ref, plo_ref, pn_ref, nu_ref, h_ref, g_ref, xs_ref,
                     buf_ref, zero_ref, sem, fill_sem):
    i = pl.program_id(0)
    nsteps = pl.num_programs(0)
    tm = h_ref.shape[0]
    rt = ROW_TILE
    rb = zero_ref.shape[0] // rt
    n_blk = xs_ref.shape[0] // (rb * rt)
    slot = i % 2
    base = i * tm

    def slot_drain(s):
        cp = pltpu.make_async_copy(buf_ref.at[s], xs_ref.at[pl.ds(0, tm * rt), :], sem.at[s])
        cp.wait()
        cp.wait()

    @pl.when(i >= 2)
    def _():
        slot_drain(slot)

    _store_row_tiles(buf_ref.at[slot], _rmsnorm(h_ref[...], g_ref[...]))

    def issue(r8, c):
        for k in range(MOE_ISSUE_UNROLL):
            r = r8 * MOE_ISSUE_UNROLL + k
            src = buf_ref.at[slot, _row_tile(r * rt), :]
            pltpu.make_async_copy(src, xs_ref.at[_row_tile(d1_ref[base + r]), :], sem.at[slot]).start(priority=0)
            pltpu.make_async_copy(src, xs_ref.at[_row_tile(d2_ref[base + r]), :], sem.at[slot]).start(priority=1)
        return c

    lax.fori_loop(0, tm // MOE_ISSUE_UNROLL, issue, 0)

    def pad_copy(off, rows):
        return pltpu.make_async_copy(zero_ref.at[pl.ds(0, rows * rt), :],
                                     xs_ref.at[pl.ds(pl.multiple_of(off, rt), rows * rt), :], fill_sem)

    pad_sizes = [1 << b for b in reversed(range(rb.bit_length() - 1))]

    def blk_copy(b):
        return pltpu.make_async_copy(zero_ref, xs_ref.at[pl.ds(pl.multiple_of(b * (rb * rt), rt), rb * rt), :],
                                     fill_sem)

    @pl.when(i == 0)
    def _():
        zero_ref[...] = jnp.zeros_like(zero_ref)

    @pl.when(i < N_EXPERTS)
    def _():
        off = plo_ref[i]
        npad = pn_ref[i]
        for rows in pad_sizes:
            has = (npad & rows) != 0

            @pl.when(has)
            def _(off=off, rows=rows):
                pad_copy(off, rows).start()

            off = off + jnp.where(has, rows * rt, 0)

    @pl.when(i == N_EXPERTS)
    def _():
        def fill(b, c):
            blk_copy(b).start()
            return c

        lax.fori_loop(nu_ref[0], n_blk, fill, 0)

    @pl.when(i == nsteps - 1)
    def _():
        slot_drain(1 - slot)
        slot_drain(slot)

        def per_expert(e, c):
            npad = pn_ref[e]
            for rows in pad_sizes:
                @pl.when((npad & rows) != 0)
                def _(rows=rows):
                    pad_copy(0, rows).wait()

            return c

        lax.fori_loop(0, N_EXPERTS, per_expert, 0)

        def one_blk(b, c):
            blk_copy(0).wait()
            return c

        lax.fori_loop(nu_ref[0], n_blk, one_blk, 0)


def _dispatch(d1, d2, pad_lo, pad_n, n_used, h, g, cap, tm):
    n, d = h.shape
    assert n // tm > N_EXPERTS + 1 and d == ROW_TILE * LANES
    return pl.pallas_call(
        _dispatch_kernel,
        out_shape=jax.ShapeDtypeStruct((cap * ROW_TILE, LANES), F32),
        grid_spec=pltpu.PrefetchScalarGridSpec(
            num_scalar_prefetch=5,
            grid=(n // tm,),
            in_specs=[
                pl.BlockSpec((tm, d), lambda i, *_: (i, 0)),
                pl.BlockSpec((1, d), lambda i, *_: (0, 0)),
            ],
            out_specs=pl.BlockSpec(memory_space=pl.ANY),
            scratch_shapes=[pltpu.VMEM((2, tm * ROW_TILE, LANES), F32),
                            pltpu.VMEM((MOE_ROWS * ROW_TILE, LANES), F32),
                            pltpu.SemaphoreType.DMA((2,)), pltpu.SemaphoreType.DMA],
        ),
        compiler_params=_cparams("arbitrary"),
        name="moe_dispatch",
    )(d1, d2, pad_lo, pad_n, n_used, h, g.reshape(1, d))


def _expert_kernel(be_ref, nu_ref, x_ref, w1_ref, w3_ref, w2_ref, y_ref, w1c, w3c, w2c):
    i = pl.program_id(0)
    used = i < nu_ref[0]
    prev = be_ref[jnp.maximum(i - 1, 0)]
    fresh = jnp.logical_or(i == 0, be_ref[i] != prev)

    @pl.when(jnp.logical_and(used, fresh))
    def _():
        w1c[...] = w1_ref[0, 0].astype(BF16)
        w3c[...] = w3_ref[0, 0].astype(BF16)
        w2c[...] = w2_ref[0, 0].astype(BF16)

    @pl.when(used)
    def _():
        rb = x_ref.shape[0] // ROW_TILE
        x = _load_row_tiles(x_ref, rb).astype(BF16)
        hf = w1c.shape[1] // 2
        y = None
        for c in range(2):
            cs = slice(c * hf, (c + 1) * hf)
            a = jnp.dot(x, w1c[:, cs], preferred_element_type=F32)
            b = jnp.dot(x, w3c[:, cs], preferred_element_type=F32)
            act = (jax.nn.silu(a) * b).astype(BF16)
            part = jnp.dot(act, w2c[cs, :], preferred_element_type=F32)
            y = part if y is None else y + part
        _store_row_tiles(y_ref, y)

    @pl.when(jnp.logical_not(used))
    def _():
        y_ref[...] = jnp.zeros_like(y_ref)


def _experts(blk_e, n_used, xs, w1, w3, w2, layer):
    d = ROW_TILE * LANES
    rb = MOE_ROWS
    n_blk = xs.shape[0] // (rb * ROW_TILE)
    ff = w1.shape[3]
    blk = (rb * ROW_TILE, LANES)
    return pl.pallas_call(
        _expert_kernel,
        out_shape=jax.ShapeDtypeStruct(xs.shape, F32),
        grid_spec=pltpu.PrefetchScalarGridSpec(
            num_scalar_prefetch=2,
            grid=(n_blk,),
            in_specs=[
                pl.BlockSpec(blk, lambda i, be, nu: (jnp.minimum(i, jnp.maximum(nu[0] - 1, 0)), 0)),
                pl.BlockSpec((1, 1, d, ff), lambda i, be, nu: (layer, be[i], 0, 0)),
                pl.BlockSpec((1, 1, d, ff), lambda i, be, nu: (layer, be[i], 0, 0)),
                pl.BlockSpec((1, 1, ff, d), lambda i, be, nu: (layer, be[i], 0, 0)),
            ],
            out_specs=pl.BlockSpec(blk, lambda i, be, nu: (i, 0)),
            scratch_shapes=[pltpu.VMEM((d, ff), BF16), pltpu.VMEM((d, ff), BF16),
                            pltpu.VMEM((ff, d), BF16)],
        ),
        compiler_params=_cparams("arbitrary"),
        name="moe_experts",
    )(blk_e, n_used, xs, w1, w3, w2)


def _combine_kernel(d1_ref, d2_ref, h_ref, meta_ref, g_ref, ys_ref, o_ref, buf_ref, sem, *, final_norm):
    i = pl.program_id(0)
    nsteps = pl.num_programs(0)
    tm = h_ref.shape[0]
    slot = i % 2

    def fetch(tile, s):
        base = tile * tm

        def issue(r8, c):
            for k in range(MOE_ISSUE_UNROLL):
                r = r8 * MOE_ISSUE_UNROLL + k
                pltpu.make_async_copy(ys_ref.at[_row_tile(d1_ref[base + r]), :],
                                      buf_ref.at[s, 0, _row_tile(r * ROW_TILE), :], sem.at[s]).start(priority=0)
                pltpu.make_async_copy(ys_ref.at[_row_tile(d2_ref[base + r]), :],
                                      buf_ref.at[s, 1, _row_tile(r * ROW_TILE), :], sem.at[s]).start(priority=1)
            return c

        lax.fori_loop(0, tm // MOE_ISSUE_UNROLL, issue, 0)

    @pl.when(i == 0)
    def _():
        fetch(0, 0)

    @pl.when(i + 1 < nsteps)
    def _():
        fetch(i + 1, 1 - slot)

    for j in range(2):
        pltpu.make_async_copy(ys_ref.at[pl.ds(0, tm * ROW_TILE), :], buf_ref.at[slot, j], sem.at[slot]).wait()

    meta = meta_ref[...]
    y1 = _load_row_tiles(buf_ref.at[slot, 0], tm)
    y2 = _load_row_tiles(buf_ref.at[slot, 1], tm)
    out = h_ref[...] + (meta[:, 2:3] * y1 + meta[:, 3:4] * y2)
    if final_norm:
        out = _rmsnorm(out, g_ref[...])
    o_ref[...] = out


def _combine(d1, d2, h, meta, ys, g_final, tm, final_norm):
    n, d = h.shape
    return pl.pallas_call(
        functools.partial(_combine_kernel, final_norm=final_norm),
        out_shape=jax.ShapeDtypeStruct((n, d), F32),
        grid_spec=pltpu.PrefetchScalarGridSpec(
            num_scalar_prefetch=2,
            grid=(n // tm,),
            in_specs=[
                pl.BlockSpec((tm, d), lambda i, *_: (i, 0)),
                pl.BlockSpec((tm, LANES), lambda i, *_: (i, 0)),
                pl.BlockSpec((1, d), lambda i, *_: (0, 0)),
                pl.BlockSpec(memory_space=pl.ANY),
            ],
            out_specs=pl.BlockSpec((tm, d), lambda i, *_: (i, 0)),
            scratch_shapes=[pltpu.VMEM((2, 2, tm * ROW_TILE, LANES), F32), pltpu.SemaphoreType.DMA((2,))],
        ),
        compiler_params=_cparams("arbitrary"),
        name="moe_combine",
    )(d1, d2, h, meta, g_final.reshape(1, d), ys)


def _hier_moe(h, routing, ln_g, w1, w3, w2, layer, g_final, final_norm):
    n, d = h.shape
    rb = MOE_ROWS
    cap = 2 * n + N_EXPERTS * rb
    n_blk = cap // rb
    meta, idx, counts = routing
    cnt = counts[0, :N_EXPERTS].astype(jnp.int32)
    padded = ((cnt + rb - 1) // rb) * rb
    pad_end = jnp.cumsum(padded)
    pad_start = pad_end - padded
    blk_start = jnp.arange(n_blk, dtype=jnp.int32) * rb
    blk_e = jnp.minimum(jnp.sum(pad_end[None, :] <= blk_start[:, None], axis=1), N_EXPERTS - 1).astype(jnp.int32)
    n_used = (pad_end[-1] // rb).astype(jnp.int32).reshape(1)
    experts = jnp.arange(N_EXPERTS, dtype=jnp.int32)[None, :]
    start_of = lambda e: jnp.sum(jnp.where(e[:, None] == experts, pad_start[None, :], 0), axis=1)
    d1 = (start_of(idx[0]) + idx[2]) * ROW_TILE
    d2 = (start_of(idx[1]) + idx[3]) * ROW_TILE
    xs = _dispatch(d1, d2, (pad_start + cnt) * ROW_TILE, padded - cnt, n_used, h, ln_g, cap, MOE_DISPATCH_ROWS)
    ys = _experts(blk_e, n_used, xs, w1, w3, w2, layer)
    return _combine(d1, d2, h, meta, ys, g_final, MOE_COMBINE_ROWS, final_norm)


def kernel(x, ln_mix, ln_ffn, ln_final, ssm_w_in, ssm_lam_re, ssm_lam_im, ssm_log_dt, ssm_b_re, ssm_b_im, ssm_c_re, ssm_c_im, ssm_d, ssm_w_out, attn_w_qkv, attn_w_o, moe_w_group, moe_w_expert, moe_w1, moe_w3, moe_w2):
    bsz, seq, d = x.shape
    n = bsz * seq
    h = x.reshape(n, d)

    u = _norm_proj(h, ln_mix[0], ssm_w_in[0].astype(BF16), DENSE_ROWS, F32, "s5_in_proj")
    mats = _s5_mats(ssm_lam_re[0], ssm_lam_im[0], ssm_log_dt[0], ssm_b_re[0], ssm_b_im[0],
                    ssm_c_re[0], ssm_c_im[0])
    z = _s5_core(u, mats, ssm_d[0], bsz, seq)
    h, routing = _glu_out(z, ssm_w_out[0].astype(BF16), h, ln_ffn[0],
                          _router_weights(moe_w_group[0], moe_w_expert[0]), MIXER_OUT_ROWS)
    h = _hier_moe(h, routing, ln_ffn[0], moe_w1, moe_w3, moe_w2, 0, ln_final, False)

    q_t, k, v_t = _qkv_proj(h, ln_mix[1], attn_w_qkv[0], DENSE_ROWS)
    o_t = _moba(q_t, k, v_t, bsz, seq)
    h, routing = _proj_res_t(o_t, attn_w_o[0].astype(BF16), h, ln_ffn[1],
                             _router_weights(moe_w_group[1], moe_w_expert[1]), MIXER_OUT_ROWS)
    h = _hier_moe(h, routing, ln_ffn[1], moe_w1, moe_w3, moe_w2, 1, ln_final, True)
    return h.reshape(bsz, seq, d)
```

```python
import functools
import math

import jax
import jax.numpy as jnp
from jax import lax
from jax.experimental import pallas as pl
from jax.experimental.pallas import tpu as pltpu

F32 = jnp.float32
BF16 = jnp.bfloat16

D_MODEL = 1024
RMS_EPS = 1e-6
NEG_INF = -1e30

SSM_GROUP = 16
SSM_GROUPS = D_MODEL // SSM_GROUP
SSM_STATE = 64
SSM_CHUNK = 16
SSM_GB = 8
SSM_WIN = 8

ATT_HEADS = 8
HEAD_DIM = 128
MOBA_BLOCK = 256
MOBA_TOPK = 3

N_GROUPS = 4
EXPERTS_PER_GROUP = 8
N_EXPERTS = 32
EXPERT_FF = 512
MOE_ROWS = 256
MOE_DISPATCH_ROWS = 256
MOE_COMBINE_ROWS = 256
DENSE_ROWS = 1024
MIXER_OUT_ROWS = 512

LANES = 128
VMEM_LIMIT = 56 * 1024 * 1024

_NT = (((1,), (1,)), ((), ()))
_TN = (((0,), (0,)), ((), ()))


def _cparams(*sem):
    return pltpu.CompilerParams(dimension_semantics=sem, vmem_limit_bytes=VMEM_LIMIT)


def _rmsnorm(x, g):
    return x * lax.rsqrt(jnp.mean(x * x, axis=-1, keepdims=True) + RMS_EPS) * g


def _norm_proj_kernel(x_ref, g_ref, w_ref, o_ref):
    xn = _rmsnorm(x_ref[...], g_ref[...]).astype(BF16)
    o_ref[...] = jnp.dot(xn, w_ref[...], preferred_element_type=F32).astype(o_ref.dtype)


def _norm_proj(x, g, w_bf16, tm, out_dtype, name):
    n, d = x.shape
    dout = w_bf16.shape[1]
    return pl.pallas_call(
        _norm_proj_kernel,
        out_shape=jax.ShapeDtypeStruct((n, dout), out_dtype),
        grid=(n // tm,),
        in_specs=[
            pl.BlockSpec((tm, d), lambda i: (i, 0)),
            pl.BlockSpec((1, d), lambda i: (0, 0)),
            pl.BlockSpec((d, dout), lambda i: (0, 0)),
        ],
        out_specs=pl.BlockSpec((tm, dout), lambda i: (i, 0)),
        compiler_params=_cparams("parallel"),
        name=name,
    )(x, g.reshape(1, d), w_bf16)


def _s5_mats(lam_re, lam_im, log_dt, b_re, b_im, c_re, c_im):
    g_, p_, c_, t_ = SSM_GROUPS, SSM_STATE, SSM_GROUP, SSM_CHUNK
    lr = jnp.minimum(lam_re, -1e-4)
    li = lam_im
    dt = jnp.exp(log_dt)[:, None]
    mag = jnp.exp(lr * dt)
    abar_re = mag * jnp.cos(li * dt)
    abar_im = mag * jnp.sin(li * dt)
    den = lr * lr + li * li
    nr = abar_re - 1.0
    gam_re = (nr * lr + abar_im * li) / den
    gam_im = (abar_im * lr - nr * li) / den
    bb_re = gam_re[..., None] * b_re - gam_im[..., None] * b_im
    bb_im = gam_re[..., None] * b_im + gam_im[..., None] * b_re

    def powers(ns):
        nf = jnp.asarray(ns, F32)[None, :, None]
        pm = jnp.exp(nf * (lr * dt)[:, None, :])
        ang = nf * (li * dt)[:, None, :]
        return pm * jnp.cos(ang), pm * jnp.sin(ang)

    pr, pi = powers(list(range(t_ + 1)))
    ca_re = c_re[:, None] * pr[:, :, None, :] - c_im[:, None] * pi[:, :, None, :]
    ca_im = c_re[:, None] * pi[:, :, None, :] + c_im[:, None] * pr[:, :, None, :]
    ca_n = jnp.concatenate([ca_re[:, :t_], ca_im[:, :t_]], axis=-1).reshape(g_, t_ * c_, 2 * p_)
    bb_s = jnp.concatenate([bb_re, -bb_im], axis=1)
    kflat = jnp.einsum('gxp,gpc->gcx', ca_n, bb_s, precision='highest')
    prs = pr[:, :t_][:, ::-1][:, :, None, :]
    pis = pi[:, :t_][:, ::-1][:, :, None, :]
    bt_re = bb_re.transpose(0, 2, 1)[:, None]
    bt_im = bb_im.transpose(0, 2, 1)[:, None]
    w_re = prs * bt_re - pis * bt_im
    w_im = prs * bt_im + pis * bt_re
    w = jnp.concatenate([w_re, w_im, w_im, w_re], axis=-1).reshape(g_, t_ * c_, 4 * p_)
    prt = pr.transpose(0, 2, 1)[:, :, 1:, None]
    pit = pi.transpose(0, 2, 1)[:, :, 1:, None]
    ct_re = c_re.transpose(0, 2, 1)[:, :, None, :]
    ct_im = c_im.transpose(0, 2, 1)[:, :, None, :]
    v_re = (ct_re * prt - ct_im * pit).reshape(g_, p_, t_ * c_)
    v_im = -(ct_re * pit + ct_im * prt).reshape(g_, p_, t_ * c_)
    v = jnp.concatenate([v_re, v_im], axis=1)
    qr, qi = powers([t_ * (1 << j) for j in range(4)])
    rows = []
    for j in range(4):
        ar, ai = qr[:, j], qi[:, j]
        rows += [jnp.concatenate([ar, ar], -1), jnp.concatenate([-ai, ai], -1),
                 jnp.concatenate([ai, -ai], -1)]
    rows += [jnp.zeros_like(rows[0])] * 4
    coef = jnp.stack(rows, axis=1)
    return kflat.astype(F32), w.astype(BF16), v.astype(BF16), coef.astype(F32)


def _s5_perm():
    r = jnp.arange(8 * LANES)
    col = ((r % LANES) // SSM_GROUP) * LANES + (r // LANES) * SSM_GROUP + r % SSM_GROUP
    p = (col[:, None] == r[None, :]).astype(BF16)
    return p, p.T


def _s5_kernel(u_ref, kf_ref, w_ref, v_ref, coef_ref, d_ref, p_ref, pt_ref, o_ref,
               m_ref, vf_ref, zf_ref, ea_ref, eb_ref, sp_ref):
    t_, c_, gb, win = SSM_CHUNK, SSM_GROUP, SSM_GB, SSM_WIN
    nk = u_ref.shape[0] // t_
    p2 = 2 * SSM_STATE
    tc = t_ * c_

    @pl.when(pl.program_id(1) == 0)
    def _():
        lane_tc = lax.broadcasted_iota(jnp.int32, (c_, tc), 1)
        for g in range(gb):
            kf = kf_ref[g]
            for s in range(t_):
                rolled = kf if s == 0 else pltpu.roll(kf, s * c_, axis=1)
                m_ref[g, s * c_:(s + 1) * c_, :] = jnp.where(lane_tc >= s * c_, rolled, 0.0).astype(BF16)

    halves = tc // LANES

    def timestep(t):
        return pl.ds(t, nk, stride=t_)

    def flat(g):
        return jnp.concatenate([vf_ref[h, :, g * LANES:(g + 1) * LANES] for h in range(halves)], axis=1)

    for h in range(halves):
        x = jnp.concatenate([u_ref[timestep(8 * h + i), :].astype(BF16) for i in range(8)], axis=1)
        vf_ref[h] = jnp.dot(x, p_ref[...], preferred_element_type=F32).astype(BF16)

    row = lax.broadcasted_iota(jnp.int32, (nk, p2), 0)

    def shift(x, d):
        return jnp.where(row < d, 0.0, pltpu.roll(x, d, axis=0))

    for g in range(gb):
        sc = jnp.dot(flat(g), w_ref[g], preferred_element_type=F32)
        xa, xb = shift(sc[:, :p2], 1), shift(sc[:, p2:], 1)
        cf = coef_ref[g]
        for lvl in range(3):
            pp, qa, qb = cf[3 * lvl:3 * lvl + 1], cf[3 * lvl + 1:3 * lvl + 2], cf[3 * lvl + 2:3 * lvl + 3]
            sa, sb = shift(xa, 1 << lvl), shift(xb, 1 << lvl)
            xa, xb = xa + pp * sa + qa * sb, xb + pp * sb + qb * sa
        ea_ref[g] = xa
        eb_ref[g] = xb

    cfs = [coef_ref[g] for g in range(gb)]
    zero = jnp.zeros((win, p2), F32)
    state = [(zero, zero)] * gb
    for j in range(nk // win):
        rs = slice(j * win, (j + 1) * win)
        for g in range(gb):
            pp, qa, qb = cfs[g][9:10], cfs[g][10:11], cfs[g][11:12]
            s_a, s_b = state[g]
            n_a = pp * s_a + qa * s_b + ea_ref[g, rs, :]
            n_b = pp * s_b + qb * s_a + eb_ref[g, rs, :]
            sp_ref[g, rs, :] = n_a
            state[g] = (n_a, n_b)

    for g in range(gb):
        x = flat(g)
        y = jnp.dot(x, m_ref[g], preferred_element_type=F32)
        y = y + jnp.dot(sp_ref[g].astype(BF16), v_ref[g], preferred_element_type=F32)
        z = jax.nn.gelu(y + d_ref[g] * x.astype(F32)).astype(BF16)
        for h in range(halves):
            zf_ref[h, :, g * LANES:(g + 1) * LANES] = z[:, h * LANES:(h + 1) * LANES]

    for h in range(halves):
        zn = jnp.dot(zf_ref[h], pt_ref[...], preferred_element_type=F32)
        for i in range(8):
            o_ref[timestep(8 * h + i), :] = zn[:, i * LANES:(i + 1) * LANES]


def _s5_core(u, mats, d_skip, bsz, seq):
    kflat, w, v, coef = mats
    n, d = u.shape
    gb, tc, p2 = SSM_GB, SSM_CHUNK * SSM_GROUP, 2 * SSM_STATE
    nk = seq // SSM_CHUNK
    halves = tc // LANES
    perm, perm_t = _s5_perm()
    dflat = jnp.tile(d_skip.reshape(SSM_GROUPS, 1, SSM_GROUP), (1, 1, SSM_CHUNK))
    spec3 = lambda a, b: pl.BlockSpec((gb, a, b), lambda j, bb: (j, 0, 0))
    const = pl.BlockSpec(perm.shape, lambda j, bb: (0, 0))
    return pl.pallas_call(
        _s5_kernel,
        out_shape=jax.ShapeDtypeStruct((n, d), F32),
        grid=(SSM_GROUPS // gb, bsz),
        in_specs=[
            pl.BlockSpec((seq, LANES), lambda j, bb: (bb, j)),
            spec3(SSM_GROUP, tc), spec3(tc, 2 * p2), spec3(p2, tc), spec3(16, p2), spec3(1, tc),
            const, const,
        ],
        out_specs=pl.BlockSpec((seq, LANES), lambda j, bb: (bb, j)),
        scratch_shapes=[pltpu.VMEM((gb, tc, tc), BF16),
                        pltpu.VMEM((halves, nk, gb * LANES), BF16),
                        pltpu.VMEM((halves, nk, gb * LANES), BF16),
                        pltpu.VMEM((gb, nk, p2), F32), pltpu.VMEM((gb, nk, p2), F32),
                        pltpu.VMEM((gb, nk, p2), F32)],
        compiler_params=_cparams("arbitrary", "arbitrary"),
        name="s5_core",
    )(u, kflat, w, v, coef, dflat, perm, perm_t)


def _glu_out_kernel(z_ref, w_ref, x_ref, g_ref, wr_ref, tri_ref, o_ref, meta_ref, idx_ref, cnt_ref):
    vg = jnp.dot(z_ref[...].astype(BF16), w_ref[...], preferred_element_type=F32)
    d = o_ref.shape[1]
    h = x_ref[...] + vg[:, :d] * jax.nn.sigmoid(vg[:, d:])
    o_ref[...] = h
    _route(h, g_ref, wr_ref, tri_ref, meta_ref, idx_ref, cnt_ref)


def _proj_res_t_kernel(at_ref, w_ref, x_ref, g_ref, wr_ref, tri_ref, o_ref, meta_ref, idx_ref, cnt_ref):
    h = x_ref[...] + lax.dot_general(at_ref[...], w_ref[...], _TN, preferred_element_type=F32)
    o_ref[...] = h
    _route(h, g_ref, wr_ref, tri_ref, meta_ref, idx_ref, cnt_ref)


def _mixer_out(kernel_fn, a, a_spec, w_bf16, x, ln_g, w_router, tm, name):
    n, d = x.shape
    tri = (jnp.arange(tm)[:, None] > jnp.arange(tm)[None, :]).astype(BF16)
    full = lambda arr: pl.BlockSpec(arr.shape, lambda i: (0, 0))
    h, meta, idx, counts = pl.pallas_call(
        kernel_fn,
        out_shape=(jax.ShapeDtypeStruct((n, d), F32), jax.ShapeDtypeStruct((n, LANES), F32),
                   jax.ShapeDtypeStruct((8, n), jnp.int32), jax.ShapeDtypeStruct((1, LANES), F32)),
        grid=(n // tm,),
        in_specs=[a_spec, full(w_bf16), pl.BlockSpec((tm, d), lambda i: (i, 0)),
                  pl.BlockSpec((1, d), lambda i: (0, 0)), full(w_router), full(tri)],
        out_specs=(pl.BlockSpec((tm, d), lambda i: (i, 0)), pl.BlockSpec((tm, LANES), lambda i: (i, 0)),
                   pl.BlockSpec((8, tm), lambda i: (0, i)), pl.BlockSpec((1, LANES), lambda i: (0, 0))),
        compiler_params=_cparams("arbitrary"),
        name=name,
    )(a, w_bf16, x, ln_g.reshape(1, d), w_router, tri)
    return h, (meta, idx, counts)


def _glu_out(z, w_bf16, x, ln_g, w_router, tm):
    spec = pl.BlockSpec((tm, z.shape[1]), lambda i: (i, 0))
    return _mixer_out(_glu_out_kernel, z, spec, w_bf16, x, ln_g, w_router, tm, "s5_glu_out")


def _proj_res_t(a_t, w_bf16, x, ln_g, w_router, tm):
    spec = pl.BlockSpec((a_t.shape[0], tm), lambda i: (0, i))
    return _mixer_out(_proj_res_t_kernel, a_t, spec, w_bf16, x, ln_g, w_router, tm, "attn_out_proj")


def _qkv_kernel(x_ref, g_ref, wq_ref, wk_ref, wv_ref, qt_ref, k_ref, vt_ref):
    xn = _rmsnorm(x_ref[...], g_ref[...]).astype(BF16)
    c = (HEAD_DIM ** -0.5) * math.log2(math.e)
    qt_ref[...] = (lax.dot_general(wq_ref[...], xn, _NT, preferred_element_type=F32) * c).astype(BF16)
    k_ref[...] = jnp.dot(xn, wk_ref[...], preferred_element_type=F32).astype(BF16)
    vt = lax.dot_general(wv_ref[...], xn, _NT, preferred_element_type=F32).astype(BF16)
    for c in range(vt_ref.shape[0]):
        vt_ref[c] = vt[:, c * MOBA_BLOCK:(c + 1) * MOBA_BLOCK]


def _qkv_proj(x, g, w_qkv, tm):
    n, d = x.shape
    da = ATT_HEADS * HEAD_DIM
    wq_t = w_qkv[:, :da].T.astype(BF16)
    wk = w_qkv[:, da:2 * da].astype(BF16)
    wv_t = w_qkv[:, 2 * da:].T.astype(BF16)
    full = lambda shp: pl.BlockSpec(shp, lambda i: (0, 0))
    tb = tm // MOBA_BLOCK
    return pl.pallas_call(
        _qkv_kernel,
        out_shape=(jax.ShapeDtypeStruct((da, n), BF16), jax.ShapeDtypeStruct((n, da), BF16),
                   jax.ShapeDtypeStruct((n // MOBA_BLOCK, da, MOBA_BLOCK), BF16)),
        grid=(n // tm,),
        in_specs=[pl.BlockSpec((tm, d), lambda i: (i, 0)), full((1, d)),
                  full((da, d)), full((d, da)), full((da, d))],
        out_specs=(pl.BlockSpec((da, tm), lambda i: (0, i)), pl.BlockSpec((tm, da), lambda i: (i, 0)),
                   pl.BlockSpec((tb, da, MOBA_BLOCK), lambda i: (i, 0, 0))),
        compiler_params=_cparams("parallel"),
        name="attn_qkv_proj",
    )(x, g.reshape(1, d), wq_t, wk, wv_t)


MOBA_CHAINS = 4
MOBA_SUM_ROWS = 16


def _moba_kernel(qt_ref, k_ref, vt_ref, et_ref, ot_ref, s_scr, acc_scr, m_scr, *, nb):
    blk, dh = MOBA_BLOCK, HEAD_DIM
    brow = lax.broadcasted_iota(jnp.int32, (nb, blk), 0).astype(F32)
    r_ix = lax.broadcasted_iota(jnp.int32, (blk, blk), 0)
    c_ix = lax.broadcasted_iota(jnp.int32, (blk, blk), 1)
    zpad = jnp.zeros((2 * LANES - dh - nb, blk), BF16)
    kmean = jnp.mean(k_ref[...].astype(F32).reshape(nb, blk, dh), axis=1)

    def k_aug(t):
        rows = pl.ds(pl.multiple_of(t * blk, blk), blk)
        return jnp.concatenate([k_ref[rows, :], et_ref[rows, :]], axis=1)

    ones = jnp.ones((MOBA_SUM_ROWS, blk), BF16)

    def values(t):
        return jnp.concatenate([vt_ref[t], ones], axis=0)

    def absorb(c, s, vt):
        m_old = m_scr[c]
        m_new = jnp.maximum(m_old, jnp.max(s, axis=0, keepdims=True))
        p = jnp.exp2(s - m_new).astype(BF16)
        acc_scr[c] = jnp.exp2(m_old - m_new) * acc_scr[c] + jnp.dot(vt, p, preferred_element_type=F32)
        m_scr[c] = m_new

    def query_block(i):
        qs = slice(i * blk, (i + 1) * blk)
        qt = qt_ref[:, qs]
        if i > MOBA_TOPK:
            gate = jnp.dot(kmean, qt.astype(F32), preferred_element_type=F32,
                           precision=lax.Precision.HIGHEST)
            gate = jnp.where(brow < i, gate, NEG_INF)
            sel = brow >= i
            for _ in range(MOBA_TOPK):
                gm = jnp.max(gate, axis=0, keepdims=True)
                first = jnp.min(jnp.where(gate == gm, brow, float(nb)), axis=0, keepdims=True)
                pick = brow == first
                sel = jnp.logical_or(sel, pick)
                gate = jnp.where(pick, -jnp.inf, gate)
            bias = jnp.where(sel, 0.0, NEG_INF).astype(BF16)
        else:
            bias = jnp.zeros((nb, blk), BF16)
        return jnp.concatenate([qt, bias, zpad], axis=0)

    nq = MOBA_CHAINS
    for i0 in range(0, nb, nq):
        chains = range(min(nq, nb - i0))
        q_aug = [query_block(i0 + c) for c in chains]
        m_scr[...] = jnp.full(m_scr.shape, NEG_INF, F32)
        acc_scr[...] = jnp.zeros(acc_scr.shape, F32)

        def scores(t, slot, cs):
            ka = k_aug(t)
            for c in cs:
                s_scr[c, slot] = jnp.dot(ka, q_aug[c], preferred_element_type=F32)

        scores(0, 0, chains)
        if i0 > 0:
            def body(j, carry):
                cur = [s_scr[c, j % 2] for c in chains]
                scores(j + 1, (j + 1) % 2, chains)
                vt = values(j)
                for c in chains:
                    absorb(c, cur[c], vt)
                return carry

            lax.fori_loop(0, i0, body, 0)
        for t in chains:
            cur = {c: s_scr[c, (i0 + t) % 2] for c in chains if c >= t}
            later = [c for c in chains if c > t]
            if later:
                scores(i0 + t + 1, (i0 + t + 1) % 2, later)
            vt = values(i0 + t)
            absorb(t, jnp.where(r_ix <= c_ix, cur[t], NEG_INF), vt)
            for c in later:
                absorb(c, cur[c], vt)
        for c in chains:
            qs = slice((i0 + c) * blk, (i0 + c + 1) * blk)
            ot_ref[:, qs] = (acc_scr[c, :dh, :] / acc_scr[c, dh:dh + 1, :]).astype(ot_ref.dtype)


def _moba(q_t, k, v_t, bsz, seq):
    nb = seq // MOBA_BLOCK
    da = ATT_HEADS * HEAD_DIM
    e_t = ((jnp.arange(seq) // MOBA_BLOCK)[:, None] == jnp.arange(LANES)[None, :]).astype(BF16)
    tspec = pl.BlockSpec((HEAD_DIM, seq), lambda b, h: (h, b))
    return pl.pallas_call(
        functools.partial(_moba_kernel, nb=nb),
        out_shape=jax.ShapeDtypeStruct((da, bsz * seq), BF16),
        grid=(bsz, ATT_HEADS),
        in_specs=[tspec, pl.BlockSpec((seq, HEAD_DIM), lambda b, h: (b, h)),
                  pl.BlockSpec((nb, HEAD_DIM, MOBA_BLOCK), lambda b, h: (b, h, 0)),
                  pl.BlockSpec((seq, LANES), lambda b, h: (0, 0))],
        out_specs=tspec,
        scratch_shapes=[pltpu.VMEM((MOBA_CHAINS, 2, MOBA_BLOCK, MOBA_BLOCK), F32),
                        pltpu.VMEM((MOBA_CHAINS, HEAD_DIM + MOBA_SUM_ROWS, MOBA_BLOCK), F32),
                        pltpu.VMEM((MOBA_CHAINS, 1, MOBA_BLOCK), F32)],
        compiler_params=_cparams("parallel", "parallel"),
        name="moba_attn",
    )(q_t, k, v_t, e_t)


def _router_weights(w_group, w_expert):
    d = w_group.shape[0]
    w_r = jnp.zeros((d, LANES), F32).at[:, :N_GROUPS].set(w_group)
    w_r = w_r.at[:, N_GROUPS:N_GROUPS + N_EXPERTS].set(w_expert)
    w_hi = w_r.astype(BF16)
    w_lo = (w_r - w_hi.astype(F32)).astype(BF16)
    return jnp.concatenate([w_hi, w_lo], axis=1)


def _route(h, g_ref, w_ref, tri_ref, meta_ref, idx_ref, cnt_ref):
    xn = _rmsnorm(h, g_ref[...])
    x_hi = xn.astype(BF16)
    x_lo = (xn - x_hi.astype(F32)).astype(BF16)
    t = jnp.dot(x_hi, w_ref[...], preferred_element_type=F32)
    logits = (t[:, :LANES] + t[:, LANES:]) + jnp.dot(x_lo, w_ref[:, :LANES], preferred_element_type=F32)
    tm = logits.shape[0]
    lane = lax.broadcasted_iota(jnp.int32, (tm, LANES), 1).astype(F32)
    ninf = -jnp.inf
    lg = jnp.where(lane < N_GROUPS, logits, ninf)
    gm = jnp.max(lg, axis=-1, keepdims=True)
    g_idx = jnp.min(jnp.where(lg == gm, lane, float(LANES)), axis=-1, keepdims=True)
    g_gate = 1.0 / jnp.sum(jnp.exp(lg - gm), axis=-1, keepdims=True)
    lo = N_GROUPS + EXPERTS_PER_GROUP * g_idx
    le = jnp.where((lane >= lo) & (lane < lo + EXPERTS_PER_GROUP), logits, ninf)
    m1 = jnp.max(le, axis=-1, keepdims=True)
    i1 = jnp.min(jnp.where(le == m1, lane, float(LANES)), axis=-1, keepdims=True)
    le2 = jnp.where(lane == i1, ninf, le)
    m2 = jnp.max(le2, axis=-1, keepdims=True)
    i2 = jnp.min(jnp.where(le2 == m2, lane, float(LANES)), axis=-1, keepdims=True)
    p2 = jnp.exp(m2 - m1)
    gate1 = g_gate / (1.0 + p2)
    gate2 = g_gate * p2 / (1.0 + p2)
    e1 = i1 - N_GROUPS
    e2 = i2 - N_GROUPS
    meta = jnp.where(lane == 0, e1, jnp.where(lane == 1, e2, jnp.where(lane == 2, gate1,
                     jnp.where(lane == 3, gate2, 0.0))))
    meta_ref[...] = meta
    oh1 = (lane == e1).astype(F32)
    oh2 = (lane == e2).astype(F32)
    both = oh1 + oh2

    @pl.when(pl.program_id(0) == 0)
    def _():
        cnt_ref[...] = jnp.zeros_like(cnt_ref)

    seen = cnt_ref[...]
    before = jnp.dot(tri_ref[...], both.astype(BF16), preferred_element_type=F32) + seen
    r1 = jnp.sum(before * oh1, axis=-1, keepdims=True)
    r2 = jnp.sum(before * oh2, axis=-1, keepdims=True)
    slab = jnp.where(lane == 0, e1, jnp.where(lane == 1, e2, jnp.where(lane == 2, r1,
                     jnp.where(lane == 3, r2, 0.0))))
    idx_ref[...] = jnp.transpose(slab)[0:8, :].astype(jnp.int32)
    cnt_ref[...] = seen + jnp.sum(both, axis=0, keepdims=True)


MOE_ISSUE_UNROLL = 8
ROW_TILE = 8


def _row_tile(off):
    return pl.ds(pl.multiple_of(off, ROW_TILE), ROW_TILE)


def _store_row_tiles(ref, x):
    rows = x.shape[0]
    for c in range(ROW_TILE):
        ref[pl.ds(c, rows, stride=ROW_TILE), :] = x[:, c * LANES:(c + 1) * LANES]


def _load_row_tiles(ref, rows):
    return jnp.concatenate([ref[pl.ds(c, rows, stride=ROW_TILE), :] for c in range(ROW_TILE)], axis=1)


def _dispatch_kernel(d1_ref, d2_ref, plo_ref, pn_ref, nu_ref, h_ref, g_ref, xs_ref,
                     buf_ref, zero_ref, sem, fill_sem):
    i = pl.program_id(0)
    nsteps = pl.num_programs(0)
    tm = h_ref.shape[0]
    rt = ROW_TILE
    rb = zero_ref.shape[0] // rt
    n_blk = xs_ref.shape[0] // (rb * rt)
    slot = i % 2
    base = i * tm

    def slot_drain(s):
        cp = pltpu.make_async_copy(buf_ref.at[s], xs_ref.at[pl.ds(0, tm * rt), :], sem.at[s])
        cp.wait()
        cp.wait()

    @pl.when(i >= 2)
    def _():
        slot_drain(slot)

    _store_row_tiles(buf_ref.at[slot], _rmsnorm(h_ref[...], g_ref[...]))

    def issue(r8, c):
        for k in range(MOE_ISSUE_UNROLL):
            r = r8 * MOE_ISSUE_UNROLL + k
            src = buf_ref.at[slot, _row_tile(r * rt), :]
            pltpu.make_async_copy(src, xs_ref.at[_row_tile(d1_ref[base + r]), :], sem.at[slot]).start(priority=0)
            pltpu.make_async_copy(src, xs_ref.at[_row_tile(d2_ref[base + r]), :], sem.at[slot]).start(priority=1)
        return c

    lax.fori_loop(0, tm // MOE_ISSUE_UNROLL, issue, 0)

    def pad_copy(off, rows):
        return pltpu.make_async_copy(zero_ref.at[pl.ds(0, rows * rt), :],
                                     xs_ref.at[pl.ds(pl.multiple_of(off, rt), rows * rt), :], fill_sem)

    pad_sizes = [1 << b for b in reversed(range(rb.bit_length() - 1))]

    def blk_copy(b):
        return pltpu.make_async_copy(zero_ref, xs_ref.at[pl.ds(pl.multiple_of(b * (rb * rt), rt), rb * rt), :],
                                     fill_sem)

    @pl.when(i == 0)
    def _():
        zero_ref[...] = jnp.zeros_like(zero_ref)

    @pl.when(i < N_EXPERTS)
    def _():
        off = plo_ref[i]
        npad = pn_ref[i]
        for rows in pad_sizes:
            has = (npad & rows) != 0

            @pl.when(has)
            def _(off=off, rows=rows):
                pad_copy(off, rows).start()

            off = off + jnp.where(has, rows * rt, 0)

    @pl.when(i == N_EXPERTS)
    def _():
        def fill(b, c):
            blk_copy(b).start()
            return c

        lax.fori_loop(nu_ref[0], n_blk, fill, 0)

    @pl.when(i == nsteps - 1)
    def _():
        slot_drain(1 - slot)
        slot_drain(slot)

        def per_expert(e, c):
            npad = pn_ref[e]
            for rows in pad_sizes:
                @pl.when((npad & rows) != 0)
                def _(rows=rows):
                    pad_copy(0, rows).wait()

            return c

        lax.fori_loop(0, N_EXPERTS, per_expert, 0)

        def one_blk(b, c):
            blk_copy(0).wait()
            return c

        lax.fori_loop(nu_ref[0], n_blk, one_blk, 0)


def _dispatch(d1, d2, pad_lo, pad_n, n_used, h, g, cap, tm):
    n, d = h.shape
    assert n // tm > N_EXPERTS + 1 and d == ROW_TILE * LANES
    return pl.pallas_call(
        _dispatch_kernel,
        out_shape=jax.ShapeDtypeStruct((cap * ROW_TILE, LANES), F32),
        grid_spec=pltpu.PrefetchScalarGridSpec(
            num_scalar_prefetch=5,
            grid=(n // tm,),
            in_specs=[
                pl.BlockSpec((tm, d), lambda i, *_: (i, 0)),
                pl.BlockSpec((1, d), lambda i, *_: (0, 0)),
            ],
            out_specs=pl.BlockSpec(memory_space=pl.ANY),
            scratch_shapes=[pltpu.VMEM((2, tm * ROW_TILE, LANES), F32),
                            pltpu.VMEM((MOE_ROWS * ROW_TILE, LANES), F32),
                            pltpu.SemaphoreType.DMA((2,)), pltpu.SemaphoreType.DMA],
        ),
        compiler_params=_cparams("arbitrary"),
        name="moe_dispatch",
    )(d1, d2, pad_lo, pad_n, n_used, h, g.reshape(1, d))


def _expert_kernel(be_ref, nu_ref, nx_ref, x_ref, w1_hbm, w3_hbm, w2_hbm, y_ref,
                   wb1, wb3, wb2, w1c, w3c, w2c, wsem, wslot, *, layer):
    i = pl.program_id(0)
    used = i < nu_ref[0]
    e = be_ref[i]
    prev = be_ref[jnp.maximum(i - 1, 0)]
    fresh = jnp.logical_or(i == 0, e != prev)

    def weight_copies(ex, s):
        return [pltpu.make_async_copy(w1_hbm.at[layer, ex], wb1.at[s], wsem.at[s]),
                pltpu.make_async_copy(w3_hbm.at[layer, ex], wb3.at[s], wsem.at[s]),
                pltpu.make_async_copy(w2_hbm.at[layer, ex], wb2.at[s], wsem.at[s])]

    @pl.when(i == 0)
    def _():
        wslot[0] = 0
        for cp in weight_copies(e, 0):
            cp.start()

    @pl.when(jnp.logical_and(used, fresh))
    def _():
        s = wslot[0]
        for cp in weight_copies(e, s):
            cp.wait()
        nxt = nx_ref[e]

        @pl.when(nxt != e)
        def _():
            for cp in weight_copies(nxt, 1 - s):
                cp.start()

        w1c[...] = wb1[s].astype(BF16)
        w3c[...] = wb3[s].astype(BF16)
        w2c[...] = wb2[s].astype(BF16)
        wslot[0] = 1 - s

    @pl.when(used)
    def _():
        rb = x_ref.shape[0] // ROW_TILE
        x = _load_row_tiles(x_ref, rb).astype(BF16)
        hf = w1c.shape[1] // 2
        y = None
        for c in range(2):
            cs = slice(c * hf, (c + 1) * hf)
            a = jnp.dot(x, w1c[:, cs], preferred_element_type=F32)
            b = jnp.dot(x, w3c[:, cs], preferred_element_type=F32)
            act = (jax.nn.silu(a) * b).astype(BF16)
            part = jnp.dot(act, w2c[cs, :], preferred_element_type=F32)
            y = part if y is None else y + part
        _store_row_tiles(y_ref, y)

    @pl.when(jnp.logical_not(used))
    def _():
        y_ref[...] = jnp.zeros_like(y_ref)


def _experts(blk_e, n_used, next_e, xs, w1, w3, w2, layer):
    d = ROW_TILE * LANES
    rb = MOE_ROWS
    n_blk = xs.shape[0] // (rb * ROW_TILE)
    ff = w1.shape[3]
    blk = (rb * ROW_TILE, LANES)
    hbm = pl.BlockSpec(memory_space=pl.ANY)
    return pl.pallas_call(
        functools.partial(_expert_kernel, layer=layer),
        out_shape=jax.ShapeDtypeStruct(xs.shape, F32),
        grid_spec=pltpu.PrefetchScalarGridSpec(
            num_scalar_prefetch=3,
            grid=(n_blk,),
            in_specs=[
                pl.BlockSpec(blk, lambda i, be, nu, nx: (jnp.minimum(i, jnp.maximum(nu[0] - 1, 0)), 0)),
                hbm, hbm, hbm,
            ],
            out_specs=pl.BlockSpec(blk, lambda i, be, nu, nx: (i, 0)),
            scratch_shapes=[pltpu.VMEM((2, d, ff), F32), pltpu.VMEM((2, d, ff), F32), pltpu.VMEM((2, ff, d), F32),
                            pltpu.VMEM((d, ff), BF16), pltpu.VMEM((d, ff), BF16), pltpu.VMEM((ff, d), BF16),
                            pltpu.SemaphoreType.DMA((2,)), pltpu.SMEM((1,), jnp.int32)],
        ),
        compiler_params=_cparams("arbitrary"),
        name="moe_experts",
    )(blk_e, n_used, next_e, xs, w1, w3, w2)


def _combine_kernel(d1_ref, d2_ref, h_ref, meta_ref, g_ref, ys_ref, o_ref, buf_ref, sem, *, final_norm):
    i = pl.program_id(0)
    nsteps = pl.num_programs(0)
    tm = h_ref.shape[0]
    slot = i % 2

    def fetch(tile, s):
        base = tile * tm

        def issue(r8, c):
            for k in range(MOE_ISSUE_UNROLL):
                r = r8 * MOE_ISSUE_UNROLL + k
                pltpu.make_async_copy(ys_ref.at[_row_tile(d1_ref[base + r]), :],
                                      buf_ref.at[s, 0, _row_tile(r * ROW_TILE), :], sem.at[s]).start(priority=0)
                pltpu.make_async_copy(ys_ref.at[_row_tile(d2_ref[base + r]), :],
                                      buf_ref.at[s, 1, _row_tile(r * ROW_TILE), :], sem.at[s]).start(priority=1)
            return c

        lax.fori_loop(0, tm // MOE_ISSUE_UNROLL, issue, 0)

    @pl.when(i == 0)
    def _():
        fetch(0, 0)

    @pl.when(i + 1 < nsteps)
    def _():
        fetch(i + 1, 1 - slot)

    for j in range(2):
        pltpu.make_async_copy(ys_ref.at[pl.ds(0, tm * ROW_TILE), :], buf_ref.at[slot, j], sem.at[slot]).wait()

    meta = meta_ref[...]
    y1 = _load_row_tiles(buf_ref.at[slot, 0], tm)
    y2 = _load_row_tiles(buf_ref.at[slot, 1], tm)
    out = h_ref[...] + (meta[:, 2:3] * y1 + meta[:, 3:4] * y2)
    if final_norm:
        out = _rmsnorm(out, g_ref[...])
    o_ref[...] = out


def _combine(d1, d2, h, meta, ys, g_final, tm, final_norm):
    n, d = h.shape
    return pl.pallas_call(
        functools.partial(_combine_kernel, final_norm=final_norm),
        out_shape=jax.ShapeDtypeStruct((n, d), F32),
        grid_spec=pltpu.PrefetchScalarGridSpec(
            num_scalar_prefetch=2,
            grid=(n // tm,),
            in_specs=[
                pl.BlockSpec((tm, d), lambda i, *_: (i, 0)),
                pl.BlockSpec((tm, LANES), lambda i, *_: (i, 0)),
                pl.BlockSpec((1, d), lambda i, *_: (0, 0)),
                pl.BlockSpec(memory_space=pl.ANY),
            ],
            out_specs=pl.BlockSpec((tm, d), lambda i, *_: (i, 0)),
            scratch_shapes=[pltpu.VMEM((2, 2, tm * ROW_TILE, LANES), F32), pltpu.SemaphoreType.DMA((2,))],
        ),
        compiler_params=_cparams("arbitrary"),
        name="moe_combine",
    )(d1, d2, h, meta, g_final.reshape(1, d), ys)


def _hier_moe(h, routing, ln_g, w1, w3, w2, layer, g_final, final_norm):
    n, d = h.shape
    rb = MOE_ROWS
    cap = 2 * n + N_EXPERTS * rb
    n_blk = cap // rb
    meta, idx, counts = routing
    cnt = counts[0, :N_EXPERTS].astype(jnp.int32)
    padded = ((cnt + rb - 1) // rb) * rb
    pad_end = jnp.cumsum(padded)
    pad_start = pad_end - padded
    blk_start = jnp.arange(n_blk, dtype=jnp.int32) * rb
    blk_e = jnp.minimum(jnp.sum(pad_end[None, :] <= blk_start[:, None], axis=1), N_EXPERTS - 1).astype(jnp.int32)
    n_used = (pad_end[-1] // rb).astype(jnp.int32).reshape(1)
    experts = jnp.arange(N_EXPERTS, dtype=jnp.int32)[None, :]
    start_of = lambda e: jnp.sum(jnp.where(e[:, None] == experts, pad_start[None, :], 0), axis=1)
    d1 = (start_of(idx[0]) + idx[2]) * ROW_TILE
    d2 = (start_of(idx[1]) + idx[3]) * ROW_TILE
    xs = _dispatch(d1, d2, (pad_start + cnt) * ROW_TILE, padded - cnt, n_used, h, ln_g, cap, MOE_DISPATCH_ROWS)
    later = (experts > experts.T) & (padded > 0)[None, :]
    next_e = jnp.min(jnp.where(later, experts, N_EXPERTS), axis=1)
    next_e = jnp.where(next_e == N_EXPERTS, experts[0], next_e).astype(jnp.int32)
    ys = _experts(blk_e, n_used, next_e, xs, w1, w3, w2, layer)
    return _combine(d1, d2, h, meta, ys, g_final, MOE_COMBINE_ROWS, final_norm)


def kernel(x, ln_mix, ln_ffn, ln_final, ssm_w_in, ssm_lam_re, ssm_lam_im, ssm_log_dt, ssm_b_re, ssm_b_im, ssm_c_re, ssm_c_im, ssm_d, ssm_w_out, attn_w_qkv, attn_w_o, moe_w_group, moe_w_expert, moe_w1, moe_w3, moe_w2):
    bsz, seq, d = x.shape
    n = bsz * seq
    h = x.reshape(n, d)

    u = _norm_proj(h, ln_mix[0], ssm_w_in[0].astype(BF16), DENSE_ROWS, F32, "s5_in_proj")
    mats = _s5_mats(ssm_lam_re[0], ssm_lam_im[0], ssm_log_dt[0], ssm_b_re[0], ssm_b_im[0],
                    ssm_c_re[0], ssm_c_im[0])
    z = _s5_core(u, mats, ssm_d[0], bsz, seq)
    h, routing = _glu_out(z, ssm_w_out[0].astype(BF16), h, ln_ffn[0],
                          _router_weights(moe_w_group[0], moe_w_expert[0]), MIXER_OUT_ROWS)
    h = _hier_moe(h, routing, ln_ffn[0], moe_w1, moe_w3, moe_w2, 0, ln_final, False)

    q_t, k, v_t = _qkv_proj(h, ln_mix[1], attn_w_qkv[0], DENSE_ROWS)
    o_t = _moba(q_t, k, v_t, bsz, seq)
    h, routing = _proj_res_t(o_t, attn_w_o[0].astype(BF16), h, ln_ffn[1],
                             _router_weights(moe_w_group[1], moe_w_expert[1]), MIXER_OUT_ROWS)
    h = _hier_moe(h, routing, ln_ffn[1], moe_w1, moe_w3, moe_w2, 1, ln_final, True)
    return h.reshape(bsz, seq, d)
```

```python
import functools
import math

import jax
import jax.numpy as jnp
from jax import lax
from jax.experimental import pallas as pl
from jax.experimental.pallas import tpu as pltpu

F32 = jnp.float32
BF16 = jnp.bfloat16

D_MODEL = 1024
RMS_EPS = 1e-6
NEG_INF = -1e30

SSM_GROUP = 16
SSM_GROUPS = D_MODEL // SSM_GROUP
SSM_STATE = 64
SSM_CHUNK = 16
SSM_GB = 8
SSM_WIN = 8

ATT_HEADS = 8
HEAD_DIM = 128
MOBA_BLOCK = 256
MOBA_TOPK = 3

N_GROUPS = 4
EXPERTS_PER_GROUP = 8
N_EXPERTS = 32
EXPERT_FF = 512
MOE_ROWS = 256
MOE_DISPATCH_ROWS = 256
MOE_COMBINE_ROWS = 256
DENSE_ROWS = 1024
MIXER_OUT_ROWS = 512

LANES = 128
VMEM_LIMIT = 56 * 1024 * 1024

_NT = (((1,), (1,)), ((), ()))
_TN = (((0,), (0,)), ((), ()))


def _cparams(*sem):
    return pltpu.CompilerParams(dimension_semantics=sem, vmem_limit_bytes=VMEM_LIMIT)


def _rmsnorm(x, g):
    return x * lax.rsqrt(jnp.mean(x * x, axis=-1, keepdims=True) + RMS_EPS) * g


def _norm_proj_kernel(x_ref, g_ref, w_ref, o_ref):
    xn = _rmsnorm(x_ref[...], g_ref[...]).astype(BF16)
    o_ref[...] = jnp.dot(xn, w_ref[...], preferred_element_type=F32).astype(o_ref.dtype)


def _norm_proj(x, g, w_bf16, tm, out_dtype, name):
    n, d = x.shape
    dout = w_bf16.shape[1]
    return pl.pallas_call(
        _norm_proj_kernel,
        out_shape=jax.ShapeDtypeStruct((n, dout), out_dtype),
        grid=(n // tm,),
        in_specs=[
            pl.BlockSpec((tm, d), lambda i: (i, 0)),
            pl.BlockSpec((1, d), lambda i: (0, 0)),
            pl.BlockSpec((d, dout), lambda i: (0, 0)),
        ],
        out_specs=pl.BlockSpec((tm, dout), lambda i: (i, 0)),
        compiler_params=_cparams("parallel"),
        name=name,
    )(x, g.reshape(1, d), w_bf16)


def _s5_mats(lam_re, lam_im, log_dt, b_re, b_im, c_re, c_im):
    g_, p_, c_, t_ = SSM_GROUPS, SSM_STATE, SSM_GROUP, SSM_CHUNK
    lr = jnp.minimum(lam_re, -1e-4)
    li = lam_im
    dt = jnp.exp(log_dt)[:, None]
    mag = jnp.exp(lr * dt)
    abar_re = mag * jnp.cos(li * dt)
    abar_im = mag * jnp.sin(li * dt)
    den = lr * lr + li * li
    nr = abar_re - 1.0
    gam_re = (nr * lr + abar_im * li) / den
    gam_im = (abar_im * lr - nr * li) / den
    bb_re = gam_re[..., None] * b_re - gam_im[..., None] * b_im
    bb_im = gam_re[..., None] * b_im + gam_im[..., None] * b_re

    def powers(ns):
        nf = jnp.asarray(ns, F32)[None, :, None]
        pm = jnp.exp(nf * (lr * dt)[:, None, :])
        ang = nf * (li * dt)[:, None, :]
        return pm * jnp.cos(ang), pm * jnp.sin(ang)

    pr, pi = powers(list(range(t_ + 1)))
    ca_re = c_re[:, None] * pr[:, :, None, :] - c_im[:, None] * pi[:, :, None, :]
    ca_im = c_re[:, None] * pi[:, :, None, :] + c_im[:, None] * pr[:, :, None, :]
    ca_n = jnp.concatenate([ca_re[:, :t_], ca_im[:, :t_]], axis=-1).reshape(g_, t_ * c_, 2 * p_)
    bb_s = jnp.concatenate([bb_re, -bb_im], axis=1)
    kflat = jnp.einsum('gxp,gpc->gcx', ca_n, bb_s, precision='highest')
    prs = pr[:, :t_][:, ::-1][:, :, None, :]
    pis = pi[:, :t_][:, ::-1][:, :, None, :]
    bt_re = bb_re.transpose(0, 2, 1)[:, None]
    bt_im = bb_im.transpose(0, 2, 1)[:, None]
    w_re = prs * bt_re - pis * bt_im
    w_im = prs * bt_im + pis * bt_re
    w = jnp.concatenate([w_re, w_im], axis=-1).reshape(g_, t_ * c_, 2 * p_)
    prt = pr.transpose(0, 2, 1)[:, :, 1:, None]
    pit = pi.transpose(0, 2, 1)[:, :, 1:, None]
    ct_re = c_re.transpose(0, 2, 1)[:, :, None, :]
    ct_im = c_im.transpose(0, 2, 1)[:, :, None, :]
    v_re = (ct_re * prt - ct_im * pit).reshape(g_, p_, t_ * c_)
    v_im = -(ct_re * pit + ct_im * prt).reshape(g_, p_, t_ * c_)
    v = jnp.concatenate([v_re, v_im], axis=1)
    qr, qi = powers([t_ * (1 << j) for j in range(4)])
    rows = []
    for j in range(4):
        ar, ai = qr[:, j], qi[:, j]
        rows += [jnp.concatenate([ar, ar], -1), jnp.concatenate([-ai, ai], -1),
                 jnp.concatenate([ai, -ai], -1)]
    rows += [jnp.zeros_like(rows[0])] * 4
    coef = jnp.stack(rows, axis=1)
    return kflat.astype(F32), w.astype(BF16), v.astype(BF16), coef.astype(F32)


def _s5_perm():
    r = jnp.arange(8 * LANES)
    col = ((r % LANES) // SSM_GROUP) * LANES + (r // LANES) * SSM_GROUP + r % SSM_GROUP
    p = (col[:, None] == r[None, :]).astype(BF16)
    return p, p.T


def _s5_kernel(u_ref, kf_ref, w_ref, v_ref, coef_ref, d_ref, p_ref, pt_ref, o_ref,
               m_ref, vf_ref, zf_ref, ea_ref, eb_ref, sp_ref):
    t_, c_, gb, win = SSM_CHUNK, SSM_GROUP, SSM_GB, SSM_WIN
    nk = u_ref.shape[0] // t_
    p2 = 2 * SSM_STATE
    tc = t_ * c_

    @pl.when(pl.program_id(1) == 0)
    def _():
        lane_tc = lax.broadcasted_iota(jnp.int32, (c_, tc), 1)
        for g in range(gb):
            kf = kf_ref[g]
            for s in range(t_):
                rolled = kf if s == 0 else pltpu.roll(kf, s * c_, axis=1)
                m_ref[g, s * c_:(s + 1) * c_, :] = jnp.where(lane_tc >= s * c_, rolled, 0.0).astype(BF16)

    halves = tc // LANES

    def timestep(t):
        return pl.ds(t, nk, stride=t_)

    def flat(g):
        return jnp.concatenate([vf_ref[h, :, g * LANES:(g + 1) * LANES] for h in range(halves)], axis=1)

    for h in range(halves):
        x = jnp.concatenate([u_ref[timestep(8 * h + i), :].astype(BF16) for i in range(8)], axis=1)
        vf_ref[h] = jnp.dot(x, p_ref[...], preferred_element_type=F32).astype(BF16)

    row = lax.broadcasted_iota(jnp.int32, (nk, p2), 0)

    def shift(x, d):
        return jnp.where(row < d, 0.0, pltpu.roll(x, d, axis=0))

    def swap(x):
        return pltpu.roll(x, SSM_STATE, axis=1)

    for g in range(gb):
        sc = jnp.dot(flat(g), w_ref[g], preferred_element_type=F32)
        xa = shift(sc, 1)
        cf = coef_ref[g]
        for lvl in range(3):
            pp, qa = cf[3 * lvl:3 * lvl + 1], cf[3 * lvl + 1:3 * lvl + 2]
            sa = shift(xa, 1 << lvl)
            xa = xa + pp * sa + qa * swap(sa)
        ea_ref[g] = xa
        eb_ref[g] = swap(xa)

    cfs = [coef_ref[g] for g in range(gb)]
    zero = jnp.zeros((win, p2), F32)
    state = [(zero, zero)] * gb
    for j in range(nk // win):
        rs = slice(j * win, (j + 1) * win)
        for g in range(gb):
            pp, qa, qb = cfs[g][9:10], cfs[g][10:11], cfs[g][11:12]
            s_a, s_b = state[g]
            n_a = pp * s_a + qa * s_b + ea_ref[g, rs, :]
            n_b = pp * s_b + qb * s_a + eb_ref[g, rs, :]
            sp_ref[g, rs, :] = n_a
            state[g] = (n_a, n_b)

    for g in range(gb):
        x = flat(g)
        y = jnp.dot(x, m_ref[g], preferred_element_type=F32)
        y = y + jnp.dot(sp_ref[g].astype(BF16), v_ref[g], preferred_element_type=F32)
        z = jax.nn.gelu(y + d_ref[g] * x.astype(F32)).astype(BF16)
        for h in range(halves):
            zf_ref[h, :, g * LANES:(g + 1) * LANES] = z[:, h * LANES:(h + 1) * LANES]

    for h in range(halves):
        zn = jnp.dot(zf_ref[h], pt_ref[...], preferred_element_type=F32)
        for i in range(8):
            o_ref[timestep(8 * h + i), :] = zn[:, i * LANES:(i + 1) * LANES]


def _s5_core(u, mats, d_skip, bsz, seq):
    kflat, w, v, coef = mats
    n, d = u.shape
    gb, tc, p2 = SSM_GB, SSM_CHUNK * SSM_GROUP, 2 * SSM_STATE
    nk = seq // SSM_CHUNK
    halves = tc // LANES
    perm, perm_t = _s5_perm()
    dflat = jnp.tile(d_skip.reshape(SSM_GROUPS, 1, SSM_GROUP), (1, 1, SSM_CHUNK))
    spec3 = lambda a, b: pl.BlockSpec((gb, a, b), lambda j, bb: (j, 0, 0))
    const = pl.BlockSpec(perm.shape, lambda j, bb: (0, 0))
    return pl.pallas_call(
        _s5_kernel,
        out_shape=jax.ShapeDtypeStruct((n, d), F32),
        grid=(SSM_GROUPS // gb, bsz),
        in_specs=[
            pl.BlockSpec((seq, LANES), lambda j, bb: (bb, j)),
            spec3(SSM_GROUP, tc), spec3(tc, p2), spec3(p2, tc), spec3(16, p2), spec3(1, tc),
            const, const,
        ],
        out_specs=pl.BlockSpec((seq, LANES), lambda j, bb: (bb, j)),
        scratch_shapes=[pltpu.VMEM((gb, tc, tc), BF16),
                        pltpu.VMEM((halves, nk, gb * LANES), BF16),
                        pltpu.VMEM((halves, nk, gb * LANES), BF16),
                        pltpu.VMEM((gb, nk, p2), F32), pltpu.VMEM((gb, nk, p2), F32),
                        pltpu.VMEM((gb, nk, p2), F32)],
        compiler_params=_cparams("arbitrary", "arbitrary"),
        name="s5_core",
    )(u, kflat, w, v, coef, dflat, perm, perm_t)


def _glu_out_kernel(z_ref, w_ref, x_ref, g_ref, wr_ref, tri_ref, o_ref, meta_ref, idx_ref, cnt_ref):
    vg = jnp.dot(z_ref[...].astype(BF16), w_ref[...], preferred_element_type=F32)
    d = o_ref.shape[1]
    h = x_ref[...] + vg[:, :d] * jax.nn.sigmoid(vg[:, d:])
    o_ref[...] = h
    _route(h, g_ref, wr_ref, tri_ref, meta_ref, idx_ref, cnt_ref)


def _proj_res_t_kernel(at_ref, w_ref, x_ref, g_ref, wr_ref, tri_ref, o_ref, meta_ref, idx_ref, cnt_ref):
    h = x_ref[...] + lax.dot_general(at_ref[...], w_ref[...], _TN, preferred_element_type=F32)
    o_ref[...] = h
    _route(h, g_ref, wr_ref, tri_ref, meta_ref, idx_ref, cnt_ref)


def _mixer_out(kernel_fn, a, a_spec, w_bf16, x, ln_g, w_router, tm, name):
    n, d = x.shape
    tri = (jnp.arange(tm)[:, None] > jnp.arange(tm)[None, :]).astype(BF16)
    full = lambda arr: pl.BlockSpec(arr.shape, lambda i: (0, 0))
    h, meta, idx, counts = pl.pallas_call(
        kernel_fn,
        out_shape=(jax.ShapeDtypeStruct((n, d), F32), jax.ShapeDtypeStruct((n, LANES), F32),
                   jax.ShapeDtypeStruct((8, n), jnp.int32), jax.ShapeDtypeStruct((1, LANES), F32)),
        grid=(n // tm,),
        in_specs=[a_spec, full(w_bf16), pl.BlockSpec((tm, d), lambda i: (i, 0)),
                  pl.BlockSpec((1, d), lambda i: (0, 0)), full(w_router), full(tri)],
        out_specs=(pl.BlockSpec((tm, d), lambda i: (i, 0)), pl.BlockSpec((tm, LANES), lambda i: (i, 0)),
                   pl.BlockSpec((8, tm), lambda i: (0, i)), pl.BlockSpec((1, LANES), lambda i: (0, 0))),
        compiler_params=_cparams("arbitrary"),
        name=name,
    )(a, w_bf16, x, ln_g.reshape(1, d), w_router, tri)
    return h, (meta, idx, counts)


def _glu_out(z, w_bf16, x, ln_g, w_router, tm):
    spec = pl.BlockSpec((tm, z.shape[1]), lambda i: (i, 0))
    return _mixer_out(_glu_out_kernel, z, spec, w_bf16, x, ln_g, w_router, tm, "s5_glu_out")


def _proj_res_t(a_t, w_bf16, x, ln_g, w_router, tm):
    spec = pl.BlockSpec((a_t.shape[0], tm), lambda i: (0, i))
    return _mixer_out(_proj_res_t_kernel, a_t, spec, w_bf16, x, ln_g, w_router, tm, "attn_out_proj")


def _qkv_kernel(x_ref, g_ref, wq_ref, wk_ref, wv_ref, qt_ref, k_ref, vt_ref):
    xn = _rmsnorm(x_ref[...], g_ref[...]).astype(BF16)
    c = (HEAD_DIM ** -0.5) * math.log2(math.e)
    qt_ref[...] = (lax.dot_general(wq_ref[...], xn, _NT, preferred_element_type=F32) * c).astype(BF16)
    k_ref[...] = jnp.dot(xn, wk_ref[...], preferred_element_type=F32).astype(BF16)
    vt = lax.dot_general(wv_ref[...], xn, _NT, preferred_element_type=F32).astype(BF16)
    for c in range(vt_ref.shape[0]):
        vt_ref[c] = vt[:, c * MOBA_BLOCK:(c + 1) * MOBA_BLOCK]


def _qkv_proj(x, g, w_qkv, tm):
    n, d = x.shape
    da = ATT_HEADS * HEAD_DIM
    wq_t = w_qkv[:, :da].T.astype(BF16)
    wk = w_qkv[:, da:2 * da].astype(BF16)
    wv_t = w_qkv[:, 2 * da:].T.astype(BF16)
    full = lambda shp: pl.BlockSpec(shp, lambda i: (0, 0))
    tb = tm // MOBA_BLOCK
    return pl.pallas_call(
        _qkv_kernel,
        out_shape=(jax.ShapeDtypeStruct((da, n), BF16), jax.ShapeDtypeStruct((n, da), BF16),
                   jax.ShapeDtypeStruct((n // MOBA_BLOCK, da, MOBA_BLOCK), BF16)),
        grid=(n // tm,),
        in_specs=[pl.BlockSpec((tm, d), lambda i: (i, 0)), full((1, d)),
                  full((da, d)), full((d, da)), full((da, d))],
        out_specs=(pl.BlockSpec((da, tm), lambda i: (0, i)), pl.BlockSpec((tm, da), lambda i: (i, 0)),
                   pl.BlockSpec((tb, da, MOBA_BLOCK), lambda i: (i, 0, 0))),
        compiler_params=_cparams("parallel"),
        name="attn_qkv_proj",
    )(x, g.reshape(1, d), wq_t, wk, wv_t)


MOBA_CHAINS = 4
MOBA_SUM_ROWS = 16


def _moba_kernel(qt_ref, k_ref, vt_ref, et_ref, ot_ref, s_scr, acc_scr, m_scr, *, nb):
    blk, dh = MOBA_BLOCK, HEAD_DIM
    brow = lax.broadcasted_iota(jnp.int32, (nb, blk), 0).astype(F32)
    r_ix = lax.broadcasted_iota(jnp.int32, (blk, blk), 0)
    c_ix = lax.broadcasted_iota(jnp.int32, (blk, blk), 1)
    zpad = jnp.zeros((2 * LANES - dh - nb, blk), BF16)
    kmean = jnp.mean(k_ref[...].astype(F32).reshape(nb, blk, dh), axis=1)
    km_hi = kmean.astype(BF16)
    kmean2 = jnp.concatenate([km_hi, (kmean - km_hi.astype(F32)).astype(BF16)], axis=0)

    def k_aug(t):
        rows = pl.ds(pl.multiple_of(t * blk, blk), blk)
        return jnp.concatenate([k_ref[rows, :], et_ref[rows, :]], axis=1)

    ones = jnp.ones((MOBA_SUM_ROWS, blk), BF16)

    def values(t):
        return jnp.concatenate([vt_ref[t], ones], axis=0)

    def absorb(c, s, vt):
        m_old = m_scr[c]
        m_new = jnp.maximum(m_old, jnp.max(s, axis=0, keepdims=True))
        p = jnp.exp2(s - m_new).astype(BF16)
        acc_scr[c] = jnp.exp2(m_old - m_new) * acc_scr[c] + jnp.dot(vt, p, preferred_element_type=F32)
        m_scr[c] = m_new

    def query_block(i):
        qs = slice(i * blk, (i + 1) * blk)
        qt = qt_ref[:, qs]
        if i > MOBA_TOPK:
            g2 = jnp.dot(kmean2, qt, preferred_element_type=F32)
            gate = g2[:nb] + g2[nb:]
            gate = jnp.where(brow < i, gate, NEG_INF)
            sel = brow >= i
            for _ in range(MOBA_TOPK):
                gm = jnp.max(gate, axis=0, keepdims=True)
                first = jnp.min(jnp.where(gate == gm, brow, float(nb)), axis=0, keepdims=True)
                pick = brow == first
                sel = jnp.logical_or(sel, pick)
                gate = jnp.where(pick, -jnp.inf, gate)
            bias = jnp.where(sel, 0.0, NEG_INF).astype(BF16)
        else:
            bias = jnp.zeros((nb, blk), BF16)
        return jnp.concatenate([qt, bias, zpad], axis=0)

    nq = MOBA_CHAINS
    for i0 in range(0, nb, nq):
        chains = range(min(nq, nb - i0))
        q_aug = [query_block(i0 + c) for c in chains]
        m_scr[...] = jnp.full(m_scr.shape, NEG_INF, F32)
        acc_scr[...] = jnp.zeros(acc_scr.shape, F32)

        def scores(t, slot, cs):
            ka = k_aug(t)
            for c in cs:
                s_scr[c, slot] = jnp.dot(ka, q_aug[c], preferred_element_type=F32)

        scores(0, 0, chains)
        if i0 > 0:
            def body(j, carry):
                cur = [s_scr[c, j % 2] for c in chains]
                scores(j + 1, (j + 1) % 2, chains)
                vt = values(j)
                for c in chains:
                    absorb(c, cur[c], vt)
                return carry

            lax.fori_loop(0, i0, body, 0)
        for t in chains:
            cur = {c: s_scr[c, (i0 + t) % 2] for c in chains if c >= t}
            later = [c for c in chains if c > t]
            if later:
                scores(i0 + t + 1, (i0 + t + 1) % 2, later)
            vt = values(i0 + t)
            absorb(t, jnp.where(r_ix <= c_ix, cur[t], NEG_INF), vt)
            for c in later:
                absorb(c, cur[c], vt)
        for c in chains:
            qs = slice((i0 + c) * blk, (i0 + c + 1) * blk)
            ot_ref[:, qs] = (acc_scr[c, :dh, :] / acc_scr[c, dh:dh + 1, :]).astype(ot_ref.dtype)


def _moba(q_t, k, v_t, bsz, seq):
    nb = seq // MOBA_BLOCK
    da = ATT_HEADS * HEAD_DIM
    e_t = ((jnp.arange(seq) // MOBA_BLOCK)[:, None] == jnp.arange(LANES)[None, :]).astype(BF16)
    tspec = pl.BlockSpec((HEAD_DIM, seq), lambda b, h: (h, b))
    return pl.pallas_call(
        functools.partial(_moba_kernel, nb=nb),
        out_shape=jax.ShapeDtypeStruct((da, bsz * seq), BF16),
        grid=(bsz, ATT_HEADS),
        in_specs=[tspec, pl.BlockSpec((seq, HEAD_DIM), lambda b, h: (b, h)),
                  pl.BlockSpec((nb, HEAD_DIM, MOBA_BLOCK), lambda b, h: (b, h, 0)),
                  pl.BlockSpec((seq, LANES), lambda b, h: (0, 0))],
        out_specs=tspec,
        scratch_shapes=[pltpu.VMEM((MOBA_CHAINS, 2, MOBA_BLOCK, MOBA_BLOCK), F32),
                        pltpu.VMEM((MOBA_CHAINS, HEAD_DIM + MOBA_SUM_ROWS, MOBA_BLOCK), F32),
                        pltpu.VMEM((MOBA_CHAINS, 1, MOBA_BLOCK), F32)],
        compiler_params=_cparams("parallel", "parallel"),
        name="moba_attn",
    )(q_t, k, v_t, e_t)


def _router_weights(w_group, w_expert):
    d = w_group.shape[0]
    w_r = jnp.zeros((d, LANES), F32).at[:, :N_GROUPS].set(w_group)
    w_r = w_r.at[:, N_GROUPS:N_GROUPS + N_EXPERTS].set(w_expert)
    w_hi = w_r.astype(BF16)
    w_lo = (w_r - w_hi.astype(F32)).astype(BF16)
    return jnp.concatenate([w_hi, w_lo], axis=1)


def _route(h, g_ref, w_ref, tri_ref, meta_ref, idx_ref, cnt_ref):
    xn = _rmsnorm(h, g_ref[...])
    x_hi = xn.astype(BF16)
    x_lo = (xn - x_hi.astype(F32)).astype(BF16)
    t = jnp.dot(x_hi, w_ref[...], preferred_element_type=F32)
    logits = (t[:, :LANES] + t[:, LANES:]) + jnp.dot(x_lo, w_ref[:, :LANES], preferred_element_type=F32)
    tm = logits.shape[0]
    lane = lax.broadcasted_iota(jnp.int32, (tm, LANES), 1).astype(F32)
    ninf = -jnp.inf
    lg = jnp.where(lane < N_GROUPS, logits, ninf)
    gm = jnp.max(lg, axis=-1, keepdims=True)
    g_idx = jnp.min(jnp.where(lg == gm, lane, float(LANES)), axis=-1, keepdims=True)
    g_gate = 1.0 / jnp.sum(jnp.exp(lg - gm), axis=-1, keepdims=True)
    lo = N_GROUPS + EXPERTS_PER_GROUP * g_idx
    le = jnp.where((lane >= lo) & (lane < lo + EXPERTS_PER_GROUP), logits, ninf)
    m1 = jnp.max(le, axis=-1, keepdims=True)
    i1 = jnp.min(jnp.where(le == m1, lane, float(LANES)), axis=-1, keepdims=True)
    le2 = jnp.where(lane == i1, ninf, le)
    m2 = jnp.max(le2, axis=-1, keepdims=True)
    i2 = jnp.min(jnp.where(le2 == m2, lane, float(LANES)), axis=-1, keepdims=True)
    p2 = jnp.exp(m2 - m1)
    gate1 = g_gate / (1.0 + p2)
    gate2 = g_gate * p2 / (1.0 + p2)
    e1 = i1 - N_GROUPS
    e2 = i2 - N_GROUPS
    meta = jnp.where(lane == 0, e1, jnp.where(lane == 1, e2, jnp.where(lane == 2, gate1,
                     jnp.where(lane == 3, gate2, 0.0))))
    meta_ref[...] = meta
    oh1 = (lane == e1).astype(F32)
    oh2 = (lane == e2).astype(F32)
    both = oh1 + oh2

    @pl.when(pl.program_id(0) == 0)
    def _():
        cnt_ref[...] = jnp.zeros_like(cnt_ref)

    seen = cnt_ref[...]
    before = jnp.dot(tri_ref[...], both.astype(BF16), preferred_element_type=F32) + seen
    r1 = jnp.sum(before * oh1, axis=-1, keepdims=True)
    r2 = jnp.sum(before * oh2, axis=-1, keepdims=True)
    slab = jnp.where(lane == 0, e1, jnp.where(lane == 1, e2, jnp.where(lane == 2, r1,
                     jnp.where(lane == 3, r2, 0.0))))
    idx_ref[...] = jnp.transpose(slab)[0:8, :].astype(jnp.int32)
    cnt_ref[...] = seen + jnp.sum(both, axis=0, keepdims=True)


MOE_ISSUE_UNROLL = 8
ROW_TILE = 8


def _row_tile(off):
    return pl.ds(pl.multiple_of(off, ROW_TILE), ROW_TILE)


def _store_row_tiles(ref, x):
    rows = x.shape[0]
    for c in range(ROW_TILE):
        ref[pl.ds(c, rows, stride=ROW_TILE), :] = x[:, c * LANES:(c + 1) * LANES]


def _load_row_tiles(ref, rows):
    return jnp.concatenate([ref[pl.ds(c, rows, stride=ROW_TILE), :] for c in range(ROW_TILE)], axis=1)


def _dispatch_kernel(d1_ref, d2_ref, plo_ref, pn_ref, nu_ref, h_ref, g_ref, xs_ref,
                     buf_ref, zero_ref, sem, fill_sem):
    i = pl.program_id(0)
    nsteps = pl.num_programs(0)
    tm = h_ref.shape[0]
    rt = ROW_TILE
    rb = zero_ref.shape[0] // rt
    n_blk = xs_ref.shape[0] // (rb * rt)
    slot = i % 2
    base = i * tm

    def slot_drain(s):
        cp = pltpu.make_async_copy(buf_ref.at[s], xs_ref.at[pl.ds(0, tm * rt), :], sem.at[s])
        cp.wait()
        cp.wait()

    @pl.when(i >= 2)
    def _():
        slot_drain(slot)

    _store_row_tiles(buf_ref.at[slot], _rmsnorm(h_ref[...], g_ref[...]))

    def issue(r8, c):
        for k in range(MOE_ISSUE_UNROLL):
            r = r8 * MOE_ISSUE_UNROLL + k
            src = buf_ref.at[slot, _row_tile(r * rt), :]
            pltpu.make_async_copy(src, xs_ref.at[_row_tile(d1_ref[base + r]), :], sem.at[slot]).start(priority=0)
            pltpu.make_async_copy(src, xs_ref.at[_row_tile(d2_ref[base + r]), :], sem.at[slot]).start(priority=1)
        return c

    lax.fori_loop(0, tm // MOE_ISSUE_UNROLL, issue, 0)

    def pad_copy(off, rows):
        return pltpu.make_async_copy(zero_ref.at[pl.ds(0, rows * rt), :],
                                     xs_ref.at[pl.ds(pl.multiple_of(off, rt), rows * rt), :], fill_sem)

    pad_sizes = [1 << b for b in reversed(range(rb.bit_length() - 1))]

    def blk_copy(b):
        return pltpu.make_async_copy(zero_ref, xs_ref.at[pl.ds(pl.multiple_of(b * (rb * rt), rt), rb * rt), :],
                                     fill_sem)

    @pl.when(i == 0)
    def _():
        zero_ref[...] = jnp.zeros_like(zero_ref)

    @pl.when(i < N_EXPERTS)
    def _():
        off = plo_ref[i]
        npad = pn_ref[i]
        for rows in pad_sizes:
            has = (npad & rows) != 0

            @pl.when(has)
            def _(off=off, rows=rows):
                pad_copy(off, rows).start()

            off = off + jnp.where(has, rows * rt, 0)

    @pl.when(i == N_EXPERTS)
    def _():
        def fill(b, c):
            blk_copy(b).start()
            return c

        lax.fori_loop(nu_ref[0], n_blk, fill, 0)

    @pl.when(i == nsteps - 1)
    def _():
        slot_drain(1 - slot)
        slot_drain(slot)

        def per_expert(e, c):
            npad = pn_ref[e]
            for rows in pad_sizes:
                @pl.when((npad & rows) != 0)
                def _(rows=rows):
                    pad_copy(0, rows).wait()

            return c

        lax.fori_loop(0, N_EXPERTS, per_expert, 0)

        def one_blk(b, c):
            blk_copy(0).wait()
            return c

        lax.fori_loop(nu_ref[0], n_blk, one_blk, 0)


def _dispatch(d1, d2, pad_lo, pad_n, n_used, h, g, cap, tm):
    n, d = h.shape
    assert n // tm > N_EXPERTS + 1 and d == ROW_TILE * LANES
    return pl.pallas_call(
        _dispatch_kernel,
        out_shape=jax.ShapeDtypeStruct((cap * ROW_TILE, LANES), F32),
        grid_spec=pltpu.PrefetchScalarGridSpec(
            num_scalar_prefetch=5,
            grid=(n // tm,),
            in_specs=[
                pl.BlockSpec((tm, d), lambda i, *_: (i, 0)),
                pl.BlockSpec((1, d), lambda i, *_: (0, 0)),
            ],
            out_specs=pl.BlockSpec(memory_space=pl.ANY),
            scratch_shapes=[pltpu.VMEM((2, tm * ROW_TILE, LANES), F32),
                            pltpu.VMEM((MOE_ROWS * ROW_TILE, LANES), F32),
                            pltpu.SemaphoreType.DMA((2,)), pltpu.SemaphoreType.DMA],
        ),
        compiler_params=_cparams("arbitrary"),
        name="moe_dispatch",
    )(d1, d2, pad_lo, pad_n, n_used, h, g.reshape(1, d))


def _expert_kernel(be_ref, nu_ref, nx_ref, x_ref, w1_hbm, w3_hbm, w2_hbm, y_ref,
                   wb1, wb3, wb2, w1c, w3c, w2c, wsem, wslot, *, layer):
    i = pl.program_id(0)
    used = i < nu_ref[0]
    e = be_ref[i]
    prev = be_ref[jnp.maximum(i - 1, 0)]
    fresh = jnp.logical_or(i == 0, e != prev)

    def weight_copies(ex, s):
        return [pltpu.make_async_copy(w1_hbm.at[layer, ex], wb1.at[s], wsem.at[s]),
                pltpu.make_async_copy(w3_hbm.at[layer, ex], wb3.at[s], wsem.at[s]),
                pltpu.make_async_copy(w2_hbm.at[layer, ex], wb2.at[s], wsem.at[s])]

    @pl.when(i == 0)
    def _():
        wslot[0] = 0
        for cp in weight_copies(e, 0):
            cp.start()

    @pl.when(jnp.logical_and(used, fresh))
    def _():
        s = wslot[0]
        for cp in weight_copies(e, s):
            cp.wait()
        nxt = nx_ref[e]

        @pl.when(nxt != e)
        def _():
            for cp in weight_copies(nxt, 1 - s):
                cp.start()

        w1c[...] = wb1[s].astype(BF16)
        w3c[...] = wb3[s].astype(BF16)
        w2c[...] = wb2[s].astype(BF16)
        wslot[0] = 1 - s

    @pl.when(used)
    def _():
        rb = x_ref.shape[0] // ROW_TILE
        x = _load_row_tiles(x_ref, rb).astype(BF16)
        hf = w1c.shape[1] // 2
        y = None
        for c in range(2):
            cs = slice(c * hf, (c + 1) * hf)
            a = jnp.dot(x, w1c[:, cs], preferred_element_type=F32)
            b = jnp.dot(x, w3c[:, cs], preferred_element_type=F32)
            act = (jax.nn.silu(a) * b).astype(BF16)
            part = jnp.dot(act, w2c[cs, :], preferred_element_type=F32)
            y = part if y is None else y + part
        _store_row_tiles(y_ref, y)

    @pl.when(jnp.logical_not(used))
    def _():
        y_ref[...] = jnp.zeros_like(y_ref)


def _experts(blk_e, n_used, next_e, xs, w1, w3, w2, layer):
    d = ROW_TILE * LANES
    rb = MOE_ROWS
    n_blk = xs.shape[0] // (rb * ROW_TILE)
    ff = w1.shape[3]
    blk = (rb * ROW_TILE, LANES)
    hbm = pl.BlockSpec(memory_space=pl.ANY)
    return pl.pallas_call(
        functools.partial(_expert_kernel, layer=layer),
        out_shape=jax.ShapeDtypeStruct(xs.shape, F32),
        grid_spec=pltpu.PrefetchScalarGridSpec(
            num_scalar_prefetch=3,
            grid=(n_blk,),
            in_specs=[
                pl.BlockSpec(blk, lambda i, be, nu, nx: (jnp.minimum(i, jnp.maximum(nu[0] - 1, 0)), 0)),
                hbm, hbm, hbm,
            ],
            out_specs=pl.BlockSpec(blk, lambda i, be, nu, nx: (i, 0)),
            scratch_shapes=[pltpu.VMEM((2, d, ff), F32), pltpu.VMEM((2, d, ff), F32), pltpu.VMEM((2, ff, d), F32),
                            pltpu.VMEM((d, ff), BF16), pltpu.VMEM((d, ff), BF16), pltpu.VMEM((ff, d), BF16),
                            pltpu.SemaphoreType.DMA((2,)), pltpu.SMEM((1,), jnp.int32)],
        ),
        compiler_params=_cparams("arbitrary"),
        name="moe_experts",
    )(blk_e, n_used, next_e, xs, w1, w3, w2)


def _combine_kernel(d1_ref, d2_ref, h_ref, meta_ref, g_ref, ys_ref, o_ref, buf_ref, sem, *, final_norm):
    i = pl.program_id(0)
    nsteps = pl.num_programs(0)
    tm = h_ref.shape[0]
    slot = i % 2

    def fetch(tile, s):
        base = tile * tm

        def issue(r8, c):
            for k in range(MOE_ISSUE_UNROLL):
                r = r8 * MOE_ISSUE_UNROLL + k
                pltpu.make_async_copy(ys_ref.at[_row_tile(d1_ref[base + r]), :],
                                      buf_ref.at[s, 0, _row_tile(r * ROW_TILE), :], sem.at[s]).start(priority=0)
                pltpu.make_async_copy(ys_ref.at[_row_tile(d2_ref[base + r]), :],
                                      buf_ref.at[s, 1, _row_tile(r * ROW_TILE), :], sem.at[s]).start(priority=1)
            return c

        lax.fori_loop(0, tm // MOE_ISSUE_UNROLL, issue, 0)

    @pl.when(i == 0)
    def _():
        fetch(0, 0)

    @pl.when(i + 1 < nsteps)
    def _():
        fetch(i + 1, 1 - slot)

    for j in range(2):
        pltpu.make_async_copy(ys_ref.at[pl.ds(0, tm * ROW_TILE), :], buf_ref.at[slot, j], sem.at[slot]).wait()

    meta = meta_ref[...]
    y1 = _load_row_tiles(buf_ref.at[slot, 0], tm)
    y2 = _load_row_tiles(buf_ref.at[slot, 1], tm)
    out = h_ref[...] + (meta[:, 2:3] * y1 + meta[:, 3:4] * y2)
    if final_norm:
        out = _rmsnorm(out, g_ref[...])
    o_ref[...] = out


def _combine(d1, d2, h, meta, ys, g_final, tm, final_norm):
    n, d = h.shape
    return pl.pallas_call(
        functools.partial(_combine_kernel, final_norm=final_norm),
        out_shape=jax.ShapeDtypeStruct((n, d), F32),
        grid_spec=pltpu.PrefetchScalarGridSpec(
            num_scalar_prefetch=2,
            grid=(n // tm,),
            in_specs=[
                pl.BlockSpec((tm, d), lambda i, *_: (i, 0)),
                pl.BlockSpec((tm, LANES), lambda i, *_: (i, 0)),
                pl.BlockSpec((1, d), lambda i, *_: (0, 0)),
                pl.BlockSpec(memory_space=pl.ANY),
            ],
            out_specs=pl.BlockSpec((tm, d), lambda i, *_: (i, 0)),
            scratch_shapes=[pltpu.VMEM((2, 2, tm * ROW_TILE, LANES), F32), pltpu.SemaphoreType.DMA((2,))],
        ),
        compiler_params=_cparams("arbitrary"),
        name="moe_combine",
    )(d1, d2, h, meta, g_final.reshape(1, d), ys)


def _hier_moe(h, routing, ln_g, w1, w3, w2, layer, g_final, final_norm):
    n, d = h.shape
    rb = MOE_ROWS
    cap = 2 * n + N_EXPERTS * rb
    n_blk = cap // rb
    meta, idx, counts = routing
    cnt = counts[0, :N_EXPERTS].astype(jnp.int32)
    padded = ((cnt + rb - 1) // rb) * rb
    pad_end = jnp.cumsum(padded)
    pad_start = pad_end - padded
    blk_start = jnp.arange(n_blk, dtype=jnp.int32) * rb
    blk_e = jnp.minimum(jnp.sum(pad_end[None, :] <= blk_start[:, None], axis=1), N_EXPERTS - 1).astype(jnp.int32)
    n_used = (pad_end[-1] // rb).astype(jnp.int32).reshape(1)
    experts = jnp.arange(N_EXPERTS, dtype=jnp.int32)[None, :]
    start_of = lambda e: jnp.sum(jnp.where(e[:, None] == experts, pad_start[None, :], 0), axis=1)
    d1 = (start_of(idx[0]) + idx[2]) * ROW_TILE
    d2 = (start_of(idx[1]) + idx[3]) * ROW_TILE
    xs = _dispatch(d1, d2, (pad_start + cnt) * ROW_TILE, padded - cnt, n_used, h, ln_g, cap, MOE_DISPATCH_ROWS)
    later = (experts > experts.T) & (padded > 0)[None, :]
    next_e = jnp.min(jnp.where(later, experts, N_EXPERTS), axis=1)
    next_e = jnp.where(next_e == N_EXPERTS, experts[0], next_e).astype(jnp.int32)
    ys = _experts(blk_e, n_used, next_e, xs, w1, w3, w2, layer)
    return _combine(d1, d2, h, meta, ys, g_final, MOE_COMBINE_ROWS, final_norm)


def kernel(x, ln_mix, ln_ffn, ln_final, ssm_w_in, ssm_lam_re, ssm_lam_im, ssm_log_dt, ssm_b_re, ssm_b_im, ssm_c_re, ssm_c_im, ssm_d, ssm_w_out, attn_w_qkv, attn_w_o, moe_w_group, moe_w_expert, moe_w1, moe_w3, moe_w2):
    bsz, seq, d = x.shape
    n = bsz * seq
    h = x.reshape(n, d)

    u = _norm_proj(h, ln_mix[0], ssm_w_in[0].astype(BF16), DENSE_ROWS, F32, "s5_in_proj")
    mats = _s5_mats(ssm_lam_re[0], ssm_lam_im[0], ssm_log_dt[0], ssm_b_re[0], ssm_b_im[0],
                    ssm_c_re[0], ssm_c_im[0])
    z = _s5_core(u, mats, ssm_d[0], bsz, seq)
    h, routing = _glu_out(z, ssm_w_out[0].astype(BF16), h, ln_ffn[0],
                          _router_weights(moe_w_group[0], moe_w_expert[0]), MIXER_OUT_ROWS)
    h = _hier_moe(h, routing, ln_ffn[0], moe_w1, moe_w3, moe_w2, 0, ln_final, False)

    q_t, k, v_t = _qkv_proj(h, ln_mix[1], attn_w_qkv[0], DENSE_ROWS)
    o_t = _moba(q_t, k, v_t, bsz, seq)
    h, routing = _proj_res_t(o_t, attn_w_o[0].astype(BF16), h, ln_ffn[1],
                             _router_weights(moe_w_group[1], moe_w_expert[1]), MIXER_OUT_ROWS)
    h = _hier_moe(h, routing, ln_ffn[1], moe_w1, moe_w3, moe_w2, 1, ln_final, True)
    return h.reshape(bsz, seq, d)
```

```python
import functools
import math

import jax
import jax.numpy as jnp
from jax import lax
from jax.experimental import pallas as pl
from jax.experimental.pallas import tpu as pltpu

F32 = jnp.float32
BF16 = jnp.bfloat16

D_MODEL = 1024
RMS_EPS = 1e-6
NEG_INF = -1e30

SSM_GROUP = 16
SSM_GROUPS = D_MODEL // SSM_GROUP
SSM_STATE = 64
SSM_CHUNK = 16
SSM_GB = 8
SSM_WIN = 8

ATT_HEADS = 8
HEAD_DIM = 128
MOBA_BLOCK = 256
MOBA_TOPK = 3

N_GROUPS = 4
EXPERTS_PER_GROUP = 8
N_EXPERTS = 32
EXPERT_FF = 512
MOE_ROWS = 512
MOE_DISPATCH_ROWS = 256
MOE_COMBINE_ROWS = 256
DENSE_ROWS = 1024
MIXER_OUT_ROWS = 512

LANES = 128
VMEM_LIMIT = 56 * 1024 * 1024

_NT = (((1,), (1,)), ((), ()))
_TN = (((0,), (0,)), ((), ()))


def _cparams(*sem):
    return pltpu.CompilerParams(dimension_semantics=sem, vmem_limit_bytes=VMEM_LIMIT)


def _rmsnorm(x, g):
    return x * lax.rsqrt(jnp.mean(x * x, axis=-1, keepdims=True) + RMS_EPS) * g


def _norm_proj_kernel(x_ref, g_ref, w_ref, o_ref):
    xn = _rmsnorm(x_ref[...], g_ref[...]).astype(BF16)
    o_ref[...] = jnp.dot(xn, w_ref[...], preferred_element_type=F32).astype(o_ref.dtype)


def _norm_proj(x, g, w_bf16, tm, out_dtype, name):
    n, d = x.shape
    dout = w_bf16.shape[1]
    return pl.pallas_call(
        _norm_proj_kernel,
        out_shape=jax.ShapeDtypeStruct((n, dout), out_dtype),
        grid=(n // tm,),
        in_specs=[
            pl.BlockSpec((tm, d), lambda i: (i, 0)),
            pl.BlockSpec((1, d), lambda i: (0, 0)),
            pl.BlockSpec((d, dout), lambda i: (0, 0)),
        ],
        out_specs=pl.BlockSpec((tm, dout), lambda i: (i, 0)),
        compiler_params=_cparams("parallel"),
        name=name,
    )(x, g.reshape(1, d), w_bf16)


def _s5_mats(lam_re, lam_im, log_dt, b_re, b_im, c_re, c_im):
    g_, p_, c_, t_ = SSM_GROUPS, SSM_STATE, SSM_GROUP, SSM_CHUNK
    lr = jnp.minimum(lam_re, -1e-4)
    li = lam_im
    dt = jnp.exp(log_dt)[:, None]
    mag = jnp.exp(lr * dt)
    abar_re = mag * jnp.cos(li * dt)
    abar_im = mag * jnp.sin(li * dt)
    den = lr * lr + li * li
    nr = abar_re - 1.0
    gam_re = (nr * lr + abar_im * li) / den
    gam_im = (abar_im * lr - nr * li) / den
    bb_re = gam_re[..., None] * b_re - gam_im[..., None] * b_im
    bb_im = gam_re[..., None] * b_im + gam_im[..., None] * b_re

    def powers(ns):
        nf = jnp.asarray(ns, F32)[None, :, None]
        pm = jnp.exp(nf * (lr * dt)[:, None, :])
        ang = nf * (li * dt)[:, None, :]
        return pm * jnp.cos(ang), pm * jnp.sin(ang)

    pr, pi = powers(list(range(t_ + 1)))
    ca_re = c_re[:, None] * pr[:, :, None, :] - c_im[:, None] * pi[:, :, None, :]
    ca_im = c_re[:, None] * pi[:, :, None, :] + c_im[:, None] * pr[:, :, None, :]
    ca_n = jnp.concatenate([ca_re[:, :t_], ca_im[:, :t_]], axis=-1).reshape(g_, t_ * c_, 2 * p_)
    bb_s = jnp.concatenate([bb_re, -bb_im], axis=1)
    kflat = jnp.einsum('gxp,gpc->gcx', ca_n, bb_s, precision='highest')
    prs = pr[:, :t_][:, ::-1][:, :, None, :]
    pis = pi[:, :t_][:, ::-1][:, :, None, :]
    bt_re = bb_re.transpose(0, 2, 1)[:, None]
    bt_im = bb_im.transpose(0, 2, 1)[:, None]
    w_re = prs * bt_re - pis * bt_im
    w_im = prs * bt_im + pis * bt_re
    w = jnp.concatenate([w_re, w_im], axis=-1).reshape(g_, t_ * c_, 2 * p_)
    prt = pr.transpose(0, 2, 1)[:, :, 1:, None]
    pit = pi.transpose(0, 2, 1)[:, :, 1:, None]
    ct_re = c_re.transpose(0, 2, 1)[:, :, None, :]
    ct_im = c_im.transpose(0, 2, 1)[:, :, None, :]
    v_re = (ct_re * prt - ct_im * pit).reshape(g_, p_, t_ * c_)
    v_im = -(ct_re * pit + ct_im * prt).reshape(g_, p_, t_ * c_)
    v = jnp.concatenate([v_re, v_im], axis=1)
    qr, qi = powers([t_ * (1 << j) for j in range(4)])
    rows = []
    for j in range(4):
        ar, ai = qr[:, j], qi[:, j]
        rows += [jnp.concatenate([ar, ar], -1), jnp.concatenate([-ai, ai], -1),
                 jnp.concatenate([ai, -ai], -1)]
    rows += [jnp.zeros_like(rows[0])] * 4
    coef = jnp.stack(rows, axis=1)
    return kflat.astype(F32), w.astype(BF16), v.astype(BF16), coef.astype(F32)


def _s5_perm():
    r = jnp.arange(8 * LANES)
    col = ((r % LANES) // SSM_GROUP) * LANES + (r // LANES) * SSM_GROUP + r % SSM_GROUP
    p = (col[:, None] == r[None, :]).astype(BF16)
    return p, p.T


def _s5_kernel(u_ref, kf_ref, w_ref, v_ref, coef_ref, d_ref, p_ref, pt_ref, o_ref,
               m_ref, vf_ref, zf_ref, ea_ref, eb_ref, sp_ref):
    t_, c_, gb, win = SSM_CHUNK, SSM_GROUP, SSM_GB, SSM_WIN
    nk = u_ref.shape[0] // t_
    p2 = 2 * SSM_STATE
    tc = t_ * c_

    @pl.when(pl.program_id(1) == 0)
    def _():
        lane_tc = lax.broadcasted_iota(jnp.int32, (c_, tc), 1)
        for g in range(gb):
            kf = kf_ref[g]
            for s in range(t_):
                rolled = kf if s == 0 else pltpu.roll(kf, s * c_, axis=1)
                m_ref[g, s * c_:(s + 1) * c_, :] = jnp.where(lane_tc >= s * c_, rolled, 0.0).astype(BF16)

    halves = tc // LANES

    def timestep(t):
        return pl.ds(t, nk, stride=t_)

    def flat(g):
        return jnp.concatenate([vf_ref[h, :, g * LANES:(g + 1) * LANES] for h in range(halves)], axis=1)

    for h in range(halves):
        x = jnp.concatenate([u_ref[timestep(8 * h + i), :].astype(BF16) for i in range(8)], axis=1)
        vf_ref[h] = jnp.dot(x, p_ref[...], preferred_element_type=F32).astype(BF16)

    row = lax.broadcasted_iota(jnp.int32, (nk, p2), 0)

    def shift(x, d):
        return jnp.where(row < d, 0.0, pltpu.roll(x, d, axis=0))

    def swap(x):
        return pltpu.roll(x, SSM_STATE, axis=1)

    for g in range(gb):
        sc = jnp.dot(flat(g), w_ref[g], preferred_element_type=F32)
        xa = shift(sc, 1)
        cf = coef_ref[g]
        for lvl in range(3):
            pp, qa = cf[3 * lvl:3 * lvl + 1], cf[3 * lvl + 1:3 * lvl + 2]
            sa = shift(xa, 1 << lvl)
            xa = xa + pp * sa + qa * swap(sa)
        ea_ref[g] = xa
        eb_ref[g] = swap(xa)

    cfs = [coef_ref[g] for g in range(gb)]
    zero = jnp.zeros((win, p2), F32)
    state = [(zero, zero)] * gb
    for j in range(nk // win):
        rs = slice(j * win, (j + 1) * win)
        for g in range(gb):
            pp, qa, qb = cfs[g][9:10], cfs[g][10:11], cfs[g][11:12]
            s_a, s_b = state[g]
            n_a = pp * s_a + qa * s_b + ea_ref[g, rs, :]
            n_b = pp * s_b + qb * s_a + eb_ref[g, rs, :]
            sp_ref[g, rs, :] = n_a
            state[g] = (n_a, n_b)

    for g in range(gb):
        x = flat(g)
        y = jnp.dot(x, m_ref[g], preferred_element_type=F32)
        y = y + jnp.dot(sp_ref[g].astype(BF16), v_ref[g], preferred_element_type=F32)
        z = jax.nn.gelu(y + d_ref[g] * x.astype(F32)).astype(BF16)
        for h in range(halves):
            zf_ref[h, :, g * LANES:(g + 1) * LANES] = z[:, h * LANES:(h + 1) * LANES]

    for h in range(halves):
        zn = jnp.dot(zf_ref[h], pt_ref[...], preferred_element_type=F32)
        for i in range(8):
            o_ref[timestep(8 * h + i), :] = zn[:, i * LANES:(i + 1) * LANES]


def _s5_core(u, mats, d_skip, bsz, seq):
    kflat, w, v, coef = mats
    n, d = u.shape
    gb, tc, p2 = SSM_GB, SSM_CHUNK * SSM_GROUP, 2 * SSM_STATE
    nk = seq // SSM_CHUNK
    halves = tc // LANES
    perm, perm_t = _s5_perm()
    dflat = jnp.tile(d_skip.reshape(SSM_GROUPS, 1, SSM_GROUP), (1, 1, SSM_CHUNK))
    spec3 = lambda a, b: pl.BlockSpec((gb, a, b), lambda j, bb: (j, 0, 0))
    const = pl.BlockSpec(perm.shape, lambda j, bb: (0, 0))
    return pl.pallas_call(
        _s5_kernel,
        out_shape=jax.ShapeDtypeStruct((n, d), F32),
        grid=(SSM_GROUPS // gb, bsz),
        in_specs=[
            pl.BlockSpec((seq, LANES), lambda j, bb: (bb, j)),
            spec3(SSM_GROUP, tc), spec3(tc, p2), spec3(p2, tc), spec3(16, p2), spec3(1, tc),
            const, const,
        ],
        out_specs=pl.BlockSpec((seq, LANES), lambda j, bb: (bb, j)),
        scratch_shapes=[pltpu.VMEM((gb, tc, tc), BF16),
                        pltpu.VMEM((halves, nk, gb * LANES), BF16),
                        pltpu.VMEM((halves, nk, gb * LANES), BF16),
                        pltpu.VMEM((gb, nk, p2), F32), pltpu.VMEM((gb, nk, p2), F32),
                        pltpu.VMEM((gb, nk, p2), F32)],
        compiler_params=_cparams("arbitrary", "arbitrary"),
        name="s5_core",
    )(u, kflat, w, v, coef, dflat, perm, perm_t)


def _glu_out_kernel(z_ref, w_ref, x_ref, g_ref, wr_ref, tri_ref, o_ref, meta_ref, idx_ref, cnt_ref):
    vg = jnp.dot(z_ref[...].astype(BF16), w_ref[...], preferred_element_type=F32)
    d = o_ref.shape[1]
    h = x_ref[...] + vg[:, :d] * jax.nn.sigmoid(vg[:, d:])
    o_ref[...] = h
    _route(h, g_ref, wr_ref, tri_ref, meta_ref, idx_ref, cnt_ref)


def _proj_res_t_kernel(at_ref, w_ref, x_ref, g_ref, wr_ref, tri_ref, o_ref, meta_ref, idx_ref, cnt_ref):
    h = x_ref[...] + lax.dot_general(at_ref[...], w_ref[...], _TN, preferred_element_type=F32)
    o_ref[...] = h
    _route(h, g_ref, wr_ref, tri_ref, meta_ref, idx_ref, cnt_ref)


def _mixer_out(kernel_fn, a, a_spec, w_bf16, x, ln_g, w_router, tm, name):
    n, d = x.shape
    tri = (jnp.arange(tm)[:, None] > jnp.arange(tm)[None, :]).astype(BF16)
    full = lambda arr: pl.BlockSpec(arr.shape, lambda i: (0, 0))
    h, meta, idx, counts = pl.pallas_call(
        kernel_fn,
        out_shape=(jax.ShapeDtypeStruct((n, d), F32), jax.ShapeDtypeStruct((n, LANES), F32),
                   jax.ShapeDtypeStruct((8, n), jnp.int32), jax.ShapeDtypeStruct((1, LANES), F32)),
        grid=(n // tm,),
        in_specs=[a_spec, full(w_bf16), pl.BlockSpec((tm, d), lambda i: (i, 0)),
                  pl.BlockSpec((1, d), lambda i: (0, 0)), full(w_router), full(tri)],
        out_specs=(pl.BlockSpec((tm, d), lambda i: (i, 0)), pl.BlockSpec((tm, LANES), lambda i: (i, 0)),
                   pl.BlockSpec((8, tm), lambda i: (0, i)), pl.BlockSpec((1, LANES), lambda i: (0, 0))),
        compiler_params=_cparams("arbitrary"),
        name=name,
    )(a, w_bf16, x, ln_g.reshape(1, d), w_router, tri)
    return h, (meta, idx, counts)


def _glu_out(z, w_bf16, x, ln_g, w_router, tm):
    spec = pl.BlockSpec((tm, z.shape[1]), lambda i: (i, 0))
    return _mixer_out(_glu_out_kernel, z, spec, w_bf16, x, ln_g, w_router, tm, "s5_glu_out")


def _proj_res_t(a_t, w_bf16, x, ln_g, w_router, tm):
    spec = pl.BlockSpec((a_t.shape[0], tm), lambda i: (0, i))
    return _mixer_out(_proj_res_t_kernel, a_t, spec, w_bf16, x, ln_g, w_router, tm, "attn_out_proj")


def _qkv_kernel(x_ref, g_ref, wq_ref, wk_ref, wv_ref, qt_ref, k_ref, vt_ref):
    xn = _rmsnorm(x_ref[...], g_ref[...]).astype(BF16)
    c = (HEAD_DIM ** -0.5) * math.log2(math.e)
    qt_ref[...] = (lax.dot_general(wq_ref[...], xn, _NT, preferred_element_type=F32) * c).astype(BF16)
    k_ref[...] = jnp.dot(xn, wk_ref[...], preferred_element_type=F32).astype(BF16)
    vt = lax.dot_general(wv_ref[...], xn, _NT, preferred_element_type=F32).astype(BF16)
    for c in range(vt_ref.shape[0]):
        vt_ref[c] = vt[:, c * MOBA_BLOCK:(c + 1) * MOBA_BLOCK]


def _qkv_proj(x, g, w_qkv, tm):
    n, d = x.shape
    da = ATT_HEADS * HEAD_DIM
    wq_t = w_qkv[:, :da].T.astype(BF16)
    wk = w_qkv[:, da:2 * da].astype(BF16)
    wv_t = w_qkv[:, 2 * da:].T.astype(BF16)
    full = lambda shp: pl.BlockSpec(shp, lambda i: (0, 0))
    tb = tm // MOBA_BLOCK
    return pl.pallas_call(
        _qkv_kernel,
        out_shape=(jax.ShapeDtypeStruct((da, n), BF16), jax.ShapeDtypeStruct((n, da), BF16),
                   jax.ShapeDtypeStruct((n // MOBA_BLOCK, da, MOBA_BLOCK), BF16)),
        grid=(n // tm,),
        in_specs=[pl.BlockSpec((tm, d), lambda i: (i, 0)), full((1, d)),
                  full((da, d)), full((d, da)), full((da, d))],
        out_specs=(pl.BlockSpec((da, tm), lambda i: (0, i)), pl.BlockSpec((tm, da), lambda i: (i, 0)),
                   pl.BlockSpec((tb, da, MOBA_BLOCK), lambda i: (i, 0, 0))),
        compiler_params=_cparams("parallel"),
        name="attn_qkv_proj",
    )(x, g.reshape(1, d), wq_t, wk, wv_t)


MOBA_CHAINS = 4
MOBA_SUM_ROWS = 16


def _moba_kernel(qt_ref, k_ref, vt_ref, et_ref, ot_ref, s_scr, acc_scr, m_scr, *, nb):
    blk, dh = MOBA_BLOCK, HEAD_DIM
    brow = lax.broadcasted_iota(jnp.int32, (nb, blk), 0).astype(F32)
    r_ix = lax.broadcasted_iota(jnp.int32, (blk, blk), 0)
    c_ix = lax.broadcasted_iota(jnp.int32, (blk, blk), 1)
    zpad = jnp.zeros((2 * LANES - dh - nb, blk), BF16)
    kmean = jnp.mean(k_ref[...].astype(F32).reshape(nb, blk, dh), axis=1)
    km_hi = kmean.astype(BF16)
    kmean2 = jnp.concatenate([km_hi, (kmean - km_hi.astype(F32)).astype(BF16)], axis=0)

    def k_aug(t):
        rows = pl.ds(pl.multiple_of(t * blk, blk), blk)
        return jnp.concatenate([k_ref[rows, :], et_ref[rows, :]], axis=1)

    ones = jnp.ones((MOBA_SUM_ROWS, blk), BF16)

    def values(t):
        return jnp.concatenate([vt_ref[t], ones], axis=0)

    def absorb(c, s, vt):
        m_old = m_scr[c]
        m_new = jnp.maximum(m_old, jnp.max(s, axis=0, keepdims=True))
        p = jnp.exp2(s - m_new).astype(BF16)
        acc_scr[c] = jnp.exp2(m_old - m_new) * acc_scr[c] + jnp.dot(vt, p, preferred_element_type=F32)
        m_scr[c] = m_new

    def query_block(i):
        qs = slice(i * blk, (i + 1) * blk)
        qt = qt_ref[:, qs]
        if i > MOBA_TOPK:
            g2 = jnp.dot(kmean2, qt, preferred_element_type=F32)
            gate = g2[:nb] + g2[nb:]
            gate = jnp.where(brow < i, gate, NEG_INF)
            sel = brow >= i
            for _ in range(MOBA_TOPK):
                gm = jnp.max(gate, axis=0, keepdims=True)
                first = jnp.min(jnp.where(gate == gm, brow, float(nb)), axis=0, keepdims=True)
                pick = brow == first
                sel = jnp.logical_or(sel, pick)
                gate = jnp.where(pick, -jnp.inf, gate)
            bias = jnp.where(sel, 0.0, NEG_INF).astype(BF16)
        else:
            bias = jnp.zeros((nb, blk), BF16)
        return jnp.concatenate([qt, bias, zpad], axis=0)

    nq = MOBA_CHAINS
    for i0 in range(0, nb, nq):
        chains = range(min(nq, nb - i0))
        q_aug = [query_block(i0 + c) for c in chains]
        m_scr[...] = jnp.full(m_scr.shape, NEG_INF, F32)
        acc_scr[...] = jnp.zeros(acc_scr.shape, F32)

        def scores(t, slot, cs):
            ka = k_aug(t)
            for c in cs:
                s_scr[c, slot] = jnp.dot(ka, q_aug[c], preferred_element_type=F32)

        scores(0, 0, chains)
        if i0 > 0:
            def body(j, carry):
                cur = [s_scr[c, j % 2] for c in chains]
                scores(j + 1, (j + 1) % 2, chains)
                vt = values(j)
                for c in chains:
                    absorb(c, cur[c], vt)
                return carry

            lax.fori_loop(0, i0, body, 0)
        for t in chains:
            cur = {c: s_scr[c, (i0 + t) % 2] for c in chains if c >= t}
            later = [c for c in chains if c > t]
            if later:
                scores(i0 + t + 1, (i0 + t + 1) % 2, later)
            vt = values(i0 + t)
            absorb(t, jnp.where(r_ix <= c_ix, cur[t], NEG_INF), vt)
            for c in later:
                absorb(c, cur[c], vt)
        for c in chains:
            qs = slice((i0 + c) * blk, (i0 + c + 1) * blk)
            ot_ref[:, qs] = (acc_scr[c, :dh, :] / acc_scr[c, dh:dh + 1, :]).astype(ot_ref.dtype)


def _moba(q_t, k, v_t, bsz, seq):
    nb = seq // MOBA_BLOCK
    da = ATT_HEADS * HEAD_DIM
    e_t = ((jnp.arange(seq) // MOBA_BLOCK)[:, None] == jnp.arange(LANES)[None, :]).astype(BF16)
    tspec = pl.BlockSpec((HEAD_DIM, seq), lambda b, h: (h, b))
    return pl.pallas_call(
        functools.partial(_moba_kernel, nb=nb),
        out_shape=jax.ShapeDtypeStruct((da, bsz * seq), BF16),
        grid=(bsz, ATT_HEADS),
        in_specs=[tspec, pl.BlockSpec((seq, HEAD_DIM), lambda b, h: (b, h)),
                  pl.BlockSpec((nb, HEAD_DIM, MOBA_BLOCK), lambda b, h: (b, h, 0)),
                  pl.BlockSpec((seq, LANES), lambda b, h: (0, 0))],
        out_specs=tspec,
        scratch_shapes=[pltpu.VMEM((MOBA_CHAINS, 2, MOBA_BLOCK, MOBA_BLOCK), F32),
                        pltpu.VMEM((MOBA_CHAINS, HEAD_DIM + MOBA_SUM_ROWS, MOBA_BLOCK), F32),
                        pltpu.VMEM((MOBA_CHAINS, 1, MOBA_BLOCK), F32)],
        compiler_params=_cparams("parallel", "parallel"),
        name="moba_attn",
    )(q_t, k, v_t, e_t)


def _router_weights(w_group, w_expert):
    d = w_group.shape[0]
    w_r = jnp.zeros((d, LANES), F32).at[:, :N_GROUPS].set(w_group)
    w_r = w_r.at[:, N_GROUPS:N_GROUPS + N_EXPERTS].set(w_expert)
    w_hi = w_r.astype(BF16)
    w_lo = (w_r - w_hi.astype(F32)).astype(BF16)
    return jnp.concatenate([w_hi, w_lo], axis=1)


def _route(h, g_ref, w_ref, tri_ref, meta_ref, idx_ref, cnt_ref):
    xn = _rmsnorm(h, g_ref[...])
    x_hi = xn.astype(BF16)
    x_lo = (xn - x_hi.astype(F32)).astype(BF16)
    t = jnp.dot(x_hi, w_ref[...], preferred_element_type=F32)
    logits = (t[:, :LANES] + t[:, LANES:]) + jnp.dot(x_lo, w_ref[:, :LANES], preferred_element_type=F32)
    tm = logits.shape[0]
    lane = lax.broadcasted_iota(jnp.int32, (tm, LANES), 1).astype(F32)
    ninf = -jnp.inf
    lg = jnp.where(lane < N_GROUPS, logits, ninf)
    gm = jnp.max(lg, axis=-1, keepdims=True)
    g_idx = jnp.min(jnp.where(lg == gm, lane, float(LANES)), axis=-1, keepdims=True)
    g_gate = 1.0 / jnp.sum(jnp.exp(lg - gm), axis=-1, keepdims=True)
    lo = N_GROUPS + EXPERTS_PER_GROUP * g_idx
    le = jnp.where((lane >= lo) & (lane < lo + EXPERTS_PER_GROUP), logits, ninf)
    m1 = jnp.max(le, axis=-1, keepdims=True)
    i1 = jnp.min(jnp.where(le == m1, lane, float(LANES)), axis=-1, keepdims=True)
    le2 = jnp.where(lane == i1, ninf, le)
    m2 = jnp.max(le2, axis=-1, keepdims=True)
    i2 = jnp.min(jnp.where(le2 == m2, lane, float(LANES)), axis=-1, keepdims=True)
    p2 = jnp.exp(m2 - m1)
    gate1 = g_gate / (1.0 + p2)
    gate2 = g_gate * p2 / (1.0 + p2)
    e1 = i1 - N_GROUPS
    e2 = i2 - N_GROUPS
    meta = jnp.where(lane == 0, e1, jnp.where(lane == 1, e2, jnp.where(lane == 2, gate1,
                     jnp.where(lane == 3, gate2, 0.0))))
    meta_ref[...] = meta
    oh1 = (lane == e1).astype(F32)
    oh2 = (lane == e2).astype(F32)
    both = oh1 + oh2

    @pl.when(pl.program_id(0) == 0)
    def _():
        cnt_ref[...] = jnp.zeros_like(cnt_ref)

    seen = cnt_ref[...]
    before = jnp.dot(tri_ref[...], both.astype(BF16), preferred_element_type=F32) + seen
    r1 = jnp.sum(before * oh1, axis=-1, keepdims=True)
    r2 = jnp.sum(before * oh2, axis=-1, keepdims=True)
    slab = jnp.where(lane == 0, e1, jnp.where(lane == 1, e2, jnp.where(lane == 2, r1,
                     jnp.where(lane == 3, r2, 0.0))))
    idx_ref[...] = jnp.transpose(slab)[0:8, :].astype(jnp.int32)
    cnt_ref[...] = seen + jnp.sum(both, axis=0, keepdims=True)


MOE_ISSUE_UNROLL = 8
ROW_TILE = 8


def _row_tile(off):
    return pl.ds(pl.multiple_of(off, ROW_TILE), ROW_TILE)


def _store_row_tiles(ref, x):
    rows = x.shape[0]
    for c in range(ROW_TILE):
        ref[pl.ds(c, rows, stride=ROW_TILE), :] = x[:, c * LANES:(c + 1) * LANES]


def _load_row_tiles(ref, rows):
    return jnp.concatenate([ref[pl.ds(c, rows, stride=ROW_TILE), :] for c in range(ROW_TILE)], axis=1)


def _dispatch_kernel(d1_ref, d2_ref, plo_ref, pn_ref, nu_ref, h_ref, g_ref, xs_ref,
                     buf_ref, zero_ref, sem, fill_sem):
    i = pl.program_id(0)
    nsteps = pl.num_programs(0)
    tm = h_ref.shape[0]
    rt = ROW_TILE
    rb = zero_ref.shape[0] // rt
    n_blk = xs_ref.shape[0] // (rb * rt)
    slot = i % 2
    base = i * tm

    def slot_drain(s):
        cp = pltpu.make_async_copy(buf_ref.at[s], xs_ref.at[pl.ds(0, tm * rt), :], sem.at[s])
        cp.wait()
        cp.wait()

    @pl.when(i >= 2)
    def _():
        slot_drain(slot)

    _store_row_tiles(buf_ref.at[slot], _rmsnorm(h_ref[...], g_ref[...]))

    def issue(r8, c):
        for k in range(MOE_ISSUE_UNROLL):
            r = r8 * MOE_ISSUE_UNROLL + k
            src = buf_ref.at[slot, _row_tile(r * rt), :]
            pltpu.make_async_copy(src, xs_ref.at[_row_tile(d1_ref[base + r]), :], sem.at[slot]).start(priority=0)
            pltpu.make_async_copy(src, xs_ref.at[_row_tile(d2_ref[base + r]), :], sem.at[slot]).start(priority=1)
        return c

    lax.fori_loop(0, tm // MOE_ISSUE_UNROLL, issue, 0)

    def pad_copy(off, rows):
        return pltpu.make_async_copy(zero_ref.at[pl.ds(0, rows * rt), :],
                                     xs_ref.at[pl.ds(pl.multiple_of(off, rt), rows * rt), :], fill_sem)

    pad_sizes = [1 << b for b in reversed(range(rb.bit_length() - 1))]

    def blk_copy(b):
        return pltpu.make_async_copy(zero_ref, xs_ref.at[pl.ds(pl.multiple_of(b * (rb * rt), rt), rb * rt), :],
                                     fill_sem)

    @pl.when(i == 0)
    def _():
        zero_ref[...] = jnp.zeros_like(zero_ref)

    @pl.when(i < N_EXPERTS)
    def _():
        off = plo_ref[i]
        npad = pn_ref[i]
        for rows in pad_sizes:
            has = (npad & rows) != 0

            @pl.when(has)
            def _(off=off, rows=rows):
                pad_copy(off, rows).start()

            off = off + jnp.where(has, rows * rt, 0)

    @pl.when(i == N_EXPERTS)
    def _():
        def fill(b, c):
            blk_copy(b).start()
            return c

        lax.fori_loop(nu_ref[0], n_blk, fill, 0)

    @pl.when(i == nsteps - 1)
    def _():
        slot_drain(1 - slot)
        slot_drain(slot)

        def per_expert(e, c):
            npad = pn_ref[e]
            for rows in pad_sizes:
                @pl.when((npad & rows) != 0)
                def _(rows=rows):
                    pad_copy(0, rows).wait()

            return c

        lax.fori_loop(0, N_EXPERTS, per_expert, 0)

        def one_blk(b, c):
            blk_copy(0).wait()
            return c

        lax.fori_loop(nu_ref[0], n_blk, one_blk, 0)


def _dispatch(d1, d2, pad_lo, pad_n, n_used, h, g, cap, tm):
    n, d = h.shape
    assert n // tm > N_EXPERTS + 1 and d == ROW_TILE * LANES
    return pl.pallas_call(
        _dispatch_kernel,
        out_shape=jax.ShapeDtypeStruct((cap * ROW_TILE, LANES), F32),
        grid_spec=pltpu.PrefetchScalarGridSpec(
            num_scalar_prefetch=5,
            grid=(n // tm,),
            in_specs=[
                pl.BlockSpec((tm, d), lambda i, *_: (i, 0)),
                pl.BlockSpec((1, d), lambda i, *_: (0, 0)),
            ],
            out_specs=pl.BlockSpec(memory_space=pl.ANY),
            scratch_shapes=[pltpu.VMEM((2, tm * ROW_TILE, LANES), F32),
                            pltpu.VMEM((MOE_ROWS * ROW_TILE, LANES), F32),
                            pltpu.SemaphoreType.DMA((2,)), pltpu.SemaphoreType.DMA],
        ),
        compiler_params=_cparams("arbitrary"),
        name="moe_dispatch",
    )(d1, d2, pad_lo, pad_n, n_used, h, g.reshape(1, d))


def _expert_kernel(be_ref, nu_ref, nx_ref, x_ref, w1_hbm, w3_hbm, w2_hbm, y_ref,
                   wb1, wb3, wb2, w1c, w3c, w2c, wsem, wslot, *, layer):
    i = pl.program_id(0)
    used = i < nu_ref[0]
    e = be_ref[i]
    prev = be_ref[jnp.maximum(i - 1, 0)]
    fresh = jnp.logical_or(i == 0, e != prev)

    def weight_copies(ex, s):
        return [pltpu.make_async_copy(w1_hbm.at[layer, ex], wb1.at[s], wsem.at[s]),
                pltpu.make_async_copy(w3_hbm.at[layer, ex], wb3.at[s], wsem.at[s]),
                pltpu.make_async_copy(w2_hbm.at[layer, ex], wb2.at[s], wsem.at[s])]

    @pl.when(i == 0)
    def _():
        wslot[0] = 0
        for cp in weight_copies(e, 0):
            cp.start()

    @pl.when(jnp.logical_and(used, fresh))
    def _():
        s = wslot[0]
        for cp in weight_copies(e, s):
            cp.wait()
        nxt = nx_ref[e]

        @pl.when(nxt != e)
        def _():
            for cp in weight_copies(nxt, 1 - s):
                cp.start()

        w1c[...] = wb1[s].astype(BF16)
        w3c[...] = wb3[s].astype(BF16)
        w2c[...] = wb2[s].astype(BF16)
        wslot[0] = 1 - s

    @pl.when(used)
    def _():
        rb = x_ref.shape[0] // ROW_TILE
        x = _load_row_tiles(x_ref, rb).astype(BF16)
        hf = w1c.shape[1] // 2
        y = None
        for c in range(2):
            cs = slice(c * hf, (c + 1) * hf)
            a = jnp.dot(x, w1c[:, cs], preferred_element_type=F32)
            b = jnp.dot(x, w3c[:, cs], preferred_element_type=F32)
            act = (jax.nn.silu(a) * b).astype(BF16)
            part = jnp.dot(act, w2c[cs, :], preferred_element_type=F32)
            y = part if y is None else y + part
        _store_row_tiles(y_ref, y)

    @pl.when(jnp.logical_not(used))
    def _():
        y_ref[...] = jnp.zeros_like(y_ref)


def _experts(blk_e, n_used, next_e, xs, w1, w3, w2, layer):
    d = ROW_TILE * LANES
    rb = MOE_ROWS
    n_blk = xs.shape[0] // (rb * ROW_TILE)
    ff = w1.shape[3]
    blk = (rb * ROW_TILE, LANES)
    hbm = pl.BlockSpec(memory_space=pl.ANY)
    return pl.pallas_call(
        functools.partial(_expert_kernel, layer=layer),
        out_shape=jax.ShapeDtypeStruct(xs.shape, F32),
        grid_spec=pltpu.PrefetchScalarGridSpec(
            num_scalar_prefetch=3,
            grid=(n_blk,),
            in_specs=[
                pl.BlockSpec(blk, lambda i, be, nu, nx: (jnp.minimum(i, jnp.maximum(nu[0] - 1, 0)), 0)),
                hbm, hbm, hbm,
            ],
            out_specs=pl.BlockSpec(blk, lambda i, be, nu, nx: (i, 0)),
            scratch_shapes=[pltpu.VMEM((2, d, ff), F32), pltpu.VMEM((2, d, ff), F32), pltpu.VMEM((2, ff, d), F32),
                            pltpu.VMEM((d, ff), BF16), pltpu.VMEM((d, ff), BF16), pltpu.VMEM((ff, d), BF16),
                            pltpu.SemaphoreType.DMA((2,)), pltpu.SMEM((1,), jnp.int32)],
        ),
        compiler_params=_cparams("arbitrary"),
        name="moe_experts",
    )(blk_e, n_used, next_e, xs, w1, w3, w2)


def _combine_kernel(d1_ref, d2_ref, h_ref, meta_ref, g_ref, ys_ref, o_ref, buf_ref, sem, *, final_norm):
    i = pl.program_id(0)
    nsteps = pl.num_programs(0)
    tm = h_ref.shape[0]
    slot = i % 2

    def fetch(tile, s):
        base = tile * tm

        def issue(r8, c):
            for k in range(MOE_ISSUE_UNROLL):
                r = r8 * MOE_ISSUE_UNROLL + k
                pltpu.make_async_copy(ys_ref.at[_row_tile(d1_ref[base + r]), :],
                                      buf_ref.at[s, 0, _row_tile(r * ROW_TILE), :], sem.at[s]).start(priority=0)
                pltpu.make_async_copy(ys_ref.at[_row_tile(d2_ref[base + r]), :],
                                      buf_ref.at[s, 1, _row_tile(r * ROW_TILE), :], sem.at[s]).start(priority=1)
            return c

        lax.fori_loop(0, tm // MOE_ISSUE_UNROLL, issue, 0)

    @pl.when(i == 0)
    def _():
        fetch(0, 0)

    @pl.when(i + 1 < nsteps)
    def _():
        fetch(i + 1, 1 - slot)

    for j in range(2):
        pltpu.make_async_copy(ys_ref.at[pl.ds(0, tm * ROW_TILE), :], buf_ref.at[slot, j], sem.at[slot]).wait()

    meta = meta_ref[...]
    y1 = _load_row_tiles(buf_ref.at[slot, 0], tm)
    y2 = _load_row_tiles(buf_ref.at[slot, 1], tm)
    out = h_ref[...] + (meta[:, 2:3] * y1 + meta[:, 3:4] * y2)
    if final_norm:
        out = _rmsnorm(out, g_ref[...])
    o_ref[...] = out


def _combine(d1, d2, h, meta, ys, g_final, tm, final_norm):
    n, d = h.shape
    return pl.pallas_call(
        functools.partial(_combine_kernel, final_norm=final_norm),
        out_shape=jax.ShapeDtypeStruct((n, d), F32),
        grid_spec=pltpu.PrefetchScalarGridSpec(
            num_scalar_prefetch=2,
            grid=(n // tm,),
            in_specs=[
                pl.BlockSpec((tm, d), lambda i, *_: (i, 0)),
                pl.BlockSpec((tm, LANES), lambda i, *_: (i, 0)),
                pl.BlockSpec((1, d), lambda i, *_: (0, 0)),
                pl.BlockSpec(memory_space=pl.ANY),
            ],
            out_specs=pl.BlockSpec((tm, d), lambda i, *_: (i, 0)),
            scratch_shapes=[pltpu.VMEM((2, 2, tm * ROW_TILE, LANES), F32), pltpu.SemaphoreType.DMA((2,))],
        ),
        compiler_params=_cparams("arbitrary"),
        name="moe_combine",
    )(d1, d2, h, meta, g_final.reshape(1, d), ys)


def _hier_moe(h, routing, ln_g, w1, w3, w2, layer, g_final, final_norm):
    n, d = h.shape
    rb = MOE_ROWS
    cap = 2 * n + N_EXPERTS * rb
    n_blk = cap // rb
    meta, idx, counts = routing
    cnt = counts[0, :N_EXPERTS].astype(jnp.int32)
    padded = ((cnt + rb - 1) // rb) * rb
    pad_end = jnp.cumsum(padded)
    pad_start = pad_end - padded
    blk_start = jnp.arange(n_blk, dtype=jnp.int32) * rb
    blk_e = jnp.minimum(jnp.sum(pad_end[None, :] <= blk_start[:, None], axis=1), N_EXPERTS - 1).astype(jnp.int32)
    n_used = (pad_end[-1] // rb).astype(jnp.int32).reshape(1)
    experts = jnp.arange(N_EXPERTS, dtype=jnp.int32)[None, :]
    start_of = lambda e: jnp.sum(jnp.where(e[:, None] == experts, pad_start[None, :], 0), axis=1)
    d1 = (start_of(idx[0]) + idx[2]) * ROW_TILE
    d2 = (start_of(idx[1]) + idx[3]) * ROW_TILE
    xs = _dispatch(d1, d2, (pad_start + cnt) * ROW_TILE, padded - cnt, n_used, h, ln_g, cap, MOE_DISPATCH_ROWS)
    later = (experts > experts.T) & (padded > 0)[None, :]
    next_e = jnp.min(jnp.where(later, experts, N_EXPERTS), axis=1)
    next_e = jnp.where(next_e == N_EXPERTS, experts[0], next_e).astype(jnp.int32)
    ys = _experts(blk_e, n_used, next_e, xs, w1, w3, w2, layer)
    return _combine(d1, d2, h, meta, ys, g_final, MOE_COMBINE_ROWS, final_norm)


def kernel(x, ln_mix, ln_ffn, ln_final, ssm_w_in, ssm_lam_re, ssm_lam_im, ssm_log_dt, ssm_b_re, ssm_b_im, ssm_c_re, ssm_c_im, ssm_d, ssm_w_out, attn_w_qkv, attn_w_o, moe_w_group, moe_w_expert, moe_w1, moe_w3, moe_w2):
    bsz, seq, d = x.shape
    n = bsz * seq
    h = x.reshape(n, d)

    u = _norm_proj(h, ln_mix[0], ssm_w_in[0].astype(BF16), DENSE_ROWS, F32, "s5_in_proj")
    mats = _s5_mats(ssm_lam_re[0], ssm_lam_im[0], ssm_log_dt[0], ssm_b_re[0], ssm_b_im[0],
                    ssm_c_re[0], ssm_c_im[0])
    z = _s5_core(u, mats, ssm_d[0], bsz, seq)
    h, routing = _glu_out(z, ssm_w_out[0].astype(BF16), h, ln_ffn[0],
                          _router_weights(moe_w_group[0], moe_w_expert[0]), MIXER_OUT_ROWS)
    h = _hier_moe(h, routing, ln_ffn[0], moe_w1, moe_w3, moe_w2, 0, ln_final, False)

    q_t, k, v_t = _qkv_proj(h, ln_mix[1], attn_w_qkv[0], DENSE_ROWS)
    o_t = _moba(q_t, k, v_t, bsz, seq)
    h, routing = _proj_res_t(o_t, attn_w_o[0].astype(BF16), h, ln_ffn[1],
                             _router_weights(moe_w_group[1], moe_w_expert[1]), MIXER_OUT_ROWS)
    h = _hier_moe(h, routing, ln_ffn[1], moe_w1, moe_w3, moe_w2, 1, ln_final, True)
    return h.reshape(bsz, seq, d)
```

```python
import functools
import math

import jax
import jax.numpy as jnp
from jax import lax
from jax.experimental import pallas as pl
from jax.experimental.pallas import tpu as pltpu

F32 = jnp.float32
BF16 = jnp.bfloat16

D_MODEL = 1024
RMS_EPS = 1e-6
NEG_INF = -1e30

SSM_GROUP = 16
SSM_GROUPS = D_MODEL // SSM_GROUP
SSM_STATE = 64
SSM_CHUNK = 16
SSM_GB = 8
SSM_WIN = 8
SSM_POW_ROWS = 24

ATT_HEADS = 8
HEAD_DIM = 128
MOBA_BLOCK = 256
MOBA_TOPK = 3

N_GROUPS = 4
EXPERTS_PER_GROUP = 8
N_EXPERTS = 32
EXPERT_FF = 512
MOE_ROWS = 512
MOE_DISPATCH_ROWS = 256
MOE_COMBINE_ROWS = 256
DENSE_ROWS = 1024
MIXER_OUT_ROWS = 512

LANES = 128
VMEM_LIMIT = 56 * 1024 * 1024

_NT = (((1,), (1,)), ((), ()))
_TN = (((0,), (0,)), ((), ()))


def _cparams(*sem):
    return pltpu.CompilerParams(dimension_semantics=sem, vmem_limit_bytes=VMEM_LIMIT)


def _rmsnorm(x, g):
    return x * lax.rsqrt(jnp.mean(x * x, axis=-1, keepdims=True) + RMS_EPS) * g


def _norm_proj_kernel(x_ref, g_ref, w_ref, o_ref):
    xn = _rmsnorm(x_ref[...], g_ref[...]).astype(BF16)
    o_ref[...] = jnp.dot(xn, w_ref[...], preferred_element_type=F32).astype(o_ref.dtype)


def _norm_proj(x, g, w_bf16, tm, out_dtype, name):
    n, d = x.shape
    dout = w_bf16.shape[1]
    return pl.pallas_call(
        _norm_proj_kernel,
        out_shape=jax.ShapeDtypeStruct((n, dout), out_dtype),
        grid=(n // tm,),
        in_specs=[
            pl.BlockSpec((tm, d), lambda i: (i, 0)),
            pl.BlockSpec((1, d), lambda i: (0, 0)),
            pl.BlockSpec((d, dout), lambda i: (0, 0)),
        ],
        out_specs=pl.BlockSpec((tm, dout), lambda i: (i, 0)),
        compiler_params=_cparams("parallel"),
        name=name,
    )(x, g.reshape(1, d), w_bf16)


def _s5_mats(lam_re, lam_im, log_dt, b_re, b_im, c_re, c_im):
    g_, p_, c_, t_ = SSM_GROUPS, SSM_STATE, SSM_GROUP, SSM_CHUNK
    lr = jnp.minimum(lam_re, -1e-4)
    li = lam_im
    dt = jnp.exp(log_dt)[:, None]
    mag = jnp.exp(lr * dt)
    abar_re = mag * jnp.cos(li * dt)
    abar_im = mag * jnp.sin(li * dt)
    den = lr * lr + li * li
    nr = abar_re - 1.0
    gam_re = (nr * lr + abar_im * li) / den
    gam_im = (abar_im * lr - nr * li) / den
    bb_re = gam_re[..., None] * b_re - gam_im[..., None] * b_im
    bb_im = gam_re[..., None] * b_im + gam_im[..., None] * b_re

    def powers(ns):
        nf = jnp.asarray(ns, F32)[None, :, None]
        pm = jnp.exp(nf * (lr * dt)[:, None, :])
        ang = nf * (li * dt)[:, None, :]
        return pm * jnp.cos(ang), pm * jnp.sin(ang)

    pr, pi = powers(list(range(t_ + 1)))
    pad = jnp.zeros((g_, SSM_POW_ROWS - (t_ + 1), 2 * p_), F32)
    pw = jnp.stack([jnp.concatenate([jnp.concatenate([pr, pr], -1), pad], axis=1),
                    jnp.concatenate([jnp.concatenate([pi, pi], -1), pad], axis=1)], axis=1)
    bt_re = bb_re.transpose(0, 2, 1)
    bt_im = bb_im.transpose(0, 2, 1)
    cat = lambda a, b: jnp.concatenate([a, b], axis=-1)
    fac = jnp.stack([cat(bt_re, bt_im), cat(-bt_im, bt_re), cat(bt_re, -bt_im),
                     cat(c_re, c_im), cat(-c_im, c_re)], axis=1)
    qr, qi = powers([t_ * (1 << j) for j in range(4)])
    rows = []
    for j in range(4):
        ar, ai = qr[:, j], qi[:, j]
        rows += [jnp.concatenate([ar, ar], -1), jnp.concatenate([-ai, ai], -1),
                 jnp.concatenate([ai, -ai], -1)]
    rows += [jnp.zeros_like(rows[0])] * 4
    coef = jnp.stack(rows, axis=1)
    return fac.astype(F32), pw.astype(F32), coef.astype(F32)


def _s5_perm():
    r = jnp.arange(8 * LANES)
    col = ((r % LANES) // SSM_GROUP) * LANES + (r // LANES) * SSM_GROUP + r % SSM_GROUP
    p = (col[:, None] == r[None, :]).astype(BF16)
    return p, p.T


def _s5_kernel(u_ref, fac_ref, pw_ref, coef_ref, d_ref, p_ref, pt_ref, o_ref,
               m_ref, w_ref, v_ref, vf_ref, zf_ref, ea_ref, eb_ref, sp_ref):
    t_, c_, gb, win = SSM_CHUNK, SSM_GROUP, SSM_GB, SSM_WIN
    nk = u_ref.shape[0] // t_
    p2 = 2 * SSM_STATE
    tc = t_ * c_

    @pl.when(pl.program_id(1) == 0)
    def _():
        lane_tc = lax.broadcasted_iota(jnp.int32, (c_, tc), 1)
        conj = jnp.where(lax.broadcasted_iota(jnp.int32, (1, p2), 1) < SSM_STATE, 1.0, -1.0)
        for g in range(gb):
            b_t, ib_t, bconj_t, cm, icm = (fac_ref[g, k] for k in range(5))
            pr, pi = pw_ref[g, 0], pw_ref[g, 1]
            for s in range(t_):
                n = t_ - 1 - s
                w_ref[g, s * c_:(s + 1) * c_, :] = (b_t * pr[n:n + 1] + ib_t * pi[n:n + 1]).astype(BF16)
            ca = jnp.concatenate([cm * pr[n:n + 1] + icm * pi[n:n + 1] for n in range(t_ + 1)], axis=0)
            v_ref[g] = jnp.transpose(ca[c_:] * conj).astype(BF16)
            kf = lax.dot_general(bconj_t, ca[:tc], _NT, preferred_element_type=F32,
                                 precision=lax.Precision.HIGHEST)
            for s in range(t_):
                rolled = kf if s == 0 else pltpu.roll(kf, s * c_, axis=1)
                m_ref[g, s * c_:(s + 1) * c_, :] = jnp.where(lane_tc >= s * c_, rolled, 0.0).astype(BF16)

    halves = tc // LANES

    def timestep(t):
        return pl.ds(t, nk, stride=t_)

    def flat(g):
        return jnp.concatenate([vf_ref[h, :, g * LANES:(g + 1) * LANES] for h in range(halves)], axis=1)

    for h in range(halves):
        x = jnp.concatenate([u_ref[timestep(8 * h + i), :].astype(BF16) for i in range(8)], axis=1)
        vf_ref[h] = jnp.dot(x, p_ref[...], preferred_element_type=F32).astype(BF16)

    row = lax.broadcasted_iota(jnp.int32, (nk, p2), 0)

    def shift(x, d):
        return jnp.where(row < d, 0.0, pltpu.roll(x, d, axis=0))

    def swap(x):
        return pltpu.roll(x, SSM_STATE, axis=1)

    for g in range(gb):
        sc = jnp.dot(flat(g), w_ref[g], preferred_element_type=F32)
        xa = shift(sc, 1)
        cf = coef_ref[g]
        for lvl in range(3):
            pp, qa = cf[3 * lvl:3 * lvl + 1], cf[3 * lvl + 1:3 * lvl + 2]
            sa = shift(xa, 1 << lvl)
            xa = xa + pp * sa + qa * swap(sa)
        ea_ref[g] = xa
        eb_ref[g] = swap(xa)

    cfs = [coef_ref[g] for g in range(gb)]
    zero = jnp.zeros((win, p2), F32)
    state = [(zero, zero)] * gb
    for j in range(nk // win):
        rs = slice(j * win, (j + 1) * win)
        for g in range(gb):
            pp, qa, qb = cfs[g][9:10], cfs[g][10:11], cfs[g][11:12]
            s_a, s_b = state[g]
            n_a = pp * s_a + qa * s_b + ea_ref[g, rs, :]
            n_b = pp * s_b + qb * s_a + eb_ref[g, rs, :]
            sp_ref[g, rs, :] = n_a
            state[g] = (n_a, n_b)

    for g in range(gb):
        x = flat(g)
        y = jnp.dot(x, m_ref[g], preferred_element_type=F32)
        y = y + jnp.dot(sp_ref[g].astype(BF16), v_ref[g], preferred_element_type=F32)
        z = jax.nn.gelu(y + d_ref[g] * x.astype(F32)).astype(BF16)
        for h in range(halves):
            zf_ref[h, :, g * LANES:(g + 1) * LANES] = z[:, h * LANES:(h + 1) * LANES]

    for h in range(halves):
        zn = jnp.dot(zf_ref[h], pt_ref[...], preferred_element_type=F32)
        for i in range(8):
            o_ref[timestep(8 * h + i), :] = zn[:, i * LANES:(i + 1) * LANES]


def _s5_core(u, mats, d_skip, bsz, seq):
    fac, pw, coef = mats
    n, d = u.shape
    gb, tc, p2 = SSM_GB, SSM_CHUNK * SSM_GROUP, 2 * SSM_STATE
    nk = seq // SSM_CHUNK
    halves = tc // LANES
    perm, perm_t = _s5_perm()
    dflat = jnp.tile(d_skip.reshape(SSM_GROUPS, 1, SSM_GROUP), (1, 1, SSM_CHUNK))
    spec3 = lambda a, b: pl.BlockSpec((gb, a, b), lambda j, bb: (j, 0, 0))
    const = pl.BlockSpec(perm.shape, lambda j, bb: (0, 0))
    return pl.pallas_call(
        _s5_kernel,
        out_shape=jax.ShapeDtypeStruct((n, d), F32),
        grid=(SSM_GROUPS // gb, bsz),
        in_specs=[
            pl.BlockSpec((seq, LANES), lambda j, bb: (bb, j)),
            pl.BlockSpec((gb,) + fac.shape[1:], lambda j, bb: (j, 0, 0, 0)),
            pl.BlockSpec((gb,) + pw.shape[1:], lambda j, bb: (j, 0, 0, 0)),
            spec3(16, p2), spec3(1, tc),
            const, const,
        ],
        out_specs=pl.BlockSpec((seq, LANES), lambda j, bb: (bb, j)),
        scratch_shapes=[pltpu.VMEM((gb, tc, tc), BF16), pltpu.VMEM((gb, tc, p2), BF16),
                        pltpu.VMEM((gb, p2, tc), BF16),
                        pltpu.VMEM((halves, nk, gb * LANES), BF16),
                        pltpu.VMEM((halves, nk, gb * LANES), BF16),
                        pltpu.VMEM((gb, nk, p2), F32), pltpu.VMEM((gb, nk, p2), F32),
                        pltpu.VMEM((gb, nk, p2), F32)],
        compiler_params=_cparams("arbitrary", "arbitrary"),
        name="s5_core",
    )(u, fac, pw, coef, dflat, perm, perm_t)


def _glu_out_kernel(z_ref, w_ref, x_ref, g_ref, wr_ref, tri_ref, o_ref, meta_ref, idx_ref, cnt_ref):
    vg = jnp.dot(z_ref[...].astype(BF16), w_ref[...], preferred_element_type=F32)
    d = o_ref.shape[1]
    h = x_ref[...] + vg[:, :d] * jax.nn.sigmoid(vg[:, d:])
    o_ref[...] = h
    _route(h, g_ref, wr_ref, tri_ref, meta_ref, idx_ref, cnt_ref)


def _proj_res_t_kernel(at_ref, w_ref, x_ref, g_ref, wr_ref, tri_ref, o_ref, meta_ref, idx_ref, cnt_ref):
    h = x_ref[...] + lax.dot_general(at_ref[...], w_ref[...], _TN, preferred_element_type=F32)
    o_ref[...] = h
    _route(h, g_ref, wr_ref, tri_ref, meta_ref, idx_ref, cnt_ref)


def _mixer_out(kernel_fn, a, a_spec, w_bf16, x, ln_g, w_router, tm, name):
    n, d = x.shape
    tri = (jnp.arange(tm)[:, None] > jnp.arange(tm)[None, :]).astype(BF16)
    full = lambda arr: pl.BlockSpec(arr.shape, lambda i: (0, 0))
    h, meta, idx, counts = pl.pallas_call(
        kernel_fn,
        out_shape=(jax.ShapeDtypeStruct((n, d), F32), jax.ShapeDtypeStruct((n, LANES), F32),
                   jax.ShapeDtypeStruct((8, n), jnp.int32), jax.ShapeDtypeStruct((1, LANES), F32)),
        grid=(n // tm,),
        in_specs=[a_spec, full(w_bf16), pl.BlockSpec((tm, d), lambda i: (i, 0)),
                  pl.BlockSpec((1, d), lambda i: (0, 0)), full(w_router), full(tri)],
        out_specs=(pl.BlockSpec((tm, d), lambda i: (i, 0)), pl.BlockSpec((tm, LANES), lambda i: (i, 0)),
                   pl.BlockSpec((8, tm), lambda i: (0, i)), pl.BlockSpec((1, LANES), lambda i: (0, 0))),
        compiler_params=_cparams("arbitrary"),
        name=name,
    )(a, w_bf16, x, ln_g.reshape(1, d), w_router, tri)
    return h, (meta, idx, counts)


def _glu_out(z, w_bf16, x, ln_g, w_router, tm):
    spec = pl.BlockSpec((tm, z.shape[1]), lambda i: (i, 0))
    return _mixer_out(_glu_out_kernel, z, spec, w_bf16, x, ln_g, w_router, tm, "s5_glu_out")


def _proj_res_t(a_t, w_bf16, x, ln_g, w_router, tm):
    spec = pl.BlockSpec((a_t.shape[0], tm), lambda i: (0, i))
    return _mixer_out(_proj_res_t_kernel, a_t, spec, w_bf16, x, ln_g, w_router, tm, "attn_out_proj")


def _qkv_kernel(x_ref, g_ref, wq_ref, wk_ref, wv_ref, qt_ref, k_ref, vt_ref):
    xn = _rmsnorm(x_ref[...], g_ref[...]).astype(BF16)
    c = (HEAD_DIM ** -0.5) * math.log2(math.e)
    qt_ref[...] = (lax.dot_general(wq_ref[...], xn, _NT, preferred_element_type=F32) * c).astype(BF16)
    k_ref[...] = jnp.dot(xn, wk_ref[...], preferred_element_type=F32).astype(BF16)
    vt = lax.dot_general(wv_ref[...], xn, _NT, preferred_element_type=F32).astype(BF16)
    for c in range(vt_ref.shape[0]):
        vt_ref[c] = vt[:, c * MOBA_BLOCK:(c + 1) * MOBA_BLOCK]


def _qkv_proj(x, g, w_qkv, tm):
    n, d = x.shape
    da = ATT_HEADS * HEAD_DIM
    wq_t = w_qkv[:, :da].T.astype(BF16)
    wk = w_qkv[:, da:2 * da].astype(BF16)
    wv_t = w_qkv[:, 2 * da:].T.astype(BF16)
    full = lambda shp: pl.BlockSpec(shp, lambda i: (0, 0))
    tb = tm // MOBA_BLOCK
    return pl.pallas_call(
        _qkv_kernel,
        out_shape=(jax.ShapeDtypeStruct((da, n), BF16), jax.ShapeDtypeStruct((n, da), BF16),
                   jax.ShapeDtypeStruct((n // MOBA_BLOCK, da, MOBA_BLOCK), BF16)),
        grid=(n // tm,),
        in_specs=[pl.BlockSpec((tm, d), lambda i: (i, 0)), full((1, d)),
                  full((da, d)), full((d, da)), full((da, d))],
        out_specs=(pl.BlockSpec((da, tm), lambda i: (0, i)), pl.BlockSpec((tm, da), lambda i: (i, 0)),
                   pl.BlockSpec((tb, da, MOBA_BLOCK), lambda i: (i, 0, 0))),
        compiler_params=_cparams("parallel"),
        name="attn_qkv_proj",
    )(x, g.reshape(1, d), wq_t, wk, wv_t)


MOBA_CHAINS = 4
MOBA_SUM_ROWS = 16


def _moba_kernel(qt_ref, k_ref, vt_ref, et_ref, ot_ref, s_scr, acc_scr, m_scr, *, nb):
    blk, dh = MOBA_BLOCK, HEAD_DIM
    brow = lax.broadcasted_iota(jnp.int32, (nb, blk), 0).astype(F32)
    r_ix = lax.broadcasted_iota(jnp.int32, (blk, blk), 0)
    c_ix = lax.broadcasted_iota(jnp.int32, (blk, blk), 1)
    zpad = jnp.zeros((2 * LANES - dh - nb, blk), BF16)
    kmean = jnp.mean(k_ref[...].astype(F32).reshape(nb, blk, dh), axis=1)
    km_hi = kmean.astype(BF16)
    kmean2 = jnp.concatenate([km_hi, (kmean - km_hi.astype(F32)).astype(BF16)], axis=0)

    def k_aug(t):
        rows = pl.ds(pl.multiple_of(t * blk, blk), blk)
        return jnp.concatenate([k_ref[rows, :], et_ref[rows, :]], axis=1)

    ones = jnp.ones((MOBA_SUM_ROWS, blk), BF16)

    def values(t):
        return jnp.concatenate([vt_ref[t], ones], axis=0)

    def absorb(c, s, vt):
        m_old = m_scr[c]
        m_new = jnp.maximum(m_old, jnp.max(s, axis=0, keepdims=True))
        p = jnp.exp2(s - m_new).astype(BF16)
        acc_scr[c] = jnp.exp2(m_old - m_new) * acc_scr[c] + jnp.dot(vt, p, preferred_element_type=F32)
        m_scr[c] = m_new

    def query_block(i):
        qs = slice(i * blk, (i + 1) * blk)
        qt = qt_ref[:, qs]
        if i > MOBA_TOPK:
            g2 = jnp.dot(kmean2, qt, preferred_element_type=F32)
            gate = g2[:nb] + g2[nb:]
            gate = jnp.where(brow < i, gate, NEG_INF)
            sel = brow >= i
            for _ in range(MOBA_TOPK):
                gm = jnp.max(gate, axis=0, keepdims=True)
                first = jnp.min(jnp.where(gate == gm, brow, float(nb)), axis=0, keepdims=True)
                pick = brow == first
                sel = jnp.logical_or(sel, pick)
                gate = jnp.where(pick, -jnp.inf, gate)
            bias = jnp.where(sel, 0.0, NEG_INF).astype(BF16)
        else:
            bias = jnp.zeros((nb, blk), BF16)
        return jnp.concatenate([qt, bias, zpad], axis=0)

    nq = MOBA_CHAINS
    for i0 in range(0, nb, nq):
        chains = range(min(nq, nb - i0))
        q_aug = [query_block(i0 + c) for c in chains]
        m_scr[...] = jnp.full(m_scr.shape, NEG_INF, F32)
        acc_scr[...] = jnp.zeros(acc_scr.shape, F32)

        def scores(t, slot, cs):
            ka = k_aug(t)
            for c in cs:
                s_scr[c, slot] = jnp.dot(ka, q_aug[c], preferred_element_type=F32)

        scores(0, 0, chains)
        if i0 > 0:
            def body(j, carry):
                cur = [s_scr[c, j % 2] for c in chains]
                scores(j + 1, (j + 1) % 2, chains)
                vt = values(j)
                for c in chains:
                    absorb(c, cur[c], vt)
                return carry

            lax.fori_loop(0, i0, body, 0)
        for t in chains:
            cur = {c: s_scr[c, (i0 + t) % 2] for c in chains if c >= t}
            later = [c for c in chains if c > t]
            if later:
                scores(i0 + t + 1, (i0 + t + 1) % 2, later)
            vt = values(i0 + t)
            absorb(t, jnp.where(r_ix <= c_ix, cur[t], NEG_INF), vt)
            for c in later:
                absorb(c, cur[c], vt)
        for c in chains:
            qs = slice((i0 + c) * blk, (i0 + c + 1) * blk)
            ot_ref[:, qs] = (acc_scr[c, :dh, :] / acc_scr[c, dh:dh + 1, :]).astype(ot_ref.dtype)


def _moba(q_t, k, v_t, bsz, seq):
    nb = seq // MOBA_BLOCK
    da = ATT_HEADS * HEAD_DIM
    e_t = ((jnp.arange(seq) // MOBA_BLOCK)[:, None] == jnp.arange(LANES)[None, :]).astype(BF16)
    tspec = pl.BlockSpec((HEAD_DIM, seq), lambda b, h: (h, b))
    return pl.pallas_call(
        functools.partial(_moba_kernel, nb=nb),
        out_shape=jax.ShapeDtypeStruct((da, bsz * seq), BF16),
        grid=(bsz, ATT_HEADS),
        in_specs=[tspec, pl.BlockSpec((seq, HEAD_DIM), lambda b, h: (b, h)),
                  pl.BlockSpec((nb, HEAD_DIM, MOBA_BLOCK), lambda b, h: (b, h, 0)),
                  pl.BlockSpec((seq, LANES), lambda b, h: (0, 0))],
        out_specs=tspec,
        scratch_shapes=[pltpu.VMEM((MOBA_CHAINS, 2, MOBA_BLOCK, MOBA_BLOCK), F32),
                        pltpu.VMEM((MOBA_CHAINS, HEAD_DIM + MOBA_SUM_ROWS, MOBA_BLOCK), F32),
                        pltpu.VMEM((MOBA_CHAINS, 1, MOBA_BLOCK), F32)],
        compiler_params=_cparams("parallel", "parallel"),
        name="moba_attn",
    )(q_t, k, v_t, e_t)


def _router_weights(w_group, w_expert):
    d = w_group.shape[0]
    w_r = jnp.zeros((d, LANES), F32).at[:, :N_GROUPS].set(w_group)
    w_r = w_r.at[:, N_GROUPS:N_GROUPS + N_EXPERTS].set(w_expert)
    w_hi = w_r.astype(BF16)
    w_lo = (w_r - w_hi.astype(F32)).astype(BF16)
    return jnp.concatenate([w_hi, w_lo], axis=1)


def _route(h, g_ref, w_ref, tri_ref, meta_ref, idx_ref, cnt_ref):
    xn = _rmsnorm(h, g_ref[...])
    x_hi = xn.astype(BF16)
    x_lo = (xn - x_hi.astype(F32)).astype(BF16)
    t = jnp.dot(x_hi, w_ref[...], preferred_element_type=F32)
    logits = (t[:, :LANES] + t[:, LANES:]) + jnp.dot(x_lo, w_ref[:, :LANES], preferred_element_type=F32)
    tm = logits.shape[0]
    lane = lax.broadcasted_iota(jnp.int32, (tm, LANES), 1).astype(F32)
    ninf = -jnp.inf
    lg = jnp.where(lane < N_GROUPS, logits, ninf)
    gm = jnp.max(lg, axis=-1, keepdims=True)
    g_idx = jnp.min(jnp.where(lg == gm, lane, float(LANES)), axis=-1, keepdims=True)
    g_gate = 1.0 / jnp.sum(jnp.exp(lg - gm), axis=-1, keepdims=True)
    lo = N_GROUPS + EXPERTS_PER_GROUP * g_idx
    le = jnp.where((lane >= lo) & (lane < lo + EXPERTS_PER_GROUP), logits, ninf)
    m1 = jnp.max(le, axis=-1, keepdims=True)
    i1 = jnp.min(jnp.where(le == m1, lane, float(LANES)), axis=-1, keepdims=True)
    le2 = jnp.where(lane == i1, ninf, le)
    m2 = jnp.max(le2, axis=-1, keepdims=True)
    i2 = jnp.min(jnp.where(le2 == m2, lane, float(LANES)), axis=-1, keepdims=True)
    p2 = jnp.exp(m2 - m1)
    gate1 = g_gate / (1.0 + p2)
    gate2 = g_gate * p2 / (1.0 + p2)
    e1 = i1 - N_GROUPS
    e2 = i2 - N_GROUPS
    meta = jnp.where(lane == 0, e1, jnp.where(lane == 1, e2, jnp.where(lane == 2, gate1,
                     jnp.where(lane == 3, gate2, 0.0))))
    meta_ref[...] = meta
    oh1 = (lane == e1).astype(F32)
    oh2 = (lane == e2).astype(F32)
    both = oh1 + oh2

    @pl.when(pl.program_id(0) == 0)
    def _():
        cnt_ref[...] = jnp.zeros_like(cnt_ref)

    seen = cnt_ref[...]
    before = jnp.dot(tri_ref[...], both.astype(BF16), preferred_element_type=F32) + seen
    r1 = jnp.sum(before * oh1, axis=-1, keepdims=True)
    r2 = jnp.sum(before * oh2, axis=-1, keepdims=True)
    slab = jnp.where(lane == 0, e1, jnp.where(lane == 1, e2, jnp.where(lane == 2, r1,
                     jnp.where(lane == 3, r2, 0.0))))
    idx_ref[...] = jnp.transpose(slab)[0:8, :].astype(jnp.int32)
    cnt_ref[...] = seen + jnp.sum(both, axis=0, keepdims=True)


MOE_ISSUE_UNROLL = 8
ROW_TILE = 8


def _row_tile(off):
    return pl.ds(pl.multiple_of(off, ROW_TILE), ROW_TILE)


def _store_row_tiles(ref, x):
    rows = x.shape[0]
    for c in range(ROW_TILE):
        ref[pl.ds(c, rows, stride=ROW_TILE), :] = x[:, c * LANES:(c + 1) * LANES]


def _load_row_tiles(ref, rows):
    return jnp.concatenate([ref[pl.ds(c, rows, stride=ROW_TILE), :] for c in range(ROW_TILE)], axis=1)


def _dispatch_kernel(d1_ref, d2_ref, plo_ref, pn_ref, nu_ref, h_ref, g_ref, xs_ref,
                     buf_ref, zero_ref, sem, fill_sem):
    i = pl.program_id(0)
    nsteps = pl.num_programs(0)
    tm = h_ref.shape[0]
    rt = ROW_TILE
    rb = zero_ref.shape[0] // rt
    n_blk = xs_ref.shape[0] // (rb * rt)
    slot = i % 2
    base = i * tm

    def slot_drain(s):
        cp = pltpu.make_async_copy(buf_ref.at[s], xs_ref.at[pl.ds(0, tm * rt), :], sem.at[s])
        cp.wait()
        cp.wait()

    @pl.when(i >= 2)
    def _():
        slot_drain(slot)

    _store_row_tiles(buf_ref.at[slot], _rmsnorm(h_ref[...], g_ref[...]))

    def issue(r8, c):
        for k in range(MOE_ISSUE_UNROLL):
            r = r8 * MOE_ISSUE_UNROLL + k
            src = buf_ref.at[slot, _row_tile(r * rt), :]
            pltpu.make_async_copy(src, xs_ref.at[_row_tile(d1_ref[base + r]), :], sem.at[slot]).start(priority=0)
            pltpu.make_async_copy(src, xs_ref.at[_row_tile(d2_ref[base + r]), :], sem.at[slot]).start(priority=1)
        return c

    lax.fori_loop(0, tm // MOE_ISSUE_UNROLL, issue, 0)

    def pad_copy(off, rows):
        return pltpu.make_async_copy(zero_ref.at[pl.ds(0, rows * rt), :],
                                     xs_ref.at[pl.ds(pl.multiple_of(off, rt), rows * rt), :], fill_sem)

    pad_sizes = [1 << b for b in reversed(range(rb.bit_length() - 1))]

    def blk_copy(b):
        return pltpu.make_async_copy(zero_ref, xs_ref.at[pl.ds(pl.multiple_of(b * (rb * rt), rt), rb * rt), :],
                                     fill_sem)

    @pl.when(i == 0)
    def _():
        zero_ref[...] = jnp.zeros_like(zero_ref)

    @pl.when(i < N_EXPERTS)
    def _():
        off = plo_ref[i]
        npad = pn_ref[i]
        for rows in pad_sizes:
            has = (npad & rows) != 0

            @pl.when(has)
            def _(off=off, rows=rows):
                pad_copy(off, rows).start()

            off = off + jnp.where(has, rows * rt, 0)

    @pl.when(i == N_EXPERTS)
    def _():
        def fill(b, c):
            blk_copy(b).start()
            return c

        lax.fori_loop(nu_ref[0], n_blk, fill, 0)

    @pl.when(i == nsteps - 1)
    def _():
        slot_drain(1 - slot)
        slot_drain(slot)

        def per_expert(e, c):
            npad = pn_ref[e]
            for rows in pad_sizes:
                @pl.when((npad & rows) != 0)
                def _(rows=rows):
                    pad_copy(0, rows).wait()

            return c

        lax.fori_loop(0, N_EXPERTS, per_expert, 0)

        def one_blk(b, c):
            blk_copy(0).wait()
            return c

        lax.fori_loop(nu_ref[0], n_blk, one_blk, 0)


def _dispatch(d1, d2, pad_lo, pad_n, n_used, h, g, cap, tm):
    n, d = h.shape
    assert n // tm > N_EXPERTS + 1 and d == ROW_TILE * LANES
    return pl.pallas_call(
        _dispatch_kernel,
        out_shape=jax.ShapeDtypeStruct((cap * ROW_TILE, LANES), F32),
        grid_spec=pltpu.PrefetchScalarGridSpec(
            num_scalar_prefetch=5,
            grid=(n // tm,),
            in_specs=[
                pl.BlockSpec((tm, d), lambda i, *_: (i, 0)),
                pl.BlockSpec((1, d), lambda i, *_: (0, 0)),
            ],
            out_specs=pl.BlockSpec(memory_space=pl.ANY),
            scratch_shapes=[pltpu.VMEM((2, tm * ROW_TILE, LANES), F32),
                            pltpu.VMEM((MOE_ROWS * ROW_TILE, LANES), F32),
                            pltpu.SemaphoreType.DMA((2,)), pltpu.SemaphoreType.DMA],
        ),
        compiler_params=_cparams("arbitrary"),
        name="moe_dispatch",
    )(d1, d2, pad_lo, pad_n, n_used, h, g.reshape(1, d))


def _expert_kernel(be_ref, nu_ref, nx_ref, x_ref, w1_hbm, w3_hbm, w2_hbm, y_ref,
                   wb1, wb3, wb2, w1c, w3c, w2c, wsem, wslot, *, layer):
    i = pl.program_id(0)
    used = i < nu_ref[0]
    e = be_ref[i]
    prev = be_ref[jnp.maximum(i - 1, 0)]
    fresh = jnp.logical_or(i == 0, e != prev)

    def weight_copies(ex, s):
        return [pltpu.make_async_copy(w1_hbm.at[layer, ex], wb1.at[s], wsem.at[s]),
                pltpu.make_async_copy(w3_hbm.at[layer, ex], wb3.at[s], wsem.at[s]),
                pltpu.make_async_copy(w2_hbm.at[layer, ex], wb2.at[s], wsem.at[s])]

    @pl.when(i == 0)
    def _():
        wslot[0] = 0
        for cp in weight_copies(e, 0):
            cp.start()

    @pl.when(jnp.logical_and(used, fresh))
    def _():
        s = wslot[0]
        for cp in weight_copies(e, s):
            cp.wait()
        nxt = nx_ref[e]

        @pl.when(nxt != e)
        def _():
            for cp in weight_copies(nxt, 1 - s):
                cp.start()

        w1c[...] = wb1[s].astype(BF16)
        w3c[...] = wb3[s].astype(BF16)
        w2c[...] = wb2[s].astype(BF16)
        wslot[0] = 1 - s

    @pl.when(used)
    def _():
        rb = x_ref.shape[0] // ROW_TILE
        x = _load_row_tiles(x_ref, rb).astype(BF16)
        hf = w1c.shape[1] // 2
        y = None
        for c in range(2):
            cs = slice(c * hf, (c + 1) * hf)
            a = jnp.dot(x, w1c[:, cs], preferred_element_type=F32)
            b = jnp.dot(x, w3c[:, cs], preferred_element_type=F32)
            act = (jax.nn.silu(a) * b).astype(BF16)
            part = jnp.dot(act, w2c[cs, :], preferred_element_type=F32)
            y = part if y is None else y + part
        _store_row_tiles(y_ref, y)

    @pl.when(jnp.logical_not(used))
    def _():
        y_ref[...] = jnp.zeros_like(y_ref)


def _experts(blk_e, n_used, next_e, xs, w1, w3, w2, layer):
    d = ROW_TILE * LANES
    rb = MOE_ROWS
    n_blk = xs.shape[0] // (rb * ROW_TILE)
    ff = w1.shape[3]
    blk = (rb * ROW_TILE, LANES)
    hbm = pl.BlockSpec(memory_space=pl.ANY)
    return pl.pallas_call(
        functools.partial(_expert_kernel, layer=layer),
        out_shape=jax.ShapeDtypeStruct(xs.shape, F32),
        grid_spec=pltpu.PrefetchScalarGridSpec(
            num_scalar_prefetch=3,
            grid=(n_blk,),
            in_specs=[
                pl.BlockSpec(blk, lambda i, be, nu, nx: (jnp.minimum(i, jnp.maximum(nu[0] - 1, 0)), 0)),
                hbm, hbm, hbm,
            ],
            out_specs=pl.BlockSpec(blk, lambda i, be, nu, nx: (i, 0)),
            scratch_shapes=[pltpu.VMEM((2, d, ff), F32), pltpu.VMEM((2, d, ff), F32), pltpu.VMEM((2, ff, d), F32),
                            pltpu.VMEM((d, ff), BF16), pltpu.VMEM((d, ff), BF16), pltpu.VMEM((ff, d), BF16),
                            pltpu.SemaphoreType.DMA((2,)), pltpu.SMEM((1,), jnp.int32)],
        ),
        compiler_params=_cparams("arbitrary"),
        name="moe_experts",
    )(blk_e, n_used, next_e, xs, w1, w3, w2)


def _combine_kernel(d1_ref, d2_ref, h_ref, meta_ref, g_ref, ys_ref, o_ref, buf_ref, sem, *, final_norm):
    i = pl.program_id(0)
    nsteps = pl.num_programs(0)
    tm = h_ref.shape[0]
    slot = i % 2

    def fetch(tile, s):
        base = tile * tm

        def issue(r8, c):
            for k in range(MOE_ISSUE_UNROLL):
                r = r8 * MOE_ISSUE_UNROLL + k
                pltpu.make_async_copy(ys_ref.at[_row_tile(d1_ref[base + r]), :],
                                      buf_ref.at[s, 0, _row_tile(r * ROW_TILE), :], sem.at[s]).start(priority=0)
                pltpu.make_async_copy(ys_ref.at[_row_tile(d2_ref[base + r]), :],
                                      buf_ref.at[s, 1, _row_tile(r * ROW_TILE), :], sem.at[s]).start(priority=1)
            return c

        lax.fori_loop(0, tm // MOE_ISSUE_UNROLL, issue, 0)

    @pl.when(i == 0)
    def _():
        fetch(0, 0)

    @pl.when(i + 1 < nsteps)
    def _():
        fetch(i + 1, 1 - slot)

    for j in range(2):
        pltpu.make_async_copy(ys_ref.at[pl.ds(0, tm * ROW_TILE), :], buf_ref.at[slot, j], sem.at[slot]).wait()

    meta = meta_ref[...]
    y1 = _load_row_tiles(buf_ref.at[slot, 0], tm)
    y2 = _load_row_tiles(buf_ref.at[slot, 1], tm)
    out = h_ref[...] + (meta[:, 2:3] * y1 + meta[:, 3:4] * y2)
    if final_norm:
        out = _rmsnorm(out, g_ref[...])
    o_ref[...] = out


def _combine(d1, d2, h, meta, ys, g_final, tm, final_norm):
    n, d = h.shape
    return pl.pallas_call(
        functools.partial(_combine_kernel, final_norm=final_norm),
        out_shape=jax.ShapeDtypeStruct((n, d), F32),
        grid_spec=pltpu.PrefetchScalarGridSpec(
            num_scalar_prefetch=2,
            grid=(n // tm,),
            in_specs=[
                pl.BlockSpec((tm, d), lambda i, *_: (i, 0)),
                pl.BlockSpec((tm, LANES), lambda i, *_: (i, 0)),
                pl.BlockSpec((1, d), lambda i, *_: (0, 0)),
                pl.BlockSpec(memory_space=pl.ANY),
            ],
            out_specs=pl.BlockSpec((tm, d), lambda i, *_: (i, 0)),
            scratch_shapes=[pltpu.VMEM((2, 2, tm * ROW_TILE, LANES), F32), pltpu.SemaphoreType.DMA((2,))],
        ),
        compiler_params=_cparams("arbitrary"),
        name="moe_combine",
    )(d1, d2, h, meta, g_final.reshape(1, d), ys)


def _hier_moe(h, routing, ln_g, w1, w3, w2, layer, g_final, final_norm):
    n, d = h.shape
    rb = MOE_ROWS
    cap = 2 * n + N_EXPERTS * rb
    n_blk = cap // rb
    meta, idx, counts = routing
    cnt = counts[0, :N_EXPERTS].astype(jnp.int32)
    padded = ((cnt + rb - 1) // rb) * rb
    pad_end = jnp.cumsum(padded)
    pad_start = pad_end - padded
    blk_start = jnp.arange(n_blk, dtype=jnp.int32) * rb
    blk_e = jnp.minimum(jnp.sum(pad_end[None, :] <= blk_start[:, None], axis=1), N_EXPERTS - 1).astype(jnp.int32)
    n_used = (pad_end[-1] // rb).astype(jnp.int32).reshape(1)
    experts = jnp.arange(N_EXPERTS, dtype=jnp.int32)[None, :]
    start_of = lambda e: jnp.sum(jnp.where(e[:, None] == experts, pad_start[None, :], 0), axis=1)
    d1 = (start_of(idx[0]) + idx[2]) * ROW_TILE
    d2 = (start_of(idx[1]) + idx[3]) * ROW_TILE
    xs = _dispatch(d1, d2, (pad_start + cnt) * ROW_TILE, padded - cnt, n_used, h, ln_g, cap, MOE_DISPATCH_ROWS)
    later = (experts > experts.T) & (padded > 0)[None, :]
    next_e = jnp.min(jnp.where(later, experts, N_EXPERTS), axis=1)
    next_e = jnp.where(next_e == N_EXPERTS, experts[0], next_e).astype(jnp.int32)
    ys = _experts(blk_e, n_used, next_e, xs, w1, w3, w2, layer)
    return _combine(d1, d2, h, meta, ys, g_final, MOE_COMBINE_ROWS, final_norm)


def kernel(x, ln_mix, ln_ffn, ln_final, ssm_w_in, ssm_lam_re, ssm_lam_im, ssm_log_dt, ssm_b_re, ssm_b_im, ssm_c_re, ssm_c_im, ssm_d, ssm_w_out, attn_w_qkv, attn_w_o, moe_w_group, moe_w_expert, moe_w1, moe_w3, moe_w2):
    bsz, seq, d = x.shape
    n = bsz * seq
    h = x.reshape(n, d)

    u = _norm_proj(h, ln_mix[0], ssm_w_in[0].astype(BF16), DENSE_ROWS, F32, "s5_in_proj")
    mats = _s5_mats(ssm_lam_re[0], ssm_lam_im[0], ssm_log_dt[0], ssm_b_re[0], ssm_b_im[0],
                    ssm_c_re[0], ssm_c_im[0])
    z = _s5_core(u, mats, ssm_d[0], bsz, seq)
    h, routing = _glu_out(z, ssm_w_out[0].astype(BF16), h, ln_ffn[0],
                          _router_weights(moe_w_group[0], moe_w_expert[0]), MIXER_OUT_ROWS)
    h = _hier_moe(h, routing, ln_ffn[0], moe_w1, moe_w3, moe_w2, 0, ln_final, False)

    q_t, k, v_t = _qkv_proj(h, ln_mix[1], attn_w_qkv[0], DENSE_ROWS)
    o_t = _moba(q_t, k, v_t, bsz, seq)
    h, routing = _proj_res_t(o_t, attn_w_o[0].astype(BF16), h, ln_ffn[1],
                             _router_weights(moe_w_group[1], moe_w_expert[1]), MIXER_OUT_ROWS)
    h = _hier_moe(h, routing, ln_ffn[1], moe_w1, moe_w3, moe_w2, 1, ln_final, True)
    return h.reshape(bsz, seq, d)
```

```python
import functools
import math

import jax
import jax.numpy as jnp
from jax import lax
from jax.experimental import pallas as pl
from jax.experimental.pallas import tpu as pltpu

F32 = jnp.float32
BF16 = jnp.bfloat16

D_MODEL = 1024
RMS_EPS = 1e-6
NEG_INF = -1e30

SSM_GROUP = 16
SSM_GROUPS = D_MODEL // SSM_GROUP
SSM_STATE = 64
SSM_CHUNK = 16
SSM_GB = 8
SSM_WIN = 8
SSM_POW_ROWS = 24

ATT_HEADS = 8
HEAD_DIM = 128
MOBA_BLOCK = 256
MOBA_TOPK = 3

N_GROUPS = 4
EXPERTS_PER_GROUP = 8
N_EXPERTS = 32
EXPERT_FF = 512
MOE_ROWS = 512
MOE_DISPATCH_ROWS = 256
MOE_COMBINE_ROWS = 256
DENSE_ROWS = 1024
MIXER_OUT_ROWS = 512

LANES = 128
VMEM_LIMIT = 56 * 1024 * 1024

_NT = (((1,), (1,)), ((), ()))
_TN = (((0,), (0,)), ((), ()))


def _cparams(*sem):
    return pltpu.CompilerParams(dimension_semantics=sem, vmem_limit_bytes=VMEM_LIMIT)


def _rmsnorm(x, g):
    return x * lax.rsqrt(jnp.mean(x * x, axis=-1, keepdims=True) + RMS_EPS) * g


def _norm_proj_kernel(x_ref, g_ref, w_ref, o_ref):
    xn = _rmsnorm(x_ref[...], g_ref[...]).astype(BF16)
    o_ref[...] = jnp.dot(xn, w_ref[...], preferred_element_type=F32).astype(o_ref.dtype)


def _norm_proj(x, g, w_bf16, tm, out_dtype, name):
    n, d = x.shape
    dout = w_bf16.shape[1]
    return pl.pallas_call(
        _norm_proj_kernel,
        out_shape=jax.ShapeDtypeStruct((n, dout), out_dtype),
        grid=(n // tm,),
        in_specs=[
            pl.BlockSpec((tm, d), lambda i: (i, 0)),
            pl.BlockSpec((1, d), lambda i: (0, 0)),
            pl.BlockSpec((d, dout), lambda i: (0, 0)),
        ],
        out_specs=pl.BlockSpec((tm, dout), lambda i: (i, 0)),
        compiler_params=_cparams("parallel"),
        name=name,
    )(x, g.reshape(1, d), w_bf16)


def _s5_mats(lam_re, lam_im, log_dt, b_re, b_im, c_re, c_im):
    g_, p_, c_, t_ = SSM_GROUPS, SSM_STATE, SSM_GROUP, SSM_CHUNK
    lr = jnp.minimum(lam_re, -1e-4)
    li = lam_im
    dt = jnp.exp(log_dt)[:, None]
    mag = jnp.exp(lr * dt)
    abar_re = mag * jnp.cos(li * dt)
    abar_im = mag * jnp.sin(li * dt)
    den = lr * lr + li * li
    nr = abar_re - 1.0
    gam_re = (nr * lr + abar_im * li) / den
    gam_im = (abar_im * lr - nr * li) / den
    bb_re = gam_re[..., None] * b_re - gam_im[..., None] * b_im
    bb_im = gam_re[..., None] * b_im + gam_im[..., None] * b_re

    def powers(ns):
        nf = jnp.asarray(ns, F32)[None, :, None]
        pm = jnp.exp(nf * (lr * dt)[:, None, :])
        ang = nf * (li * dt)[:, None, :]
        return pm * jnp.cos(ang), pm * jnp.sin(ang)

    pr, pi = powers(list(range(t_ + 1)))
    pad = jnp.zeros((g_, SSM_POW_ROWS - (t_ + 1), 2 * p_), F32)
    pw = jnp.stack([jnp.concatenate([jnp.concatenate([pr, pr], -1), pad], axis=1),
                    jnp.concatenate([jnp.concatenate([pi, pi], -1), pad], axis=1)], axis=1)
    bt_re = bb_re.transpose(0, 2, 1)
    bt_im = bb_im.transpose(0, 2, 1)
    cat = lambda a, b: jnp.concatenate([a, b], axis=-1)
    fac = jnp.stack([cat(bt_re, bt_im), cat(-bt_im, bt_re), cat(bt_re, -bt_im),
                     cat(c_re, c_im), cat(-c_im, c_re)], axis=1)
    qr, qi = powers([t_ * (1 << j) for j in range(4)])
    rows = []
    for j in range(4):
        ar, ai = qr[:, j], qi[:, j]
        rows += [jnp.concatenate([ar, ar], -1), jnp.concatenate([-ai, ai], -1),
                 jnp.concatenate([ai, -ai], -1)]
    rows += [jnp.zeros_like(rows[0])] * 4
    coef = jnp.stack(rows, axis=1)
    return fac.astype(F32), pw.astype(F32), coef.astype(F32)


def _s5_perm():
    r = jnp.arange(8 * LANES)
    col = ((r % LANES) // SSM_GROUP) * LANES + (r // LANES) * SSM_GROUP + r % SSM_GROUP
    p = (col[:, None] == r[None, :]).astype(BF16)
    return p, p.T


def _s5_kernel(u_ref, fac_ref, pw_ref, coef_ref, d_ref, p_ref, pt_ref, o_ref,
               m_ref, w_ref, v_ref, vf_ref, zf_ref, ea_ref, eb_ref, sp_ref):
    t_, c_, gb, win = SSM_CHUNK, SSM_GROUP, SSM_GB, SSM_WIN
    nk = u_ref.shape[0] // t_
    p2 = 2 * SSM_STATE
    tc = t_ * c_

    @pl.when(pl.program_id(1) == 0)
    def _():
        lane_tc = lax.broadcasted_iota(jnp.int32, (c_, tc), 1)
        conj = jnp.where(lax.broadcasted_iota(jnp.int32, (1, p2), 1) < SSM_STATE, 1.0, -1.0)
        for g in range(gb):
            b_t, ib_t, bconj_t, cm, icm = (fac_ref[g, k] for k in range(5))
            pr, pi = pw_ref[g, 0], pw_ref[g, 1]
            for s in range(t_):
                n = t_ - 1 - s
                w_ref[g, s * c_:(s + 1) * c_, :] = (b_t * pr[n:n + 1] + ib_t * pi[n:n + 1]).astype(BF16)
            ca = jnp.concatenate([cm * pr[n:n + 1] + icm * pi[n:n + 1] for n in range(t_ + 1)], axis=0)
            v_ref[g] = jnp.transpose(ca[c_:] * conj).astype(BF16)
            kf = lax.dot_general(bconj_t, ca[:tc], _NT, preferred_element_type=F32,
                                 precision=lax.Precision.HIGHEST)
            for s in range(t_):
                rolled = kf if s == 0 else pltpu.roll(kf, s * c_, axis=1)
                m_ref[g, s * c_:(s + 1) * c_, :] = jnp.where(lane_tc >= s * c_, rolled, 0.0).astype(BF16)

    halves = tc // LANES

    def timestep(t):
        return pl.ds(t, nk, stride=t_)

    def flat(g):
        return jnp.concatenate([vf_ref[h, :, g * LANES:(g + 1) * LANES] for h in range(halves)], axis=1)

    for h in range(halves):
        x = jnp.concatenate([u_ref[timestep(8 * h + i), :].astype(BF16) for i in range(8)], axis=1)
        vf_ref[h] = jnp.dot(x, p_ref[...], preferred_element_type=F32).astype(BF16)

    row = lax.broadcasted_iota(jnp.int32, (nk, p2), 0)

    def shift(x, d):
        return jnp.where(row < d, 0.0, pltpu.roll(x, d, axis=0))

    def swap(x):
        return pltpu.roll(x, SSM_STATE, axis=1)

    for g in range(gb):
        sc = jnp.dot(flat(g), w_ref[g], preferred_element_type=F32)
        xa = shift(sc, 1)
        cf = coef_ref[g]
        for lvl in range(3):
            pp, qa = cf[3 * lvl:3 * lvl + 1], cf[3 * lvl + 1:3 * lvl + 2]
            sa = shift(xa, 1 << lvl)
            xa = xa + pp * sa + qa * swap(sa)
        ea_ref[g] = xa
        eb_ref[g] = swap(xa)

    cfs = [coef_ref[g] for g in range(gb)]
    zero = jnp.zeros((win, p2), F32)
    state = [(zero, zero)] * gb
    for j in range(nk // win):
        rs = slice(j * win, (j + 1) * win)
        for g in range(gb):
            pp, qa, qb = cfs[g][9:10], cfs[g][10:11], cfs[g][11:12]
            s_a, s_b = state[g]
            n_a = pp * s_a + qa * s_b + ea_ref[g, rs, :]
            n_b = pp * s_b + qb * s_a + eb_ref[g, rs, :]
            sp_ref[g, rs, :] = n_a
            state[g] = (n_a, n_b)

    for g in range(gb):
        x = flat(g)
        y = jnp.dot(x, m_ref[g], preferred_element_type=F32)
        y = y + jnp.dot(sp_ref[g].astype(BF16), v_ref[g], preferred_element_type=F32)
        z = jax.nn.gelu(y + d_ref[g] * x.astype(F32)).astype(BF16)
        for h in range(halves):
            zf_ref[h, :, g * LANES:(g + 1) * LANES] = z[:, h * LANES:(h + 1) * LANES]

    for h in range(halves):
        zn = jnp.dot(zf_ref[h], pt_ref[...], preferred_element_type=F32)
        for i in range(8):
            o_ref[timestep(8 * h + i), :] = zn[:, i * LANES:(i + 1) * LANES]


def _s5_core(u, mats, d_skip, bsz, seq):
    fac, pw, coef = mats
    n, d = u.shape
    gb, tc, p2 = SSM_GB, SSM_CHUNK * SSM_GROUP, 2 * SSM_STATE
    nk = seq // SSM_CHUNK
    halves = tc // LANES
    perm, perm_t = _s5_perm()
    dflat = jnp.tile(d_skip.reshape(SSM_GROUPS, 1, SSM_GROUP), (1, 1, SSM_CHUNK))
    spec3 = lambda a, b: pl.BlockSpec((gb, a, b), lambda j, bb: (j, 0, 0))
    const = pl.BlockSpec(perm.shape, lambda j, bb: (0, 0))
    return pl.pallas_call(
        _s5_kernel,
        out_shape=jax.ShapeDtypeStruct((n, d), F32),
        grid=(SSM_GROUPS // gb, bsz),
        in_specs=[
            pl.BlockSpec((seq, LANES), lambda j, bb: (bb, j)),
            pl.BlockSpec((gb,) + fac.shape[1:], lambda j, bb: (j, 0, 0, 0)),
            pl.BlockSpec((gb,) + pw.shape[1:], lambda j, bb: (j, 0, 0, 0)),
            spec3(16, p2), spec3(1, tc),
            const, const,
        ],
        out_specs=pl.BlockSpec((seq, LANES), lambda j, bb: (bb, j)),
        scratch_shapes=[pltpu.VMEM((gb, tc, tc), BF16), pltpu.VMEM((gb, tc, p2), BF16),
                        pltpu.VMEM((gb, p2, tc), BF16),
                        pltpu.VMEM((halves, nk, gb * LANES), BF16),
                        pltpu.VMEM((halves, nk, gb * LANES), BF16),
                        pltpu.VMEM((gb, nk, p2), F32), pltpu.VMEM((gb, nk, p2), F32),
                        pltpu.VMEM((gb, nk, p2), F32)],
        compiler_params=_cparams("arbitrary", "arbitrary"),
        name="s5_core",
    )(u, fac, pw, coef, dflat, perm, perm_t)


def _glu_out_kernel(z_ref, w_ref, x_ref, g_ref, wr_ref, tri_ref, o_ref, meta_ref, idx_ref, cnt_ref):
    vg = jnp.dot(z_ref[...].astype(BF16), w_ref[...], preferred_element_type=F32)
    d = o_ref.shape[1]
    h = x_ref[...] + vg[:, :d] * jax.nn.sigmoid(vg[:, d:])
    o_ref[...] = h
    _route(h, g_ref, wr_ref, tri_ref, meta_ref, idx_ref, cnt_ref)


def _proj_res_t_kernel(at_ref, w_ref, x_ref, g_ref, wr_ref, tri_ref, o_ref, meta_ref, idx_ref, cnt_ref):
    h = x_ref[...] + lax.dot_general(at_ref[...], w_ref[...], _TN, preferred_element_type=F32)
    o_ref[...] = h
    _route(h, g_ref, wr_ref, tri_ref, meta_ref, idx_ref, cnt_ref)


def _mixer_out(kernel_fn, a, a_spec, w_bf16, x, ln_g, w_router, tm, name):
    n, d = x.shape
    tri = (jnp.arange(tm)[:, None] > jnp.arange(tm)[None, :]).astype(BF16)
    full = lambda arr: pl.BlockSpec(arr.shape, lambda i: (0, 0))
    h, meta, idx, counts = pl.pallas_call(
        kernel_fn,
        out_shape=(jax.ShapeDtypeStruct((n, d), F32), jax.ShapeDtypeStruct((n, LANES), F32),
                   jax.ShapeDtypeStruct((8, n), jnp.int32), jax.ShapeDtypeStruct((1, LANES), F32)),
        grid=(n // tm,),
        in_specs=[a_spec, full(w_bf16), pl.BlockSpec((tm, d), lambda i: (i, 0)),
                  pl.BlockSpec((1, d), lambda i: (0, 0)), full(w_router), full(tri)],
        out_specs=(pl.BlockSpec((tm, d), lambda i: (i, 0)), pl.BlockSpec((tm, LANES), lambda i: (i, 0)),
                   pl.BlockSpec((8, tm), lambda i: (0, i)), pl.BlockSpec((1, LANES), lambda i: (0, 0))),
        compiler_params=_cparams("arbitrary"),
        name=name,
    )(a, w_bf16, x, ln_g.reshape(1, d), w_router, tri)
    return h, (meta, idx, counts)


def _glu_out(z, w_bf16, x, ln_g, w_router, tm):
    spec = pl.BlockSpec((tm, z.shape[1]), lambda i: (i, 0))
    return _mixer_out(_glu_out_kernel, z, spec, w_bf16, x, ln_g, w_router, tm, "s5_glu_out")


def _proj_res_t(a_t, w_bf16, x, ln_g, w_router, tm):
    spec = pl.BlockSpec((a_t.shape[0], tm), lambda i: (0, i))
    return _mixer_out(_proj_res_t_kernel, a_t, spec, w_bf16, x, ln_g, w_router, tm, "attn_out_proj")


def _qkv_body(x, g_ref, wq_ref, wk_ref, wv_ref, qt_ref, k_ref, vt_ref):
    xn = _rmsnorm(x, g_ref[...]).astype(BF16)
    c = (HEAD_DIM ** -0.5) * math.log2(math.e)
    qt_ref[...] = (lax.dot_general(wq_ref[...], xn, _NT, preferred_element_type=F32) * c).astype(BF16)
    k_ref[...] = jnp.dot(xn, wk_ref[...], preferred_element_type=F32).astype(BF16)
    vt = lax.dot_general(wv_ref[...], xn, _NT, preferred_element_type=F32).astype(BF16)
    for c in range(vt_ref.shape[0]):
        vt_ref[c] = vt[:, c * MOBA_BLOCK:(c + 1) * MOBA_BLOCK]


MOBA_CHAINS = 4
MOBA_SUM_ROWS = 16


def _moba_kernel(qt_ref, k_ref, vt_ref, et_ref, ot_ref, s_scr, acc_scr, m_scr, *, nb):
    blk, dh = MOBA_BLOCK, HEAD_DIM
    brow = lax.broadcasted_iota(jnp.int32, (nb, blk), 0).astype(F32)
    r_ix = lax.broadcasted_iota(jnp.int32, (blk, blk), 0)
    c_ix = lax.broadcasted_iota(jnp.int32, (blk, blk), 1)
    zpad = jnp.zeros((2 * LANES - dh - nb, blk), BF16)
    kmean = jnp.mean(k_ref[...].astype(F32).reshape(nb, blk, dh), axis=1)
    km_hi = kmean.astype(BF16)
    kmean2 = jnp.concatenate([km_hi, (kmean - km_hi.astype(F32)).astype(BF16)], axis=0)

    def k_aug(t):
        rows = pl.ds(pl.multiple_of(t * blk, blk), blk)
        return jnp.concatenate([k_ref[rows, :], et_ref[rows, :]], axis=1)

    ones = jnp.ones((MOBA_SUM_ROWS, blk), BF16)

    def values(t):
        return jnp.concatenate([vt_ref[t], ones], axis=0)

    def absorb(c, s, vt):
        m_old = m_scr[c]
        m_new = jnp.maximum(m_old, jnp.max(s, axis=0, keepdims=True))
        p = jnp.exp2(s - m_new).astype(BF16)
        acc_scr[c] = jnp.exp2(m_old - m_new) * acc_scr[c] + jnp.dot(vt, p, preferred_element_type=F32)
        m_scr[c] = m_new

    def query_block(i):
        qs = slice(i * blk, (i + 1) * blk)
        qt = qt_ref[:, qs]
        if i > MOBA_TOPK:
            g2 = jnp.dot(kmean2, qt, preferred_element_type=F32)
            gate = g2[:nb] + g2[nb:]
            gate = jnp.where(brow < i, gate, NEG_INF)
            sel = brow >= i
            for _ in range(MOBA_TOPK):
                gm = jnp.max(gate, axis=0, keepdims=True)
                first = jnp.min(jnp.where(gate == gm, brow, float(nb)), axis=0, keepdims=True)
                pick = brow == first
                sel = jnp.logical_or(sel, pick)
                gate = jnp.where(pick, -jnp.inf, gate)
            bias = jnp.where(sel, 0.0, NEG_INF).astype(BF16)
        else:
            bias = jnp.zeros((nb, blk), BF16)
        return jnp.concatenate([qt, bias, zpad], axis=0)

    nq = MOBA_CHAINS
    for i0 in range(0, nb, nq):
        chains = range(min(nq, nb - i0))
        q_aug = [query_block(i0 + c) for c in chains]
        m_scr[...] = jnp.full(m_scr.shape, NEG_INF, F32)
        acc_scr[...] = jnp.zeros(acc_scr.shape, F32)

        def scores(t, slot, cs):
            ka = k_aug(t)
            for c in cs:
                s_scr[c, slot] = jnp.dot(ka, q_aug[c], preferred_element_type=F32)

        scores(0, 0, chains)
        if i0 > 0:
            def body(j, carry):
                cur = [s_scr[c, j % 2] for c in chains]
                scores(j + 1, (j + 1) % 2, chains)
                vt = values(j)
                for c in chains:
                    absorb(c, cur[c], vt)
                return carry

            lax.fori_loop(0, i0, body, 0)
        for t in chains:
            cur = {c: s_scr[c, (i0 + t) % 2] for c in chains if c >= t}
            later = [c for c in chains if c > t]
            if later:
                scores(i0 + t + 1, (i0 + t + 1) % 2, later)
            vt = values(i0 + t)
            absorb(t, jnp.where(r_ix <= c_ix, cur[t], NEG_INF), vt)
            for c in later:
                absorb(c, cur[c], vt)
        for c in chains:
            qs = slice((i0 + c) * blk, (i0 + c + 1) * blk)
            ot_ref[:, qs] = (acc_scr[c, :dh, :] / acc_scr[c, dh:dh + 1, :]).astype(ot_ref.dtype)


def _moba(q_t, k, v_t, bsz, seq):
    nb = seq // MOBA_BLOCK
    da = ATT_HEADS * HEAD_DIM
    e_t = ((jnp.arange(seq) // MOBA_BLOCK)[:, None] == jnp.arange(LANES)[None, :]).astype(BF16)
    tspec = pl.BlockSpec((HEAD_DIM, seq), lambda b, h: (h, b))
    return pl.pallas_call(
        functools.partial(_moba_kernel, nb=nb),
        out_shape=jax.ShapeDtypeStruct((da, bsz * seq), BF16),
        grid=(bsz, ATT_HEADS),
        in_specs=[tspec, pl.BlockSpec((seq, HEAD_DIM), lambda b, h: (b, h)),
                  pl.BlockSpec((nb, HEAD_DIM, MOBA_BLOCK), lambda b, h: (b, h, 0)),
                  pl.BlockSpec((seq, LANES), lambda b, h: (0, 0))],
        out_specs=tspec,
        scratch_shapes=[pltpu.VMEM((MOBA_CHAINS, 2, MOBA_BLOCK, MOBA_BLOCK), F32),
                        pltpu.VMEM((MOBA_CHAINS, HEAD_DIM + MOBA_SUM_ROWS, MOBA_BLOCK), F32),
                        pltpu.VMEM((MOBA_CHAINS, 1, MOBA_BLOCK), F32)],
        compiler_params=_cparams("parallel", "parallel"),
        name="moba_attn",
    )(q_t, k, v_t, e_t)


def _router_weights(w_group, w_expert):
    d = w_group.shape[0]
    w_r = jnp.zeros((d, LANES), F32).at[:, :N_GROUPS].set(w_group)
    w_r = w_r.at[:, N_GROUPS:N_GROUPS + N_EXPERTS].set(w_expert)
    w_hi = w_r.astype(BF16)
    w_lo = (w_r - w_hi.astype(F32)).astype(BF16)
    return jnp.concatenate([w_hi, w_lo], axis=1)


def _route(h, g_ref, w_ref, tri_ref, meta_ref, idx_ref, cnt_ref):
    xn = _rmsnorm(h, g_ref[...])
    x_hi = xn.astype(BF16)
    x_lo = (xn - x_hi.astype(F32)).astype(BF16)
    t = jnp.dot(x_hi, w_ref[...], preferred_element_type=F32)
    logits = (t[:, :LANES] + t[:, LANES:]) + jnp.dot(x_lo, w_ref[:, :LANES], preferred_element_type=F32)
    tm = logits.shape[0]
    lane = lax.broadcasted_iota(jnp.int32, (tm, LANES), 1).astype(F32)
    ninf = -jnp.inf
    lg = jnp.where(lane < N_GROUPS, logits, ninf)
    gm = jnp.max(lg, axis=-1, keepdims=True)
    g_idx = jnp.min(jnp.where(lg == gm, lane, float(LANES)), axis=-1, keepdims=True)
    g_gate = 1.0 / jnp.sum(jnp.exp(lg - gm), axis=-1, keepdims=True)
    lo = N_GROUPS + EXPERTS_PER_GROUP * g_idx
    le = jnp.where((lane >= lo) & (lane < lo + EXPERTS_PER_GROUP), logits, ninf)
    m1 = jnp.max(le, axis=-1, keepdims=True)
    i1 = jnp.min(jnp.where(le == m1, lane, float(LANES)), axis=-1, keepdims=True)
    le2 = jnp.where(lane == i1, ninf, le)
    m2 = jnp.max(le2, axis=-1, keepdims=True)
    i2 = jnp.min(jnp.where(le2 == m2, lane, float(LANES)), axis=-1, keepdims=True)
    p2 = jnp.exp(m2 - m1)
    gate1 = g_gate / (1.0 + p2)
    gate2 = g_gate * p2 / (1.0 + p2)
    e1 = i1 - N_GROUPS
    e2 = i2 - N_GROUPS
    meta = jnp.where(lane == 0, e1, jnp.where(lane == 1, e2, jnp.where(lane == 2, gate1,
                     jnp.where(lane == 3, gate2, 0.0))))
    meta_ref[...] = meta
    oh1 = (lane == e1).astype(F32)
    oh2 = (lane == e2).astype(F32)
    both = oh1 + oh2

    @pl.when(pl.program_id(0) == 0)
    def _():
        cnt_ref[...] = jnp.zeros_like(cnt_ref)

    seen = cnt_ref[...]
    before = jnp.dot(tri_ref[...], both.astype(BF16), preferred_element_type=F32) + seen
    r1 = jnp.sum(before * oh1, axis=-1, keepdims=True)
    r2 = jnp.sum(before * oh2, axis=-1, keepdims=True)
    slab = jnp.where(lane == 0, e1, jnp.where(lane == 1, e2, jnp.where(lane == 2, r1,
                     jnp.where(lane == 3, r2, 0.0))))
    idx_ref[...] = jnp.transpose(slab)[0:8, :].astype(jnp.int32)
    cnt_ref[...] = seen + jnp.sum(both, axis=0, keepdims=True)


MOE_ISSUE_UNROLL = 8
ROW_TILE = 8


def _row_tile(off):
    return pl.ds(pl.multiple_of(off, ROW_TILE), ROW_TILE)


def _store_row_tiles(ref, x):
    rows = x.shape[0]
    for c in range(ROW_TILE):
        ref[pl.ds(c, rows, stride=ROW_TILE), :] = x[:, c * LANES:(c + 1) * LANES]


def _load_row_tiles(ref, rows):
    return jnp.concatenate([ref[pl.ds(c, rows, stride=ROW_TILE), :] for c in range(ROW_TILE)], axis=1)


def _dispatch_kernel(d1_ref, d2_ref, plo_ref, pn_ref, nu_ref, h_ref, g_ref, xs_ref,
                     buf_ref, zero_ref, sem, fill_sem):
    i = pl.program_id(0)
    nsteps = pl.num_programs(0)
    tm = h_ref.shape[0]
    rt = ROW_TILE
    rb = zero_ref.shape[0] // rt
    n_blk = xs_ref.shape[0] // (rb * rt)
    slot = i % 2
    base = i * tm

    def slot_drain(s):
        cp = pltpu.make_async_copy(buf_ref.at[s], xs_ref.at[pl.ds(0, tm * rt), :], sem.at[s])
        cp.wait()
        cp.wait()

    @pl.when(i >= 2)
    def _():
        slot_drain(slot)

    _store_row_tiles(buf_ref.at[slot], _rmsnorm(h_ref[...], g_ref[...]))

    def issue(r8, c):
        for k in range(MOE_ISSUE_UNROLL):
            r = r8 * MOE_ISSUE_UNROLL + k
            src = buf_ref.at[slot, _row_tile(r * rt), :]
            pltpu.make_async_copy(src, xs_ref.at[_row_tile(d1_ref[base + r]), :], sem.at[slot]).start(priority=0)
            pltpu.make_async_copy(src, xs_ref.at[_row_tile(d2_ref[base + r]), :], sem.at[slot]).start(priority=1)
        return c

    lax.fori_loop(0, tm // MOE_ISSUE_UNROLL, issue, 0)

    def pad_copy(off, rows):
        return pltpu.make_async_copy(zero_ref.at[pl.ds(0, rows * rt), :],
                                     xs_ref.at[pl.ds(pl.multiple_of(off, rt), rows * rt), :], fill_sem)

    pad_sizes = [1 << b for b in reversed(range(rb.bit_length() - 1))]

    def blk_copy(b):
        return pltpu.make_async_copy(zero_ref, xs_ref.at[pl.ds(pl.multiple_of(b * (rb * rt), rt), rb * rt), :],
                                     fill_sem)

    @pl.when(i == 0)
    def _():
        zero_ref[...] = jnp.zeros_like(zero_ref)

    @pl.when(i < N_EXPERTS)
    def _():
        off = plo_ref[i]
        npad = pn_ref[i]
        for rows in pad_sizes:
            has = (npad & rows) != 0

            @pl.when(has)
            def _(off=off, rows=rows):
                pad_copy(off, rows).start()

            off = off + jnp.where(has, rows * rt, 0)

    @pl.when(i == N_EXPERTS)
    def _():
        def fill(b, c):
            blk_copy(b).start()
            return c

        lax.fori_loop(nu_ref[0], n_blk, fill, 0)

    @pl.when(i == nsteps - 1)
    def _():
        slot_drain(1 - slot)
        slot_drain(slot)

        def per_expert(e, c):
            npad = pn_ref[e]
            for rows in pad_sizes:
                @pl.when((npad & rows) != 0)
                def _(rows=rows):
                    pad_copy(0, rows).wait()

            return c

        lax.fori_loop(0, N_EXPERTS, per_expert, 0)

        def one_blk(b, c):
            blk_copy(0).wait()
            return c

        lax.fori_loop(nu_ref[0], n_blk, one_blk, 0)


def _dispatch(d1, d2, pad_lo, pad_n, n_used, h, g, cap, tm):
    n, d = h.shape
    assert n // tm > N_EXPERTS + 1 and d == ROW_TILE * LANES
    return pl.pallas_call(
        _dispatch_kernel,
        out_shape=jax.ShapeDtypeStruct((cap * ROW_TILE, LANES), F32),
        grid_spec=pltpu.PrefetchScalarGridSpec(
            num_scalar_prefetch=5,
            grid=(n // tm,),
            in_specs=[
                pl.BlockSpec((tm, d), lambda i, *_: (i, 0)),
                pl.BlockSpec((1, d), lambda i, *_: (0, 0)),
            ],
            out_specs=pl.BlockSpec(memory_space=pl.ANY),
            scratch_shapes=[pltpu.VMEM((2, tm * ROW_TILE, LANES), F32),
                            pltpu.VMEM((MOE_ROWS * ROW_TILE, LANES), F32),
                            pltpu.SemaphoreType.DMA((2,)), pltpu.SemaphoreType.DMA],
        ),
        compiler_params=_cparams("arbitrary"),
        name="moe_dispatch",
    )(d1, d2, pad_lo, pad_n, n_used, h, g.reshape(1, d))


def _expert_kernel(be_ref, nu_ref, nx_ref, x_ref, w1_hbm, w3_hbm, w2_hbm, y_ref,
                   wb1, wb3, wb2, w1c, w3c, w2c, wsem, wslot, *, layer):
    i = pl.program_id(0)
    used = i < nu_ref[0]
    e = be_ref[i]
    prev = be_ref[jnp.maximum(i - 1, 0)]
    fresh = jnp.logical_or(i == 0, e != prev)

    def weight_copies(ex, s):
        return [pltpu.make_async_copy(w1_hbm.at[layer, ex], wb1.at[s], wsem.at[s]),
                pltpu.make_async_copy(w3_hbm.at[layer, ex], wb3.at[s], wsem.at[s]),
                pltpu.make_async_copy(w2_hbm.at[layer, ex], wb2.at[s], wsem.at[s])]

    @pl.when(i == 0)
    def _():
        wslot[0] = 0
        for cp in weight_copies(e, 0):
            cp.start()

    @pl.when(jnp.logical_and(used, fresh))
    def _():
        s = wslot[0]
        for cp in weight_copies(e, s):
            cp.wait()
        nxt = nx_ref[e]

        @pl.when(nxt != e)
        def _():
            for cp in weight_copies(nxt, 1 - s):
                cp.start()

        w1c[...] = wb1[s].astype(BF16)
        w3c[...] = wb3[s].astype(BF16)
        w2c[...] = wb2[s].astype(BF16)
        wslot[0] = 1 - s

    @pl.when(used)
    def _():
        rb = x_ref.shape[0] // ROW_TILE
        x = _load_row_tiles(x_ref, rb).astype(BF16)
        hf = w1c.shape[1] // 2
        y = None
        for c in range(2):
            cs = slice(c * hf, (c + 1) * hf)
            a = jnp.dot(x, w1c[:, cs], preferred_element_type=F32)
            b = jnp.dot(x, w3c[:, cs], preferred_element_type=F32)
            act = (jax.nn.silu(a) * b).astype(BF16)
            part = jnp.dot(act, w2c[cs, :], preferred_element_type=F32)
            y = part if y is None else y + part
        _store_row_tiles(y_ref, y)

    @pl.when(jnp.logical_not(used))
    def _():
        y_ref[...] = jnp.zeros_like(y_ref)


def _experts(blk_e, n_used, next_e, xs, w1, w3, w2, layer):
    d = ROW_TILE * LANES
    rb = MOE_ROWS
    n_blk = xs.shape[0] // (rb * ROW_TILE)
    ff = w1.shape[3]
    blk = (rb * ROW_TILE, LANES)
    hbm = pl.BlockSpec(memory_space=pl.ANY)
    return pl.pallas_call(
        functools.partial(_expert_kernel, layer=layer),
        out_shape=jax.ShapeDtypeStruct(xs.shape, F32),
        grid_spec=pltpu.PrefetchScalarGridSpec(
            num_scalar_prefetch=3,
            grid=(n_blk,),
            in_specs=[
                pl.BlockSpec(blk, lambda i, be, nu, nx: (jnp.minimum(i, jnp.maximum(nu[0] - 1, 0)), 0)),
                hbm, hbm, hbm,
            ],
            out_specs=pl.BlockSpec(blk, lambda i, be, nu, nx: (i, 0)),
            scratch_shapes=[pltpu.VMEM((2, d, ff), F32), pltpu.VMEM((2, d, ff), F32), pltpu.VMEM((2, ff, d), F32),
                            pltpu.VMEM((d, ff), BF16), pltpu.VMEM((d, ff), BF16), pltpu.VMEM((ff, d), BF16),
                            pltpu.SemaphoreType.DMA((2,)), pltpu.SMEM((1,), jnp.int32)],
        ),
        compiler_params=_cparams("arbitrary"),
        name="moe_experts",
    )(blk_e, n_used, next_e, xs, w1, w3, w2)


def _gathered_sum(d1_ref, d2_ref, h_ref, meta_ref, ys_ref, buf_ref, sem):
    i = pl.program_id(0)
    nsteps = pl.num_programs(0)
    tm = h_ref.shape[0]
    slot = i % 2

    def fetch(tile, s):
        base = tile * tm

        def issue(r8, c):
            for k in range(MOE_ISSUE_UNROLL):
                r = r8 * MOE_ISSUE_UNROLL + k
                pltpu.make_async_copy(ys_ref.at[_row_tile(d1_ref[base + r]), :],
                                      buf_ref.at[s, 0, _row_tile(r * ROW_TILE), :], sem.at[s]).start(priority=0)
                pltpu.make_async_copy(ys_ref.at[_row_tile(d2_ref[base + r]), :],
                                      buf_ref.at[s, 1, _row_tile(r * ROW_TILE), :], sem.at[s]).start(priority=1)
            return c

        lax.fori_loop(0, tm // MOE_ISSUE_UNROLL, issue, 0)

    @pl.when(i == 0)
    def _():
        fetch(0, 0)

    @pl.when(i + 1 < nsteps)
    def _():
        fetch(i + 1, 1 - slot)

    for j in range(2):
        pltpu.make_async_copy(ys_ref.at[pl.ds(0, tm * ROW_TILE), :], buf_ref.at[slot, j], sem.at[slot]).wait()

    meta = meta_ref[...]
    y1 = _load_row_tiles(buf_ref.at[slot, 0], tm)
    y2 = _load_row_tiles(buf_ref.at[slot, 1], tm)
    return h_ref[...] + (meta[:, 2:3] * y1 + meta[:, 3:4] * y2)


def _combine_norm_kernel(d1_ref, d2_ref, h_ref, meta_ref, g_ref, ys_ref, o_ref, buf_ref, sem):
    o_ref[...] = _rmsnorm(_gathered_sum(d1_ref, d2_ref, h_ref, meta_ref, ys_ref, buf_ref, sem), g_ref[...])


def _combine_qkv_kernel(d1_ref, d2_ref, h_ref, meta_ref, g_ref, wq_ref, wk_ref, wv_ref, ys_ref,
                        o_ref, qt_ref, k_ref, vt_ref, buf_ref, sem):
    h = _gathered_sum(d1_ref, d2_ref, h_ref, meta_ref, ys_ref, buf_ref, sem)
    o_ref[...] = h
    _qkv_body(h, g_ref, wq_ref, wk_ref, wv_ref, qt_ref, k_ref, vt_ref)


def _combine(slots, h, g, tm, w_qkv=None):
    d1, d2, meta, ys = slots
    n, d = h.shape
    row = lambda w: pl.BlockSpec((tm, w), lambda i, *_: (i, 0))
    full = lambda arr: pl.BlockSpec(arr.shape, lambda i, *_: (0, 0))
    in_specs = [row(d), row(LANES), pl.BlockSpec((1, d), lambda i, *_: (0, 0))]
    operands = [h, meta, g.reshape(1, d)]
    out_shape = [jax.ShapeDtypeStruct((n, d), F32)]
    out_specs = [row(d)]
    if w_qkv is None:
        kernel_fn, name = _combine_norm_kernel, "moe_combine_norm"
    else:
        kernel_fn, name = _combine_qkv_kernel, "moe_combine_qkv"
        da = ATT_HEADS * HEAD_DIM
        weights = [w_qkv[:, :da].T.astype(BF16), w_qkv[:, da:2 * da].astype(BF16), w_qkv[:, 2 * da:].T.astype(BF16)]
        in_specs += [full(w) for w in weights]
        operands += weights
        out_shape += [jax.ShapeDtypeStruct((da, n), BF16), jax.ShapeDtypeStruct((n, da), BF16),
                      jax.ShapeDtypeStruct((n // MOBA_BLOCK, da, MOBA_BLOCK), BF16)]
        out_specs += [pl.BlockSpec((da, tm), lambda i, *_: (0, i)), row(da),
                      pl.BlockSpec((tm // MOBA_BLOCK, da, MOBA_BLOCK), lambda i, *_: (i, 0, 0))]
    return pl.pallas_call(
        kernel_fn,
        out_shape=tuple(out_shape),
        grid_spec=pltpu.PrefetchScalarGridSpec(
            num_scalar_prefetch=2,
            grid=(n // tm,),
            in_specs=in_specs + [pl.BlockSpec(memory_space=pl.ANY)],
            out_specs=tuple(out_specs),
            scratch_shapes=[pltpu.VMEM((2, 2, tm * ROW_TILE, LANES), F32), pltpu.SemaphoreType.DMA((2,))],
        ),
        compiler_params=_cparams("arbitrary"),
        name=name,
    )(d1, d2, *operands, ys)


def _hier_moe_slots(h, routing, ln_g, w1, w3, w2, layer):
    n, d = h.shape
    rb = MOE_ROWS
    cap = 2 * n + N_EXPERTS * rb
    n_blk = cap // rb
    meta, idx, counts = routing
    cnt = counts[0, :N_EXPERTS].astype(jnp.int32)
    padded = ((cnt + rb - 1) // rb) * rb
    pad_end = jnp.cumsum(padded)
    pad_start = pad_end - padded
    blk_start = jnp.arange(n_blk, dtype=jnp.int32) * rb
    blk_e = jnp.minimum(jnp.sum(pad_end[None, :] <= blk_start[:, None], axis=1), N_EXPERTS - 1).astype(jnp.int32)
    n_used = (pad_end[-1] // rb).astype(jnp.int32).reshape(1)
    experts = jnp.arange(N_EXPERTS, dtype=jnp.int32)[None, :]
    start_of = lambda e: jnp.sum(jnp.where(e[:, None] == experts, pad_start[None, :], 0), axis=1)
    d1 = (start_of(idx[0]) + idx[2]) * ROW_TILE
    d2 = (start_of(idx[1]) + idx[3]) * ROW_TILE
    xs = _dispatch(d1, d2, (pad_start + cnt) * ROW_TILE, padded - cnt, n_used, h, ln_g, cap, MOE_DISPATCH_ROWS)
    later = (experts > experts.T) & (padded > 0)[None, :]
    next_e = jnp.min(jnp.where(later, experts, N_EXPERTS), axis=1)
    next_e = jnp.where(next_e == N_EXPERTS, experts[0], next_e).astype(jnp.int32)
    ys = _experts(blk_e, n_used, next_e, xs, w1, w3, w2, layer)
    return d1, d2, meta, ys


def kernel(x, ln_mix, ln_ffn, ln_final, ssm_w_in, ssm_lam_re, ssm_lam_im, ssm_log_dt, ssm_b_re, ssm_b_im, ssm_c_re, ssm_c_im, ssm_d, ssm_w_out, attn_w_qkv, attn_w_o, moe_w_group, moe_w_expert, moe_w1, moe_w3, moe_w2):
    bsz, seq, d = x.shape
    n = bsz * seq
    h = x.reshape(n, d)

    u = _norm_proj(h, ln_mix[0], ssm_w_in[0].astype(BF16), DENSE_ROWS, F32, "s5_in_proj")
    mats = _s5_mats(ssm_lam_re[0], ssm_lam_im[0], ssm_log_dt[0], ssm_b_re[0], ssm_b_im[0],
                    ssm_c_re[0], ssm_c_im[0])
    z = _s5_core(u, mats, ssm_d[0], bsz, seq)
    h, routing = _glu_out(z, ssm_w_out[0].astype(BF16), h, ln_ffn[0],
                          _router_weights(moe_w_group[0], moe_w_expert[0]), MIXER_OUT_ROWS)
    slots = _hier_moe_slots(h, routing, ln_ffn[0], moe_w1, moe_w3, moe_w2, 0)

    h, q_t, k, v_t = _combine(slots, h, ln_mix[1], MOE_COMBINE_ROWS, attn_w_qkv[0])
    o_t = _moba(q_t, k, v_t, bsz, seq)
    h, routing = _proj_res_t(o_t, attn_w_o[0].astype(BF16), h, ln_ffn[1],
                             _router_weights(moe_w_group[1], moe_w_expert[1]), MIXER_OUT_ROWS)
    slots = _hier_moe_slots(h, routing, ln_ffn[1], moe_w1, moe_w3, moe_w2, 1)
    (out,) = _combine(slots, h, ln_final, MOE_COMBINE_ROWS)
    return out.reshape(bsz, seq, d)
```

```python
import functools
import math

import jax
import jax.numpy as jnp
from jax import lax
from jax.experimental import pallas as pl
from jax.experimental.pallas import tpu as pltpu

F32 = jnp.float32
BF16 = jnp.bfloat16

D_MODEL = 1024
RMS_EPS = 1e-6
NEG_INF = -1e30

SSM_GROUP = 16
SSM_GROUPS = D_MODEL // SSM_GROUP
SSM_STATE = 64
SSM_CHUNK = 16
SSM_GB = 8
SSM_WIN = 8
SSM_POW_ROWS = 24

ATT_HEADS = 8
HEAD_DIM = 128
MOBA_BLOCK = 256
MOBA_TOPK = 3

N_GROUPS = 4
EXPERTS_PER_GROUP = 8
N_EXPERTS = 32
EXPERT_FF = 512
MOE_ROWS = 512
MOE_DISPATCH_ROWS = 256
MOE_COMBINE_ROWS = 256
DENSE_ROWS = 1024
MIXER_OUT_ROWS = 512

LANES = 128
VMEM_LIMIT = 56 * 1024 * 1024

_NT = (((1,), (1,)), ((), ()))
_TN = (((0,), (0,)), ((), ()))


def _cparams(*sem):
    return pltpu.CompilerParams(dimension_semantics=sem, vmem_limit_bytes=VMEM_LIMIT)


def _rmsnorm(x, g):
    return x * lax.rsqrt(jnp.mean(x * x, axis=-1, keepdims=True) + RMS_EPS) * g


def _norm_proj_kernel(x_ref, g_ref, w_ref, o_ref):
    xn = _rmsnorm(x_ref[...], g_ref[...]).astype(BF16)
    o_ref[...] = jnp.dot(xn, w_ref[...], preferred_element_type=F32).astype(o_ref.dtype)


def _norm_proj(x, g, w_bf16, tm, out_dtype, name):
    n, d = x.shape
    dout = w_bf16.shape[1]
    return pl.pallas_call(
        _norm_proj_kernel,
        out_shape=jax.ShapeDtypeStruct((n, dout), out_dtype),
        grid=(n // tm,),
        in_specs=[
            pl.BlockSpec((tm, d), lambda i: (i, 0)),
            pl.BlockSpec((1, d), lambda i: (0, 0)),
            pl.BlockSpec((d, dout), lambda i: (0, 0)),
        ],
        out_specs=pl.BlockSpec((tm, dout), lambda i: (i, 0)),
        compiler_params=_cparams("parallel"),
        name=name,
    )(x, g.reshape(1, d), w_bf16)


def _s5_mats(lam_re, lam_im, log_dt, b_re, b_im, c_re, c_im):
    g_, p_, c_, t_ = SSM_GROUPS, SSM_STATE, SSM_GROUP, SSM_CHUNK
    lr = jnp.minimum(lam_re, -1e-4)
    li = lam_im
    dt = jnp.exp(log_dt)[:, None]
    mag = jnp.exp(lr * dt)
    abar_re = mag * jnp.cos(li * dt)
    abar_im = mag * jnp.sin(li * dt)
    den = lr * lr + li * li
    nr = abar_re - 1.0
    gam_re = (nr * lr + abar_im * li) / den
    gam_im = (abar_im * lr - nr * li) / den
    bb_re = gam_re[..., None] * b_re - gam_im[..., None] * b_im
    bb_im = gam_re[..., None] * b_im + gam_im[..., None] * b_re

    def powers(ns):
        nf = jnp.asarray(ns, F32)[None, :, None]
        pm = jnp.exp(nf * (lr * dt)[:, None, :])
        ang = nf * (li * dt)[:, None, :]
        return pm * jnp.cos(ang), pm * jnp.sin(ang)

    pr, pi = powers(list(range(t_ + 1)))
    pad = jnp.zeros((g_, SSM_POW_ROWS - (t_ + 1), 2 * p_), F32)
    pw = jnp.stack([jnp.concatenate([jnp.concatenate([pr, pr], -1), pad], axis=1),
                    jnp.concatenate([jnp.concatenate([pi, pi], -1), pad], axis=1)], axis=1)
    bt_re = bb_re.transpose(0, 2, 1)
    bt_im = bb_im.transpose(0, 2, 1)
    cat = lambda a, b: jnp.concatenate([a, b], axis=-1)
    fac = jnp.stack([cat(bt_re, bt_im), cat(-bt_im, bt_re), cat(bt_re, -bt_im),
                     cat(c_re, c_im), cat(-c_im, c_re)], axis=1)
    qr, qi = powers([t_ * (1 << j) for j in range(4)])
    rows = []
    for j in range(4):
        ar, ai = qr[:, j], qi[:, j]
        rows += [jnp.concatenate([ar, ar], -1), jnp.concatenate([-ai, ai], -1),
                 jnp.concatenate([ai, -ai], -1)]
    rows += [jnp.zeros_like(rows[0])] * 4
    coef = jnp.stack(rows, axis=1)
    return fac.astype(F32), pw.astype(F32), coef.astype(F32)


def _s5_perm():
    r = jnp.arange(8 * LANES)
    col = ((r % LANES) // SSM_GROUP) * LANES + (r // LANES) * SSM_GROUP + r % SSM_GROUP
    p = (col[:, None] == r[None, :]).astype(BF16)
    return p, p.T


def _s5_kernel(u_ref, fac_ref, pw_ref, coef_ref, d_ref, p_ref, pt_ref, o_ref,
               m_ref, w_ref, v_ref, vf_ref, zf_ref, ea_ref, eb_ref, sp_ref):
    t_, c_, gb, win = SSM_CHUNK, SSM_GROUP, SSM_GB, SSM_WIN
    nk = u_ref.shape[0] // t_
    p2 = 2 * SSM_STATE
    tc = t_ * c_

    @pl.when(pl.program_id(1) == 0)
    def _():
        lane_tc = lax.broadcasted_iota(jnp.int32, (c_, tc), 1)
        conj = jnp.where(lax.broadcasted_iota(jnp.int32, (1, p2), 1) < SSM_STATE, 1.0, -1.0)
        for g in range(gb):
            b_t, ib_t, bconj_t, cm, icm = (fac_ref[g, k] for k in range(5))
            pr, pi = pw_ref[g, 0], pw_ref[g, 1]
            for s in range(t_):
                n = t_ - 1 - s
                w_ref[g, s * c_:(s + 1) * c_, :] = (b_t * pr[n:n + 1] + ib_t * pi[n:n + 1]).astype(BF16)
            ca = jnp.concatenate([cm * pr[n:n + 1] + icm * pi[n:n + 1] for n in range(t_ + 1)], axis=0)
            v_ref[g] = jnp.transpose(ca[c_:] * conj).astype(BF16)
            kf = lax.dot_general(bconj_t, ca[:tc], _NT, preferred_element_type=F32,
                                 precision=lax.Precision.HIGHEST)
            for s in range(t_):
                rolled = kf if s == 0 else pltpu.roll(kf, s * c_, axis=1)
                m_ref[g, s * c_:(s + 1) * c_, :] = jnp.where(lane_tc >= s * c_, rolled, 0.0).astype(BF16)

    halves = tc // LANES

    def timestep(t):
        return pl.ds(t, nk, stride=t_)

    def flat(g):
        return jnp.concatenate([vf_ref[h, :, g * LANES:(g + 1) * LANES] for h in range(halves)], axis=1)

    for h in range(halves):
        x = jnp.concatenate([u_ref[timestep(8 * h + i), :].astype(BF16) for i in range(8)], axis=1)
        vf_ref[h] = jnp.dot(x, p_ref[...], preferred_element_type=F32).astype(BF16)

    row = lax.broadcasted_iota(jnp.int32, (nk, p2), 0)

    def shift(x, d):
        return jnp.where(row < d, 0.0, pltpu.roll(x, d, axis=0))

    def swap(x):
        return pltpu.roll(x, SSM_STATE, axis=1)

    for g in range(gb):
        sc = jnp.dot(flat(g), w_ref[g], preferred_element_type=F32)
        xa = shift(sc, 1)
        cf = coef_ref[g]
        for lvl in range(3):
            pp, qa = cf[3 * lvl:3 * lvl + 1], cf[3 * lvl + 1:3 * lvl + 2]
            sa = shift(xa, 1 << lvl)
            xa = xa + pp * sa + qa * swap(sa)
        ea_ref[g] = xa
        eb_ref[g] = swap(xa)

    cfs = [coef_ref[g] for g in range(gb)]
    zero = jnp.zeros((win, p2), F32)
    state = [(zero, zero)] * gb
    for j in range(nk // win):
        rs = slice(j * win, (j + 1) * win)
        for g in range(gb):
            pp, qa, qb = cfs[g][9:10], cfs[g][10:11], cfs[g][11:12]
            s_a, s_b = state[g]
            n_a = pp * s_a + qa * s_b + ea_ref[g, rs, :]
            n_b = pp * s_b + qb * s_a + eb_ref[g, rs, :]
            sp_ref[g, rs, :] = n_a
            state[g] = (n_a, n_b)

    for g in range(gb):
        x = flat(g)
        y = jnp.dot(x, m_ref[g], preferred_element_type=F32)
        y = y + jnp.dot(sp_ref[g].astype(BF16), v_ref[g], preferred_element_type=F32)
        z = jax.nn.gelu(y + d_ref[g] * x.astype(F32)).astype(BF16)
        for h in range(halves):
            zf_ref[h, :, g * LANES:(g + 1) * LANES] = z[:, h * LANES:(h + 1) * LANES]

    for h in range(halves):
        zn = jnp.dot(zf_ref[h], pt_ref[...], preferred_element_type=F32)
        for i in range(8):
            o_ref[timestep(8 * h + i), :] = zn[:, i * LANES:(i + 1) * LANES]


def _s5_core(u, mats, d_skip, bsz, seq):
    fac, pw, coef = mats
    n, d = u.shape
    gb, tc, p2 = SSM_GB, SSM_CHUNK * SSM_GROUP, 2 * SSM_STATE
    nk = seq // SSM_CHUNK
    halves = tc // LANES
    perm, perm_t = _s5_perm()
    dflat = jnp.tile(d_skip.reshape(SSM_GROUPS, 1, SSM_GROUP), (1, 1, SSM_CHUNK))
    spec3 = lambda a, b: pl.BlockSpec((gb, a, b), lambda j, bb: (j, 0, 0))
    const = pl.BlockSpec(perm.shape, lambda j, bb: (0, 0))
    return pl.pallas_call(
        _s5_kernel,
        out_shape=jax.ShapeDtypeStruct((n, d), F32),
        grid=(SSM_GROUPS // gb, bsz),
        in_specs=[
            pl.BlockSpec((seq, LANES), lambda j, bb: (bb, j)),
            pl.BlockSpec((gb,) + fac.shape[1:], lambda j, bb: (j, 0, 0, 0)),
            pl.BlockSpec((gb,) + pw.shape[1:], lambda j, bb: (j, 0, 0, 0)),
            spec3(16, p2), spec3(1, tc),
            const, const,
        ],
        out_specs=pl.BlockSpec((seq, LANES), lambda j, bb: (bb, j)),
        scratch_shapes=[pltpu.VMEM((gb, tc, tc), BF16), pltpu.VMEM((gb, tc, p2), BF16),
                        pltpu.VMEM((gb, p2, tc), BF16),
                        pltpu.VMEM((halves, nk, gb * LANES), BF16),
                        pltpu.VMEM((halves, nk, gb * LANES), BF16),
                        pltpu.VMEM((gb, nk, p2), F32), pltpu.VMEM((gb, nk, p2), F32),
                        pltpu.VMEM((gb, nk, p2), F32)],
        compiler_params=_cparams("arbitrary", "arbitrary"),
        name="s5_core",
    )(u, fac, pw, coef, dflat, perm, perm_t)


def _glu_out_kernel(z_ref, w_ref, x_ref, g_ref, wr_ref, tri_ref, o_ref, meta_ref, idx_ref, cnt_ref):
    vg = jnp.dot(z_ref[...].astype(BF16), w_ref[...], preferred_element_type=F32)
    d = o_ref.shape[1]
    h = x_ref[...] + vg[:, :d] * jax.nn.sigmoid(vg[:, d:])
    o_ref[...] = h
    _route(h, g_ref, wr_ref, tri_ref, meta_ref, idx_ref, cnt_ref)


def _proj_res_t_kernel(at_ref, w_ref, x_ref, g_ref, wr_ref, tri_ref, o_ref, meta_ref, idx_ref, cnt_ref):
    h = x_ref[...] + lax.dot_general(at_ref[...], w_ref[...], _TN, preferred_element_type=F32)
    o_ref[...] = h
    _route(h, g_ref, wr_ref, tri_ref, meta_ref, idx_ref, cnt_ref)


def _mixer_out(kernel_fn, a, a_spec, w_bf16, x, ln_g, w_router, tm, name):
    n, d = x.shape
    tri = (jnp.arange(tm)[:, None] > jnp.arange(tm)[None, :]).astype(BF16)
    full = lambda arr: pl.BlockSpec(arr.shape, lambda i: (0, 0))
    h, meta, idx, counts = pl.pallas_call(
        kernel_fn,
        out_shape=(jax.ShapeDtypeStruct((n, d), F32), jax.ShapeDtypeStruct((n, LANES), F32),
                   jax.ShapeDtypeStruct((8, n), jnp.int32), jax.ShapeDtypeStruct((1, LANES), F32)),
        grid=(n // tm,),
        in_specs=[a_spec, full(w_bf16), pl.BlockSpec((tm, d), lambda i: (i, 0)),
                  pl.BlockSpec((1, d), lambda i: (0, 0)), full(w_router), full(tri)],
        out_specs=(pl.BlockSpec((tm, d), lambda i: (i, 0)), pl.BlockSpec((tm, LANES), lambda i: (i, 0)),
                   pl.BlockSpec((8, tm), lambda i: (0, i)), pl.BlockSpec((1, LANES), lambda i: (0, 0))),
        compiler_params=_cparams("arbitrary"),
        name=name,
    )(a, w_bf16, x, ln_g.reshape(1, d), w_router, tri)
    return h, (meta, idx, counts)


def _glu_out(z, w_bf16, x, ln_g, w_router, tm):
    spec = pl.BlockSpec((tm, z.shape[1]), lambda i: (i, 0))
    return _mixer_out(_glu_out_kernel, z, spec, w_bf16, x, ln_g, w_router, tm, "s5_glu_out")


def _proj_res_t(a_t, w_bf16, x, ln_g, w_router, tm):
    spec = pl.BlockSpec((a_t.shape[0], tm), lambda i: (0, i))
    return _mixer_out(_proj_res_t_kernel, a_t, spec, w_bf16, x, ln_g, w_router, tm, "attn_out_proj")


def _qkv_kernel(x_ref, g_ref, wq_ref, wk_ref, wv_ref, qt_ref, k_ref, vt_ref):
    xn = _rmsnorm(x_ref[...], g_ref[...]).astype(BF16)
    c = (HEAD_DIM ** -0.5) * math.log2(math.e)
    qt_ref[...] = (lax.dot_general(wq_ref[...], xn, _NT, preferred_element_type=F32) * c).astype(BF16)
    k_ref[...] = jnp.dot(xn, wk_ref[...], preferred_element_type=F32).astype(BF16)
    vt = lax.dot_general(wv_ref[...], xn, _NT, preferred_element_type=F32).astype(BF16)
    for c in range(vt_ref.shape[0]):
        vt_ref[c] = vt[:, c * MOBA_BLOCK:(c + 1) * MOBA_BLOCK]


def _qkv_proj(x, g, w_qkv, tm):
    n, d = x.shape
    da = ATT_HEADS * HEAD_DIM
    wq_t = w_qkv[:, :da].T.astype(BF16)
    wk = w_qkv[:, da:2 * da].astype(BF16)
    wv_t = w_qkv[:, 2 * da:].T.astype(BF16)
    full = lambda shp: pl.BlockSpec(shp, lambda i: (0, 0))
    tb = tm // MOBA_BLOCK
    return pl.pallas_call(
        _qkv_kernel,
        out_shape=(jax.ShapeDtypeStruct((da, n), BF16), jax.ShapeDtypeStruct((n, da), BF16),
                   jax.ShapeDtypeStruct((n // MOBA_BLOCK, da, MOBA_BLOCK), BF16)),
        grid=(n // tm,),
        in_specs=[pl.BlockSpec((tm, d), lambda i: (i, 0)), full((1, d)),
                  full((da, d)), full((d, da)), full((da, d))],
        out_specs=(pl.BlockSpec((da, tm), lambda i: (0, i)), pl.BlockSpec((tm, da), lambda i: (i, 0)),
                   pl.BlockSpec((tb, da, MOBA_BLOCK), lambda i: (i, 0, 0))),
        compiler_params=_cparams("parallel"),
        name="attn_qkv_proj",
    )(x, g.reshape(1, d), wq_t, wk, wv_t)


MOBA_CHAINS = 8
MOBA_SUM_ROWS = 16


def _moba_kernel(qt_ref, k_ref, vt_ref, et_ref, ot_ref, s_scr, acc_scr, m_scr, *, nb):
    blk, dh = MOBA_BLOCK, HEAD_DIM
    brow = lax.broadcasted_iota(jnp.int32, (nb, blk), 0).astype(F32)
    r_ix = lax.broadcasted_iota(jnp.int32, (blk, blk), 0)
    c_ix = lax.broadcasted_iota(jnp.int32, (blk, blk), 1)
    zpad = jnp.zeros((2 * LANES - dh - nb, blk), BF16)
    kmean = jnp.mean(k_ref[...].astype(F32).reshape(nb, blk, dh), axis=1)
    km_hi = kmean.astype(BF16)
    kmean2 = jnp.concatenate([km_hi, (kmean - km_hi.astype(F32)).astype(BF16)], axis=0)

    def k_aug(t):
        rows = slice(t * blk, (t + 1) * blk)
        return jnp.concatenate([k_ref[rows, :], et_ref[rows, :]], axis=1)

    ones = jnp.ones((MOBA_SUM_ROWS, blk), BF16)

    def values(t):
        return jnp.concatenate([vt_ref[t], ones], axis=0)

    def absorb(c, s, vt):
        m_old = m_scr[c]
        m_new = jnp.maximum(m_old, jnp.max(s, axis=0, keepdims=True))
        p = jnp.exp2(s - m_new).astype(BF16)
        acc_scr[c] = jnp.exp2(m_old - m_new) * acc_scr[c] + jnp.dot(vt, p, preferred_element_type=F32)
        m_scr[c] = m_new

    def query_block(i):
        qs = slice(i * blk, (i + 1) * blk)
        qt = qt_ref[:, qs]
        if i > MOBA_TOPK:
            g2 = jnp.dot(kmean2, qt, preferred_element_type=F32)
            gate = g2[:nb] + g2[nb:]
            gate = jnp.where(brow < i, gate, NEG_INF)
            sel = brow >= i
            for _ in range(MOBA_TOPK):
                gm = jnp.max(gate, axis=0, keepdims=True)
                first = jnp.min(jnp.where(gate == gm, brow, float(nb)), axis=0, keepdims=True)
                pick = brow == first
                sel = jnp.logical_or(sel, pick)
                gate = jnp.where(pick, -jnp.inf, gate)
            bias = jnp.where(sel, 0.0, NEG_INF).astype(BF16)
        else:
            bias = jnp.zeros((nb, blk), BF16)
        return jnp.concatenate([qt, bias, zpad], axis=0)

    nq = MOBA_CHAINS
    for i0 in range(0, nb, nq):
        chains = range(min(nq, nb - i0))
        q_aug = [query_block(i0 + c) for c in chains]
        m_scr[...] = jnp.full(m_scr.shape, NEG_INF, F32)
        acc_scr[...] = jnp.zeros(acc_scr.shape, F32)

        def scores(t, slot, cs):
            ka = k_aug(t)
            for c in cs:
                s_scr[c, slot] = jnp.dot(ka, q_aug[c], preferred_element_type=F32)

        scores(0, 0, chains)
        for j in range(i0):
            cur = [s_scr[c, j % 2] for c in chains]
            scores(j + 1, (j + 1) % 2, chains)
            vt = values(j)
            for c in chains:
                absorb(c, cur[c], vt)
        for t in chains:
            cur = {c: s_scr[c, (i0 + t) % 2] for c in chains if c >= t}
            later = [c for c in chains if c > t]
            if later:
                scores(i0 + t + 1, (i0 + t + 1) % 2, later)
            vt = values(i0 + t)
            absorb(t, jnp.where(r_ix <= c_ix, cur[t], NEG_INF), vt)
            for c in later:
                absorb(c, cur[c], vt)
        for c in chains:
            qs = slice((i0 + c) * blk, (i0 + c + 1) * blk)
            ot_ref[:, qs] = (acc_scr[c, :dh, :] / acc_scr[c, dh:dh + 1, :]).astype(ot_ref.dtype)


def _moba(q_t, k, v_t, bsz, seq):
    nb = seq // MOBA_BLOCK
    da = ATT_HEADS * HEAD_DIM
    e_t = ((jnp.arange(seq) // MOBA_BLOCK)[:, None] == jnp.arange(LANES)[None, :]).astype(BF16)
    tspec = pl.BlockSpec((HEAD_DIM, seq), lambda b, h: (h, b))
    return pl.pallas_call(
        functools.partial(_moba_kernel, nb=nb),
        out_shape=jax.ShapeDtypeStruct((da, bsz * seq), BF16),
        grid=(bsz, ATT_HEADS),
        in_specs=[tspec, pl.BlockSpec((seq, HEAD_DIM), lambda b, h: (b, h)),
                  pl.BlockSpec((nb, HEAD_DIM, MOBA_BLOCK), lambda b, h: (b, h, 0)),
                  pl.BlockSpec((seq, LANES), lambda b, h: (0, 0))],
        out_specs=tspec,
        scratch_shapes=[pltpu.VMEM((MOBA_CHAINS, 2, MOBA_BLOCK, MOBA_BLOCK), F32),
                        pltpu.VMEM((MOBA_CHAINS, HEAD_DIM + MOBA_SUM_ROWS, MOBA_BLOCK), F32),
                        pltpu.VMEM((MOBA_CHAINS, 1, MOBA_BLOCK), F32)],
        compiler_params=_cparams("parallel", "parallel"),
        name="moba_attn",
    )(q_t, k, v_t, e_t)


def _router_weights(w_group, w_expert):
    d = w_group.shape[0]
    w_r = jnp.zeros((d, LANES), F32).at[:, :N_GROUPS].set(w_group)
    w_r = w_r.at[:, N_GROUPS:N_GROUPS + N_EXPERTS].set(w_expert)
    w_hi = w_r.astype(BF16)
    w_lo = (w_r - w_hi.astype(F32)).astype(BF16)
    return jnp.concatenate([w_hi, w_lo], axis=1)


def _route(h, g_ref, w_ref, tri_ref, meta_ref, idx_ref, cnt_ref):
    xn = _rmsnorm(h, g_ref[...])
    x_hi = xn.astype(BF16)
    x_lo = (xn - x_hi.astype(F32)).astype(BF16)
    t = jnp.dot(x_hi, w_ref[...], preferred_element_type=F32)
    logits = (t[:, :LANES] + t[:, LANES:]) + jnp.dot(x_lo, w_ref[:, :LANES], preferred_element_type=F32)
    tm = logits.shape[0]
    lane = lax.broadcasted_iota(jnp.int32, (tm, LANES), 1).astype(F32)
    ninf = -jnp.inf
    lg = jnp.where(lane < N_GROUPS, logits, ninf)
    gm = jnp.max(lg, axis=-1, keepdims=True)
    g_idx = jnp.min(jnp.where(lg == gm, lane, float(LANES)), axis=-1, keepdims=True)
    g_gate = 1.0 / jnp.sum(jnp.exp(lg - gm), axis=-1, keepdims=True)
    lo = N_GROUPS + EXPERTS_PER_GROUP * g_idx
    le = jnp.where((lane >= lo) & (lane < lo + EXPERTS_PER_GROUP), logits, ninf)
    m1 = jnp.max(le, axis=-1, keepdims=True)
    i1 = jnp.min(jnp.where(le == m1, lane, float(LANES)), axis=-1, keepdims=True)
    le2 = jnp.where(lane == i1, ninf, le)
    m2 = jnp.max(le2, axis=-1, keepdims=True)
    i2 = jnp.min(jnp.where(le2 == m2, lane, float(LANES)), axis=-1, keepdims=True)
    p2 = jnp.exp(m2 - m1)
    gate1 = g_gate / (1.0 + p2)
    gate2 = g_gate * p2 / (1.0 + p2)
    e1 = i1 - N_GROUPS
    e2 = i2 - N_GROUPS
    meta = jnp.where(lane == 0, e1, jnp.where(lane == 1, e2, jnp.where(lane == 2, gate1,
                     jnp.where(lane == 3, gate2, 0.0))))
    meta_ref[...] = meta
    oh1 = (lane == e1).astype(F32)
    oh2 = (lane == e2).astype(F32)
    both = oh1 + oh2

    @pl.when(pl.program_id(0) == 0)
    def _():
        cnt_ref[...] = jnp.zeros_like(cnt_ref)

    seen = cnt_ref[...]
    before = jnp.dot(tri_ref[...], both.astype(BF16), preferred_element_type=F32) + seen
    r1 = jnp.sum(before * oh1, axis=-1, keepdims=True)
    r2 = jnp.sum(before * oh2, axis=-1, keepdims=True)
    slab = jnp.where(lane == 0, e1, jnp.where(lane == 1, e2, jnp.where(lane == 2, r1,
                     jnp.where(lane == 3, r2, 0.0))))
    idx_ref[...] = jnp.transpose(slab)[0:8, :].astype(jnp.int32)
    cnt_ref[...] = seen + jnp.sum(both, axis=0, keepdims=True)


MOE_ISSUE_UNROLL = 8
ROW_TILE = 8


def _row_tile(off):
    return pl.ds(pl.multiple_of(off, ROW_TILE), ROW_TILE)


def _store_row_tiles(ref, x):
    rows = x.shape[0]
    for c in range(ROW_TILE):
        ref[pl.ds(c, rows, stride=ROW_TILE), :] = x[:, c * LANES:(c + 1) * LANES]


def _load_row_tiles(ref, rows):
    return jnp.concatenate([ref[pl.ds(c, rows, stride=ROW_TILE), :] for c in range(ROW_TILE)], axis=1)


def _dispatch_kernel(d1_ref, d2_ref, plo_ref, pn_ref, nu_ref, h_ref, g_ref, xs_ref,
                     buf_ref, zero_ref, sem, fill_sem):
    i = pl.program_id(0)
    nsteps = pl.num_programs(0)
    tm = h_ref.shape[0]
    rt = ROW_TILE
    rb = zero_ref.shape[0] // rt
    n_blk = xs_ref.shape[0] // (rb * rt)
    slot = i % 2
    base = i * tm

    def slot_drain(s):
        cp = pltpu.make_async_copy(buf_ref.at[s], xs_ref.at[pl.ds(0, tm * rt), :], sem.at[s])
        cp.wait()
        cp.wait()

    @pl.when(i >= 2)
    def _():
        slot_drain(slot)

    _store_row_tiles(buf_ref.at[slot], _rmsnorm(h_ref[...], g_ref[...]))

    def issue(r8, c):
        for k in range(MOE_ISSUE_UNROLL):
            r = r8 * MOE_ISSUE_UNROLL + k
            src = buf_ref.at[slot, _row_tile(r * rt), :]
            pltpu.make_async_copy(src, xs_ref.at[_row_tile(d1_ref[base + r]), :], sem.at[slot]).start(priority=0)
            pltpu.make_async_copy(src, xs_ref.at[_row_tile(d2_ref[base + r]), :], sem.at[slot]).start(priority=1)
        return c

    lax.fori_loop(0, tm // MOE_ISSUE_UNROLL, issue, 0)

    def pad_copy(off, rows):
        return pltpu.make_async_copy(zero_ref.at[pl.ds(0, rows * rt), :],
                                     xs_ref.at[pl.ds(pl.multiple_of(off, rt), rows * rt), :], fill_sem)

    pad_sizes = [1 << b for b in reversed(range(rb.bit_length() - 1))]

    def blk_copy(b):
        return pltpu.make_async_copy(zero_ref, xs_ref.at[pl.ds(pl.multiple_of(b * (rb * rt), rt), rb * rt), :],
                                     fill_sem)

    @pl.when(i == 0)
    def _():
        zero_ref[...] = jnp.zeros_like(zero_ref)

    @pl.when(i < N_EXPERTS)
    def _():
        off = plo_ref[i]
        npad = pn_ref[i]
        for rows in pad_sizes:
            has = (npad & rows) != 0

            @pl.when(has)
            def _(off=off, rows=rows):
                pad_copy(off, rows).start()

            off = off + jnp.where(has, rows * rt, 0)

    @pl.when(i == N_EXPERTS)
    def _():
        def fill(b, c):
            blk_copy(b).start()
            return c

        lax.fori_loop(nu_ref[0], n_blk, fill, 0)

    @pl.when(i == nsteps - 1)
    def _():
        slot_drain(1 - slot)
        slot_drain(slot)

        def per_expert(e, c):
            npad = pn_ref[e]
            for rows in pad_sizes:
                @pl.when((npad & rows) != 0)
                def _(rows=rows):
                    pad_copy(0, rows).wait()

            return c

        lax.fori_loop(0, N_EXPERTS, per_expert, 0)

        def one_blk(b, c):
            blk_copy(0).wait()
            return c

        lax.fori_loop(nu_ref[0], n_blk, one_blk, 0)


def _dispatch(d1, d2, pad_lo, pad_n, n_used, h, g, cap, tm):
    n, d = h.shape
    assert n // tm > N_EXPERTS + 1 and d == ROW_TILE * LANES
    return pl.pallas_call(
        _dispatch_kernel,
        out_shape=jax.ShapeDtypeStruct((cap * ROW_TILE, LANES), F32),
        grid_spec=pltpu.PrefetchScalarGridSpec(
            num_scalar_prefetch=5,
            grid=(n // tm,),
            in_specs=[
                pl.BlockSpec((tm, d), lambda i, *_: (i, 0)),
                pl.BlockSpec((1, d), lambda i, *_: (0, 0)),
            ],
            out_specs=pl.BlockSpec(memory_space=pl.ANY),
            scratch_shapes=[pltpu.VMEM((2, tm * ROW_TILE, LANES), F32),
                            pltpu.VMEM((MOE_ROWS * ROW_TILE, LANES), F32),
                            pltpu.SemaphoreType.DMA((2,)), pltpu.SemaphoreType.DMA],
        ),
        compiler_params=_cparams("arbitrary"),
        name="moe_dispatch",
    )(d1, d2, pad_lo, pad_n, n_used, h, g.reshape(1, d))


def _expert_kernel(be_ref, nu_ref, nx_ref, x_ref, w1_hbm, w3_hbm, w2_hbm, y_ref,
                   wb1, wb3, wb2, w1c, w3c, w2c, wsem, wslot, *, layer):
    i = pl.program_id(0)
    used = i < nu_ref[0]
    e = be_ref[i]
    prev = be_ref[jnp.maximum(i - 1, 0)]
    fresh = jnp.logical_or(i == 0, e != prev)

    def weight_copies(ex, s):
        return [pltpu.make_async_copy(w1_hbm.at[layer, ex], wb1.at[s], wsem.at[s]),
                pltpu.make_async_copy(w3_hbm.at[layer, ex], wb3.at[s], wsem.at[s]),
                pltpu.make_async_copy(w2_hbm.at[layer, ex], wb2.at[s], wsem.at[s])]

    @pl.when(i == 0)
    def _():
        wslot[0] = 0
        for cp in weight_copies(e, 0):
            cp.start()

    @pl.when(jnp.logical_and(used, fresh))
    def _():
        s = wslot[0]
        for cp in weight_copies(e, s):
            cp.wait()
        nxt = nx_ref[e]

        @pl.when(nxt != e)
        def _():
            for cp in weight_copies(nxt, 1 - s):
                cp.start()

        w1c[...] = wb1[s].astype(BF16)
        w3c[...] = wb3[s].astype(BF16)
        w2c[...] = wb2[s].astype(BF16)
        wslot[0] = 1 - s

    @pl.when(used)
    def _():
        rb = x_ref.shape[0] // ROW_TILE
        x = _load_row_tiles(x_ref, rb).astype(BF16)
        hf = w1c.shape[1] // 2
        y = None
        for c in range(2):
            cs = slice(c * hf, (c + 1) * hf)
            a = jnp.dot(x, w1c[:, cs], preferred_element_type=F32)
            b = jnp.dot(x, w3c[:, cs], preferred_element_type=F32)
            act = (jax.nn.silu(a) * b).astype(BF16)
            part = jnp.dot(act, w2c[cs, :], preferred_element_type=F32)
            y = part if y is None else y + part
        _store_row_tiles(y_ref, y)

    @pl.when(jnp.logical_not(used))
    def _():
        y_ref[...] = jnp.zeros_like(y_ref)


def _experts(blk_e, n_used, next_e, xs, w1, w3, w2, layer):
    d = ROW_TILE * LANES
    rb = MOE_ROWS
    n_blk = xs.shape[0] // (rb * ROW_TILE)
    ff = w1.shape[3]
    blk = (rb * ROW_TILE, LANES)
    hbm = pl.BlockSpec(memory_space=pl.ANY)
    return pl.pallas_call(
        functools.partial(_expert_kernel, layer=layer),
        out_shape=jax.ShapeDtypeStruct(xs.shape, F32),
        grid_spec=pltpu.PrefetchScalarGridSpec(
            num_scalar_prefetch=3,
            grid=(n_blk,),
            in_specs=[
                pl.BlockSpec(blk, lambda i, be, nu, nx: (jnp.minimum(i, jnp.maximum(nu[0] - 1, 0)), 0)),
                hbm, hbm, hbm,
            ],
            out_specs=pl.BlockSpec(blk, lambda i, be, nu, nx: (i, 0)),
            scratch_shapes=[pltpu.VMEM((2, d, ff), F32), pltpu.VMEM((2, d, ff), F32), pltpu.VMEM((2, ff, d), F32),
                            pltpu.VMEM((d, ff), BF16), pltpu.VMEM((d, ff), BF16), pltpu.VMEM((ff, d), BF16),
                            pltpu.SemaphoreType.DMA((2,)), pltpu.SMEM((1,), jnp.int32)],
        ),
        compiler_params=_cparams("arbitrary"),
        name="moe_experts",
    )(blk_e, n_used, next_e, xs, w1, w3, w2)


def _combine_kernel(d1_ref, d2_ref, h_ref, meta_ref, g_ref, ys_ref, o_ref, buf_ref, sem, *, final_norm):
    i = pl.program_id(0)
    nsteps = pl.num_programs(0)
    tm = h_ref.shape[0]
    slot = i % 2

    def fetch(tile, s):
        base = tile * tm

        def issue(r8, c):
            for k in range(MOE_ISSUE_UNROLL):
                r = r8 * MOE_ISSUE_UNROLL + k
                pltpu.make_async_copy(ys_ref.at[_row_tile(d1_ref[base + r]), :],
                                      buf_ref.at[s, 0, _row_tile(r * ROW_TILE), :], sem.at[s]).start(priority=0)
                pltpu.make_async_copy(ys_ref.at[_row_tile(d2_ref[base + r]), :],
                                      buf_ref.at[s, 1, _row_tile(r * ROW_TILE), :], sem.at[s]).start(priority=1)
            return c

        lax.fori_loop(0, tm // MOE_ISSUE_UNROLL, issue, 0)

    @pl.when(i == 0)
    def _():
        fetch(0, 0)

    @pl.when(i + 1 < nsteps)
    def _():
        fetch(i + 1, 1 - slot)

    for j in range(2):
        pltpu.make_async_copy(ys_ref.at[pl.ds(0, tm * ROW_TILE), :], buf_ref.at[slot, j], sem.at[slot]).wait()

    meta = meta_ref[...]
    y1 = _load_row_tiles(buf_ref.at[slot, 0], tm)
    y2 = _load_row_tiles(buf_ref.at[slot, 1], tm)
    out = h_ref[...] + (meta[:, 2:3] * y1 + meta[:, 3:4] * y2)
    if final_norm:
        out = _rmsnorm(out, g_ref[...])
    o_ref[...] = out


def _combine(d1, d2, h, meta, ys, g_final, tm, final_norm):
    n, d = h.shape
    return pl.pallas_call(
        functools.partial(_combine_kernel, final_norm=final_norm),
        out_shape=jax.ShapeDtypeStruct((n, d), F32),
        grid_spec=pltpu.PrefetchScalarGridSpec(
            num_scalar_prefetch=2,
            grid=(n // tm,),
            in_specs=[
                pl.BlockSpec((tm, d), lambda i, *_: (i, 0)),
                pl.BlockSpec((tm, LANES), lambda i, *_: (i, 0)),
                pl.BlockSpec((1, d), lambda i, *_: (0, 0)),
                pl.BlockSpec(memory_space=pl.ANY),
            ],
            out_specs=pl.BlockSpec((tm, d), lambda i, *_: (i, 0)),
            scratch_shapes=[pltpu.VMEM((2, 2, tm * ROW_TILE, LANES), F32), pltpu.SemaphoreType.DMA((2,))],
        ),
        compiler_params=_cparams("arbitrary"),
        name="moe_combine",
    )(d1, d2, h, meta, g_final.reshape(1, d), ys)


def _hier_moe(h, routing, ln_g, w1, w3, w2, layer, g_final, final_norm):
    n, d = h.shape
    rb = MOE_ROWS
    cap = 2 * n + N_EXPERTS * rb
    n_blk = cap // rb
    meta, idx, counts = routing
    cnt = counts[0, :N_EXPERTS].astype(jnp.int32)
    padded = ((cnt + rb - 1) // rb) * rb
    pad_end = jnp.cumsum(padded)
    pad_start = pad_end - padded
    blk_start = jnp.arange(n_blk, dtype=jnp.int32) * rb
    blk_e = jnp.minimum(jnp.sum(pad_end[None, :] <= blk_start[:, None], axis=1), N_EXPERTS - 1).astype(jnp.int32)
    n_used = (pad_end[-1] // rb).astype(jnp.int32).reshape(1)
    experts = jnp.arange(N_EXPERTS, dtype=jnp.int32)[None, :]
    start_of = lambda e: jnp.sum(jnp.where(e[:, None] == experts, pad_start[None, :], 0), axis=1)
    d1 = (start_of(idx[0]) + idx[2]) * ROW_TILE
    d2 = (start_of(idx[1]) + idx[3]) * ROW_TILE
    xs = _dispatch(d1, d2, (pad_start + cnt) * ROW_TILE, padded - cnt, n_used, h, ln_g, cap, MOE_DISPATCH_ROWS)
    later = (experts > experts.T) & (padded > 0)[None, :]
    next_e = jnp.min(jnp.where(later, experts, N_EXPERTS), axis=1)
    next_e = jnp.where(next_e == N_EXPERTS, experts[0], next_e).astype(jnp.int32)
    ys = _experts(blk_e, n_used, next_e, xs, w1, w3, w2, layer)
    return _combine(d1, d2, h, meta, ys, g_final, MOE_COMBINE_ROWS, final_norm)


def kernel(x, ln_mix, ln_ffn, ln_final, ssm_w_in, ssm_lam_re, ssm_lam_im, ssm_log_dt, ssm_b_re, ssm_b_im, ssm_c_re, ssm_c_im, ssm_d, ssm_w_out, attn_w_qkv, attn_w_o, moe_w_group, moe_w_expert, moe_w1, moe_w3, moe_w2):
    bsz, seq, d = x.shape
    n = bsz * seq
    h = x.reshape(n, d)

    u = _norm_proj(h, ln_mix[0], ssm_w_in[0].astype(BF16), DENSE_ROWS, F32, "s5_in_proj")
    mats = _s5_mats(ssm_lam_re[0], ssm_lam_im[0], ssm_log_dt[0], ssm_b_re[0], ssm_b_im[0],
                    ssm_c_re[0], ssm_c_im[0])
    z = _s5_core(u, mats, ssm_d[0], bsz, seq)
    h, routing = _glu_out(z, ssm_w_out[0].astype(BF16), h, ln_ffn[0],
                          _router_weights(moe_w_group[0], moe_w_expert[0]), MIXER_OUT_ROWS)
    h = _hier_moe(h, routing, ln_ffn[0], moe_w1, moe_w3, moe_w2, 0, ln_final, False)

    q_t, k, v_t = _qkv_proj(h, ln_mix[1], attn_w_qkv[0], DENSE_ROWS)
    o_t = _moba(q_t, k, v_t, bsz, seq)
    h, routing = _proj_res_t(o_t, attn_w_o[0].astype(BF16), h, ln_ffn[1],
                             _router_weights(moe_w_group[1], moe_w_expert[1]), MIXER_OUT_ROWS)
    h = _hier_moe(h, routing, ln_ffn[1], moe_w1, moe_w3, moe_w2, 1, ln_final, True)
    return h.reshape(bsz, seq, d)
```

```python
import functools
import math

import jax
import jax.numpy as jnp
from jax import lax
from jax.experimental import pallas as pl
from jax.experimental.pallas import tpu as pltpu

F32 = jnp.float32
BF16 = jnp.bfloat16

D_MODEL = 1024
RMS_EPS = 1e-6
NEG_INF = -1e30

SSM_GROUP = 16
SSM_GROUPS = D_MODEL // SSM_GROUP
SSM_STATE = 64
SSM_CHUNK = 16
SSM_GB = 8
SSM_WIN = 8
SSM_POW_ROWS = 24

ATT_HEADS = 8
HEAD_DIM = 128
MOBA_BLOCK = 256
MOBA_TOPK = 3

N_GROUPS = 4
EXPERTS_PER_GROUP = 8
N_EXPERTS = 32
MOE_ROWS = 512
MOE_DISPATCH_ROWS = 256
MOE_COMBINE_ROWS = 256
DENSE_ROWS = 1024
MIXER_OUT_ROWS = 1024

LANES = 128
VMEM_LIMIT = 56 * 1024 * 1024

_NT = (((1,), (1,)), ((), ()))
_TN = (((0,), (0,)), ((), ()))


def _cparams(*sem):
    return pltpu.CompilerParams(dimension_semantics=sem, vmem_limit_bytes=VMEM_LIMIT)


def _rmsnorm(x, g):
    return x * lax.rsqrt(jnp.mean(x * x, axis=-1, keepdims=True) + RMS_EPS) * g


def _norm_proj_kernel(x_ref, g_ref, w_ref, o_ref):
    xn = _rmsnorm(x_ref[...], g_ref[...]).astype(BF16)
    o_ref[...] = jnp.dot(xn, w_ref[...], preferred_element_type=F32).astype(o_ref.dtype)


def _norm_proj(x, g, w_bf16, tm, out_dtype, name):
    n, d = x.shape
    dout = w_bf16.shape[1]
    return pl.pallas_call(
        _norm_proj_kernel,
        out_shape=jax.ShapeDtypeStruct((n, dout), out_dtype),
        grid=(n // tm,),
        in_specs=[
            pl.BlockSpec((tm, d), lambda i: (i, 0)),
            pl.BlockSpec((1, d), lambda i: (0, 0)),
            pl.BlockSpec((d, dout), lambda i: (0, 0)),
        ],
        out_specs=pl.BlockSpec((tm, dout), lambda i: (i, 0)),
        compiler_params=_cparams("parallel"),
        name=name,
    )(x, g.reshape(1, d), w_bf16)


def _s5_mats(lam_re, lam_im, log_dt, b_re, b_im, c_re, c_im):
    g_, p_, c_, t_ = SSM_GROUPS, SSM_STATE, SSM_GROUP, SSM_CHUNK
    lr = jnp.minimum(lam_re, -1e-4)
    li = lam_im
    dt = jnp.exp(log_dt)[:, None]
    mag = jnp.exp(lr * dt)
    abar_re = mag * jnp.cos(li * dt)
    abar_im = mag * jnp.sin(li * dt)
    den = lr * lr + li * li
    nr = abar_re - 1.0
    gam_re = (nr * lr + abar_im * li) / den
    gam_im = (abar_im * lr - nr * li) / den
    bb_re = gam_re[..., None] * b_re - gam_im[..., None] * b_im
    bb_im = gam_re[..., None] * b_im + gam_im[..., None] * b_re

    def powers(ns):
        nf = jnp.asarray(ns, F32)[None, :, None]
        pm = jnp.exp(nf * (lr * dt)[:, None, :])
        ang = nf * (li * dt)[:, None, :]
        return pm * jnp.cos(ang), pm * jnp.sin(ang)

    pr, pi = powers(list(range(t_ + 1)))
    pad = jnp.zeros((g_, SSM_POW_ROWS - (t_ + 1), 2 * p_), F32)
    pw = jnp.stack([jnp.concatenate([jnp.concatenate([pr, pr], -1), pad], axis=1),
                    jnp.concatenate([jnp.concatenate([pi, pi], -1), pad], axis=1)], axis=1)
    bt_re = bb_re.transpose(0, 2, 1)
    bt_im = bb_im.transpose(0, 2, 1)
    cat = lambda a, b: jnp.concatenate([a, b], axis=-1)
    fac = jnp.stack([cat(bt_re, bt_im), cat(-bt_im, bt_re), cat(bt_re, -bt_im),
                     cat(c_re, c_im), cat(-c_im, c_re)], axis=1)
    qr, qi = powers([t_ * (1 << j) for j in range(4)])
    rows = []
    for j in range(4):
        ar, ai = qr[:, j], qi[:, j]
        rows += [jnp.concatenate([ar, ar], -1), jnp.concatenate([-ai, ai], -1),
                 jnp.concatenate([ai, -ai], -1)]
    rows += [jnp.zeros_like(rows[0])] * 4
    coef = jnp.stack(rows, axis=1)
    return fac.astype(F32), pw.astype(F32), coef.astype(F32)


def _s5_perm():
    r = jnp.arange(8 * LANES)
    col = ((r % LANES) // SSM_GROUP) * LANES + (r // LANES) * SSM_GROUP + r % SSM_GROUP
    p = (col[:, None] == r[None, :]).astype(BF16)
    return p, p.T


def _s5_kernel(u_ref, fac_ref, pw_ref, coef_ref, d_ref, p_ref, pt_ref, o_ref,
               m_ref, w_ref, v_ref, vf_ref, zf_ref, ea_ref, eb_ref, sp_ref):
    t_, c_, gb, win = SSM_CHUNK, SSM_GROUP, SSM_GB, SSM_WIN
    nk = u_ref.shape[0] // t_
    p2 = 2 * SSM_STATE
    tc = t_ * c_

    @pl.when(pl.program_id(1) == 0)
    def _():
        lane_tc = lax.broadcasted_iota(jnp.int32, (c_, tc), 1)
        conj = jnp.where(lax.broadcasted_iota(jnp.int32, (1, p2), 1) < SSM_STATE, 1.0, -1.0)
        for g in range(gb):
            b_t, ib_t, bconj_t, cm, icm = (fac_ref[g, k] for k in range(5))
            pr, pi = pw_ref[g, 0], pw_ref[g, 1]
            for s in range(t_):
                n = t_ - 1 - s
                w_ref[g, s * c_:(s + 1) * c_, :] = (b_t * pr[n:n + 1] + ib_t * pi[n:n + 1]).astype(BF16)
            ca = jnp.concatenate([cm * pr[n:n + 1] + icm * pi[n:n + 1] for n in range(t_ + 1)], axis=0)
            v_ref[g] = jnp.transpose(ca[c_:] * conj).astype(BF16)
            kf = lax.dot_general(bconj_t, ca[:tc], _NT, preferred_element_type=F32,
                                 precision=lax.Precision.HIGHEST)
            for s in range(t_):
                rolled = kf if s == 0 else pltpu.roll(kf, s * c_, axis=1)
                m_ref[g, s * c_:(s + 1) * c_, :] = jnp.where(lane_tc >= s * c_, rolled, 0.0).astype(BF16)

    halves = tc // LANES

    def timestep(t):
        return pl.ds(t, nk, stride=t_)

    def flat(g):
        return jnp.concatenate([vf_ref[h, :, g * LANES:(g + 1) * LANES] for h in range(halves)], axis=1)

    for h in range(halves):
        x = jnp.concatenate([u_ref[timestep(8 * h + i), :].astype(BF16) for i in range(8)], axis=1)
        vf_ref[h] = jnp.dot(x, p_ref[...], preferred_element_type=F32).astype(BF16)

    row = lax.broadcasted_iota(jnp.int32, (nk, p2), 0)

    def shift(x, d):
        return jnp.where(row < d, 0.0, pltpu.roll(x, d, axis=0))

    def swap(x):
        return pltpu.roll(x, SSM_STATE, axis=1)

    for g in range(gb):
        sc = jnp.dot(flat(g), w_ref[g], preferred_element_type=F32)
        xa = shift(sc, 1)
        cf = coef_ref[g]
        for lvl in range(3):
            pp, qa = cf[3 * lvl:3 * lvl + 1], cf[3 * lvl + 1:3 * lvl + 2]
            sa = shift(xa, 1 << lvl)
            xa = xa + pp * sa + qa * swap(sa)
        ea_ref[g] = xa
        eb_ref[g] = swap(xa)

    cfs = [coef_ref[g] for g in range(gb)]
    zero = jnp.zeros((win, p2), F32)
    state = [(zero, zero)] * gb
    for j in range(nk // win):
        rs = slice(j * win, (j + 1) * win)
        for g in range(gb):
            pp, qa, qb = cfs[g][9:10], cfs[g][10:11], cfs[g][11:12]
            s_a, s_b = state[g]
            n_a = pp * s_a + qa * s_b + ea_ref[g, rs, :]
            n_b = pp * s_b + qb * s_a + eb_ref[g, rs, :]
            sp_ref[g, rs, :] = n_a
            state[g] = (n_a, n_b)

    for g in range(gb):
        x = flat(g)
        y = jnp.dot(x, m_ref[g], preferred_element_type=F32)
        y = y + jnp.dot(sp_ref[g].astype(BF16), v_ref[g], preferred_element_type=F32)
        z = jax.nn.gelu(y + d_ref[g] * x.astype(F32)).astype(BF16)
        for h in range(halves):
            zf_ref[h, :, g * LANES:(g + 1) * LANES] = z[:, h * LANES:(h + 1) * LANES]

    for h in range(halves):
        zn = jnp.dot(zf_ref[h], pt_ref[...], preferred_element_type=F32)
        for i in range(8):
            o_ref[timestep(8 * h + i), :] = zn[:, i * LANES:(i + 1) * LANES]


def _s5_core(u, mats, d_skip, bsz, seq):
    fac, pw, coef = mats
    n, d = u.shape
    gb, tc, p2 = SSM_GB, SSM_CHUNK * SSM_GROUP, 2 * SSM_STATE
    nk = seq // SSM_CHUNK
    halves = tc // LANES
    perm, perm_t = _s5_perm()
    dflat = jnp.tile(d_skip.reshape(SSM_GROUPS, 1, SSM_GROUP), (1, 1, SSM_CHUNK))
    spec3 = lambda a, b: pl.BlockSpec((gb, a, b), lambda j, bb: (j, 0, 0))
    const = pl.BlockSpec(perm.shape, lambda j, bb: (0, 0))
    return pl.pallas_call(
        _s5_kernel,
        out_shape=jax.ShapeDtypeStruct((n, d), F32),
        grid=(SSM_GROUPS // gb, bsz),
        in_specs=[
            pl.BlockSpec((seq, LANES), lambda j, bb: (bb, j)),
            pl.BlockSpec((gb,) + fac.shape[1:], lambda j, bb: (j, 0, 0, 0)),
            pl.BlockSpec((gb,) + pw.shape[1:], lambda j, bb: (j, 0, 0, 0)),
            spec3(16, p2), spec3(1, tc),
            const, const,
        ],
        out_specs=pl.BlockSpec((seq, LANES), lambda j, bb: (bb, j)),
        scratch_shapes=[pltpu.VMEM((gb, tc, tc), BF16), pltpu.VMEM((gb, tc, p2), BF16),
                        pltpu.VMEM((gb, p2, tc), BF16),
                        pltpu.VMEM((halves, nk, gb * LANES), BF16),
                        pltpu.VMEM((halves, nk, gb * LANES), BF16),
                        pltpu.VMEM((gb, nk, p2), F32), pltpu.VMEM((gb, nk, p2), F32),
                        pltpu.VMEM((gb, nk, p2), F32)],
        compiler_params=_cparams("arbitrary", "arbitrary"),
        name="s5_core",
    )(u, fac, pw, coef, dflat, perm, perm_t)


def _glu_out_kernel(z_ref, w_ref, x_ref, g_ref, wr_ref, tri_ref, o_ref, meta_ref, idx_ref, cnt_ref):
    vg = jnp.dot(z_ref[...].astype(BF16), w_ref[...], preferred_element_type=F32)
    d = o_ref.shape[1]
    h = x_ref[...] + vg[:, :d] * jax.nn.sigmoid(vg[:, d:])
    o_ref[...] = h
    _route(h, g_ref, wr_ref, tri_ref, meta_ref, idx_ref, cnt_ref)


def _proj_res_t_kernel(at_ref, w_ref, x_ref, g_ref, wr_ref, tri_ref, o_ref, meta_ref, idx_ref, cnt_ref):
    h = x_ref[...] + lax.dot_general(at_ref[...], w_ref[...], _TN, preferred_element_type=F32)
    o_ref[...] = h
    _route(h, g_ref, wr_ref, tri_ref, meta_ref, idx_ref, cnt_ref)


def _mixer_out(kernel_fn, a, a_spec, w_bf16, x, ln_g, w_router, tm, name):
    n, d = x.shape
    tri = (jnp.arange(tm)[:, None] > jnp.arange(tm)[None, :]).astype(BF16)
    full = lambda arr: pl.BlockSpec(arr.shape, lambda i: (0, 0))
    h, meta, idx, counts = pl.pallas_call(
        kernel_fn,
        out_shape=(jax.ShapeDtypeStruct((n, d), F32), jax.ShapeDtypeStruct((n, LANES), F32),
                   jax.ShapeDtypeStruct((8, n), jnp.int32), jax.ShapeDtypeStruct((1, LANES), F32)),
        grid=(n // tm,),
        in_specs=[a_spec, full(w_bf16), pl.BlockSpec((tm, d), lambda i: (i, 0)),
                  pl.BlockSpec((1, d), lambda i: (0, 0)), full(w_router), full(tri)],
        out_specs=(pl.BlockSpec((tm, d), lambda i: (i, 0)), pl.BlockSpec((tm, LANES), lambda i: (i, 0)),
                   pl.BlockSpec((8, tm), lambda i: (0, i)), pl.BlockSpec((1, LANES), lambda i: (0, 0))),
        compiler_params=_cparams("arbitrary"),
        name=name,
    )(a, w_bf16, x, ln_g.reshape(1, d), w_router, tri)
    return h, (meta, idx, counts)


def _glu_out(z, w_bf16, x, ln_g, w_router, tm):
    spec = pl.BlockSpec((tm, z.shape[1]), lambda i: (i, 0))
    return _mixer_out(_glu_out_kernel, z, spec, w_bf16, x, ln_g, w_router, tm, "s5_glu_out")


def _proj_res_t(a_t, w_bf16, x, ln_g, w_router, tm):
    spec = pl.BlockSpec((a_t.shape[0], tm), lambda i: (0, i))
    return _mixer_out(_proj_res_t_kernel, a_t, spec, w_bf16, x, ln_g, w_router, tm, "attn_out_proj")


def _qkv_kernel(x_ref, g_ref, wq_ref, wk_ref, wv_ref, qt_ref, k_ref, vt_ref):
    xn = _rmsnorm(x_ref[...], g_ref[...]).astype(BF16)
    c = (HEAD_DIM ** -0.5) * math.log2(math.e)
    qt_ref[...] = (lax.dot_general(wq_ref[...], xn, _NT, preferred_element_type=F32) * c).astype(BF16)
    k_ref[...] = jnp.dot(xn, wk_ref[...], preferred_element_type=F32).astype(BF16)
    vt = lax.dot_general(wv_ref[...], xn, _NT, preferred_element_type=F32).astype(BF16)
    for c in range(vt_ref.shape[0]):
        vt_ref[c] = vt[:, c * MOBA_BLOCK:(c + 1) * MOBA_BLOCK]


def _qkv_proj(x, g, w_qkv, tm):
    n, d = x.shape
    da = ATT_HEADS * HEAD_DIM
    wq_t = w_qkv[:, :da].T.astype(BF16)
    wk = w_qkv[:, da:2 * da].astype(BF16)
    wv_t = w_qkv[:, 2 * da:].T.astype(BF16)
    full = lambda shp: pl.BlockSpec(shp, lambda i: (0, 0))
    tb = tm // MOBA_BLOCK
    return pl.pallas_call(
        _qkv_kernel,
        out_shape=(jax.ShapeDtypeStruct((da, n), BF16), jax.ShapeDtypeStruct((n, da), BF16),
                   jax.ShapeDtypeStruct((n // MOBA_BLOCK, da, MOBA_BLOCK), BF16)),
        grid=(n // tm,),
        in_specs=[pl.BlockSpec((tm, d), lambda i: (i, 0)), full((1, d)),
                  full((da, d)), full((d, da)), full((da, d))],
        out_specs=(pl.BlockSpec((da, tm), lambda i: (0, i)), pl.BlockSpec((tm, da), lambda i: (i, 0)),
                   pl.BlockSpec((tb, da, MOBA_BLOCK), lambda i: (i, 0, 0))),
        compiler_params=_cparams("parallel"),
        name="attn_qkv_proj",
    )(x, g.reshape(1, d), wq_t, wk, wv_t)


MOBA_CHAINS = 8
MOBA_SUM_ROWS = 16


def _moba_kernel(qt_ref, k_ref, vt_ref, et_ref, ot_ref, s_scr, acc_scr, m_scr, *, nb):
    blk, dh = MOBA_BLOCK, HEAD_DIM
    brow = lax.broadcasted_iota(jnp.int32, (nb, blk), 0).astype(F32)
    r_ix = lax.broadcasted_iota(jnp.int32, (blk, blk), 0)
    c_ix = lax.broadcasted_iota(jnp.int32, (blk, blk), 1)
    zpad = jnp.zeros((2 * LANES - dh - nb, blk), BF16)
    kmean = jnp.mean(k_ref[...].astype(F32).reshape(nb, blk, dh), axis=1)
    km_hi = kmean.astype(BF16)
    kmean2 = jnp.concatenate([km_hi, (kmean - km_hi.astype(F32)).astype(BF16)], axis=0)

    def k_aug(t):
        rows = slice(t * blk, (t + 1) * blk)
        return jnp.concatenate([k_ref[rows, :], et_ref[rows, :]], axis=1)

    ones = jnp.ones((MOBA_SUM_ROWS, blk), BF16)

    def values(t):
        return jnp.concatenate([vt_ref[t], ones], axis=0)

    def absorb(c, s, vt):
        m_old = m_scr[c]
        m_new = jnp.maximum(m_old, jnp.max(s, axis=0, keepdims=True))
        p = jnp.exp2(s - m_new).astype(BF16)
        acc_scr[c] = jnp.exp2(m_old - m_new) * acc_scr[c] + jnp.dot(vt, p, preferred_element_type=F32)
        m_scr[c] = m_new

    def query_block(i):
        qs = slice(i * blk, (i + 1) * blk)
        qt = qt_ref[:, qs]
        if i > MOBA_TOPK:
            g2 = jnp.dot(kmean2, qt, preferred_element_type=F32)
            gate = g2[:nb] + g2[nb:]
            gate = jnp.where(brow < i, gate, NEG_INF)
            sel = brow >= i
            for _ in range(MOBA_TOPK):
                gm = jnp.max(gate, axis=0, keepdims=True)
                first = jnp.min(jnp.where(gate == gm, brow, float(nb)), axis=0, keepdims=True)
                pick = brow == first
                sel = jnp.logical_or(sel, pick)
                gate = jnp.where(pick, -jnp.inf, gate)
            bias = jnp.where(sel, 0.0, NEG_INF).astype(BF16)
        else:
            bias = jnp.zeros((nb, blk), BF16)
        return jnp.concatenate([qt, bias, zpad], axis=0)

    nq = MOBA_CHAINS
    for i0 in range(0, nb, nq):
        chains = range(min(nq, nb - i0))
        q_aug = [query_block(i0 + c) for c in chains]
        m_scr[...] = jnp.full(m_scr.shape, NEG_INF, F32)
        acc_scr[...] = jnp.zeros(acc_scr.shape, F32)

        def scores(t, slot, cs):
            ka = k_aug(t)
            for c in cs:
                s_scr[c, slot] = jnp.dot(ka, q_aug[c], preferred_element_type=F32)

        scores(0, 0, chains)
        for j in range(i0):
            cur = [s_scr[c, j % 2] for c in chains]
            scores(j + 1, (j + 1) % 2, chains)
            vt = values(j)
            for c in chains:
                absorb(c, cur[c], vt)
        for t in chains:
            cur = {c: s_scr[c, (i0 + t) % 2] for c in chains if c >= t}
            later = [c for c in chains if c > t]
            if later:
                scores(i0 + t + 1, (i0 + t + 1) % 2, later)
            vt = values(i0 + t)
            absorb(t, jnp.where(r_ix <= c_ix, cur[t], NEG_INF), vt)
            for c in later:
                absorb(c, cur[c], vt)
        for c in chains:
            qs = slice((i0 + c) * blk, (i0 + c + 1) * blk)
            ot_ref[:, qs] = (acc_scr[c, :dh, :] / acc_scr[c, dh:dh + 1, :]).astype(ot_ref.dtype)


def _moba(q_t, k, v_t, bsz, seq):
    nb = seq // MOBA_BLOCK
    da = ATT_HEADS * HEAD_DIM
    e_t = ((jnp.arange(seq) // MOBA_BLOCK)[:, None] == jnp.arange(LANES)[None, :]).astype(BF16)
    tspec = pl.BlockSpec((HEAD_DIM, seq), lambda b, h: (h, b))
    return pl.pallas_call(
        functools.partial(_moba_kernel, nb=nb),
        out_shape=jax.ShapeDtypeStruct((da, bsz * seq), BF16),
        grid=(bsz, ATT_HEADS),
        in_specs=[tspec, pl.BlockSpec((seq, HEAD_DIM), lambda b, h: (b, h)),
                  pl.BlockSpec((nb, HEAD_DIM, MOBA_BLOCK), lambda b, h: (b, h, 0)),
                  pl.BlockSpec((seq, LANES), lambda b, h: (0, 0))],
        out_specs=tspec,
        scratch_shapes=[pltpu.VMEM((MOBA_CHAINS, 2, MOBA_BLOCK, MOBA_BLOCK), F32),
                        pltpu.VMEM((MOBA_CHAINS, HEAD_DIM + MOBA_SUM_ROWS, MOBA_BLOCK), F32),
                        pltpu.VMEM((MOBA_CHAINS, 1, MOBA_BLOCK), F32)],
        compiler_params=_cparams("parallel", "parallel"),
        name="moba_attn",
    )(q_t, k, v_t, e_t)


def _router_weights(w_group, w_expert):
    d = w_group.shape[0]
    w_r = jnp.zeros((d, LANES), F32).at[:, :N_GROUPS].set(w_group)
    w_r = w_r.at[:, N_GROUPS:N_GROUPS + N_EXPERTS].set(w_expert)
    w_hi = w_r.astype(BF16)
    w_lo = (w_r - w_hi.astype(F32)).astype(BF16)
    return jnp.concatenate([w_hi, w_lo], axis=1)


def _route(h, g_ref, w_ref, tri_ref, meta_ref, idx_ref, cnt_ref):
    xn = _rmsnorm(h, g_ref[...])
    x_hi = xn.astype(BF16)
    x_lo = (xn - x_hi.astype(F32)).astype(BF16)
    t = jnp.dot(x_hi, w_ref[...], preferred_element_type=F32)
    logits = (t[:, :LANES] + t[:, LANES:]) + jnp.dot(x_lo, w_ref[:, :LANES], preferred_element_type=F32)
    tm = logits.shape[0]
    lane = lax.broadcasted_iota(jnp.int32, (tm, LANES), 1).astype(F32)
    ninf = -jnp.inf
    lg = jnp.where(lane < N_GROUPS, logits, ninf)
    gm = jnp.max(lg, axis=-1, keepdims=True)
    g_idx = jnp.min(jnp.where(lg == gm, lane, float(LANES)), axis=-1, keepdims=True)
    g_gate = 1.0 / jnp.sum(jnp.exp(lg - gm), axis=-1, keepdims=True)
    lo = N_GROUPS + EXPERTS_PER_GROUP * g_idx
    le = jnp.where((lane >= lo) & (lane < lo + EXPERTS_PER_GROUP), logits, ninf)
    m1 = jnp.max(le, axis=-1, keepdims=True)
    i1 = jnp.min(jnp.where(le == m1, lane, float(LANES)), axis=-1, keepdims=True)
    le2 = jnp.where(lane == i1, ninf, le)
    m2 = jnp.max(le2, axis=-1, keepdims=True)
    i2 = jnp.min(jnp.where(le2 == m2, lane, float(LANES)), axis=-1, keepdims=True)
    p2 = jnp.exp(m2 - m1)
    gate1 = g_gate / (1.0 + p2)
    gate2 = g_gate * p2 / (1.0 + p2)
    e1 = i1 - N_GROUPS
    e2 = i2 - N_GROUPS
    meta = jnp.where(lane == 0, e1, jnp.where(lane == 1, e2, jnp.where(lane == 2, gate1,
                     jnp.where(lane == 3, gate2, 0.0))))
    meta_ref[...] = meta
    oh1 = (lane == e1).astype(F32)
    oh2 = (lane == e2).astype(F32)
    both = oh1 + oh2

    @pl.when(pl.program_id(0) == 0)
    def _():
        cnt_ref[...] = jnp.zeros_like(cnt_ref)

    seen = cnt_ref[...]
    before = jnp.dot(tri_ref[...], both.astype(BF16), preferred_element_type=F32) + seen
    r1 = jnp.sum(before * oh1, axis=-1, keepdims=True)
    r2 = jnp.sum(before * oh2, axis=-1, keepdims=True)
    slab = jnp.where(lane == 0, e1, jnp.where(lane == 1, e2, jnp.where(lane == 2, r1,
                     jnp.where(lane == 3, r2, 0.0))))
    idx_ref[...] = jnp.transpose(slab)[0:8, :].astype(jnp.int32)
    cnt_ref[...] = seen + jnp.sum(both, axis=0, keepdims=True)


MOE_ISSUE_UNROLL = 8
ROW_TILE = 8


def _row_tile(off):
    return pl.ds(pl.multiple_of(off, ROW_TILE), ROW_TILE)


def _store_row_tiles(ref, x):
    rows = x.shape[0]
    for c in range(ROW_TILE):
        ref[pl.ds(c, rows, stride=ROW_TILE), :] = x[:, c * LANES:(c + 1) * LANES]


def _load_row_tiles(ref, rows):
    return jnp.concatenate([ref[pl.ds(c, rows, stride=ROW_TILE), :] for c in range(ROW_TILE)], axis=1)


def _dispatch_kernel(d1_ref, d2_ref, plo_ref, pn_ref, nu_ref, h_ref, g_ref, xs_ref,
                     buf_ref, zero_ref, sem, fill_sem):
    i = pl.program_id(0)
    nsteps = pl.num_programs(0)
    tm = h_ref.shape[0]
    rt = ROW_TILE
    rb = zero_ref.shape[0] // rt
    n_blk = xs_ref.shape[0] // (rb * rt)
    slot = i % 2
    base = i * tm

    def slot_drain(s):
        cp = pltpu.make_async_copy(buf_ref.at[s], xs_ref.at[pl.ds(0, tm * rt), :], sem.at[s])
        cp.wait()
        cp.wait()

    @pl.when(i >= 2)
    def _():
        slot_drain(slot)

    _store_row_tiles(buf_ref.at[slot], _rmsnorm(h_ref[...], g_ref[...]))

    def issue(r8, c):
        for k in range(MOE_ISSUE_UNROLL):
            r = r8 * MOE_ISSUE_UNROLL + k
            src = buf_ref.at[slot, _row_tile(r * rt), :]
            pltpu.make_async_copy(src, xs_ref.at[_row_tile(d1_ref[base + r]), :], sem.at[slot]).start(priority=0)
            pltpu.make_async_copy(src, xs_ref.at[_row_tile(d2_ref[base + r]), :], sem.at[slot]).start(priority=1)
        return c

    lax.fori_loop(0, tm // MOE_ISSUE_UNROLL, issue, 0)

    def pad_copy(off, rows):
        return pltpu.make_async_copy(zero_ref.at[pl.ds(0, rows * rt), :],
                                     xs_ref.at[pl.ds(pl.multiple_of(off, rt), rows * rt), :], fill_sem)

    pad_sizes = [1 << b for b in reversed(range(rb.bit_length() - 1))]

    def blk_copy(b):
        return pltpu.make_async_copy(zero_ref, xs_ref.at[pl.ds(pl.multiple_of(b * (rb * rt), rt), rb * rt), :],
                                     fill_sem)

    @pl.when(i == 0)
    def _():
        zero_ref[...] = jnp.zeros_like(zero_ref)

    @pl.when(i < N_EXPERTS)
    def _():
        off = plo_ref[i]
        npad = pn_ref[i]
        for rows in pad_sizes:
            has = (npad & rows) != 0

            @pl.when(has)
            def _(off=off, rows=rows):
                pad_copy(off, rows).start()

            off = off + jnp.where(has, rows * rt, 0)

    @pl.when(i == N_EXPERTS)
    def _():
        def fill(b, c):
            blk_copy(b).start()
            return c

        lax.fori_loop(nu_ref[0], n_blk, fill, 0)

    @pl.when(i == nsteps - 1)
    def _():
        slot_drain(1 - slot)
        slot_drain(slot)

        def per_expert(e, c):
            npad = pn_ref[e]
            for rows in pad_sizes:
                @pl.when((npad & rows) != 0)
                def _(rows=rows):
                    pad_copy(0, rows).wait()

            return c

        lax.fori_loop(0, N_EXPERTS, per_expert, 0)

        def one_blk(b, c):
            blk_copy(0).wait()
            return c

        lax.fori_loop(nu_ref[0], n_blk, one_blk, 0)


def _dispatch(d1, d2, pad_lo, pad_n, n_used, h, g, cap, tm):
    n, d = h.shape
    assert n // tm > N_EXPERTS + 1 and d == ROW_TILE * LANES
    return pl.pallas_call(
        _dispatch_kernel,
        out_shape=jax.ShapeDtypeStruct((cap * ROW_TILE, LANES), F32),
        grid_spec=pltpu.PrefetchScalarGridSpec(
            num_scalar_prefetch=5,
            grid=(n // tm,),
            in_specs=[
                pl.BlockSpec((tm, d), lambda i, *_: (i, 0)),
                pl.BlockSpec((1, d), lambda i, *_: (0, 0)),
            ],
            out_specs=pl.BlockSpec(memory_space=pl.ANY),
            scratch_shapes=[pltpu.VMEM((2, tm * ROW_TILE, LANES), F32),
                            pltpu.VMEM((MOE_ROWS * ROW_TILE, LANES), F32),
                            pltpu.SemaphoreType.DMA((2,)), pltpu.SemaphoreType.DMA],
        ),
        compiler_params=_cparams("arbitrary"),
        name="moe_dispatch",
    )(d1, d2, pad_lo, pad_n, n_used, h, g.reshape(1, d))


def _expert_kernel(be_ref, nu_ref, nx_ref, x_ref, w1_hbm, w3_hbm, w2_hbm, y_ref,
                   wb1, wb3, wb2, w1c, w3c, w2c, wsem, wslot, *, layer):
    i = pl.program_id(0)
    used = i < nu_ref[0]
    e = be_ref[i]
    prev = be_ref[jnp.maximum(i - 1, 0)]
    fresh = jnp.logical_or(i == 0, e != prev)

    def weight_copies(ex, s):
        return [pltpu.make_async_copy(w1_hbm.at[layer, ex], wb1.at[s], wsem.at[s]),
                pltpu.make_async_copy(w3_hbm.at[layer, ex], wb3.at[s], wsem.at[s]),
                pltpu.make_async_copy(w2_hbm.at[layer, ex], wb2.at[s], wsem.at[s])]

    @pl.when(i == 0)
    def _():
        wslot[0] = 0
        for cp in weight_copies(e, 0):
            cp.start()

    @pl.when(jnp.logical_and(used, fresh))
    def _():
        s = wslot[0]
        for cp in weight_copies(e, s):
            cp.wait()
        nxt = nx_ref[e]

        @pl.when(nxt != e)
        def _():
            for cp in weight_copies(nxt, 1 - s):
                cp.start()

        w1c[...] = wb1[s].astype(BF16)
        w3c[...] = wb3[s].astype(BF16)
        w2c[...] = wb2[s].astype(BF16)
        wslot[0] = 1 - s

    @pl.when(used)
    def _():
        rb = x_ref.shape[0] // ROW_TILE
        x = _load_row_tiles(x_ref, rb).astype(BF16)
        hf = w1c.shape[1] // 2
        y = None
        for c in range(2):
            cs = slice(c * hf, (c + 1) * hf)
            a = jnp.dot(x, w1c[:, cs], preferred_element_type=F32)
            b = jnp.dot(x, w3c[:, cs], preferred_element_type=F32)
            act = (jax.nn.silu(a) * b).astype(BF16)
            part = jnp.dot(act, w2c[cs, :], preferred_element_type=F32)
            y = part if y is None else y + part
        _store_row_tiles(y_ref, y)

    @pl.when(jnp.logical_not(used))
    def _():
        y_ref[...] = jnp.zeros_like(y_ref)


def _experts(blk_e, n_used, next_e, xs, w1, w3, w2, layer):
    d = ROW_TILE * LANES
    rb = MOE_ROWS
    n_blk = xs.shape[0] // (rb * ROW_TILE)
    ff = w1.shape[3]
    blk = (rb * ROW_TILE, LANES)
    hbm = pl.BlockSpec(memory_space=pl.ANY)
    return pl.pallas_call(
        functools.partial(_expert_kernel, layer=layer),
        out_shape=jax.ShapeDtypeStruct(xs.shape, F32),
        grid_spec=pltpu.PrefetchScalarGridSpec(
            num_scalar_prefetch=3,
            grid=(n_blk,),
            in_specs=[
                pl.BlockSpec(blk, lambda i, be, nu, nx: (jnp.minimum(i, jnp.maximum(nu[0] - 1, 0)), 0)),
                hbm, hbm, hbm,
            ],
            out_specs=pl.BlockSpec(blk, lambda i, be, nu, nx: (i, 0)),
            scratch_shapes=[pltpu.VMEM((2, d, ff), F32), pltpu.VMEM((2, d, ff), F32), pltpu.VMEM((2, ff, d), F32),
                            pltpu.VMEM((d, ff), BF16), pltpu.VMEM((d, ff), BF16), pltpu.VMEM((ff, d), BF16),
                            pltpu.SemaphoreType.DMA((2,)), pltpu.SMEM((1,), jnp.int32)],
        ),
        compiler_params=_cparams("arbitrary"),
        name="moe_experts",
    )(blk_e, n_used, next_e, xs, w1, w3, w2)


def _combine_kernel(d1_ref, d2_ref, h_ref, meta_ref, g_ref, ys_ref, o_ref, buf_ref, sem, *, final_norm):
    i = pl.program_id(0)
    nsteps = pl.num_programs(0)
    tm = h_ref.shape[0]
    slot = i % 2

    def fetch(tile, s):
        base = tile * tm

        def issue(r8, c):
            for k in range(MOE_ISSUE_UNROLL):
                r = r8 * MOE_ISSUE_UNROLL + k
                pltpu.make_async_copy(ys_ref.at[_row_tile(d1_ref[base + r]), :],
                                      buf_ref.at[s, 0, _row_tile(r * ROW_TILE), :], sem.at[s]).start(priority=0)
                pltpu.make_async_copy(ys_ref.at[_row_tile(d2_ref[base + r]), :],
                                      buf_ref.at[s, 1, _row_tile(r * ROW_TILE), :], sem.at[s]).start(priority=1)
            return c

        lax.fori_loop(0, tm // MOE_ISSUE_UNROLL, issue, 0)

    @pl.when(i == 0)
    def _():
        fetch(0, 0)

    @pl.when(i + 1 < nsteps)
    def _():
        fetch(i + 1, 1 - slot)

    for j in range(2):
        pltpu.make_async_copy(ys_ref.at[pl.ds(0, tm * ROW_TILE), :], buf_ref.at[slot, j], sem.at[slot]).wait()

    meta = meta_ref[...]
    y1 = _load_row_tiles(buf_ref.at[slot, 0], tm)
    y2 = _load_row_tiles(buf_ref.at[slot, 1], tm)
    out = h_ref[...] + (meta[:, 2:3] * y1 + meta[:, 3:4] * y2)
    if final_norm:
        out = _rmsnorm(out, g_ref[...])
    o_ref[...] = out


def _combine(d1, d2, h, meta, ys, g_final, tm, final_norm):
    n, d = h.shape
    return pl.pallas_call(
        functools.partial(_combine_kernel, final_norm=final_norm),
        out_shape=jax.ShapeDtypeStruct((n, d), F32),
        grid_spec=pltpu.PrefetchScalarGridSpec(
            num_scalar_prefetch=2,
            grid=(n // tm,),
            in_specs=[
                pl.BlockSpec((tm, d), lambda i, *_: (i, 0)),
                pl.BlockSpec((tm, LANES), lambda i, *_: (i, 0)),
                pl.BlockSpec((1, d), lambda i, *_: (0, 0)),
                pl.BlockSpec(memory_space=pl.ANY),
            ],
            out_specs=pl.BlockSpec((tm, d), lambda i, *_: (i, 0)),
            scratch_shapes=[pltpu.VMEM((2, 2, tm * ROW_TILE, LANES), F32), pltpu.SemaphoreType.DMA((2,))],
        ),
        compiler_params=_cparams("arbitrary"),
        name="moe_combine",
    )(d1, d2, h, meta, g_final.reshape(1, d), ys)


def _hier_moe(h, routing, ln_g, w1, w3, w2, layer, g_final, final_norm):
    n, d = h.shape
    rb = MOE_ROWS
    cap = 2 * n + N_EXPERTS * rb
    n_blk = cap // rb
    meta, idx, counts = routing
    cnt = counts[0, :N_EXPERTS].astype(jnp.int32)
    padded = ((cnt + rb - 1) // rb) * rb
    pad_end = jnp.cumsum(padded)
    pad_start = pad_end - padded
    blk_start = jnp.arange(n_blk, dtype=jnp.int32) * rb
    blk_e = jnp.minimum(jnp.sum(pad_end[None, :] <= blk_start[:, None], axis=1), N_EXPERTS - 1).astype(jnp.int32)
    n_used = (pad_end[-1] // rb).astype(jnp.int32).reshape(1)
    experts = jnp.arange(N_EXPERTS, dtype=jnp.int32)[None, :]
    start_of = lambda e: jnp.sum(jnp.where(e[:, None] == experts, pad_start[None, :], 0), axis=1)
    d1 = (start_of(idx[0]) + idx[2]) * ROW_TILE
    d2 = (start_of(idx[1]) + idx[3]) * ROW_TILE
    xs = _dispatch(d1, d2, (pad_start + cnt) * ROW_TILE, padded - cnt, n_used, h, ln_g, cap, MOE_DISPATCH_ROWS)
    later = (experts > experts.T) & (padded > 0)[None, :]
    next_e = jnp.min(jnp.where(later, experts, N_EXPERTS), axis=1)
    next_e = jnp.where(next_e == N_EXPERTS, experts[0], next_e).astype(jnp.int32)
    ys = _experts(blk_e, n_used, next_e, xs, w1, w3, w2, layer)
    return _combine(d1, d2, h, meta, ys, g_final, MOE_COMBINE_ROWS, final_norm)


def kernel(x, ln_mix, ln_ffn, ln_final, ssm_w_in, ssm_lam_re, ssm_lam_im, ssm_log_dt, ssm_b_re, ssm_b_im, ssm_c_re, ssm_c_im, ssm_d, ssm_w_out, attn_w_qkv, attn_w_o, moe_w_group, moe_w_expert, moe_w1, moe_w3, moe_w2):
    bsz, seq, d = x.shape
    n = bsz * seq
    h = x.reshape(n, d)

    u = _norm_proj(h, ln_mix[0], ssm_w_in[0].astype(BF16), DENSE_ROWS, F32, "s5_in_proj")
    mats = _s5_mats(ssm_lam_re[0], ssm_lam_im[0], ssm_log_dt[0], ssm_b_re[0], ssm_b_im[0],
                    ssm_c_re[0], ssm_c_im[0])
    z = _s5_core(u, mats, ssm_d[0], bsz, seq)
    h, routing = _glu_out(z, ssm_w_out[0].astype(BF16), h, ln_ffn[0],
                          _router_weights(moe_w_group[0], moe_w_expert[0]), MIXER_OUT_ROWS)
    h = _hier_moe(h, routing, ln_ffn[0], moe_w1, moe_w3, moe_w2, 0, ln_final, False)

    q_t, k, v_t = _qkv_proj(h, ln_mix[1], attn_w_qkv[0], DENSE_ROWS)
    o_t = _moba(q_t, k, v_t, bsz, seq)
    h, routing = _proj_res_t(o_t, attn_w_o[0].astype(BF16), h, ln_ffn[1],
                             _router_weights(moe_w_group[1], moe_w_expert[1]), MIXER_OUT_ROWS)
    h = _hier_moe(h, routing, ln_ffn[1], moe_w1, moe_w3, moe_w2, 1, ln_final, True)
    return h.reshape(bsz, seq, d)
```

```python
import functools
import math

import jax
import jax.numpy as jnp
from jax import lax
from jax.experimental import pallas as pl
from jax.experimental.pallas import tpu as pltpu

F32 = jnp.float32
BF16 = jnp.bfloat16

D_MODEL = 1024
RMS_EPS = 1e-6
NEG_INF = -1e30

SSM_GROUP = 16
SSM_GROUPS = D_MODEL // SSM_GROUP
SSM_STATE = 64
SSM_CHUNK = 16
SSM_GB = 8
SSM_WIN = 8
SSM_POW_ROWS = 24

ATT_HEADS = 8
HEAD_DIM = 128
MOBA_BLOCK = 256
MOBA_TOPK = 3

N_GROUPS = 4
EXPERTS_PER_GROUP = 8
N_EXPERTS = 32
EXPERT_FF = 512
MOE_ROWS = 512
MOE_DISPATCH_ROWS = 256
MOE_COMBINE_ROWS = 256
DENSE_ROWS = 1024
MIXER_OUT_ROWS = 512

LANES = 128
VMEM_LIMIT = 56 * 1024 * 1024

_NT = (((1,), (1,)), ((), ()))
_TN = (((0,), (0,)), ((), ()))


def _cparams(*sem):
    return pltpu.CompilerParams(dimension_semantics=sem, vmem_limit_bytes=VMEM_LIMIT)


def _rmsnorm(x, g):
    return x * lax.rsqrt(jnp.mean(x * x, axis=-1, keepdims=True) + RMS_EPS) * g


def _norm_proj_kernel(x_ref, g_ref, w_ref, o_ref):
    xn = _rmsnorm(x_ref[...], g_ref[...]).astype(BF16)
    o_ref[...] = jnp.dot(xn, w_ref[...], preferred_element_type=F32).astype(o_ref.dtype)


def _norm_proj(x, g, w_bf16, tm, out_dtype, name):
    n, d = x.shape
    dout = w_bf16.shape[1]
    return pl.pallas_call(
        _norm_proj_kernel,
        out_shape=jax.ShapeDtypeStruct((n, dout), out_dtype),
        grid=(n // tm,),
        in_specs=[
            pl.BlockSpec((tm, d), lambda i: (i, 0)),
            pl.BlockSpec((1, d), lambda i: (0, 0)),
            pl.BlockSpec((d, dout), lambda i: (0, 0)),
        ],
        out_specs=pl.BlockSpec((tm, dout), lambda i: (i, 0)),
        compiler_params=_cparams("parallel"),
        name=name,
    )(x, g.reshape(1, d), w_bf16)


def _s5_mats(lam_re, lam_im, log_dt, b_re, b_im, c_re, c_im):
    g_, p_, c_, t_ = SSM_GROUPS, SSM_STATE, SSM_GROUP, SSM_CHUNK
    lr = jnp.minimum(lam_re, -1e-4)
    li = lam_im
    dt = jnp.exp(log_dt)[:, None]
    mag = jnp.exp(lr * dt)
    abar_re = mag * jnp.cos(li * dt)
    abar_im = mag * jnp.sin(li * dt)
    den = lr * lr + li * li
    nr = abar_re - 1.0
    gam_re = (nr * lr + abar_im * li) / den
    gam_im = (abar_im * lr - nr * li) / den
    bb_re = gam_re[..., None] * b_re - gam_im[..., None] * b_im
    bb_im = gam_re[..., None] * b_im + gam_im[..., None] * b_re

    def powers(ns):
        nf = jnp.asarray(ns, F32)[None, :, None]
        pm = jnp.exp(nf * (lr * dt)[:, None, :])
        ang = nf * (li * dt)[:, None, :]
        return pm * jnp.cos(ang), pm * jnp.sin(ang)

    pr, pi = powers(list(range(t_ + 1)))
    pad = jnp.zeros((g_, SSM_POW_ROWS - (t_ + 1), 2 * p_), F32)
    pw = jnp.stack([jnp.concatenate([jnp.concatenate([pr, pr], -1), pad], axis=1),
                    jnp.concatenate([jnp.concatenate([pi, pi], -1), pad], axis=1)], axis=1)
    bt_re = bb_re.transpose(0, 2, 1)
    bt_im = bb_im.transpose(0, 2, 1)
    cat = lambda a, b: jnp.concatenate([a, b], axis=-1)
    fac = jnp.stack([cat(bt_re, bt_im), cat(-bt_im, bt_re), cat(bt_re, -bt_im),
                     cat(c_re, c_im), cat(-c_im, c_re)], axis=1)
    qr, qi = powers([t_ * (1 << j) for j in range(4)])
    rows = []
    for j in range(4):
        ar, ai = qr[:, j], qi[:, j]
        rows += [jnp.concatenate([ar, ar], -1), jnp.concatenate([-ai, ai], -1),
                 jnp.concatenate([ai, -ai], -1)]
    rows += [jnp.zeros_like(rows[0])] * 4
    coef = jnp.stack(rows, axis=1)
    return fac.astype(F32), pw.astype(F32), coef.astype(F32)


def _s5_perm():
    r = jnp.arange(8 * LANES)
    col = ((r % LANES) // SSM_GROUP) * LANES + (r // LANES) * SSM_GROUP + r % SSM_GROUP
    p = (col[:, None] == r[None, :]).astype(BF16)
    return p, p.T


def _s5_kernel(u_ref, fac_ref, pw_ref, coef_ref, d_ref, p_ref, pt_ref, o_ref,
               m_ref, w_ref, v_ref, vf_ref, zf_ref, ea_ref, eb_ref, sp_ref):
    t_, c_, gb, win = SSM_CHUNK, SSM_GROUP, SSM_GB, SSM_WIN
    nk = u_ref.shape[0] // t_
    p2 = 2 * SSM_STATE
    tc = t_ * c_

    @pl.when(pl.program_id(1) == 0)
    def _():
        lane_tc = lax.broadcasted_iota(jnp.int32, (c_, tc), 1)
        conj = jnp.where(lax.broadcasted_iota(jnp.int32, (1, p2), 1) < SSM_STATE, 1.0, -1.0)
        for g in range(gb):
            b_t, ib_t, bconj_t, cm, icm = (fac_ref[g, k] for k in range(5))
            pr, pi = pw_ref[g, 0], pw_ref[g, 1]
            for s in range(t_):
                n = t_ - 1 - s
                w_ref[g, s * c_:(s + 1) * c_, :] = (b_t * pr[n:n + 1] + ib_t * pi[n:n + 1]).astype(BF16)
            ca = jnp.concatenate([cm * pr[n:n + 1] + icm * pi[n:n + 1] for n in range(t_ + 1)], axis=0)
            v_ref[g] = jnp.transpose(ca[c_:] * conj).astype(BF16)
            kf = lax.dot_general(bconj_t, ca[:tc], _NT, preferred_element_type=F32,
                                 precision=lax.Precision.HIGHEST)
            for s in range(t_):
                rolled = kf if s == 0 else pltpu.roll(kf, s * c_, axis=1)
                m_ref[g, s * c_:(s + 1) * c_, :] = jnp.where(lane_tc >= s * c_, rolled, 0.0).astype(BF16)

    halves = tc // LANES

    def timestep(t):
        return pl.ds(t, nk, stride=t_)

    def flat(g):
        return jnp.concatenate([vf_ref[h, :, g * LANES:(g + 1) * LANES] for h in range(halves)], axis=1)

    for h in range(halves):
        x = jnp.concatenate([u_ref[timestep(8 * h + i), :].astype(BF16) for i in range(8)], axis=1)
        vf_ref[h] = jnp.dot(x, p_ref[...], preferred_element_type=F32).astype(BF16)

    row = lax.broadcasted_iota(jnp.int32, (nk, p2), 0)

    def shift(x, d):
        return jnp.where(row < d, 0.0, pltpu.roll(x, d, axis=0))

    def swap(x):
        return pltpu.roll(x, SSM_STATE, axis=1)

    for g in range(gb):
        sc = jnp.dot(flat(g), w_ref[g], preferred_element_type=F32)
        xa = shift(sc, 1)
        cf = coef_ref[g]
        for lvl in range(3):
            pp, qa = cf[3 * lvl:3 * lvl + 1], cf[3 * lvl + 1:3 * lvl + 2]
            sa = shift(xa, 1 << lvl)
            xa = xa + pp * sa + qa * swap(sa)
        ea_ref[g] = xa
        eb_ref[g] = swap(xa)

    cfs = [coef_ref[g] for g in range(gb)]
    zero = jnp.zeros((win, p2), F32)
    state = [(zero, zero)] * gb
    for j in range(nk // win):
        rs = slice(j * win, (j + 1) * win)
        for g in range(gb):
            pp, qa, qb = cfs[g][9:10], cfs[g][10:11], cfs[g][11:12]
            s_a, s_b = state[g]
            n_a = pp * s_a + qa * s_b + ea_ref[g, rs, :]
            n_b = pp * s_b + qb * s_a + eb_ref[g, rs, :]
            sp_ref[g, rs, :] = n_a
            state[g] = (n_a, n_b)

    for g in range(gb):
        x = flat(g)
        y = jnp.dot(x, m_ref[g], preferred_element_type=F32)
        y = y + jnp.dot(sp_ref[g].astype(BF16), v_ref[g], preferred_element_type=F32)
        z = jax.nn.gelu(y + d_ref[g] * x.astype(F32)).astype(BF16)
        for h in range(halves):
            zf_ref[h, :, g * LANES:(g + 1) * LANES] = z[:, h * LANES:(h + 1) * LANES]

    for h in range(halves):
        zn = jnp.dot(zf_ref[h], pt_ref[...], preferred_element_type=F32)
        for i in range(8):
            o_ref[timestep(8 * h + i), :] = zn[:, i * LANES:(i + 1) * LANES]


def _s5_core(u, mats, d_skip, bsz, seq):
    fac, pw, coef = mats
    n, d = u.shape
    gb, tc, p2 = SSM_GB, SSM_CHUNK * SSM_GROUP, 2 * SSM_STATE
    nk = seq // SSM_CHUNK
    halves = tc // LANES
    perm, perm_t = _s5_perm()
    dflat = jnp.tile(d_skip.reshape(SSM_GROUPS, 1, SSM_GROUP), (1, 1, SSM_CHUNK))
    spec3 = lambda a, b: pl.BlockSpec((gb, a, b), lambda j, bb: (j, 0, 0))
    const = pl.BlockSpec(perm.shape, lambda j, bb: (0, 0))
    return pl.pallas_call(
        _s5_kernel,
        out_shape=jax.ShapeDtypeStruct((n, d), F32),
        grid=(SSM_GROUPS // gb, bsz),
        in_specs=[
            pl.BlockSpec((seq, LANES), lambda j, bb: (bb, j)),
            pl.BlockSpec((gb,) + fac.shape[1:], lambda j, bb: (j, 0, 0, 0)),
            pl.BlockSpec((gb,) + pw.shape[1:], lambda j, bb: (j, 0, 0, 0)),
            spec3(16, p2), spec3(1, tc),
            const, const,
        ],
        out_specs=pl.BlockSpec((seq, LANES), lambda j, bb: (bb, j)),
        scratch_shapes=[pltpu.VMEM((gb, tc, tc), BF16), pltpu.VMEM((gb, tc, p2), BF16),
                        pltpu.VMEM((gb, p2, tc), BF16),
                        pltpu.VMEM((halves, nk, gb * LANES), BF16),
                        pltpu.VMEM((halves, nk, gb * LANES), BF16),
                        pltpu.VMEM((gb, nk, p2), F32), pltpu.VMEM((gb, nk, p2), F32),
                        pltpu.VMEM((gb, nk, p2), F32)],
        compiler_params=_cparams("arbitrary", "arbitrary"),
        name="s5_core",
    )(u, fac, pw, coef, dflat, perm, perm_t)


def _glu_out_kernel(z_ref, w_ref, x_ref, g_ref, wr_ref, tri_ref, o_ref, meta_ref, idx_ref, cnt_ref):
    vg = jnp.dot(z_ref[...].astype(BF16), w_ref[...], preferred_element_type=F32)
    d = o_ref.shape[1]
    h = x_ref[...] + vg[:, :d] * jax.nn.sigmoid(vg[:, d:])
    o_ref[...] = h
    _route(h, g_ref, wr_ref, tri_ref, meta_ref, idx_ref, cnt_ref)


def _proj_res_t_kernel(at_ref, w_ref, x_ref, g_ref, wr_ref, tri_ref, o_ref, meta_ref, idx_ref, cnt_ref):
    h = x_ref[...] + lax.dot_general(at_ref[...], w_ref[...], _TN, preferred_element_type=F32)
    o_ref[...] = h
    _route(h, g_ref, wr_ref, tri_ref, meta_ref, idx_ref, cnt_ref)


def _mixer_out(kernel_fn, a, a_spec, w_bf16, x, ln_g, w_router, tm, name):
    n, d = x.shape
    tri = (jnp.arange(tm)[:, None] > jnp.arange(tm)[None, :]).astype(BF16)
    full = lambda arr: pl.BlockSpec(arr.shape, lambda i: (0, 0))
    h, meta, idx, counts = pl.pallas_call(
        kernel_fn,
        out_shape=(jax.ShapeDtypeStruct((n, d), F32), jax.ShapeDtypeStruct((n, LANES), F32),
                   jax.ShapeDtypeStruct((8, n), jnp.int32), jax.ShapeDtypeStruct((1, LANES), F32)),
        grid=(n // tm,),
        in_specs=[a_spec, full(w_bf16), pl.BlockSpec((tm, d), lambda i: (i, 0)),
                  pl.BlockSpec((1, d), lambda i: (0, 0)), full(w_router), full(tri)],
        out_specs=(pl.BlockSpec((tm, d), lambda i: (i, 0)), pl.BlockSpec((tm, LANES), lambda i: (i, 0)),
                   pl.BlockSpec((8, tm), lambda i: (0, i)), pl.BlockSpec((1, LANES), lambda i: (0, 0))),
        compiler_params=_cparams("arbitrary"),
        name=name,
    )(a, w_bf16, x, ln_g.reshape(1, d), w_router, tri)
    return h, (meta, idx, counts)


def _glu_out(z, w_bf16, x, ln_g, w_router, tm):
    spec = pl.BlockSpec((tm, z.shape[1]), lambda i: (i, 0))
    return _mixer_out(_glu_out_kernel, z, spec, w_bf16, x, ln_g, w_router, tm, "s5_glu_out")


def _proj_res_t(a_t, w_bf16, x, ln_g, w_router, tm):
    spec = pl.BlockSpec((a_t.shape[0], tm), lambda i: (0, i))
    return _mixer_out(_proj_res_t_kernel, a_t, spec, w_bf16, x, ln_g, w_router, tm, "attn_out_proj")


def _qkv_kernel(x_ref, g_ref, wq_ref, wk_ref, wv_ref, qt_ref, k_ref, vt_ref):
    xn = _rmsnorm(x_ref[...], g_ref[...]).astype(BF16)
    c = (HEAD_DIM ** -0.5) * math.log2(math.e)
    qt_ref[...] = (lax.dot_general(wq_ref[...], xn, _NT, preferred_element_type=F32) * c).astype(BF16)
    k_ref[...] = jnp.dot(xn, wk_ref[...], preferred_element_type=F32).astype(BF16)
    vt = lax.dot_general(wv_ref[...], xn, _NT, preferred_element_type=F32).astype(BF16)
    for c in range(vt_ref.shape[0]):
        vt_ref[c] = vt[:, c * MOBA_BLOCK:(c + 1) * MOBA_BLOCK]


def _qkv_proj(x, g, w_qkv, tm):
    n, d = x.shape
    da = ATT_HEADS * HEAD_DIM
    wq_t = w_qkv[:, :da].T.astype(BF16)
    wk = w_qkv[:, da:2 * da].astype(BF16)
    wv_t = w_qkv[:, 2 * da:].T.astype(BF16)
    full = lambda shp: pl.BlockSpec(shp, lambda i: (0, 0))
    tb = tm // MOBA_BLOCK
    return pl.pallas_call(
        _qkv_kernel,
        out_shape=(jax.ShapeDtypeStruct((da, n), BF16), jax.ShapeDtypeStruct((n, da), BF16),
                   jax.ShapeDtypeStruct((n // MOBA_BLOCK, da, MOBA_BLOCK), BF16)),
        grid=(n // tm,),
        in_specs=[pl.BlockSpec((tm, d), lambda i: (i, 0)), full((1, d)),
                  full((da, d)), full((d, da)), full((da, d))],
        out_specs=(pl.BlockSpec((da, tm), lambda i: (0, i)), pl.BlockSpec((tm, da), lambda i: (i, 0)),
                   pl.BlockSpec((tb, da, MOBA_BLOCK), lambda i: (i, 0, 0))),
        compiler_params=_cparams("parallel"),
        name="attn_qkv_proj",
    )(x, g.reshape(1, d), wq_t, wk, wv_t)


MOBA_CHAINS = 8
MOBA_SUM_ROWS = 16


def _moba_kernel(qt_ref, k_ref, vt_ref, et_ref, ot_ref, s_scr, acc_scr, m_scr, *, nb):
    blk, dh = MOBA_BLOCK, HEAD_DIM
    brow = lax.broadcasted_iota(jnp.int32, (nb, blk), 0).astype(F32)
    r_ix = lax.broadcasted_iota(jnp.int32, (blk, blk), 0)
    c_ix = lax.broadcasted_iota(jnp.int32, (blk, blk), 1)
    zpad = jnp.zeros((2 * LANES - dh - nb, blk), BF16)
    kmean = jnp.mean(k_ref[...].astype(F32).reshape(nb, blk, dh), axis=1)
    km_hi = kmean.astype(BF16)
    kmean2 = jnp.concatenate([km_hi, (kmean - km_hi.astype(F32)).astype(BF16)], axis=0)

    def k_aug(t):
        rows = slice(t * blk, (t + 1) * blk)
        return jnp.concatenate([k_ref[rows, :], et_ref[rows, :]], axis=1)

    ones = jnp.ones((MOBA_SUM_ROWS, blk), BF16)

    def values(t):
        return jnp.concatenate([vt_ref[t], ones], axis=0)

    def absorb(c, s, vt):
        m_old = m_scr[c]
        m_new = jnp.maximum(m_old, jnp.max(s, axis=0, keepdims=True))
        p = jnp.exp2(s - m_new).astype(BF16)
        acc_scr[c] = jnp.exp2(m_old - m_new) * acc_scr[c] + jnp.dot(vt, p, preferred_element_type=F32)
        m_scr[c] = m_new

    def query_block(i):
        qs = slice(i * blk, (i + 1) * blk)
        qt = qt_ref[:, qs]
        if i > MOBA_TOPK:
            g2 = jnp.dot(kmean2, qt, preferred_element_type=F32)
            gate = g2[:nb] + g2[nb:]
            gate = jnp.where(brow < i, gate, NEG_INF)
            sel = brow >= i
            for _ in range(MOBA_TOPK):
                gm = jnp.max(gate, axis=0, keepdims=True)
                first = jnp.min(jnp.where(gate == gm, brow, float(nb)), axis=0, keepdims=True)
                pick = brow == first
                sel = jnp.logical_or(sel, pick)
                gate = jnp.where(pick, -jnp.inf, gate)
            bias = jnp.where(sel, 0.0, NEG_INF).astype(BF16)
        else:
            bias = jnp.zeros((nb, blk), BF16)
        return jnp.concatenate([qt, bias, zpad], axis=0)

    nq = MOBA_CHAINS
    for i0 in range(0, nb, nq):
        chains = range(min(nq, nb - i0))
        q_aug = [query_block(i0 + c) for c in chains]
        m_scr[...] = jnp.full(m_scr.shape, NEG_INF, F32)
        acc_scr[...] = jnp.zeros(acc_scr.shape, F32)

        def scores(t, slot, cs):
            ka = k_aug(t)
            for c in cs:
                s_scr[c, slot] = jnp.dot(ka, q_aug[c], preferred_element_type=F32)

        scores(0, 0, chains)
        for j in range(i0):
            cur = [s_scr[c, j % 2] for c in chains]
            scores(j + 1, (j + 1) % 2, chains)
            vt = values(j)
            for c in chains:
                absorb(c, cur[c], vt)
        for t in chains:
            cur = {c: s_scr[c, (i0 + t) % 2] for c in chains if c >= t}
            later = [c for c in chains if c > t]
            if later:
                scores(i0 + t + 1, (i0 + t + 1) % 2, later)
            vt = values(i0 + t)
            absorb(t, jnp.where(r_ix <= c_ix, cur[t], NEG_INF), vt)
            for c in later:
                absorb(c, cur[c], vt)
        for c in chains:
            qs = slice((i0 + c) * blk, (i0 + c + 1) * blk)
            ot_ref[:, qs] = (acc_scr[c, :dh, :] / acc_scr[c, dh:dh + 1, :]).astype(ot_ref.dtype)


def _moba(q_t, k, v_t, bsz, seq):
    nb = seq // MOBA_BLOCK
    da = ATT_HEADS * HEAD_DIM
    e_t = ((jnp.arange(seq) // MOBA_BLOCK)[:, None] == jnp.arange(LANES)[None, :]).astype(BF16)
    tspec = pl.BlockSpec((HEAD_DIM, seq), lambda b, h: (h, b))
    return pl.pallas_call(
        functools.partial(_moba_kernel, nb=nb),
        out_shape=jax.ShapeDtypeStruct((da, bsz * seq), BF16),
        grid=(bsz, ATT_HEADS),
        in_specs=[tspec, pl.BlockSpec((seq, HEAD_DIM), lambda b, h: (b, h)),
                  pl.BlockSpec((nb, HEAD_DIM, MOBA_BLOCK), lambda b, h: (b, h, 0)),
                  pl.BlockSpec((seq, LANES), lambda b, h: (0, 0))],
        out_specs=tspec,
        scratch_shapes=[pltpu.VMEM((MOBA_CHAINS, 2, MOBA_BLOCK, MOBA_BLOCK), F32),
                        pltpu.VMEM((MOBA_CHAINS, HEAD_DIM + MOBA_SUM_ROWS, MOBA_BLOCK), F32),
                        pltpu.VMEM((MOBA_CHAINS, 1, MOBA_BLOCK), F32)],
        compiler_params=_cparams("parallel", "parallel"),
        name="moba_attn",
    )(q_t, k, v_t, e_t)


def _router_weights(w_group, w_expert):
    d = w_group.shape[0]
    w_r = jnp.zeros((d, LANES), F32).at[:, :N_GROUPS].set(w_group)
    w_r = w_r.at[:, N_GROUPS:N_GROUPS + N_EXPERTS].set(w_expert)
    w_hi = w_r.astype(BF16)
    w_lo = (w_r - w_hi.astype(F32)).astype(BF16)
    return jnp.concatenate([w_hi, w_lo], axis=1)


def _route(h, g_ref, w_ref, tri_ref, meta_ref, idx_ref, cnt_ref):
    xn = _rmsnorm(h, g_ref[...])
    x_hi = xn.astype(BF16)
    x_lo = (xn - x_hi.astype(F32)).astype(BF16)
    t = jnp.dot(x_hi, w_ref[...], preferred_element_type=F32)
    logits = (t[:, :LANES] + t[:, LANES:]) + jnp.dot(x_lo, w_ref[:, :LANES], preferred_element_type=F32)
    tm = logits.shape[0]
    lane = lax.broadcasted_iota(jnp.int32, (tm, LANES), 1).astype(F32)
    ninf = -jnp.inf
    lg = jnp.where(lane < N_GROUPS, logits, ninf)
    gm = jnp.max(lg, axis=-1, keepdims=True)
    g_idx = jnp.min(jnp.where(lg == gm, lane, float(LANES)), axis=-1, keepdims=True)
    g_gate = 1.0 / jnp.sum(jnp.exp(lg - gm), axis=-1, keepdims=True)
    lo = N_GROUPS + EXPERTS_PER_GROUP * g_idx
    le = jnp.where((lane >= lo) & (lane < lo + EXPERTS_PER_GROUP), logits, ninf)
    m1 = jnp.max(le, axis=-1, keepdims=True)
    i1 = jnp.min(jnp.where(le == m1, lane, float(LANES)), axis=-1, keepdims=True)
    le2 = jnp.where(lane == i1, ninf, le)
    m2 = jnp.max(le2, axis=-1, keepdims=True)
    i2 = jnp.min(jnp.where(le2 == m2, lane, float(LANES)), axis=-1, keepdims=True)
    p2 = jnp.exp(m2 - m1)
    gate1 = g_gate / (1.0 + p2)
    gate2 = g_gate * p2 / (1.0 + p2)
    e1 = i1 - N_GROUPS
    e2 = i2 - N_GROUPS
    meta = jnp.where(lane == 0, e1, jnp.where(lane == 1, e2, jnp.where(lane == 2, gate1,
                     jnp.where(lane == 3, gate2, 0.0))))
    meta_ref[...] = meta
    oh1 = (lane == e1).astype(F32)
    oh2 = (lane == e2).astype(F32)
    both = oh1 + oh2

    @pl.when(pl.program_id(0) == 0)
    def _():
        cnt_ref[...] = jnp.zeros_like(cnt_ref)

    seen = cnt_ref[...]
    before = jnp.dot(tri_ref[...], both.astype(BF16), preferred_element_type=F32) + seen
    r1 = jnp.sum(before * oh1, axis=-1, keepdims=True)
    r2 = jnp.sum(before * oh2, axis=-1, keepdims=True)
    slab = jnp.where(lane == 0, e1, jnp.where(lane == 1, e2, jnp.where(lane == 2, r1,
                     jnp.where(lane == 3, r2, 0.0))))
    idx_ref[...] = jnp.transpose(slab)[0:8, :].astype(jnp.int32)
    cnt_ref[...] = seen + jnp.sum(both, axis=0, keepdims=True)


MOE_ISSUE_UNROLL = 8
MOE_X_BUFS = 3
ROW_TILE = 8


def _row_tile(off):
    return pl.ds(pl.multiple_of(off, ROW_TILE), ROW_TILE)


def _store_row_tiles(ref, x):
    rows = x.shape[0]
    for c in range(ROW_TILE):
        ref[pl.ds(c, rows, stride=ROW_TILE), :] = x[:, c * LANES:(c + 1) * LANES]


def _load_row_tiles(ref, rows):
    return jnp.concatenate([ref[pl.ds(c, rows, stride=ROW_TILE), :] for c in range(ROW_TILE)], axis=1)


def _dispatch_kernel(d1_ref, d2_ref, plo_ref, pn_ref, nu_ref, h_ref, g_ref, xs_ref,
                     buf_ref, zero_ref, sem, fill_sem):
    i = pl.program_id(0)
    nsteps = pl.num_programs(0)
    tm = h_ref.shape[0]
    rt = ROW_TILE
    rb = zero_ref.shape[0] // rt
    n_blk = xs_ref.shape[0] // (rb * rt)
    slot = i % 2
    base = i * tm

    def slot_drain(s):
        cp = pltpu.make_async_copy(buf_ref.at[s], xs_ref.at[pl.ds(0, tm * rt), :], sem.at[s])
        cp.wait()
        cp.wait()

    @pl.when(i >= 2)
    def _():
        slot_drain(slot)

    _store_row_tiles(buf_ref.at[slot], _rmsnorm(h_ref[...], g_ref[...]))

    def issue(r8, c):
        for k in range(MOE_ISSUE_UNROLL):
            r = r8 * MOE_ISSUE_UNROLL + k
            src = buf_ref.at[slot, _row_tile(r * rt), :]
            pltpu.make_async_copy(src, xs_ref.at[_row_tile(d1_ref[base + r]), :], sem.at[slot]).start(priority=0)
            pltpu.make_async_copy(src, xs_ref.at[_row_tile(d2_ref[base + r]), :], sem.at[slot]).start(priority=1)
        return c

    lax.fori_loop(0, tm // MOE_ISSUE_UNROLL, issue, 0)

    def pad_copy(off, rows):
        return pltpu.make_async_copy(zero_ref.at[pl.ds(0, rows * rt), :],
                                     xs_ref.at[pl.ds(pl.multiple_of(off, rt), rows * rt), :], fill_sem)

    pad_sizes = [1 << b for b in reversed(range(rb.bit_length() - 1))]

    def blk_copy(b):
        return pltpu.make_async_copy(zero_ref, xs_ref.at[pl.ds(pl.multiple_of(b * (rb * rt), rt), rb * rt), :],
                                     fill_sem)

    @pl.when(i == 0)
    def _():
        zero_ref[...] = jnp.zeros_like(zero_ref)

    @pl.when(i < N_EXPERTS)
    def _():
        off = plo_ref[i]
        npad = pn_ref[i]
        for rows in pad_sizes:
            has = (npad & rows) != 0

            @pl.when(has)
            def _(off=off, rows=rows):
                pad_copy(off, rows).start()

            off = off + jnp.where(has, rows * rt, 0)

    @pl.when(i == N_EXPERTS)
    def _():
        def fill(b, c):
            blk_copy(b).start()
            return c

        lax.fori_loop(nu_ref[0], n_blk, fill, 0)

    @pl.when(i == nsteps - 1)
    def _():
        slot_drain(1 - slot)
        slot_drain(slot)

        def per_expert(e, c):
            npad = pn_ref[e]
            for rows in pad_sizes:
                @pl.when((npad & rows) != 0)
                def _(rows=rows):
                    pad_copy(0, rows).wait()

            return c

        lax.fori_loop(0, N_EXPERTS, per_expert, 0)

        def one_blk(b, c):
            blk_copy(0).wait()
            return c

        lax.fori_loop(nu_ref[0], n_blk, one_blk, 0)


def _dispatch(d1, d2, pad_lo, pad_n, n_used, h, g, cap, tm):
    n, d = h.shape
    assert n // tm > N_EXPERTS + 1 and d == ROW_TILE * LANES
    return pl.pallas_call(
        _dispatch_kernel,
        out_shape=jax.ShapeDtypeStruct((cap * ROW_TILE, LANES), F32),
        grid_spec=pltpu.PrefetchScalarGridSpec(
            num_scalar_prefetch=5,
            grid=(n // tm,),
            in_specs=[
                pl.BlockSpec((tm, d), lambda i, *_: (i, 0)),
                pl.BlockSpec((1, d), lambda i, *_: (0, 0)),
            ],
            out_specs=pl.BlockSpec(memory_space=pl.ANY),
            scratch_shapes=[pltpu.VMEM((2, tm * ROW_TILE, LANES), F32),
                            pltpu.VMEM((MOE_ROWS * ROW_TILE, LANES), F32),
                            pltpu.SemaphoreType.DMA((2,)), pltpu.SemaphoreType.DMA],
        ),
        compiler_params=_cparams("arbitrary"),
        name="moe_dispatch",
    )(d1, d2, pad_lo, pad_n, n_used, h, g.reshape(1, d))


def _expert_kernel(be_ref, nu_ref, nx_ref, x_hbm, w1_hbm, w3_hbm, w2_hbm, y_ref,
                   xb, wb1, wb3, wb2, w1c, w3c, w2c, xsem, wsem, wslot, *, layer):
    i = pl.program_id(0)
    nu = nu_ref[0]
    used = i < nu
    e = be_ref[i]
    rows_blk = xb.shape[1]
    ahead = MOE_X_BUFS - 1

    def x_copy(b, s):
        return pltpu.make_async_copy(x_hbm.at[pl.ds(pl.multiple_of(b * rows_blk, rows_blk), rows_blk), :],
                                     xb.at[s], xsem.at[s])

    @pl.when(i == 0)
    def _():
        for b in range(ahead):
            @pl.when(b < nu)
            def _(b=b):
                x_copy(b, b).start()

    prev = be_ref[jnp.maximum(i - 1, 0)]
    fresh = jnp.logical_or(i == 0, e != prev)

    def weight_copies(ex, s):
        return [pltpu.make_async_copy(w1_hbm.at[layer, ex], wb1.at[s], wsem.at[s]),
                pltpu.make_async_copy(w3_hbm.at[layer, ex], wb3.at[s], wsem.at[s]),
                pltpu.make_async_copy(w2_hbm.at[layer, ex], wb2.at[s], wsem.at[s])]

    @pl.when(i == 0)
    def _():
        wslot[0] = 0
        for cp in weight_copies(e, 0):
            cp.start()

    @pl.when(jnp.logical_and(used, fresh))
    def _():
        s = wslot[0]
        for cp in weight_copies(e, s):
            cp.wait()
        nxt = nx_ref[e]

        @pl.when(nxt != e)
        def _():
            for cp in weight_copies(nxt, 1 - s):
                cp.start()

        w1c[...] = wb1[s].astype(BF16)
        w3c[...] = wb3[s].astype(BF16)
        w2c[...] = wb2[s].astype(BF16)
        wslot[0] = 1 - s

    @pl.when(used)
    def _():
        slot = i % MOE_X_BUFS
        x_copy(i, slot).wait()

        @pl.when(i + ahead < nu)
        def _():
            x_copy(i + ahead, (i + ahead) % MOE_X_BUFS).start()

        rb = rows_blk // ROW_TILE
        x = _load_row_tiles(xb.at[slot], rb).astype(BF16)
        hf = w1c.shape[1] // 2
        y = None
        for c in range(2):
            cs = slice(c * hf, (c + 1) * hf)
            a = jnp.dot(x, w1c[:, cs], preferred_element_type=F32)
            b = jnp.dot(x, w3c[:, cs], preferred_element_type=F32)
            act = (jax.nn.silu(a) * b).astype(BF16)
            part = jnp.dot(act, w2c[cs, :], preferred_element_type=F32)
            y = part if y is None else y + part
        _store_row_tiles(y_ref, y)

    @pl.when(jnp.logical_not(used))
    def _():
        y_ref[...] = jnp.zeros_like(y_ref)


def _experts(blk_e, n_used, next_e, xs, w1, w3, w2, layer):
    d = ROW_TILE * LANES
    rb = MOE_ROWS
    n_blk = xs.shape[0] // (rb * ROW_TILE)
    ff = w1.shape[3]
    blk = (rb * ROW_TILE, LANES)
    hbm = pl.BlockSpec(memory_space=pl.ANY)
    return pl.pallas_call(
        functools.partial(_expert_kernel, layer=layer),
        out_shape=jax.ShapeDtypeStruct(xs.shape, F32),
        grid_spec=pltpu.PrefetchScalarGridSpec(
            num_scalar_prefetch=3,
            grid=(n_blk,),
            in_specs=[hbm, hbm, hbm, hbm],
            out_specs=pl.BlockSpec(blk, lambda i, be, nu, nx: (i, 0)),
            scratch_shapes=[pltpu.VMEM((MOE_X_BUFS,) + blk, F32),
                            pltpu.VMEM((2, d, ff), F32), pltpu.VMEM((2, d, ff), F32), pltpu.VMEM((2, ff, d), F32),
                            pltpu.VMEM((d, ff), BF16), pltpu.VMEM((d, ff), BF16), pltpu.VMEM((ff, d), BF16),
                            pltpu.SemaphoreType.DMA((MOE_X_BUFS,)), pltpu.SemaphoreType.DMA((2,)),
                            pltpu.SMEM((1,), jnp.int32)],
        ),
        compiler_params=_cparams("arbitrary"),
        name="moe_experts",
    )(blk_e, n_used, next_e, xs, w1, w3, w2)


def _combine_kernel(d1_ref, d2_ref, h_ref, meta_ref, g_ref, ys_ref, o_ref, buf_ref, sem, *, final_norm):
    i = pl.program_id(0)
    nsteps = pl.num_programs(0)
    tm = h_ref.shape[0]
    slot = i % 2

    def fetch(tile, s):
        base = tile * tm

        def issue(r8, c):
            for k in range(MOE_ISSUE_UNROLL):
                r = r8 * MOE_ISSUE_UNROLL + k
                pltpu.make_async_copy(ys_ref.at[_row_tile(d1_ref[base + r]), :],
                                      buf_ref.at[s, 0, _row_tile(r * ROW_TILE), :], sem.at[s]).start(priority=0)
                pltpu.make_async_copy(ys_ref.at[_row_tile(d2_ref[base + r]), :],
                                      buf_ref.at[s, 1, _row_tile(r * ROW_TILE), :], sem.at[s]).start(priority=1)
            return c

        lax.fori_loop(0, tm // MOE_ISSUE_UNROLL, issue, 0)

    @pl.when(i == 0)
    def _():
        fetch(0, 0)

    @pl.when(i + 1 < nsteps)
    def _():
        fetch(i + 1, 1 - slot)

    for j in range(2):
        pltpu.make_async_copy(ys_ref.at[pl.ds(0, tm * ROW_TILE), :], buf_ref.at[slot, j], sem.at[slot]).wait()

    meta = meta_ref[...]
    y1 = _load_row_tiles(buf_ref.at[slot, 0], tm)
    y2 = _load_row_tiles(buf_ref.at[slot, 1], tm)
    out = h_ref[...] + (meta[:, 2:3] * y1 + meta[:, 3:4] * y2)
    if final_norm:
        out = _rmsnorm(out, g_ref[...])
    o_ref[...] = out


def _combine(d1, d2, h, meta, ys, g_final, tm, final_norm):
    n, d = h.shape
    return pl.pallas_call(
        functools.partial(_combine_kernel, final_norm=final_norm),
        out_shape=jax.ShapeDtypeStruct((n, d), F32),
        grid_spec=pltpu.PrefetchScalarGridSpec(
            num_scalar_prefetch=2,
            grid=(n // tm,),
            in_specs=[
                pl.BlockSpec((tm, d), lambda i, *_: (i, 0)),
                pl.BlockSpec((tm, LANES), lambda i, *_: (i, 0)),
                pl.BlockSpec((1, d), lambda i, *_: (0, 0)),
                pl.BlockSpec(memory_space=pl.ANY),
            ],
            out_specs=pl.BlockSpec((tm, d), lambda i, *_: (i, 0)),
            scratch_shapes=[pltpu.VMEM((2, 2, tm * ROW_TILE, LANES), F32), pltpu.SemaphoreType.DMA((2,))],
        ),
        compiler_params=_cparams("arbitrary"),
        name="moe_combine",
    )(d1, d2, h, meta, g_final.reshape(1, d), ys)


def _hier_moe(h, routing, ln_g, w1, w3, w2, layer, g_final, final_norm):
    n, d = h.shape
    rb = MOE_ROWS
    cap = 2 * n + N_EXPERTS * rb
    n_blk = cap // rb
    meta, idx, counts = routing
    cnt = counts[0, :N_EXPERTS].astype(jnp.int32)
    padded = ((cnt + rb - 1) // rb) * rb
    pad_end = jnp.cumsum(padded)
    pad_start = pad_end - padded
    blk_start = jnp.arange(n_blk, dtype=jnp.int32) * rb
    blk_e = jnp.minimum(jnp.sum(pad_end[None, :] <= blk_start[:, None], axis=1), N_EXPERTS - 1).astype(jnp.int32)
    n_used = (pad_end[-1] // rb).astype(jnp.int32).reshape(1)
    experts = jnp.arange(N_EXPERTS, dtype=jnp.int32)[None, :]
    start_of = lambda e: jnp.sum(jnp.where(e[:, None] == experts, pad_start[None, :], 0), axis=1)
    d1 = (start_of(idx[0]) + idx[2]) * ROW_TILE
    d2 = (start_of(idx[1]) + idx[3]) * ROW_TILE
    xs = _dispatch(d1, d2, (pad_start + cnt) * ROW_TILE, padded - cnt, n_used, h, ln_g, cap, MOE_DISPATCH_ROWS)
    later = (experts > experts.T) & (padded > 0)[None, :]
    next_e = jnp.min(jnp.where(later, experts, N_EXPERTS), axis=1)
    next_e = jnp.where(next_e == N_EXPERTS, experts[0], next_e).astype(jnp.int32)
    ys = _experts(blk_e, n_used, next_e, xs, w1, w3, w2, layer)
    return _combine(d1, d2, h, meta, ys, g_final, MOE_COMBINE_ROWS, final_norm)


def kernel(x, ln_mix, ln_ffn, ln_final, ssm_w_in, ssm_lam_re, ssm_lam_im, ssm_log_dt, ssm_b_re, ssm_b_im, ssm_c_re, ssm_c_im, ssm_d, ssm_w_out, attn_w_qkv, attn_w_o, moe_w_group, moe_w_expert, moe_w1, moe_w3, moe_w2):
    bsz, seq, d = x.shape
    n = bsz * seq
    h = x.reshape(n, d)

    u = _norm_proj(h, ln_mix[0], ssm_w_in[0].astype(BF16), DENSE_ROWS, F32, "s5_in_proj")
    mats = _s5_mats(ssm_lam_re[0], ssm_lam_im[0], ssm_log_dt[0], ssm_b_re[0], ssm_b_im[0],
                    ssm_c_re[0], ssm_c_im[0])
    z = _s5_core(u, mats, ssm_d[0], bsz, seq)
    h, routing = _glu_out(z, ssm_w_out[0].astype(BF16), h, ln_ffn[0],
                          _router_weights(moe_w_group[0], moe_w_expert[0]), MIXER_OUT_ROWS)
    h = _hier_moe(h, routing, ln_ffn[0], moe_w1, moe_w3, moe_w2, 0, ln_final, False)

    q_t, k, v_t = _qkv_proj(h, ln_mix[1], attn_w_qkv[0], DENSE_ROWS)
    o_t = _moba(q_t, k, v_t, bsz, seq)
    h, routing = _proj_res_t(o_t, attn_w_o[0].astype(BF16), h, ln_ffn[1],
                             _router_weights(moe_w_group[1], moe_w_expert[1]), MIXER_OUT_ROWS)
    h = _hier_moe(h, routing, ln_ffn[1], moe_w1, moe_w3, moe_w2, 1, ln_final, True)
    return h.reshape(bsz, seq, d)
```

```python
import functools
import math

import jax
import jax.numpy as jnp
from jax import lax
from jax.experimental import pallas as pl
from jax.experimental.pallas import tpu as pltpu

F32 = jnp.float32
BF16 = jnp.bfloat16

D_MODEL = 1024
RMS_EPS = 1e-6
NEG_INF = -1e30

SSM_GROUP = 16
SSM_GROUPS = D_MODEL // SSM_GROUP
SSM_STATE = 64
SSM_CHUNK = 16
SSM_GB = 8
SSM_WIN = 8
SSM_POW_ROWS = 24

ATT_HEADS = 8
HEAD_DIM = 128
MOBA_BLOCK = 256
MOBA_TOPK = 3

N_GROUPS = 4
EXPERTS_PER_GROUP = 8
N_EXPERTS = 32
EXPERT_FF = 512
MOE_ROWS = 512
MOE_DISPATCH_ROWS = 256
MOE_COMBINE_ROWS = 256
DENSE_ROWS = 1024
MIXER_OUT_ROWS = 512

LANES = 128
VMEM_LIMIT = 56 * 1024 * 1024

_NT = (((1,), (1,)), ((), ()))
_TN = (((0,), (0,)), ((), ()))


def _cparams(*sem):
    return pltpu.CompilerParams(dimension_semantics=sem, vmem_limit_bytes=VMEM_LIMIT)


def _rmsnorm(x, g):
    return x * lax.rsqrt(jnp.mean(x * x, axis=-1, keepdims=True) + RMS_EPS) * g


def _norm_proj_kernel(x_ref, g_ref, w_ref, o_ref):
    xn = _rmsnorm(x_ref[...], g_ref[...]).astype(BF16)
    o_ref[...] = jnp.dot(xn, w_ref[...], preferred_element_type=F32).astype(o_ref.dtype)


def _norm_proj(x, g, w_bf16, tm, out_dtype, name):
    n, d = x.shape
    dout = w_bf16.shape[1]
    return pl.pallas_call(
        _norm_proj_kernel,
        out_shape=jax.ShapeDtypeStruct((n, dout), out_dtype),
        grid=(n // tm,),
        in_specs=[
            pl.BlockSpec((tm, d), lambda i: (i, 0)),
            pl.BlockSpec((1, d), lambda i: (0, 0)),
            pl.BlockSpec((d, dout), lambda i: (0, 0)),
        ],
        out_specs=pl.BlockSpec((tm, dout), lambda i: (i, 0)),
        compiler_params=_cparams("parallel"),
        name=name,
    )(x, g.reshape(1, d), w_bf16)


def _s5_mats(lam_re, lam_im, log_dt, b_re, b_im, c_re, c_im):
    g_, p_, c_, t_ = SSM_GROUPS, SSM_STATE, SSM_GROUP, SSM_CHUNK
    lr = jnp.minimum(lam_re, -1e-4)
    li = lam_im
    dt = jnp.exp(log_dt)[:, None]
    mag = jnp.exp(lr * dt)
    abar_re = mag * jnp.cos(li * dt)
    abar_im = mag * jnp.sin(li * dt)
    den = lr * lr + li * li
    nr = abar_re - 1.0
    gam_re = (nr * lr + abar_im * li) / den
    gam_im = (abar_im * lr - nr * li) / den
    bb_re = gam_re[..., None] * b_re - gam_im[..., None] * b_im
    bb_im = gam_re[..., None] * b_im + gam_im[..., None] * b_re

    def powers(ns):
        nf = jnp.asarray(ns, F32)[None, :, None]
        pm = jnp.exp(nf * (lr * dt)[:, None, :])
        ang = nf * (li * dt)[:, None, :]
        return pm * jnp.cos(ang), pm * jnp.sin(ang)

    pr, pi = powers(list(range(t_ + 1)))
    pad = jnp.zeros((g_, SSM_POW_ROWS - (t_ + 1), 2 * p_), F32)
    pw = jnp.stack([jnp.concatenate([jnp.concatenate([pr, pr], -1), pad], axis=1),
                    jnp.concatenate([jnp.concatenate([pi, pi], -1), pad], axis=1)], axis=1)
    bt_re = bb_re.transpose(0, 2, 1)
    bt_im = bb_im.transpose(0, 2, 1)
    cat = lambda a, b: jnp.concatenate([a, b], axis=-1)
    fac = jnp.stack([cat(bt_re, bt_im), cat(-bt_im, bt_re), cat(bt_re, -bt_im),
                     cat(c_re, c_im), cat(-c_im, c_re)], axis=1)
    qr, qi = powers([t_ * (1 << j) for j in range(4)])
    rows = []
    for j in range(4):
        ar, ai = qr[:, j], qi[:, j]
        rows += [jnp.concatenate([ar, ar], -1), jnp.concatenate([-ai, ai], -1),
                 jnp.concatenate([ai, -ai], -1)]
    rows += [jnp.zeros_like(rows[0])] * 4
    coef = jnp.stack(rows, axis=1)
    return fac.astype(F32), pw.astype(F32), coef.astype(F32)


def _s5_perm():
    r = jnp.arange(8 * LANES)
    col = ((r % LANES) // SSM_GROUP) * LANES + (r // LANES) * SSM_GROUP + r % SSM_GROUP
    p = (col[:, None] == r[None, :]).astype(BF16)
    return p, p.T


def _s5_kernel(u_ref, fac_ref, pw_ref, coef_ref, d_ref, p_ref, pt_ref, o_ref,
               m_ref, w_ref, v_ref, vf_ref, zf_ref, ea_ref, eb_ref, sp_ref):
    t_, c_, gb, win = SSM_CHUNK, SSM_GROUP, SSM_GB, SSM_WIN
    nk = u_ref.shape[0] // t_
    p2 = 2 * SSM_STATE
    tc = t_ * c_

    @pl.when(pl.program_id(1) == 0)
    def _():
        lane_tc = lax.broadcasted_iota(jnp.int32, (c_, tc), 1)
        conj = jnp.where(lax.broadcasted_iota(jnp.int32, (1, p2), 1) < SSM_STATE, 1.0, -1.0)
        for g in range(gb):
            b_t, ib_t, bconj_t, cm, icm = (fac_ref[g, k] for k in range(5))
            pr, pi = pw_ref[g, 0], pw_ref[g, 1]
            for s in range(t_):
                n = t_ - 1 - s
                w_ref[g, s * c_:(s + 1) * c_, :] = (b_t * pr[n:n + 1] + ib_t * pi[n:n + 1]).astype(BF16)
            ca = jnp.concatenate([cm * pr[n:n + 1] + icm * pi[n:n + 1] for n in range(t_ + 1)], axis=0)
            v_ref[g] = jnp.transpose(ca[c_:] * conj).astype(BF16)
            kf = lax.dot_general(bconj_t, ca[:tc], _NT, preferred_element_type=F32,
                                 precision=lax.Precision.HIGHEST)
            for s in range(t_):
                rolled = kf if s == 0 else pltpu.roll(kf, s * c_, axis=1)
                m_ref[g, s * c_:(s + 1) * c_, :] = jnp.where(lane_tc >= s * c_, rolled, 0.0).astype(BF16)

    halves = tc // LANES

    def timestep(t):
        return pl.ds(t, nk, stride=t_)

    def flat(g):
        return jnp.concatenate([vf_ref[h, :, g * LANES:(g + 1) * LANES] for h in range(halves)], axis=1)

    for h in range(halves):
        x = jnp.concatenate([u_ref[timestep(8 * h + i), :].astype(BF16) for i in range(8)], axis=1)
        vf_ref[h] = jnp.dot(x, p_ref[...], preferred_element_type=F32).astype(BF16)

    row = lax.broadcasted_iota(jnp.int32, (nk, p2), 0)

    def shift(x, d):
        return jnp.where(row < d, 0.0, pltpu.roll(x, d, axis=0))

    def swap(x):
        return pltpu.roll(x, SSM_STATE, axis=1)

    for g in range(gb):
        sc = jnp.dot(flat(g), w_ref[g], preferred_element_type=F32)
        xa = shift(sc, 1)
        cf = coef_ref[g]
        for lvl in range(3):
            pp, qa = cf[3 * lvl:3 * lvl + 1], cf[3 * lvl + 1:3 * lvl + 2]
            sa = shift(xa, 1 << lvl)
            xa = xa + pp * sa + qa * swap(sa)
        ea_ref[g] = xa
        eb_ref[g] = swap(xa)

    cfs = [coef_ref[g] for g in range(gb)]
    zero = jnp.zeros((win, p2), F32)
    state = [(zero, zero)] * gb
    for j in range(nk // win):
        rs = slice(j * win, (j + 1) * win)
        for g in range(gb):
            pp, qa, qb = cfs[g][9:10], cfs[g][10:11], cfs[g][11:12]
            s_a, s_b = state[g]
            n_a = pp * s_a + qa * s_b + ea_ref[g, rs, :]
            n_b = pp * s_b + qb * s_a + eb_ref[g, rs, :]
            sp_ref[g, rs, :] = n_a
            state[g] = (n_a, n_b)

    for g in range(gb):
        x = flat(g)
        y = jnp.dot(x, m_ref[g], preferred_element_type=F32)
        y = y + jnp.dot(sp_ref[g].astype(BF16), v_ref[g], preferred_element_type=F32)
        z = jax.nn.gelu(y + d_ref[g] * x.astype(F32)).astype(BF16)
        for h in range(halves):
            zf_ref[h, :, g * LANES:(g + 1) * LANES] = z[:, h * LANES:(h + 1) * LANES]

    for h in range(halves):
        zn = jnp.dot(zf_ref[h], pt_ref[...], preferred_element_type=F32)
        for i in range(8):
            o_ref[timestep(8 * h + i), :] = zn[:, i * LANES:(i + 1) * LANES]


def _s5_core(u, mats, d_skip, bsz, seq):
    fac, pw, coef = mats
    n, d = u.shape
    gb, tc, p2 = SSM_GB, SSM_CHUNK * SSM_GROUP, 2 * SSM_STATE
    nk = seq // SSM_CHUNK
    halves = tc // LANES
    perm, perm_t = _s5_perm()
    dflat = jnp.tile(d_skip.reshape(SSM_GROUPS, 1, SSM_GROUP), (1, 1, SSM_CHUNK))
    spec3 = lambda a, b: pl.BlockSpec((gb, a, b), lambda j, bb: (j, 0, 0))
    const = pl.BlockSpec(perm.shape, lambda j, bb: (0, 0))
    return pl.pallas_call(
        _s5_kernel,
        out_shape=jax.ShapeDtypeStruct((n, d), F32),
        grid=(SSM_GROUPS // gb, bsz),
        in_specs=[
            pl.BlockSpec((seq, LANES), lambda j, bb: (bb, j)),
            pl.BlockSpec((gb,) + fac.shape[1:], lambda j, bb: (j, 0, 0, 0)),
            pl.BlockSpec((gb,) + pw.shape[1:], lambda j, bb: (j, 0, 0, 0)),
            spec3(16, p2), spec3(1, tc),
            const, const,
        ],
        out_specs=pl.BlockSpec((seq, LANES), lambda j, bb: (bb, j)),
        scratch_shapes=[pltpu.VMEM((gb, tc, tc), BF16), pltpu.VMEM((gb, tc, p2), BF16),
                        pltpu.VMEM((gb, p2, tc), BF16),
                        pltpu.VMEM((halves, nk, gb * LANES), BF16),
                        pltpu.VMEM((halves, nk, gb * LANES), BF16),
                        pltpu.VMEM((gb, nk, p2), F32), pltpu.VMEM((gb, nk, p2), F32),
                        pltpu.VMEM((gb, nk, p2), F32)],
        compiler_params=_cparams("arbitrary", "arbitrary"),
        name="s5_core",
    )(u, fac, pw, coef, dflat, perm, perm_t)


def _glu_out_kernel(z_ref, w_ref, x_ref, g_ref, wr_ref, tri_ref, o_ref, meta_ref, idx_ref, cnt_ref):
    vg = jnp.dot(z_ref[...].astype(BF16), w_ref[...], preferred_element_type=F32)
    d = o_ref.shape[1]
    h = x_ref[...] + vg[:, :d] * jax.nn.sigmoid(vg[:, d:])
    o_ref[...] = h
    _route(h, g_ref, wr_ref, tri_ref, meta_ref, idx_ref, cnt_ref)


def _proj_res_t_kernel(at_ref, w_ref, x_ref, g_ref, wr_ref, tri_ref, o_ref, meta_ref, idx_ref, cnt_ref):
    h = x_ref[...] + lax.dot_general(at_ref[...], w_ref[...], _TN, preferred_element_type=F32)
    o_ref[...] = h
    _route(h, g_ref, wr_ref, tri_ref, meta_ref, idx_ref, cnt_ref)


def _mixer_out(kernel_fn, a, a_spec, w_bf16, x, ln_g, w_router, tm, name):
    n, d = x.shape
    tri = (jnp.arange(tm)[:, None] > jnp.arange(tm)[None, :]).astype(BF16)
    full = lambda arr: pl.BlockSpec(arr.shape, lambda i: (0, 0))
    h, meta, idx, counts = pl.pallas_call(
        kernel_fn,
        out_shape=(jax.ShapeDtypeStruct((n, d), F32), jax.ShapeDtypeStruct((n, LANES), F32),
                   jax.ShapeDtypeStruct((8, n), jnp.int32), jax.ShapeDtypeStruct((1, LANES), F32)),
        grid=(n // tm,),
        in_specs=[a_spec, full(w_bf16), pl.BlockSpec((tm, d), lambda i: (i, 0)),
                  pl.BlockSpec((1, d), lambda i: (0, 0)), full(w_router), full(tri)],
        out_specs=(pl.BlockSpec((tm, d), lambda i: (i, 0)), pl.BlockSpec((tm, LANES), lambda i: (i, 0)),
                   pl.BlockSpec((8, tm), lambda i: (0, i)), pl.BlockSpec((1, LANES), lambda i: (0, 0))),
        compiler_params=_cparams("arbitrary"),
        name=name,
    )(a, w_bf16, x, ln_g.reshape(1, d), w_router, tri)
    return h, (meta, idx, counts)


def _glu_out(z, w_bf16, x, ln_g, w_router, tm):
    spec = pl.BlockSpec((tm, z.shape[1]), lambda i: (i, 0))
    return _mixer_out(_glu_out_kernel, z, spec, w_bf16, x, ln_g, w_router, tm, "s5_glu_out")


def _proj_res_t(a_t, w_bf16, x, ln_g, w_router, tm):
    spec = pl.BlockSpec((a_t.shape[0], tm), lambda i: (0, i))
    return _mixer_out(_proj_res_t_kernel, a_t, spec, w_bf16, x, ln_g, w_router, tm, "attn_out_proj")


def _qkv_kernel(x_ref, g_ref, wq_ref, wk_ref, wv_ref, qt_ref, k_ref, vt_ref):
    xn = _rmsnorm(x_ref[...], g_ref[...]).astype(BF16)
    c = (HEAD_DIM ** -0.5) * math.log2(math.e)
    qt_ref[...] = (lax.dot_general(wq_ref[...], xn, _NT, preferred_element_type=F32) * c).astype(BF16)
    k_ref[...] = jnp.dot(xn, wk_ref[...], preferred_element_type=F32).astype(BF16)
    vt = lax.dot_general(wv_ref[...], xn, _NT, preferred_element_type=F32).astype(BF16)
    for c in range(vt_ref.shape[0]):
        vt_ref[c] = vt[:, c * MOBA_BLOCK:(c + 1) * MOBA_BLOCK]


def _qkv_proj(x, g, w_qkv, tm):
    n, d = x.shape
    da = ATT_HEADS * HEAD_DIM
    wq_t = w_qkv[:, :da].T.astype(BF16)
    wk = w_qkv[:, da:2 * da].astype(BF16)
    wv_t = w_qkv[:, 2 * da:].T.astype(BF16)
    full = lambda shp: pl.BlockSpec(shp, lambda i: (0, 0))
    tb = tm // MOBA_BLOCK
    return pl.pallas_call(
        _qkv_kernel,
        out_shape=(jax.ShapeDtypeStruct((da, n), BF16), jax.ShapeDtypeStruct((n, da), BF16),
                   jax.ShapeDtypeStruct((n // MOBA_BLOCK, da, MOBA_BLOCK), BF16)),
        grid=(n // tm,),
        in_specs=[pl.BlockSpec((tm, d), lambda i: (i, 0)), full((1, d)),
                  full((da, d)), full((d, da)), full((da, d))],
        out_specs=(pl.BlockSpec((da, tm), lambda i: (0, i)), pl.BlockSpec((tm, da), lambda i: (i, 0)),
                   pl.BlockSpec((tb, da, MOBA_BLOCK), lambda i: (i, 0, 0))),
        compiler_params=_cparams("parallel"),
        name="attn_qkv_proj",
    )(x, g.reshape(1, d), wq_t, wk, wv_t)


MOBA_CHAINS = 8
MOBA_SUM_ROWS = 16


def _moba_kernel(qt_ref, k_ref, vt_ref, et_ref, ot_ref, s_scr, acc_scr, m_scr, *, nb):
    blk, dh = MOBA_BLOCK, HEAD_DIM
    brow = lax.broadcasted_iota(jnp.int32, (nb, blk), 0).astype(F32)
    r_ix = lax.broadcasted_iota(jnp.int32, (blk, blk), 0)
    c_ix = lax.broadcasted_iota(jnp.int32, (blk, blk), 1)
    zpad = jnp.zeros((2 * LANES - dh - nb, blk), BF16)
    kmean = jnp.mean(k_ref[...].astype(F32).reshape(nb, blk, dh), axis=1)
    km_hi = kmean.astype(BF16)
    kmean2 = jnp.concatenate([km_hi, (kmean - km_hi.astype(F32)).astype(BF16)], axis=0)

    def k_aug(t):
        rows = slice(t * blk, (t + 1) * blk)
        return jnp.concatenate([k_ref[rows, :], et_ref[rows, :]], axis=1)

    ones = jnp.ones((MOBA_SUM_ROWS, blk), BF16)

    def values(t):
        return jnp.concatenate([vt_ref[t], ones], axis=0)

    def absorb(c, s, vt):
        m_old = m_scr[c]
        m_new = jnp.maximum(m_old, jnp.max(s, axis=0, keepdims=True))
        p = jnp.exp2(s - m_new).astype(BF16)
        acc_scr[c] = jnp.exp2(m_old - m_new) * acc_scr[c] + jnp.dot(vt, p, preferred_element_type=F32)
        m_scr[c] = m_new

    def query_block(i):
        qs = slice(i * blk, (i + 1) * blk)
        qt = qt_ref[:, qs]
        if i > MOBA_TOPK:
            g2 = jnp.dot(kmean2, qt, preferred_element_type=F32)
            gate = g2[:nb] + g2[nb:]
            gate = jnp.where(brow < i, gate, NEG_INF)
            sel = brow >= i
            for _ in range(MOBA_TOPK):
                gm = jnp.max(gate, axis=0, keepdims=True)
                first = jnp.min(jnp.where(gate == gm, brow, float(nb)), axis=0, keepdims=True)
                pick = brow == first
                sel = jnp.logical_or(sel, pick)
                gate = jnp.where(pick, -jnp.inf, gate)
            bias = jnp.where(sel, 0.0, NEG_INF).astype(BF16)
        else:
            bias = jnp.zeros((nb, blk), BF16)
        return jnp.concatenate([qt, bias, zpad], axis=0)

    nq = MOBA_CHAINS
    for i0 in range(0, nb, nq):
        chains = range(min(nq, nb - i0))
        q_aug = [query_block(i0 + c) for c in chains]
        m_scr[...] = jnp.full(m_scr.shape, NEG_INF, F32)
        acc_scr[...] = jnp.zeros(acc_scr.shape, F32)

        def scores(t, slot, cs):
            ka = k_aug(t)
            for c in cs:
                s_scr[c, slot] = jnp.dot(ka, q_aug[c], preferred_element_type=F32)

        scores(0, 0, chains)
        for j in range(i0):
            cur = [s_scr[c, j % 2] for c in chains]
            scores(j + 1, (j + 1) % 2, chains)
            vt = values(j)
            for c in chains:
                absorb(c, cur[c], vt)
        for t in chains:
            cur = {c: s_scr[c, (i0 + t) % 2] for c in chains if c >= t}
            later = [c for c in chains if c > t]
            if later:
                scores(i0 + t + 1, (i0 + t + 1) % 2, later)
            vt = values(i0 + t)
            absorb(t, jnp.where(r_ix <= c_ix, cur[t], NEG_INF), vt)
            for c in later:
                absorb(c, cur[c], vt)
        for c in chains:
            qs = slice((i0 + c) * blk, (i0 + c + 1) * blk)
            ot_ref[:, qs] = (acc_scr[c, :dh, :] / acc_scr[c, dh:dh + 1, :]).astype(ot_ref.dtype)


def _moba(q_t, k, v_t, bsz, seq):
    nb = seq // MOBA_BLOCK
    da = ATT_HEADS * HEAD_DIM
    e_t = ((jnp.arange(seq) // MOBA_BLOCK)[:, None] == jnp.arange(LANES)[None, :]).astype(BF16)
    tspec = pl.BlockSpec((HEAD_DIM, seq), lambda b, h: (h, b))
    return pl.pallas_call(
        functools.partial(_moba_kernel, nb=nb),
        out_shape=jax.ShapeDtypeStruct((da, bsz * seq), BF16),
        grid=(bsz, ATT_HEADS),
        in_specs=[tspec, pl.BlockSpec((seq, HEAD_DIM), lambda b, h: (b, h)),
                  pl.BlockSpec((nb, HEAD_DIM, MOBA_BLOCK), lambda b, h: (b, h, 0)),
                  pl.BlockSpec((seq, LANES), lambda b, h: (0, 0))],
        out_specs=tspec,
        scratch_shapes=[pltpu.VMEM((MOBA_CHAINS, 2, MOBA_BLOCK, MOBA_BLOCK), F32),
                        pltpu.VMEM((MOBA_CHAINS, HEAD_DIM + MOBA_SUM_ROWS, MOBA_BLOCK), F32),
                        pltpu.VMEM((MOBA_CHAINS, 1, MOBA_BLOCK), F32)],
        compiler_params=_cparams("parallel", "parallel"),
        name="moba_attn",
    )(q_t, k, v_t, e_t)


def _router_weights(w_group, w_expert):
    d = w_group.shape[0]
    w_r = jnp.zeros((d, LANES), F32).at[:, :N_GROUPS].set(w_group)
    w_r = w_r.at[:, N_GROUPS:N_GROUPS + N_EXPERTS].set(w_expert)
    w_hi = w_r.astype(BF16)
    w_lo = (w_r - w_hi.astype(F32)).astype(BF16)
    return jnp.concatenate([w_hi, w_lo], axis=1)


def _route(h, g_ref, w_ref, tri_ref, meta_ref, idx_ref, cnt_ref):
    xn = _rmsnorm(h, g_ref[...])
    x_hi = xn.astype(BF16)
    x_lo = (xn - x_hi.astype(F32)).astype(BF16)
    t = jnp.dot(x_hi, w_ref[...], preferred_element_type=F32)
    logits = (t[:, :LANES] + t[:, LANES:]) + jnp.dot(x_lo, w_ref[:, :LANES], preferred_element_type=F32)
    tm = logits.shape[0]
    lane = lax.broadcasted_iota(jnp.int32, (tm, LANES), 1).astype(F32)
    ninf = -jnp.inf
    lg = jnp.where(lane < N_GROUPS, logits, ninf)
    gm = jnp.max(lg, axis=-1, keepdims=True)
    g_idx = jnp.min(jnp.where(lg == gm, lane, float(LANES)), axis=-1, keepdims=True)
    g_gate = 1.0 / jnp.sum(jnp.exp(lg - gm), axis=-1, keepdims=True)
    lo = N_GROUPS + EXPERTS_PER_GROUP * g_idx
    le = jnp.where((lane >= lo) & (lane < lo + EXPERTS_PER_GROUP), logits, ninf)
    m1 = jnp.max(le, axis=-1, keepdims=True)
    i1 = jnp.min(jnp.where(le == m1, lane, float(LANES)), axis=-1, keepdims=True)
    le2 = jnp.where(lane == i1, ninf, le)
    m2 = jnp.max(le2, axis=-1, keepdims=True)
    i2 = jnp.min(jnp.where(le2 == m2, lane, float(LANES)), axis=-1, keepdims=True)
    p2 = jnp.exp(m2 - m1)
    gate1 = g_gate / (1.0 + p2)
    gate2 = g_gate * p2 / (1.0 + p2)
    e1 = i1 - N_GROUPS
    e2 = i2 - N_GROUPS
    meta = jnp.where(lane == 0, e1, jnp.where(lane == 1, e2, jnp.where(lane == 2, gate1,
                     jnp.where(lane == 3, gate2, 0.0))))
    meta_ref[...] = meta
    oh1 = (lane == e1).astype(F32)
    oh2 = (lane == e2).astype(F32)
    both = oh1 + oh2

    @pl.when(pl.program_id(0) == 0)
    def _():
        cnt_ref[...] = jnp.zeros_like(cnt_ref)

    seen = cnt_ref[...]
    before = jnp.dot(tri_ref[...], both.astype(BF16), preferred_element_type=F32) + seen
    r1 = jnp.sum(before * oh1, axis=-1, keepdims=True)
    r2 = jnp.sum(before * oh2, axis=-1, keepdims=True)
    slab = jnp.where(lane == 0, e1, jnp.where(lane == 1, e2, jnp.where(lane == 2, r1,
                     jnp.where(lane == 3, r2, 0.0))))
    idx_ref[...] = jnp.transpose(slab)[0:8, :].astype(jnp.int32)
    cnt_ref[...] = seen + jnp.sum(both, axis=0, keepdims=True)


MOE_ISSUE_UNROLL = 8
MOE_X_BUFS = 3
ROW_TILE = 8


def _row_tile(off):
    return pl.ds(pl.multiple_of(off, ROW_TILE), ROW_TILE)


def _store_row_tiles(ref, x):
    rows = x.shape[0]
    for c in range(ROW_TILE):
        ref[pl.ds(c, rows, stride=ROW_TILE), :] = x[:, c * LANES:(c + 1) * LANES]


def _load_row_tiles(ref, rows):
    return jnp.concatenate([ref[pl.ds(c, rows, stride=ROW_TILE), :] for c in range(ROW_TILE)], axis=1)


def _dispatch_kernel(d1_ref, d2_ref, plo_ref, pn_ref, nu_ref, h_ref, g_ref, xs_ref,
                     buf_ref, zero_ref, sem, fill_sem):
    i = pl.program_id(0)
    nsteps = pl.num_programs(0)
    tm = h_ref.shape[0]
    rt = ROW_TILE
    rb = zero_ref.shape[0] // rt
    n_blk = xs_ref.shape[0] // (rb * rt)
    slot = i % 2
    base = i * tm

    def slot_drain(s):
        cp = pltpu.make_async_copy(buf_ref.at[s], xs_ref.at[pl.ds(0, tm * rt), :], sem.at[s])
        cp.wait()
        cp.wait()

    @pl.when(i >= 2)
    def _():
        slot_drain(slot)

    _store_row_tiles(buf_ref.at[slot], _rmsnorm(h_ref[...], g_ref[...]))

    def issue(r8, c):
        for k in range(MOE_ISSUE_UNROLL):
            r = r8 * MOE_ISSUE_UNROLL + k
            src = buf_ref.at[slot, _row_tile(r * rt), :]
            pltpu.make_async_copy(src, xs_ref.at[_row_tile(d1_ref[base + r]), :], sem.at[slot]).start(priority=0)
            pltpu.make_async_copy(src, xs_ref.at[_row_tile(d2_ref[base + r]), :], sem.at[slot]).start(priority=1)
        return c

    lax.fori_loop(0, tm // MOE_ISSUE_UNROLL, issue, 0)

    def pad_copy(off, rows):
        return pltpu.make_async_copy(zero_ref.at[pl.ds(0, rows * rt), :],
                                     xs_ref.at[pl.ds(pl.multiple_of(off, rt), rows * rt), :], fill_sem)

    pad_sizes = [1 << b for b in reversed(range(rb.bit_length() - 1))]

    def blk_copy(b):
        return pltpu.make_async_copy(zero_ref, xs_ref.at[pl.ds(pl.multiple_of(b * (rb * rt), rt), rb * rt), :],
                                     fill_sem)

    @pl.when(i == 0)
    def _():
        zero_ref[...] = jnp.zeros_like(zero_ref)

    @pl.when(i < N_EXPERTS)
    def _():
        off = plo_ref[i]
        npad = pn_ref[i]
        for rows in pad_sizes:
            has = (npad & rows) != 0

            @pl.when(has)
            def _(off=off, rows=rows):
                pad_copy(off, rows).start()

            off = off + jnp.where(has, rows * rt, 0)

    @pl.when(i == N_EXPERTS)
    def _():
        def fill(b, c):
            blk_copy(b).start()
            return c

        lax.fori_loop(nu_ref[0], n_blk, fill, 0)

    @pl.when(i == nsteps - 1)
    def _():
        slot_drain(1 - slot)
        slot_drain(slot)

        def per_expert(e, c):
            npad = pn_ref[e]
            for rows in pad_sizes:
                @pl.when((npad & rows) != 0)
                def _(rows=rows):
                    pad_copy(0, rows).wait()

            return c

        lax.fori_loop(0, N_EXPERTS, per_expert, 0)

        def one_blk(b, c):
            blk_copy(0).wait()
            return c

        lax.fori_loop(nu_ref[0], n_blk, one_blk, 0)


def _dispatch(d1, d2, pad_lo, pad_n, n_used, h, g, cap, tm):
    n, d = h.shape
    assert n // tm > N_EXPERTS + 1 and d == ROW_TILE * LANES
    return pl.pallas_call(
        _dispatch_kernel,
        out_shape=jax.ShapeDtypeStruct((cap * ROW_TILE, LANES), F32),
        grid_spec=pltpu.PrefetchScalarGridSpec(
            num_scalar_prefetch=5,
            grid=(n // tm,),
            in_specs=[
                pl.BlockSpec((tm, d), lambda i, *_: (i, 0)),
                pl.BlockSpec((1, d), lambda i, *_: (0, 0)),
            ],
            out_specs=pl.BlockSpec(memory_space=pl.ANY),
            scratch_shapes=[pltpu.VMEM((2, tm * ROW_TILE, LANES), F32),
                            pltpu.VMEM((MOE_ROWS * ROW_TILE, LANES), F32),
                            pltpu.SemaphoreType.DMA((2,)), pltpu.SemaphoreType.DMA],
        ),
        compiler_params=_cparams("arbitrary"),
        name="moe_dispatch",
    )(d1, d2, pad_lo, pad_n, n_used, h, g.reshape(1, d))


def _expert_kernel(be_ref, nu_ref, nx_ref, x_hbm, w1_hbm, w3_hbm, w2_hbm, y_hbm,
                   xb, yb, wb1, wb3, wb2, w1c, w3c, w2c, xsem, ysem, wsem, wslot, *, layer):
    i = pl.program_id(0)
    nu = nu_ref[0]
    used = i < nu
    e = be_ref[i]
    rows_blk = xb.shape[1]
    ahead = MOE_X_BUFS - 1

    def x_copy(b, s):
        return pltpu.make_async_copy(x_hbm.at[pl.ds(pl.multiple_of(b * rows_blk, rows_blk), rows_blk), :],
                                     xb.at[s], xsem.at[s])

    nsteps = pl.num_programs(0)
    oslot = i % MOE_X_BUFS

    def y_copy(b, s):
        return pltpu.make_async_copy(yb.at[s], y_hbm.at[pl.ds(pl.multiple_of(b * rows_blk, rows_blk), rows_blk), :],
                                     ysem.at[s])

    @pl.when(i >= MOE_X_BUFS)
    def _():
        y_copy(0, oslot).wait()

    @pl.when(i == 0)
    def _():
        for b in range(ahead):
            @pl.when(b < nu)
            def _(b=b):
                x_copy(b, b).start()

    prev = be_ref[jnp.maximum(i - 1, 0)]
    fresh = jnp.logical_or(i == 0, e != prev)

    def weight_copies(ex, s):
        return [pltpu.make_async_copy(w1_hbm.at[layer, ex], wb1.at[s], wsem.at[s]),
                pltpu.make_async_copy(w3_hbm.at[layer, ex], wb3.at[s], wsem.at[s]),
                pltpu.make_async_copy(w2_hbm.at[layer, ex], wb2.at[s], wsem.at[s])]

    @pl.when(i == 0)
    def _():
        wslot[0] = 0
        for cp in weight_copies(e, 0):
            cp.start()

    @pl.when(jnp.logical_and(used, fresh))
    def _():
        s = wslot[0]
        for cp in weight_copies(e, s):
            cp.wait()
        nxt = nx_ref[e]

        @pl.when(nxt != e)
        def _():
            for cp in weight_copies(nxt, 1 - s):
                cp.start()

        w1c[...] = wb1[s].astype(BF16)
        w3c[...] = wb3[s].astype(BF16)
        w2c[...] = wb2[s].astype(BF16)
        wslot[0] = 1 - s

    @pl.when(used)
    def _():
        slot = i % MOE_X_BUFS
        x_copy(i, slot).wait()

        @pl.when(i + ahead < nu)
        def _():
            x_copy(i + ahead, (i + ahead) % MOE_X_BUFS).start()

        rb = rows_blk // ROW_TILE
        x = _load_row_tiles(xb.at[slot], rb).astype(BF16)
        hf = w1c.shape[1] // 2
        y = None
        for c in range(2):
            cs = slice(c * hf, (c + 1) * hf)
            a = jnp.dot(x, w1c[:, cs], preferred_element_type=F32)
            b = jnp.dot(x, w3c[:, cs], preferred_element_type=F32)
            act = (jax.nn.silu(a) * b).astype(BF16)
            part = jnp.dot(act, w2c[cs, :], preferred_element_type=F32)
            y = part if y is None else y + part
        _store_row_tiles(yb.at[oslot], y)

    @pl.when(jnp.logical_not(used))
    def _():
        yb[oslot] = jnp.zeros(yb.shape[1:], F32)

    y_copy(i, oslot).start()

    @pl.when(i == nsteps - 1)
    def _():
        for j in range(MOE_X_BUFS):
            @pl.when(nsteps - 1 - j >= 0)
            def _(j=j):
                y_copy(0, (nsteps - 1 - j) % MOE_X_BUFS).wait()


def _experts(blk_e, n_used, next_e, xs, w1, w3, w2, layer):
    d = ROW_TILE * LANES
    rb = MOE_ROWS
    n_blk = xs.shape[0] // (rb * ROW_TILE)
    ff = w1.shape[3]
    blk = (rb * ROW_TILE, LANES)
    hbm = pl.BlockSpec(memory_space=pl.ANY)
    return pl.pallas_call(
        functools.partial(_expert_kernel, layer=layer),
        out_shape=jax.ShapeDtypeStruct(xs.shape, F32),
        grid_spec=pltpu.PrefetchScalarGridSpec(
            num_scalar_prefetch=3,
            grid=(n_blk,),
            in_specs=[hbm, hbm, hbm, hbm],
            out_specs=hbm,
            scratch_shapes=[pltpu.VMEM((MOE_X_BUFS,) + blk, F32), pltpu.VMEM((MOE_X_BUFS,) + blk, F32),
                            pltpu.VMEM((2, d, ff), F32), pltpu.VMEM((2, d, ff), F32), pltpu.VMEM((2, ff, d), F32),
                            pltpu.VMEM((d, ff), BF16), pltpu.VMEM((d, ff), BF16), pltpu.VMEM((ff, d), BF16),
                            pltpu.SemaphoreType.DMA((MOE_X_BUFS,)), pltpu.SemaphoreType.DMA((MOE_X_BUFS,)),
                            pltpu.SemaphoreType.DMA((2,)), pltpu.SMEM((1,), jnp.int32)],
        ),
        compiler_params=_cparams("arbitrary"),
        name="moe_experts",
    )(blk_e, n_used, next_e, xs, w1, w3, w2)


def _combine_kernel(d1_ref, d2_ref, h_ref, meta_ref, g_ref, ys_ref, o_ref, buf_ref, sem, *, final_norm):
    i = pl.program_id(0)
    nsteps = pl.num_programs(0)
    tm = h_ref.shape[0]
    slot = i % 2

    def fetch(tile, s):
        base = tile * tm

        def issue(r8, c):
            for k in range(MOE_ISSUE_UNROLL):
                r = r8 * MOE_ISSUE_UNROLL + k
                pltpu.make_async_copy(ys_ref.at[_row_tile(d1_ref[base + r]), :],
                                      buf_ref.at[s, 0, _row_tile(r * ROW_TILE), :], sem.at[s]).start(priority=0)
                pltpu.make_async_copy(ys_ref.at[_row_tile(d2_ref[base + r]), :],
                                      buf_ref.at[s, 1, _row_tile(r * ROW_TILE), :], sem.at[s]).start(priority=1)
            return c

        lax.fori_loop(0, tm // MOE_ISSUE_UNROLL, issue, 0)

    @pl.when(i == 0)
    def _():
        fetch(0, 0)

    @pl.when(i + 1 < nsteps)
    def _():
        fetch(i + 1, 1 - slot)

    for j in range(2):
        pltpu.make_async_copy(ys_ref.at[pl.ds(0, tm * ROW_TILE), :], buf_ref.at[slot, j], sem.at[slot]).wait()

    meta = meta_ref[...]
    y1 = _load_row_tiles(buf_ref.at[slot, 0], tm)
    y2 = _load_row_tiles(buf_ref.at[slot, 1], tm)
    out = h_ref[...] + (meta[:, 2:3] * y1 + meta[:, 3:4] * y2)
    if final_norm:
        out = _rmsnorm(out, g_ref[...])
    o_ref[...] = out


def _combine(d1, d2, h, meta, ys, g_final, tm, final_norm):
    n, d = h.shape
    return pl.pallas_call(
        functools.partial(_combine_kernel, final_norm=final_norm),
        out_shape=jax.ShapeDtypeStruct((n, d), F32),
        grid_spec=pltpu.PrefetchScalarGridSpec(
            num_scalar_prefetch=2,
            grid=(n // tm,),
            in_specs=[
                pl.BlockSpec((tm, d), lambda i, *_: (i, 0)),
                pl.BlockSpec((tm, LANES), lambda i, *_: (i, 0)),
                pl.BlockSpec((1, d), lambda i, *_: (0, 0)),
                pl.BlockSpec(memory_space=pl.ANY),
            ],
            out_specs=pl.BlockSpec((tm, d), lambda i, *_: (i, 0)),
            scratch_shapes=[pltpu.VMEM((2, 2, tm * ROW_TILE, LANES), F32), pltpu.SemaphoreType.DMA((2,))],
        ),
        compiler_params=_cparams("arbitrary"),
        name="moe_combine",
    )(d1, d2, h, meta, g_final.reshape(1, d), ys)


def _hier_moe(h, routing, ln_g, w1, w3, w2, layer, g_final, final_norm):
    n, d = h.shape
    rb = MOE_ROWS
    cap = 2 * n + N_EXPERTS * rb
    n_blk = cap // rb
    meta, idx, counts = routing
    cnt = counts[0, :N_EXPERTS].astype(jnp.int32)
    padded = ((cnt + rb - 1) // rb) * rb
    pad_end = jnp.cumsum(padded)
    pad_start = pad_end - padded
    blk_start = jnp.arange(n_blk, dtype=jnp.int32) * rb
    blk_e = jnp.minimum(jnp.sum(pad_end[None, :] <= blk_start[:, None], axis=1), N_EXPERTS - 1).astype(jnp.int32)
    n_used = (pad_end[-1] // rb).astype(jnp.int32).reshape(1)
    experts = jnp.arange(N_EXPERTS, dtype=jnp.int32)[None, :]
    start_of = lambda e: jnp.sum(jnp.where(e[:, None] == experts, pad_start[None, :], 0), axis=1)
    d1 = (start_of(idx[0]) + idx[2]) * ROW_TILE
    d2 = (start_of(idx[1]) + idx[3]) * ROW_TILE
    xs = _dispatch(d1, d2, (pad_start + cnt) * ROW_TILE, padded - cnt, n_used, h, ln_g, cap, MOE_DISPATCH_ROWS)
    later = (experts > experts.T) & (padded > 0)[None, :]
    next_e = jnp.min(jnp.where(later, experts, N_EXPERTS), axis=1)
    next_e = jnp.where(next_e == N_EXPERTS, experts[0], next_e).astype(jnp.int32)
    ys = _experts(blk_e, n_used, next_e, xs, w1, w3, w2, layer)
    return _combine(d1, d2, h, meta, ys, g_final, MOE_COMBINE_ROWS, final_norm)


def kernel(x, ln_mix, ln_ffn, ln_final, ssm_w_in, ssm_lam_re, ssm_lam_im, ssm_log_dt, ssm_b_re, ssm_b_im, ssm_c_re, ssm_c_im, ssm_d, ssm_w_out, attn_w_qkv, attn_w_o, moe_w_group, moe_w_expert, moe_w1, moe_w3, moe_w2):
    bsz, seq, d = x.shape
    n = bsz * seq
    h = x.reshape(n, d)

    u = _norm_proj(h, ln_mix[0], ssm_w_in[0].astype(BF16), DENSE_ROWS, F32, "s5_in_proj")
    mats = _s5_mats(ssm_lam_re[0], ssm_lam_im[0], ssm_log_dt[0], ssm_b_re[0], ssm_b_im[0],
                    ssm_c_re[0], ssm_c_im[0])
    z = _s5_core(u, mats, ssm_d[0], bsz, seq)
    h, routing = _glu_out(z, ssm_w_out[0].astype(BF16), h, ln_ffn[0],
                          _router_weights(moe_w_group[0], moe_w_expert[0]), MIXER_OUT_ROWS)
    h = _hier_moe(h, routing, ln_ffn[0], moe_w1, moe_w3, moe_w2, 0, ln_final, False)

    q_t, k, v_t = _qkv_proj(h, ln_mix[1], attn_w_qkv[0], DENSE_ROWS)
    o_t = _moba(q_t, k, v_t, bsz, seq)
    h, routing = _proj_res_t(o_t, attn_w_o[0].astype(BF16), h, ln_ffn[1],
                             _router_weights(moe_w_group[1], moe_w_expert[1]), MIXER_OUT_ROWS)
    h = _hier_moe(h, routing, ln_ffn[1], moe_w1, moe_w3, moe_w2, 1, ln_final, True)
    return h.reshape(bsz, seq, d)
```
